```python
import math
import jax, jax.numpy as jnp
from jax import lax
import numpy as np

D_MODEL = 1024
BATCH = 2
SEQ = 8192
DEPTH = 4

N_MIXERS = 3
POOL_WINDOWS = (2, 4, 8, 16)
POOL_GROUPS = len(POOL_WINDOWS)
POOL_CH = D_MODEL // POOL_GROUPS
HEAD_DIM = 64
N_Q_HEADS = D_MODEL // HEAD_DIM
N_KV_HEADS = 4
GQA_GROUP = N_Q_HEADS // N_KV_HEADS
WINDOW = 128
ATTN_BLOCK = WINDOW
ROPE_THETA = 10000.0
QKV_WIDTH = (N_Q_HEADS + 2 * N_KV_HEADS) * HEAD_DIM
CONV_WIDTH = 3
N_EXPERTS = 64
TOP_K = 8
N_EXPERT_GROUPS = 8
TOPK_GROUPS = 4
D_EXPERT = 256
D_SHARED = 256
ROUTED_SCALE = 2.5
MOE_BLOCK = 128
LN_EPS = 1e-5
DEEPNORM_ALPHA = (2 * DEPTH) ** 0.25
DEEPNORM_BETA = (8 * DEPTH) ** -0.25

kernel_name = "hybrid_pool_swa_conv_moe_deepnorm"


def _layer_norm(x, g, b):
    xf = x.astype(jnp.float32)
    mu = xf.mean(-1, keepdims=True)
    var = jnp.square(xf - mu).mean(-1, keepdims=True)
    y = (xf - mu) * lax.rsqrt(var + LN_EPS) * g.astype(jnp.float32) + b.astype(jnp.float32)
    return y.astype(x.dtype)


def _rope_tables(seq):
    pos = jnp.arange(seq, dtype=jnp.float32)
    inv_freq = ROPE_THETA ** (-jnp.arange(0, HEAD_DIM, 2, dtype=jnp.float32) / HEAD_DIM)
    ang = pos[:, None] * inv_freq[None, :]
    return jnp.cos(ang), jnp.sin(ang)


def _apply_rope(t, cos, sin):
    tf = t.astype(jnp.float32)
    t1, t2 = jnp.split(tf, 2, axis=-1)
    c = cos[None, :, None, :]
    s = sin[None, :, None, :]
    return jnp.concatenate([t1 * c - t2 * s, t2 * c + t1 * s], axis=-1).astype(t.dtype)


def _pool_mixer(x, w_grp, scale):
    bsz, seq, d = x.shape
    xg = x.astype(jnp.float32).reshape(bsz, seq, POOL_GROUPS, POOL_CH)
    csum = jnp.cumsum(xg, axis=1)
    pos = jnp.arange(seq)
    outs = []
    for g, w in enumerate(POOL_WINDOWS):
        cg = csum[:, :, g]
        shifted = jnp.pad(cg, ((0, 0), (w, 0), (0, 0)))[:, :seq]
        cnt = jnp.minimum(pos + 1, w).astype(jnp.float32)[None, :, None]
        outs.append((cg - shifted) / cnt - xg[:, :, g])
    pooled = jnp.stack(outs, axis=2).astype(x.dtype)
    y = jnp.einsum('bsgc,gcd->bsgd', pooled, w_grp).reshape(bsz, seq, d)
    return y * scale


def _swa_attention(x, wqkv, bqkv, sinks, wo, bo, cos, sin):
    bsz, seq, d = x.shape
    nb = seq // ATTN_BLOCK
    qkv = x @ wqkv + bqkv
    q, k, v = jnp.split(qkv, [N_Q_HEADS * HEAD_DIM, (N_Q_HEADS + N_KV_HEADS) * HEAD_DIM], axis=-1)
    q = _apply_rope(q.reshape(bsz, seq, N_Q_HEADS, HEAD_DIM), cos, sin) * (HEAD_DIM ** -0.5)
    k = _apply_rope(k.reshape(bsz, seq, N_KV_HEADS, HEAD_DIM), cos, sin)
    v = v.reshape(bsz, seq, N_KV_HEADS, HEAD_DIM)
    qb = q.reshape(bsz, nb, ATTN_BLOCK, N_KV_HEADS, GQA_GROUP, HEAD_DIM)
    kb = k.reshape(bsz, nb, ATTN_BLOCK, N_KV_HEADS, HEAD_DIM)
    vb = v.reshape(bsz, nb, ATTN_BLOCK, N_KV_HEADS, HEAD_DIM)
    kk = jnp.concatenate([jnp.concatenate([jnp.zeros_like(kb[:, :1]), kb[:, :-1]], axis=1), kb], axis=2)
    vv = jnp.concatenate([jnp.concatenate([jnp.zeros_like(vb[:, :1]), vb[:, :-1]], axis=1), vb], axis=2)
    s = jnp.einsum('bnqhgd,bnkhd->bnhgqk', qb, kk, preferred_element_type=jnp.float32)
    qi = jnp.arange(ATTN_BLOCK)[:, None]
    ki = jnp.arange(2 * ATTN_BLOCK)[None, :]
    band = (ki > qi) & (ki <= qi + ATTN_BLOCK)
    exists = (jnp.arange(nb)[:, None, None] > 0) | (ki >= ATTN_BLOCK)[None]
    valid = band[None] & exists
    s = jnp.where(valid[None, :, None, None], s, -jnp.inf)
    sink = sinks.astype(jnp.float32).reshape(N_KV_HEADS, GQA_GROUP)[None, None, :, :, None, None]
    m = jnp.maximum(s.max(-1, keepdims=True), sink)
    p = jnp.exp(s - m)
    p = p / (p.sum(-1, keepdims=True) + jnp.exp(sink - m))
    o = jnp.einsum('bnhgqk,bnkhd->bnqhgd', p.astype(vv.dtype), vv)
    o = o.reshape(bsz, seq, N_Q_HEADS * HEAD_DIM)
    return o @ wo + bo


def _gated_conv(x, w_in, conv_w, w_out):
    bsz, seq, d = x.shape
    proj = x @ w_in
    gb, gc, h = jnp.split(proj, 3, axis=-1)
    u = gc * h
    conv = lax.conv_general_dilated(u, conv_w[:, None, :], window_strides=(1,),
                                    padding=[(CONV_WIDTH - 1, 0)],
                                    dimension_numbers=('NWC', 'WIO', 'NWC'),
                                    feature_group_count=d)
    return (gb * conv) @ w_out


def _moe(x2, w_router, b_router, w1, w3, w2, ws1, ws3, ws2):
    t = x2.shape[0]
    scores = jax.nn.sigmoid(x2.astype(jnp.float32) @ w_router.astype(jnp.float32))
    sel = scores + b_router.astype(jnp.float32)
    grp = sel.reshape(t, N_EXPERT_GROUPS, N_EXPERTS // N_EXPERT_GROUPS)
    grp_score = lax.top_k(grp, 2)[0].sum(-1)
    grp_idx = lax.top_k(grp_score, TOPK_GROUPS)[1]
    grp_mask = (grp_idx[..., None] == jnp.arange(N_EXPERT_GROUPS)).any(1)
    exp_mask = jnp.repeat(grp_mask, N_EXPERTS // N_EXPERT_GROUPS, axis=1)
    exp_idx = lax.top_k(jnp.where(exp_mask, sel, -jnp.inf), TOP_K)[1]
    gate = jnp.take_along_axis(scores, exp_idx, axis=1)
    gate = gate / gate.sum(-1, keepdims=True) * ROUTED_SCALE

    n_assign = t * TOP_K
    flat_e = exp_idx.reshape(n_assign)
    flat_t = jnp.repeat(jnp.arange(t), TOP_K)
    flat_g = gate.reshape(n_assign).astype(x2.dtype)
    order = jnp.argsort(flat_e)
    se = flat_e[order]
    counts = jnp.bincount(flat_e, length=N_EXPERTS)
    padded = (counts + MOE_BLOCK - 1) // MOE_BLOCK * MOE_BLOCK
    start = jnp.cumsum(counts) - counts
    pend = jnp.cumsum(padded)
    pstart = pend - padded
    dest = pstart[se] + jnp.arange(n_assign) - start[se]
    n_blocks = (n_assign + N_EXPERTS * (MOE_BLOCK - 1) + MOE_BLOCK - 1) // MOE_BLOCK
    n_rows = n_blocks * MOE_BLOCK
    row_t = jnp.zeros((n_rows,), jnp.int32).at[dest].set(flat_t[order].astype(jnp.int32))
    row_g = jnp.zeros((n_rows,), x2.dtype).at[dest].set(flat_g[order])
    block_e = jnp.clip(jnp.searchsorted(pend, jnp.arange(n_blocks) * MOE_BLOCK, side='right'),
                       0, N_EXPERTS - 1)

    def expert_block(args):
        e, toks, gs = args
        xb = x2[toks]
        hb = jax.nn.silu(xb @ w1[e]) * (xb @ w3[e])
        return (hb @ w2[e]) * gs[:, None]

    yb = lax.map(expert_block, (block_e, row_t.reshape(n_blocks, MOE_BLOCK),
                                row_g.reshape(n_blocks, MOE_BLOCK)))
    routed = jax.ops.segment_sum(yb.reshape(n_rows, -1), row_t, num_segments=t)
    shared = (jax.nn.silu(x2 @ ws1) * (x2 @ ws3)) @ ws2
    return routed + shared


def setup_inputs(seed: int = 0) -> dict:
    key = jax.random.key(seed)
    ks = jax.random.split(key, 24)
    n_pool = len(range(0, DEPTH, N_MIXERS))
    n_attn = len(range(1, DEPTH, N_MIXERS))
    n_conv = len(range(2, DEPTH, N_MIXERS))
    f32 = jnp.float32

    def nrm(k, shape, scale):
        return jax.random.normal(k, shape, f32) * scale

    d = D_MODEL
    return {
        "x": nrm(ks[0], (BATCH, SEQ, d), 1.0),
        "ln_gain": 1.0 + nrm(ks[1], (DEPTH, 2, d), 0.02),
        "ln_bias": nrm(ks[2], (DEPTH, 2, d), 0.02),
        "pool_w": nrm(ks[3], (n_pool, POOL_GROUPS, POOL_CH, POOL_CH), POOL_CH ** -0.5 * DEEPNORM_BETA),
        "pool_scale": 1.0 + nrm(ks[4], (n_pool, d), 0.02),
        "attn_wqkv": nrm(ks[5], (n_attn, d, QKV_WIDTH), d ** -0.5),
        "attn_bqkv": nrm(ks[6], (n_attn, QKV_WIDTH), 0.02),
        "attn_sinks": nrm(ks[7], (n_attn, N_Q_HEADS), 1.0),
        "attn_wo": nrm(ks[8], (n_attn, N_Q_HEADS * HEAD_DIM, d), (N_Q_HEADS * HEAD_DIM) ** -0.5 * DEEPNORM_BETA),
        "attn_bo": nrm(ks[9], (n_attn, d), 0.02),
        "conv_w_in": nrm(ks[10], (n_conv, d, 3 * d), d ** -0.5),
        "conv_w": nrm(ks[11], (n_conv, CONV_WIDTH, d), CONV_WIDTH ** -0.5),
        "conv_w_out": nrm(ks[12], (n_conv, d, d), d ** -0.5 * DEEPNORM_BETA),
        "router_w": nrm(ks[13], (DEPTH, d, N_EXPERTS), d ** -0.5),
        "router_bias": nrm(ks[14], (DEPTH, N_EXPERTS), 0.01),
        "expert_w1": nrm(ks[15], (DEPTH, N_EXPERTS, d, D_EXPERT), d ** -0.5),
        "expert_w3": nrm(ks[16], (DEPTH, N_EXPERTS, d, D_EXPERT), d ** -0.5),
        "expert_w2": nrm(ks[17], (DEPTH, N_EXPERTS, D_EXPERT, d), D_EXPERT ** -0.5 * DEEPNORM_BETA),
        "shared_w1": nrm(ks[18], (DEPTH, d, D_SHARED), d ** -0.5),
        "shared_w3": nrm(ks[19], (DEPTH, d, D_SHARED), d ** -0.5),
        "shared_w2": nrm(ks[20], (DEPTH, D_SHARED, d), D_SHARED ** -0.5 * DEEPNORM_BETA),
    }


def reference(x, ln_gain, ln_bias, pool_w, pool_scale, attn_wqkv, attn_bqkv, attn_sinks,
              attn_wo, attn_bo, conv_w_in, conv_w, conv_w_out, router_w, router_bias,
              expert_w1, expert_w3, expert_w2, shared_w1, shared_w3, shared_w2):
    bsz, seq, d = x.shape
    cos, sin = _rope_tables(seq)
    for i in range(DEPTH):
        kind = i % N_MIXERS
        j = i // N_MIXERS
        if kind == 0:
            mix = _pool_mixer(x, pool_w[j], pool_scale[j])
        elif kind == 1:
            mix = _swa_attention(x, attn_wqkv[j], attn_bqkv[j], attn_sinks[j],
                                 attn_wo[j], attn_bo[j], cos, sin)
        else:
            mix = _gated_conv(x, conv_w_in[j], conv_w[j], conv_w_out[j])
        x = _layer_norm(DEEPNORM_ALPHA * x + mix, ln_gain[i, 0], ln_bias[i, 0])
        ff = _moe(x.reshape(bsz * seq, d), router_w[i], router_bias[i], expert_w1[i],
                  expert_w3[i], expert_w2[i], shared_w1[i], shared_w3[i], shared_w2[i])
        x = _layer_norm(DEEPNORM_ALPHA * x + ff.reshape(bsz, seq, d), ln_gain[i, 1], ln_bias[i, 1])
    return x
```

```python
import functools

import jax
import jax.numpy as jnp
from jax import lax
from jax.experimental import pallas as pl
from jax.experimental.pallas import tpu as pltpu

POOL_WINDOWS = (2, 4, 8, 16)
N_MIXERS = 3
HEAD_DIM = 64
N_Q_HEADS = 16
N_KV_HEADS = 4
GQA_GROUP = N_Q_HEADS // N_KV_HEADS
WINDOW = 128
ROPE_THETA = 10000.0
N_EXPERTS = 64
TOP_K = 8
N_EXPERT_GROUPS = 8
GROUP_SIZE = N_EXPERTS // N_EXPERT_GROUPS
TOPK_GROUPS = 4
ROUTED_SCALE = 2.5
LN_EPS = 1e-5

LANES = 128
SUBLANES = 8
POOL_HALO = 16
VMEM_LIMIT = 48 * 1024 * 1024

POOL_TT = 512
QKV_TT = 512
ATTN_TQ = 256
CONV_TT = 256
ROUTE_TT = 512
DISPATCH_TD = 256
EXPERT_BM = 256

F32 = jnp.float32
BF16 = jnp.bfloat16
NEG_INF = float("-inf")


def _layer_norm(z, g, b):
    mu = jnp.mean(z, axis=-1, keepdims=True)
    zc = z - mu
    var = jnp.mean(zc * zc, axis=-1, keepdims=True)
    return zc * lax.rsqrt(var + LN_EPS) * g + b


def _params(*sem):
    return pltpu.CompilerParams(dimension_semantics=sem, vmem_limit_bytes=VMEM_LIMIT)


def _pool_kernel(x_ref, halo_ref, w_ref, scale_ref, g_ref, b_ref, o_ref, buf_ref, *, alpha, tt):
    i = pl.program_id(1)
    x = x_ref[...]
    ch = x.shape[1] // len(POOL_WINDOWS)
    buf_ref[0:POOL_HALO, :] = jnp.where(i > 0, halo_ref[...], 0.0)
    buf_ref[POOL_HALO:POOL_HALO + tt, :] = x
    pos = i * tt + lax.broadcasted_iota(jnp.int32, (tt, 1), 0)
    ys = []
    for g, w in enumerate(POOL_WINDOWS):
        c0 = g * ch
        xg = x[:, c0:c0 + ch]
        s = xg
        for j in range(1, w):
            s = s + buf_ref[POOL_HALO - j:POOL_HALO - j + tt, c0:c0 + ch]
        inv_cnt = 1.0 / jnp.minimum(pos + 1, w).astype(F32)
        pooled = s * inv_cnt - xg
        ys.append(jnp.dot(pooled.astype(BF16), w_ref[g], preferred_element_type=F32))
    y = jnp.concatenate(ys, axis=1) * scale_ref[...]
    o_ref[...] = _layer_norm(alpha * x + y, g_ref[...], b_ref[...])


def _pool_mixer(x2, seq, w_grp, scale, g, b, alpha):
    t, d = x2.shape
    tt = min(POOL_TT, seq)
    nt = seq // tt
    nb = t // seq
    row = lambda v: v.reshape(1, d)
    const2 = lambda bi, i: (0, 0)
    return pl.pallas_call(
        functools.partial(_pool_kernel, alpha=alpha, tt=tt),
        grid=(nb, nt),
        in_specs=[
            pl.BlockSpec((tt, d), lambda bi, i: (bi * nt + i, 0)),
            pl.BlockSpec((POOL_HALO, d),
                         lambda bi, i: (jnp.maximum((bi * seq + i * tt) // POOL_HALO - 1, 0), 0)),
            pl.BlockSpec(w_grp.shape, lambda bi, i: (0, 0, 0)),
            pl.BlockSpec((1, d), const2),
            pl.BlockSpec((1, d), const2),
            pl.BlockSpec((1, d), const2),
        ],
        out_specs=pl.BlockSpec((tt, d), lambda bi, i: (bi * nt + i, 0)),
        out_shape=jax.ShapeDtypeStruct((t, d), F32),
        scratch_shapes=[pltpu.VMEM((POOL_HALO + tt, d), F32)],
        compiler_params=_params("parallel", "parallel"),
        name="pool_mixer",
    )(x2, x2, w_grp.astype(BF16), row(scale), row(g), row(b))


def _qkv_kernel(x_ref, w_ref, b_ref, c_ref, s_ref, q_ref, k_ref, v_ref, *, nq, nk):
    y = jnp.dot(x_ref[...].astype(BF16), w_ref[...], preferred_element_type=F32) + b_ref[...]
    c = c_ref[...]
    s = s_ref[...]
    lane = lax.broadcasted_iota(jnp.int32, c.shape, 1)
    first_half = (lane % HEAD_DIM) < (HEAD_DIM // 2)

    def rope(v):
        partner = jnp.where(first_half, pltpu.roll(v, LANES - HEAD_DIM // 2, 1),
                            pltpu.roll(v, HEAD_DIM // 2, 1))
        return v * c + partner * s

    for j in range(nq // LANES):
        q_ref[:, j * LANES:(j + 1) * LANES] = (
            rope(y[:, j * LANES:(j + 1) * LANES]) * (HEAD_DIM ** -0.5)).astype(BF16)
    for j in range(nk // LANES):
        k_ref[:, j * LANES:(j + 1) * LANES] = rope(
            y[:, nq + j * LANES:nq + (j + 1) * LANES]).astype(BF16)
    v_ref[...] = y[:, nq + nk:].astype(BF16)


def _attn_kernel(sink_ref, q_ref, k_ref, kh_ref, v_ref, vh_ref, x_ref, wo_ref, bo_ref, g_ref, b_ref,
                 o_ref, *, alpha, tq):
    i = pl.program_id(1)
    tk = tq + WINDOW
    r = lax.broadcasted_iota(jnp.int32, (tq, tk), 0)
    c = lax.broadcasted_iota(jnp.int32, (tq, tk), 1)
    dist = r + WINDOW - c
    valid = (dist >= 0) & (dist < WINDOW) & ((c >= WINDOW) | (i > 0))
    lane = lax.broadcasted_iota(jnp.int32, (tq, LANES), 1)
    low = lane < HEAD_DIM
    dn = (((1,), (1,)), ((), ()))
    outs = []
    for kv in range(N_KV_HEADS):
        sl = slice(kv * LANES, (kv + 1) * LANES)
        kk = jnp.concatenate([kh_ref[:, sl], k_ref[:, sl]], axis=0)
        vv = jnp.concatenate([vh_ref[:, sl], v_ref[:, sl]], axis=0)
        for p in range(GQA_GROUP // 2):
            c0 = (kv * GQA_GROUP + 2 * p) * HEAD_DIM
            qp = q_ref[:, c0:c0 + LANES]
            res = []
            for half in range(2):
                keep = low if half == 0 else jnp.logical_not(low)
                qh = jnp.where(keep, qp, jnp.zeros_like(qp))
                sc = lax.dot_general(qh, kk, dn, preferred_element_type=F32)
                sc = jnp.where(valid, sc, NEG_INF)
                sink = sink_ref[kv * GQA_GROUP + 2 * p + half]
                m = jnp.maximum(jnp.max(sc, axis=1, keepdims=True), sink)
                pm = jnp.exp(sc - m)
                den = jnp.sum(pm, axis=1, keepdims=True) + jnp.exp(sink - m)
                res.append(jnp.dot(pm.astype(BF16), vv, preferred_element_type=F32) / den)
            outs.append(jnp.where(low, res[0], res[1]))
    o_all = jnp.concatenate(outs, axis=1).astype(BF16)
    mix = jnp.dot(o_all, wo_ref[...], preferred_element_type=F32) + bo_ref[...]
    o_ref[...] = _layer_norm(alpha * x_ref[...] + mix, g_ref[...], b_ref[...])


def _dup_heads(w, n_heads):
    lead = w.shape[:-1]
    w = w.reshape(lead + (n_heads, 1, HEAD_DIM))
    w = jnp.broadcast_to(w, lead + (n_heads, 2, HEAD_DIM))
    return w.reshape(lead + (n_heads * 2 * HEAD_DIM,))


def _swa_mixer(x2, seq, wqkv, bqkv, sinks, wo, bo, g, b, alpha):
    t, d = x2.shape
    nb = t // seq
    nq = N_Q_HEADS * HEAD_DIM
    nkv = N_KV_HEADS * HEAD_DIM
    nk2 = 2 * nkv
    w_ext = jnp.concatenate([wqkv[:, :nq], _dup_heads(wqkv[:, nq:nq + nkv], N_KV_HEADS),
                             _dup_heads(wqkv[:, nq + nkv:], N_KV_HEADS)], axis=1).astype(BF16)
    b_ext = jnp.concatenate([bqkv[:nq], _dup_heads(bqkv[nq:nq + nkv], N_KV_HEADS),
                             _dup_heads(bqkv[nq + nkv:], N_KV_HEADS)]).reshape(1, -1)
    n_ext = nq + 2 * nk2

    pos = jnp.arange(seq, dtype=F32)
    inv_freq = ROPE_THETA ** (-jnp.arange(0, HEAD_DIM, 2, dtype=F32) / HEAD_DIM)
    ang = pos[:, None] * inv_freq[None, :]
    cos, sin = jnp.cos(ang), jnp.sin(ang)
    cos_t = jnp.concatenate([cos, cos, cos, cos], axis=1)
    sin_t = jnp.concatenate([-sin, sin, -sin, sin], axis=1)

    tt = min(QKV_TT, seq)
    nt = seq // tt
    const2 = lambda i: (0, 0)
    q, k2, v2 = pl.pallas_call(
        functools.partial(_qkv_kernel, nq=nq, nk=nk2),
        grid=(t // tt,),
        in_specs=[
            pl.BlockSpec((tt, d), lambda i: (i, 0)),
            pl.BlockSpec((d, n_ext), const2),
            pl.BlockSpec((1, n_ext), const2),
            pl.BlockSpec((tt, LANES), lambda i: (i % nt, 0)),
            pl.BlockSpec((tt, LANES), lambda i: (i % nt, 0)),
        ],
        out_specs=[pl.BlockSpec((tt, nq), lambda i: (i, 0)),
                   pl.BlockSpec((tt, nk2), lambda i: (i, 0)),
                   pl.BlockSpec((tt, nk2), lambda i: (i, 0))],
        out_shape=[jax.ShapeDtypeStruct((t, nq), BF16),
                   jax.ShapeDtypeStruct((t, nk2), BF16),
                   jax.ShapeDtypeStruct((t, nk2), BF16)],
        compiler_params=_params("parallel"),
        name="swa_qkv",
    )(x2, w_ext, b_ext, cos_t, sin_t)

    tq = min(ATTN_TQ, seq)
    ntq = seq // tq
    main = lambda bi, i: (bi * ntq + i, 0)
    halo = lambda bi, i: (jnp.maximum((bi * seq + i * tq) // WINDOW - 1, 0), 0)
    const = lambda bi, i: (0, 0)
    row = lambda v: v.reshape(1, d)
    return pl.pallas_call(
        functools.partial(_attn_kernel, alpha=alpha, tq=tq),
        grid=(nb, ntq),
        in_specs=[
            pl.BlockSpec(memory_space=pltpu.SMEM),
            pl.BlockSpec((tq, nq), main),
            pl.BlockSpec((tq, nk2), main),
            pl.BlockSpec((WINDOW, nk2), halo),
            pl.BlockSpec((tq, nk2), main),
            pl.BlockSpec((WINDOW, nk2), halo),
            pl.BlockSpec((tq, d), main),
            pl.BlockSpec((nq, d), const),
            pl.BlockSpec((1, d), const),
            pl.BlockSpec((1, d), const),
            pl.BlockSpec((1, d), const),
        ],
        out_specs=pl.BlockSpec((tq, d), main),
        out_shape=jax.ShapeDtypeStruct((t, d), F32),
        compiler_params=_params("parallel", "parallel"),
        name="swa_attention",
    )(sinks, q, k2, k2, v2, v2, x2, wo.astype(BF16), row(bo), row(g), row(b))


def _conv_kernel(x_ref, win_ref, cw_ref, wout_ref, g_ref, b_ref, o_ref, ubuf_ref, *, alpha, tt):
    i = pl.program_id(1)
    d = x_ref.shape[1]

    @pl.when(i == 0)
    def _():
        ubuf_ref[0:SUBLANES, :] = jnp.zeros((SUBLANES, d), F32)

    x = x_ref[...]
    proj = jnp.dot(x.astype(BF16), win_ref[...], preferred_element_type=F32)
    gate_b = proj[:, :d]
    u = proj[:, d:2 * d] * proj[:, 2 * d:]
    ubuf_ref[SUBLANES:SUBLANES + tt, :] = u
    cw = cw_ref[...]
    conv = (cw[0:1, :] * ubuf_ref[SUBLANES - 2:SUBLANES - 2 + tt, :]
            + cw[1:2, :] * ubuf_ref[SUBLANES - 1:SUBLANES - 1 + tt, :]
            + cw[2:3, :] * u)
    ubuf_ref[0:SUBLANES, :] = u[tt - SUBLANES:tt, :]
    y = jnp.dot((gate_b * conv).astype(BF16), wout_ref[...], preferred_element_type=F32)
    o_ref[...] = _layer_norm(alpha * x + y, g_ref[...], b_ref[...])


def _conv_mixer(x2, seq, w_in, conv_w, w_out, g, b, alpha):
    t, d = x2.shape
    nb = t // seq
    tt = min(CONV_TT, seq)
    nt = seq // tt
    const = lambda bi, i: (0, 0)
    row = lambda v: v.reshape(1, d)
    return pl.pallas_call(
        functools.partial(_conv_kernel, alpha=alpha, tt=tt),
        grid=(nb, nt),
        in_specs=[
            pl.BlockSpec((tt, d), lambda bi, i: (bi * nt + i, 0)),
            pl.BlockSpec((d, 3 * d), const),
            pl.BlockSpec(conv_w.shape, const),
            pl.BlockSpec((d, d), const),
            pl.BlockSpec((1, d), const),
            pl.BlockSpec((1, d), const),
        ],
        out_specs=pl.BlockSpec((tt, d), lambda bi, i: (bi * nt + i, 0)),
        out_shape=jax.ShapeDtypeStruct((t, d), F32),
        scratch_shapes=[pltpu.VMEM((SUBLANES + tt, d), F32)],
        compiler_params=_params("arbitrary", "arbitrary"),
        name="conv_mixer",
    )(x2, w_in.astype(BF16), conv_w, w_out.astype(BF16), row(g), row(b))


def _first_argmax(v, rows, n):
    m = jnp.max(v, axis=0, keepdims=True)
    first = jnp.min(jnp.where(v == m, rows, n), axis=0, keepdims=True)
    return m, first


def _route_kernel(x_ref, wt_ref, bias_ref, idx_ref, gate_ref, rank_ref, cnt_ref, base_ref, *, tt):
    i = pl.program_id(0)

    @pl.when(i == 0)
    def _():
        base_ref[...] = jnp.zeros_like(base_ref)

    x = x_ref[...]
    xh = x.astype(BF16)
    xl = (x - xh.astype(F32)).astype(BF16)
    wt = wt_ref[...]
    wh = wt.astype(BF16)
    wl = (wt - wh.astype(F32)).astype(BF16)
    dn = (((1,), (1,)), ((), ()))
    logits = (lax.dot_general(wh, xh, dn, preferred_element_type=F32)
              + (lax.dot_general(wh, xl, dn, preferred_element_type=F32)
                 + lax.dot_general(wl, xh, dn, preferred_element_type=F32)))
    scores = jax.nn.sigmoid(logits)
    sel = scores + bias_ref[...]

    rows8 = lax.broadcasted_iota(jnp.int32, (GROUP_SIZE, tt), 0)
    gsc = []
    for gi in range(N_EXPERT_GROUPS):
        blk = sel[gi * GROUP_SIZE:(gi + 1) * GROUP_SIZE, :]
        m1, f1 = _first_argmax(blk, rows8, GROUP_SIZE)
        m2 = jnp.max(jnp.where(rows8 == f1, NEG_INF, blk), axis=0, keepdims=True)
        gsc.append(m1 + m2)
    cur = jnp.concatenate(gsc, axis=0)
    rows_g = lax.broadcasted_iota(jnp.int32, (N_EXPERT_GROUPS, tt), 0)
    gmask = jnp.zeros((N_EXPERT_GROUPS, tt), F32)
    for _ in range(TOPK_GROUPS):
        _, f = _first_argmax(cur, rows_g, N_EXPERT_GROUPS)
        hit = rows_g == f
        gmask = jnp.where(hit, 1.0, gmask)
        cur = jnp.where(hit, NEG_INF, cur)
    emask = jnp.concatenate(
        [jnp.broadcast_to(gmask[gi:gi + 1, :], (GROUP_SIZE, tt)) for gi in range(N_EXPERT_GROUPS)],
        axis=0)

    rows_e = lax.broadcasted_iota(jnp.int32, (N_EXPERTS, tt), 0)
    cur = jnp.where(emask > 0.0, sel, NEG_INF)
    idxs, gates = [], []
    member = jnp.zeros((N_EXPERTS, tt), F32)
    for _ in range(TOP_K):
        _, f = _first_argmax(cur, rows_e, N_EXPERTS)
        hit = rows_e == f
        idxs.append(f)
        gates.append(jnp.sum(jnp.where(hit, scores, 0.0), axis=0, keepdims=True))
        member = jnp.where(hit, 1.0, member)
        cur = jnp.where(hit, NEG_INF, cur)
    gate = jnp.concatenate(gates, axis=0)
    gate = gate / jnp.sum(gate, axis=0, keepdims=True) * ROUTED_SCALE

    ta = lax.broadcasted_iota(jnp.int32, (tt, tt), 0)
    tb = lax.broadcasted_iota(jnp.int32, (tt, tt), 1)
    earlier = (ta < tb).astype(BF16)
    before = jnp.dot(member.astype(BF16), earlier, preferred_element_type=F32)
    before = before + base_ref[:, 0:1]
    for k in range(TOP_K):
        rk = jnp.sum(jnp.where(rows_e == idxs[k], before, 0.0), axis=0, keepdims=True)
        rank_ref[k:k + 1, :] = rk.astype(jnp.int32)
        idx_ref[k:k + 1, :] = idxs[k]
    gate_ref[...] = gate
    new_base = base_ref[...] + jnp.sum(member, axis=1, keepdims=True)
    base_ref[...] = new_base
    cnt_ref[...] = new_base


def _route(x1, router_w, router_bias):
    t, d = x1.shape
    tt = min(ROUTE_TT, t)
    row_blk = lambda i: (0, i)
    return pl.pallas_call(
        functools.partial(_route_kernel, tt=tt),
        grid=(t // tt,),
        in_specs=[
            pl.BlockSpec((tt, d), lambda i: (i, 0)),
            pl.BlockSpec((N_EXPERTS, d), lambda i: (0, 0)),
            pl.BlockSpec((N_EXPERTS, 1), lambda i: (0, 0)),
        ],
        out_specs=[pl.BlockSpec((TOP_K, tt), row_blk),
                   pl.BlockSpec((TOP_K, tt), row_blk),
                   pl.BlockSpec((TOP_K, tt), row_blk),
                   pl.BlockSpec((N_EXPERTS, LANES), lambda i: (0, 0))],
        out_shape=[jax.ShapeDtypeStruct((TOP_K, t), jnp.int32),
                   jax.ShapeDtypeStruct((TOP_K, t), F32),
                   jax.ShapeDtypeStruct((TOP_K, t), jnp.int32),
                   jax.ShapeDtypeStruct((N_EXPERTS, LANES), F32)],
        scratch_shapes=[pltpu.VMEM((N_EXPERTS, LANES), F32)],
        compiler_params=_params("arbitrary"),
        name="moe_router",
    )(x1, router_w.T, router_bias.reshape(N_EXPERTS, 1))


def _scatter_kernel(zrow_ref, zflag_ref, dest_ref, x_ref, xs_ref, zbuf_ref, sem, zsem, *, td, bm):
    i = pl.program_id(0)

    def zero_copy(e):
        row = pl.multiple_of(zrow_ref[e], bm)
        return pltpu.make_async_copy(zbuf_ref, xs_ref.at[pl.ds(row, bm), :], zsem)

    @pl.when(i == 0)
    def _():
        zbuf_ref[...] = jnp.zeros_like(zbuf_ref)

        def start(e, carry):
            @pl.when(zflag_ref[e] > 0)
            def _():
                zero_copy(e).start()
            return carry

        def wait(e, carry):
            @pl.when(zflag_ref[e] > 0)
            def _():
                zero_copy(e).wait()
            return carry

        lax.fori_loop(0, N_EXPERTS, start, 0)
        lax.fori_loop(0, N_EXPERTS, wait, 0)

    def row_copy(t, dst):
        return pltpu.make_async_copy(x_ref.at[pl.ds(t, 1), :], xs_ref.at[pl.ds(dst, 1), :], sem)

    def start(t, carry):
        for k in range(TOP_K):
            row_copy(t, dest_ref[k * td + t]).start()
        return carry

    def wait(t, carry):
        for k in range(TOP_K):
            row_copy(t, dest_ref[k * td + t]).wait()
        return carry

    lax.fori_loop(0, td, start, 0, unroll=4)
    lax.fori_loop(0, td, wait, 0, unroll=4)


def _scatter_rows(x1, dest_flat, zrow, zflag, n_rows, td, bm):
    t, d = x1.shape
    grid_spec = pltpu.PrefetchScalarGridSpec(
        num_scalar_prefetch=2,
        grid=(t // td,),
        in_specs=[
            pl.BlockSpec((TOP_K * td,), lambda i, zr, zf: (i,), memory_space=pltpu.SMEM),
            pl.BlockSpec((td, d), lambda i, zr, zf: (i, 0)),
        ],
        out_specs=pl.BlockSpec(memory_space=pl.ANY),
        scratch_shapes=[pltpu.VMEM((bm, d), F32), pltpu.SemaphoreType.DMA, pltpu.SemaphoreType.DMA],
    )
    return pl.pallas_call(
        functools.partial(_scatter_kernel, td=td, bm=bm),
        grid_spec=grid_spec,
        out_shape=jax.ShapeDtypeStruct((n_rows, d), F32),
        compiler_params=_params("arbitrary"),
        name="moe_scatter",
    )(zrow, zflag, dest_flat, x1)


def _gmm_kernel(be_ref, nv_ref, xs_ref, w1_ref, w3_ref, w2_ref, y_ref, w13_ref, w2b_ref):
    i = pl.program_id(0)
    f = w1_ref.shape[1]
    changed = jnp.logical_or(i == 0, be_ref[i] != be_ref[jnp.maximum(i - 1, 0)])

    @pl.when(changed)
    def _():
        w13_ref[:, :f] = w1_ref[...].astype(BF16)
        w13_ref[:, f:] = w3_ref[...].astype(BF16)
        w2b_ref[...] = w2_ref[...].astype(BF16)

    @pl.when(i < nv_ref[0])
    def _():
        h13 = jnp.dot(xs_ref[...].astype(BF16), w13_ref[...], preferred_element_type=F32)
        h = jax.nn.silu(h13[:, :f]) * h13[:, f:]
        y_ref[...] = jnp.dot(h.astype(BF16), w2b_ref[...], preferred_element_type=F32)


def _gmm(xs, blk_expert, n_valid, w1, w3, w2, layer, bm):
    n_rows, d = xs.shape
    f = w1.shape[-1]
    nblk = n_rows // bm
    xmap = lambda i, be, nv: (jnp.minimum(i, nv[0] - 1), 0)
    wmap = lambda i, be, nv: (layer, be[i], 0, 0)
    grid_spec = pltpu.PrefetchScalarGridSpec(
        num_scalar_prefetch=2,
        grid=(nblk,),
        in_specs=[
            pl.BlockSpec((bm, d), xmap),
            pl.BlockSpec((None, None, d, f), wmap),
            pl.BlockSpec((None, None, d, f), wmap),
            pl.BlockSpec((None, None, f, d), wmap),
        ],
        out_specs=pl.BlockSpec((bm, d), xmap),
        scratch_shapes=[pltpu.VMEM((d, 2 * f), BF16), pltpu.VMEM((f, d), BF16)],
    )
    return pl.pallas_call(
        _gmm_kernel,
        grid_spec=grid_spec,
        out_shape=jax.ShapeDtypeStruct((n_rows, d), F32),
        compiler_params=_params("arbitrary"),
        name="moe_experts",
    )(blk_expert, n_valid, xs, w1, w3, w2)


def _combine_kernel(dest_ref, yb_ref, gate_ref, x_ref, ws13_ref, ws2_ref, g_ref, b_ref, o_ref,
                    ybuf_ref, sem, *, alpha, td):
    def row_copy(t, k, src):
        return pltpu.make_async_copy(yb_ref.at[pl.ds(src, 1), :], ybuf_ref.at[k, pl.ds(t, 1), :], sem)

    def start(t, carry):
        for k in range(TOP_K):
            row_copy(t, k, dest_ref[k * td + t]).start()
        return carry

    def wait(t, carry):
        for k in range(TOP_K):
            row_copy(t, k, dest_ref[k * td + t]).wait()
        return carry

    lax.fori_loop(0, td, start, 0, unroll=4)

    x = x_ref[...]
    f = ws2_ref.shape[0]
    h13 = jnp.dot(x.astype(BF16), ws13_ref[...], preferred_element_type=F32)
    hs = jax.nn.silu(h13[:, :f]) * h13[:, f:]
    shared = jnp.dot(hs.astype(BF16), ws2_ref[...], preferred_element_type=F32)

    lax.fori_loop(0, td, wait, 0, unroll=4)
    gate = gate_ref[...]
    routed = gate[:, 0:1] * ybuf_ref[0]
    for k in range(1, TOP_K):
        routed = routed + gate[:, k:k + 1] * ybuf_ref[k]
    o_ref[...] = _layer_norm(alpha * x + (routed + shared), g_ref[...], b_ref[...])


def _combine(yb, dest_flat, gate_tk, x1, ws1, ws3, ws2, g, b, alpha, td):
    t, d = x1.shape
    f = ws1.shape[1]
    ws13 = jnp.concatenate([ws1, ws3], axis=1).astype(BF16)
    const = lambda i: (0, 0)
    row = lambda v: v.reshape(1, d)
    return pl.pallas_call(
        functools.partial(_combine_kernel, alpha=alpha, td=td),
        grid=(t // td,),
        in_specs=[
            pl.BlockSpec((TOP_K * td,), lambda i: (i,), memory_space=pltpu.SMEM),
            pl.BlockSpec(memory_space=pl.ANY),
            pl.BlockSpec((td, TOP_K), lambda i: (i, 0)),
            pl.BlockSpec((td, d), lambda i: (i, 0)),
            pl.BlockSpec((d, 2 * f), const),
            pl.BlockSpec((f, d), const),
            pl.BlockSpec((1, d), const),
            pl.BlockSpec((1, d), const),
        ],
        out_specs=pl.BlockSpec((td, d), lambda i: (i, 0)),
        out_shape=jax.ShapeDtypeStruct((t, d), F32),
        scratch_shapes=[pltpu.VMEM((TOP_K, td, d), F32), pltpu.SemaphoreType.DMA],
        compiler_params=_params("arbitrary"),
        name="moe_combine",
    )(dest_flat, yb, gate_tk, x1, ws13, ws2.astype(BF16), row(g), row(b))


def _moe_layer(x1, layer, router_w, router_bias, w1, w3, w2, ws1, ws3, ws2, g, b, alpha):
    t, d = x1.shape
    td = min(DISPATCH_TD, t)
    bm = EXPERT_BM
    idx, gate, rank, counts = _route(x1, router_w, router_bias)

    cnt = counts[:, 0].astype(jnp.int32)
    padded = (cnt + bm - 1) // bm * bm
    pend = jnp.cumsum(padded)
    pstart = pend - padded
    nblk = (t * TOP_K) // bm + N_EXPERTS
    n_valid = pend[-1] // bm
    blk = jnp.arange(nblk, dtype=jnp.int32)
    blk_expert = jnp.clip(jnp.searchsorted(pend, blk * bm, side="right"), 0, N_EXPERTS - 1)
    blk_expert = jnp.where(blk < n_valid, blk_expert, blk_expert[n_valid - 1]).astype(jnp.int32)
    dest = pstart[idx] + rank
    dest_flat = dest.reshape(TOP_K, t // td, td).transpose(1, 0, 2).reshape(-1)
    zrow = jnp.maximum(pend - bm, 0).astype(jnp.int32)
    zflag = (cnt > 0).astype(jnp.int32)

    xs = _scatter_rows(x1, dest_flat, zrow, zflag, nblk * bm, td, bm)
    yb = _gmm(xs, blk_expert, n_valid.reshape(1).astype(jnp.int32), w1, w3, w2, layer, bm)
    return _combine(yb, dest_flat, gate.T, x1, ws1, ws3, ws2, g, b, alpha, td)


def kernel(x, ln_gain, ln_bias, pool_w, pool_scale, attn_wqkv, attn_bqkv, attn_sinks, attn_wo, attn_bo,
           conv_w_in, conv_w, conv_w_out, router_w, router_bias, expert_w1, expert_w3, expert_w2,
           shared_w1, shared_w3, shared_w2):
    bsz, seq, d = x.shape
    depth = ln_gain.shape[0]
    alpha = float((2 * depth) ** 0.25)
    h = x.reshape(bsz * seq, d)
    for i in range(depth):
        kind = i % N_MIXERS
        j = i // N_MIXERS
        g1, b1 = ln_gain[i, 0], ln_bias[i, 0]
        if kind == 0:
            h = _pool_mixer(h, seq, pool_w[j], pool_scale[j], g1, b1, alpha)
        elif kind == 1:
            h = _swa_mixer(h, seq, attn_wqkv[j], attn_bqkv[j], attn_sinks[j], attn_wo[j], attn_bo[j],
                           g1, b1, alpha)
        else:
            h = _conv_mixer(h, seq, conv_w_in[j], conv_w[j], conv_w_out[j], g1, b1, alpha)
        h = _moe_layer(h, i, router_w[i], router_bias[i], expert_w1, expert_w3, expert_w2,
                       shared_w1[i], shared_w3[i], shared_w2[i], ln_gain[i, 1], ln_bias[i, 1], alpha)
    return h.reshape(bsz, seq, d)
```

```python
import functools

import jax
import jax.numpy as jnp
from jax import lax
from jax.experimental import pallas as pl
from jax.experimental.pallas import tpu as pltpu

POOL_WINDOWS = (2, 4, 8, 16)
N_MIXERS = 3
HEAD_DIM = 64
N_Q_HEADS = 16
N_KV_HEADS = 4
GQA_GROUP = N_Q_HEADS // N_KV_HEADS
WINDOW = 128
ROPE_THETA = 10000.0
N_EXPERTS = 64
TOP_K = 8
N_EXPERT_GROUPS = 8
GROUP_SIZE = N_EXPERTS // N_EXPERT_GROUPS
TOPK_GROUPS = 4
ROUTED_SCALE = 2.5
LN_EPS = 1e-5

LANES = 128
SUBLANES = 8
POOL_HALO = 16
VMEM_LIMIT = 48 * 1024 * 1024

POOL_TT = 512
QKV_TT = 512
ATTN_TQ = 256
CONV_TT = 256
ROUTE_TT = 512
DISPATCH_TD = 256
EXPERT_BM = 256

F32 = jnp.float32
BF16 = jnp.bfloat16
NEG_INF = float("-inf")


def _layer_norm(z, g, b):
    mu = jnp.mean(z, axis=-1, keepdims=True)
    zc = z - mu
    var = jnp.mean(zc * zc, axis=-1, keepdims=True)
    return zc * lax.rsqrt(var + LN_EPS) * g + b


def _params(*sem):
    return pltpu.CompilerParams(dimension_semantics=sem, vmem_limit_bytes=VMEM_LIMIT)


def _pool_kernel(x_ref, halo_ref, w_ref, scale_ref, g_ref, b_ref, o_ref, buf_ref, *, alpha, tt):
    i = pl.program_id(1)
    x = x_ref[...]
    ch = x.shape[1] // len(POOL_WINDOWS)
    buf_ref[0:POOL_HALO, :] = jnp.where(i > 0, halo_ref[...], 0.0)
    buf_ref[POOL_HALO:POOL_HALO + tt, :] = x
    pos = i * tt + lax.broadcasted_iota(jnp.int32, (tt, 1), 0)
    ys = []
    for g, w in enumerate(POOL_WINDOWS):
        c0 = g * ch
        xg = x[:, c0:c0 + ch]
        s = xg
        for j in range(1, w):
            s = s + buf_ref[POOL_HALO - j:POOL_HALO - j + tt, c0:c0 + ch]
        inv_cnt = 1.0 / jnp.minimum(pos + 1, w).astype(F32)
        pooled = s * inv_cnt - xg
        ys.append(jnp.dot(pooled.astype(BF16), w_ref[g], preferred_element_type=F32))
    y = jnp.concatenate(ys, axis=1) * scale_ref[...]
    o_ref[...] = _layer_norm(alpha * x + y, g_ref[...], b_ref[...])


def _pool_mixer(x2, seq, w_grp, scale, g, b, alpha):
    t, d = x2.shape
    tt = min(POOL_TT, seq)
    nt = seq // tt
    nb = t // seq
    row = lambda v: v.reshape(1, d)
    const2 = lambda bi, i: (0, 0)
    return pl.pallas_call(
        functools.partial(_pool_kernel, alpha=alpha, tt=tt),
        grid=(nb, nt),
        in_specs=[
            pl.BlockSpec((tt, d), lambda bi, i: (bi * nt + i, 0)),
            pl.BlockSpec((POOL_HALO, d),
                         lambda bi, i: (jnp.maximum((bi * seq + i * tt) // POOL_HALO - 1, 0), 0)),
            pl.BlockSpec(w_grp.shape, lambda bi, i: (0, 0, 0)),
            pl.BlockSpec((1, d), const2),
            pl.BlockSpec((1, d), const2),
            pl.BlockSpec((1, d), const2),
        ],
        out_specs=pl.BlockSpec((tt, d), lambda bi, i: (bi * nt + i, 0)),
        out_shape=jax.ShapeDtypeStruct((t, d), F32),
        scratch_shapes=[pltpu.VMEM((POOL_HALO + tt, d), F32)],
        compiler_params=_params("parallel", "parallel"),
        name="pool_mixer",
    )(x2, x2, w_grp.astype(BF16), row(scale), row(g), row(b))


def _qkv_kernel(x_ref, w_ref, b_ref, c_ref, s_ref, q_ref, k_ref, v_ref, *, nq, nk):
    y = jnp.dot(x_ref[...].astype(BF16), w_ref[...], preferred_element_type=F32) + b_ref[...]
    c = c_ref[...]
    s = s_ref[...]
    lane = lax.broadcasted_iota(jnp.int32, c.shape, 1)
    first_half = (lane % HEAD_DIM) < (HEAD_DIM // 2)

    def rope(v):
        partner = jnp.where(first_half, pltpu.roll(v, LANES - HEAD_DIM // 2, 1),
                            pltpu.roll(v, HEAD_DIM // 2, 1))
        return v * c + partner * s

    for j in range(nq // LANES):
        q_ref[:, j * LANES:(j + 1) * LANES] = (
            rope(y[:, j * LANES:(j + 1) * LANES]) * (HEAD_DIM ** -0.5)).astype(BF16)
    for j in range(nk // LANES):
        k_ref[:, j * LANES:(j + 1) * LANES] = rope(
            y[:, nq + j * LANES:nq + (j + 1) * LANES]).astype(BF16)
    v_ref[...] = y[:, nq + nk:].astype(BF16)


def _attn_kernel(sink_ref, q_ref, k_ref, kh_ref, v_ref, vh_ref, x_ref, wo_ref, bo_ref, g_ref, b_ref,
                 o_ref, *, alpha, tq):
    i = pl.program_id(1)
    tk = tq + WINDOW
    r = lax.broadcasted_iota(jnp.int32, (tq, tk), 0)
    c = lax.broadcasted_iota(jnp.int32, (tq, tk), 1)
    dist = r + WINDOW - c
    valid = (dist >= 0) & (dist < WINDOW) & ((c >= WINDOW) | (i > 0))
    lane = lax.broadcasted_iota(jnp.int32, (tq, LANES), 1)
    low = lane < HEAD_DIM
    dn = (((1,), (1,)), ((), ()))
    outs = []
    for kv in range(N_KV_HEADS):
        sl = slice(kv * LANES, (kv + 1) * LANES)
        kk = jnp.concatenate([kh_ref[:, sl], k_ref[:, sl]], axis=0)
        vv = jnp.concatenate([vh_ref[:, sl], v_ref[:, sl]], axis=0)
        for p in range(GQA_GROUP // 2):
            c0 = (kv * GQA_GROUP + 2 * p) * HEAD_DIM
            qp = q_ref[:, c0:c0 + LANES]
            res = []
            for half in range(2):
                keep = low if half == 0 else jnp.logical_not(low)
                qh = jnp.where(keep, qp, jnp.zeros_like(qp))
                sc = lax.dot_general(qh, kk, dn, preferred_element_type=F32)
                sc = jnp.where(valid, sc, NEG_INF)
                sink = sink_ref[kv * GQA_GROUP + 2 * p + half]
                m = jnp.maximum(jnp.max(sc, axis=1, keepdims=True), sink)
                pm = jnp.exp(sc - m)
                den = jnp.sum(pm, axis=1, keepdims=True) + jnp.exp(sink - m)
                res.append(jnp.dot(pm.astype(BF16), vv, preferred_element_type=F32) / den)
            outs.append(jnp.where(low, res[0], res[1]))
    o_all = jnp.concatenate(outs, axis=1).astype(BF16)
    mix = jnp.dot(o_all, wo_ref[...], preferred_element_type=F32) + bo_ref[...]
    o_ref[...] = _layer_norm(alpha * x_ref[...] + mix, g_ref[...], b_ref[...])


def _dup_heads(w, n_heads):
    lead = w.shape[:-1]
    w = w.reshape(lead + (n_heads, 1, HEAD_DIM))
    w = jnp.broadcast_to(w, lead + (n_heads, 2, HEAD_DIM))
    return w.reshape(lead + (n_heads * 2 * HEAD_DIM,))


def _swa_mixer(x2, seq, wqkv, bqkv, sinks, wo, bo, g, b, alpha):
    t, d = x2.shape
    nb = t // seq
    nq = N_Q_HEADS * HEAD_DIM
    nkv = N_KV_HEADS * HEAD_DIM
    nk2 = 2 * nkv
    w_ext = jnp.concatenate([wqkv[:, :nq], _dup_heads(wqkv[:, nq:nq + nkv], N_KV_HEADS),
                             _dup_heads(wqkv[:, nq + nkv:], N_KV_HEADS)], axis=1).astype(BF16)
    b_ext = jnp.concatenate([bqkv[:nq], _dup_heads(bqkv[nq:nq + nkv], N_KV_HEADS),
                             _dup_heads(bqkv[nq + nkv:], N_KV_HEADS)]).reshape(1, -1)
    n_ext = nq + 2 * nk2

    pos = jnp.arange(seq, dtype=F32)
    inv_freq = ROPE_THETA ** (-jnp.arange(0, HEAD_DIM, 2, dtype=F32) / HEAD_DIM)
    ang = pos[:, None] * inv_freq[None, :]
    cos, sin = jnp.cos(ang), jnp.sin(ang)
    cos_t = jnp.concatenate([cos, cos, cos, cos], axis=1)
    sin_t = jnp.concatenate([-sin, sin, -sin, sin], axis=1)

    tt = min(QKV_TT, seq)
    nt = seq // tt
    const2 = lambda i: (0, 0)
    q, k2, v2 = pl.pallas_call(
        functools.partial(_qkv_kernel, nq=nq, nk=nk2),
        grid=(t // tt,),
        in_specs=[
            pl.BlockSpec((tt, d), lambda i: (i, 0)),
            pl.BlockSpec((d, n_ext), const2),
            pl.BlockSpec((1, n_ext), const2),
            pl.BlockSpec((tt, LANES), lambda i: (i % nt, 0)),
            pl.BlockSpec((tt, LANES), lambda i: (i % nt, 0)),
        ],
        out_specs=[pl.BlockSpec((tt, nq), lambda i: (i, 0)),
                   pl.BlockSpec((tt, nk2), lambda i: (i, 0)),
                   pl.BlockSpec((tt, nk2), lambda i: (i, 0))],
        out_shape=[jax.ShapeDtypeStruct((t, nq), BF16),
                   jax.ShapeDtypeStruct((t, nk2), BF16),
                   jax.ShapeDtypeStruct((t, nk2), BF16)],
        compiler_params=_params("parallel"),
        name="swa_qkv",
    )(x2, w_ext, b_ext, cos_t, sin_t)

    tq = min(ATTN_TQ, seq)
    ntq = seq // tq
    main = lambda bi, i: (bi * ntq + i, 0)
    halo = lambda bi, i: (jnp.maximum((bi * seq + i * tq) // WINDOW - 1, 0), 0)
    const = lambda bi, i: (0, 0)
    row = lambda v: v.reshape(1, d)
    return pl.pallas_call(
        functools.partial(_attn_kernel, alpha=alpha, tq=tq),
        grid=(nb, ntq),
        in_specs=[
            pl.BlockSpec(memory_space=pltpu.SMEM),
            pl.BlockSpec((tq, nq), main),
            pl.BlockSpec((tq, nk2), main),
            pl.BlockSpec((WINDOW, nk2), halo),
            pl.BlockSpec((tq, nk2), main),
            pl.BlockSpec((WINDOW, nk2), halo),
            pl.BlockSpec((tq, d), main),
            pl.BlockSpec((nq, d), const),
            pl.BlockSpec((1, d), const),
            pl.BlockSpec((1, d), const),
            pl.BlockSpec((1, d), const),
        ],
        out_specs=pl.BlockSpec((tq, d), main),
        out_shape=jax.ShapeDtypeStruct((t, d), F32),
        compiler_params=_params("parallel", "parallel"),
        name="swa_attention",
    )(sinks, q, k2, k2, v2, v2, x2, wo.astype(BF16), row(bo), row(g), row(b))


def _conv_kernel(x_ref, win_ref, cw_ref, wout_ref, g_ref, b_ref, o_ref, ubuf_ref, *, alpha, tt):
    i = pl.program_id(1)
    d = x_ref.shape[1]

    @pl.when(i == 0)
    def _():
        ubuf_ref[0:SUBLANES, :] = jnp.zeros((SUBLANES, d), F32)

    x = x_ref[...]
    proj = jnp.dot(x.astype(BF16), win_ref[...], preferred_element_type=F32)
    gate_b = proj[:, :d]
    u = proj[:, d:2 * d] * proj[:, 2 * d:]
    ubuf_ref[SUBLANES:SUBLANES + tt, :] = u
    cw = cw_ref[...]
    conv = (cw[0:1, :] * ubuf_ref[SUBLANES - 2:SUBLANES - 2 + tt, :]
            + cw[1:2, :] * ubuf_ref[SUBLANES - 1:SUBLANES - 1 + tt, :]
            + cw[2:3, :] * u)
    ubuf_ref[0:SUBLANES, :] = u[tt - SUBLANES:tt, :]
    y = jnp.dot((gate_b * conv).astype(BF16), wout_ref[...], preferred_element_type=F32)
    o_ref[...] = _layer_norm(alpha * x + y, g_ref[...], b_ref[...])


def _conv_mixer(x2, seq, w_in, conv_w, w_out, g, b, alpha):
    t, d = x2.shape
    nb = t // seq
    tt = min(CONV_TT, seq)
    nt = seq // tt
    const = lambda bi, i: (0, 0)
    row = lambda v: v.reshape(1, d)
    return pl.pallas_call(
        functools.partial(_conv_kernel, alpha=alpha, tt=tt),
        grid=(nb, nt),
        in_specs=[
            pl.BlockSpec((tt, d), lambda bi, i: (bi * nt + i, 0)),
            pl.BlockSpec((d, 3 * d), const),
            pl.BlockSpec(conv_w.shape, const),
            pl.BlockSpec((d, d), const),
            pl.BlockSpec((1, d), const),
            pl.BlockSpec((1, d), const),
        ],
        out_specs=pl.BlockSpec((tt, d), lambda bi, i: (bi * nt + i, 0)),
        out_shape=jax.ShapeDtypeStruct((t, d), F32),
        scratch_shapes=[pltpu.VMEM((SUBLANES + tt, d), F32)],
        compiler_params=_params("arbitrary", "arbitrary"),
        name="conv_mixer",
    )(x2, w_in.astype(BF16), conv_w, w_out.astype(BF16), row(g), row(b))


def _first_argmax(v, rows, n):
    m = jnp.max(v, axis=0, keepdims=True)
    first = jnp.min(jnp.where(v == m, rows, n), axis=0, keepdims=True)
    return m, first


def _route_kernel(x_ref, wt_ref, bias_ref, idx_ref, gate_ref, rank_ref, cnt_ref, base_ref, *, tt):
    i = pl.program_id(0)

    @pl.when(i == 0)
    def _():
        base_ref[...] = jnp.zeros_like(base_ref)

    x = x_ref[...]
    xh = x.astype(BF16)
    xl = (x - xh.astype(F32)).astype(BF16)
    wt = wt_ref[...]
    wh = wt.astype(BF16)
    wl = (wt - wh.astype(F32)).astype(BF16)
    dn = (((1,), (1,)), ((), ()))
    logits = (lax.dot_general(wh, xh, dn, preferred_element_type=F32)
              + (lax.dot_general(wh, xl, dn, preferred_element_type=F32)
                 + lax.dot_general(wl, xh, dn, preferred_element_type=F32)))
    scores = jax.nn.sigmoid(logits)
    sel = scores + bias_ref[...]

    rows8 = lax.broadcasted_iota(jnp.int32, (GROUP_SIZE, tt), 0)
    gsc = []
    for gi in range(N_EXPERT_GROUPS):
        blk = sel[gi * GROUP_SIZE:(gi + 1) * GROUP_SIZE, :]
        m1, f1 = _first_argmax(blk, rows8, GROUP_SIZE)
        m2 = jnp.max(jnp.where(rows8 == f1, NEG_INF, blk), axis=0, keepdims=True)
        gsc.append(m1 + m2)
    cur = jnp.concatenate(gsc, axis=0)
    rows_g = lax.broadcasted_iota(jnp.int32, (N_EXPERT_GROUPS, tt), 0)
    gmask = jnp.zeros((N_EXPERT_GROUPS, tt), F32)
    for _ in range(TOPK_GROUPS):
        _, f = _first_argmax(cur, rows_g, N_EXPERT_GROUPS)
        hit = rows_g == f
        gmask = jnp.where(hit, 1.0, gmask)
        cur = jnp.where(hit, NEG_INF, cur)
    emask = jnp.concatenate(
        [jnp.broadcast_to(gmask[gi:gi + 1, :], (GROUP_SIZE, tt)) for gi in range(N_EXPERT_GROUPS)],
        axis=0)

    rows_e = lax.broadcasted_iota(jnp.int32, (N_EXPERTS, tt), 0)
    cur = jnp.where(emask > 0.0, sel, NEG_INF)
    idxs, gates = [], []
    member = jnp.zeros((N_EXPERTS, tt), F32)
    for _ in range(TOP_K):
        _, f = _first_argmax(cur, rows_e, N_EXPERTS)
        hit = rows_e == f
        idxs.append(f)
        gates.append(jnp.sum(jnp.where(hit, scores, 0.0), axis=0, keepdims=True))
        member = jnp.where(hit, 1.0, member)
        cur = jnp.where(hit, NEG_INF, cur)
    gate = jnp.concatenate(gates, axis=0)
    gate = gate / jnp.sum(gate, axis=0, keepdims=True) * ROUTED_SCALE

    ta = lax.broadcasted_iota(jnp.int32, (tt, tt), 0)
    tb = lax.broadcasted_iota(jnp.int32, (tt, tt), 1)
    earlier = (ta < tb).astype(BF16)
    before = jnp.dot(member.astype(BF16), earlier, preferred_element_type=F32)
    before = before + base_ref[:, 0:1]
    for k in range(TOP_K):
        rk = jnp.sum(jnp.where(rows_e == idxs[k], before, 0.0), axis=0, keepdims=True)
        rank_ref[k:k + 1, :] = rk.astype(jnp.int32)
        idx_ref[k:k + 1, :] = idxs[k]
    gate_ref[...] = gate
    new_base = base_ref[...] + jnp.sum(member, axis=1, keepdims=True)
    base_ref[...] = new_base
    cnt_ref[...] = new_base


def _route(x1, router_w, router_bias):
    t, d = x1.shape
    tt = min(ROUTE_TT, t)
    row_blk = lambda i: (0, i)
    return pl.pallas_call(
        functools.partial(_route_kernel, tt=tt),
        grid=(t // tt,),
        in_specs=[
            pl.BlockSpec((tt, d), lambda i: (i, 0)),
            pl.BlockSpec((N_EXPERTS, d), lambda i: (0, 0)),
            pl.BlockSpec((N_EXPERTS, 1), lambda i: (0, 0)),
        ],
        out_specs=[pl.BlockSpec((TOP_K, tt), row_blk),
                   pl.BlockSpec((TOP_K, tt), row_blk),
                   pl.BlockSpec((TOP_K, tt), row_blk),
                   pl.BlockSpec((N_EXPERTS, LANES), lambda i: (0, 0))],
        out_shape=[jax.ShapeDtypeStruct((TOP_K, t), jnp.int32),
                   jax.ShapeDtypeStruct((TOP_K, t), F32),
                   jax.ShapeDtypeStruct((TOP_K, t), jnp.int32),
                   jax.ShapeDtypeStruct((N_EXPERTS, LANES), F32)],
        scratch_shapes=[pltpu.VMEM((N_EXPERTS, LANES), F32)],
        compiler_params=_params("arbitrary"),
        name="moe_router",
    )(x1, router_w.T, router_bias.reshape(N_EXPERTS, 1))


def _scatter_kernel(zrow_ref, zflag_ref, dest_ref, x_ref, xs_ref, zbuf_ref, sem, zsem, *, td, bm):
    i = pl.program_id(0)

    def zero_copy(e):
        row = pl.multiple_of(zrow_ref[e], bm)
        return pltpu.make_async_copy(zbuf_ref, xs_ref.at[pl.ds(row, bm), :], zsem)

    @pl.when(i == 0)
    def _():
        zbuf_ref[...] = jnp.zeros_like(zbuf_ref)

        def start(e, carry):
            @pl.when(zflag_ref[e] > 0)
            def _():
                zero_copy(e).start()
            return carry

        def wait(e, carry):
            @pl.when(zflag_ref[e] > 0)
            def _():
                zero_copy(e).wait()
            return carry

        lax.fori_loop(0, N_EXPERTS, start, 0)
        lax.fori_loop(0, N_EXPERTS, wait, 0)

    def row_copy(t, dst):
        return pltpu.make_async_copy(x_ref.at[pl.ds(t, 1), :], xs_ref.at[pl.ds(dst, 1), :], sem)

    def start(t, carry):
        for k in range(TOP_K):
            row_copy(t, dest_ref[k * td + t]).start(priority=k % 2)
        return carry

    def wait(t, carry):
        for k in range(TOP_K):
            row_copy(t, dest_ref[k * td + t]).wait()
        return carry

    lax.fori_loop(0, td, start, 0, unroll=4)
    lax.fori_loop(0, td, wait, 0, unroll=4)


def _scatter_rows(x1, dest_flat, zrow, zflag, n_rows, td, bm):
    t, d = x1.shape
    grid_spec = pltpu.PrefetchScalarGridSpec(
        num_scalar_prefetch=2,
        grid=(t // td,),
        in_specs=[
            pl.BlockSpec((TOP_K * td,), lambda i, zr, zf: (i,), memory_space=pltpu.SMEM),
            pl.BlockSpec((td, d), lambda i, zr, zf: (i, 0)),
        ],
        out_specs=pl.BlockSpec(memory_space=pl.ANY),
        scratch_shapes=[pltpu.VMEM((bm, d), F32), pltpu.SemaphoreType.DMA, pltpu.SemaphoreType.DMA],
    )
    return pl.pallas_call(
        functools.partial(_scatter_kernel, td=td, bm=bm),
        grid_spec=grid_spec,
        out_shape=jax.ShapeDtypeStruct((n_rows, d), F32),
        compiler_params=_params("arbitrary"),
        name="moe_scatter",
    )(zrow, zflag, dest_flat, x1)


def _gmm_kernel(be_ref, nv_ref, xs_ref, w1_ref, w3_ref, w2_ref, y_ref, w13_ref, w2b_ref):
    i = pl.program_id(0)
    f = w1_ref.shape[1]
    changed = jnp.logical_or(i == 0, be_ref[i] != be_ref[jnp.maximum(i - 1, 0)])

    @pl.when(changed)
    def _():
        w13_ref[:, :f] = w1_ref[...].astype(BF16)
        w13_ref[:, f:] = w3_ref[...].astype(BF16)
        w2b_ref[...] = w2_ref[...].astype(BF16)

    @pl.when(i < nv_ref[0])
    def _():
        h13 = jnp.dot(xs_ref[...].astype(BF16), w13_ref[...], preferred_element_type=F32)
        h = jax.nn.silu(h13[:, :f]) * h13[:, f:]
        y_ref[...] = jnp.dot(h.astype(BF16), w2b_ref[...], preferred_element_type=F32)


def _gmm(xs, blk_expert, n_valid, w1, w3, w2, layer, bm):
    n_rows, d = xs.shape
    f = w1.shape[-1]
    nblk = n_rows // bm
    xmap = lambda i, be, nv: (jnp.minimum(i, nv[0] - 1), 0)
    wmap = lambda i, be, nv: (layer, be[i], 0, 0)
    grid_spec = pltpu.PrefetchScalarGridSpec(
        num_scalar_prefetch=2,
        grid=(nblk,),
        in_specs=[
            pl.BlockSpec((bm, d), xmap),
            pl.BlockSpec((None, None, d, f), wmap),
            pl.BlockSpec((None, None, d, f), wmap),
            pl.BlockSpec((None, None, f, d), wmap),
        ],
        out_specs=pl.BlockSpec((bm, d), xmap),
        scratch_shapes=[pltpu.VMEM((d, 2 * f), BF16), pltpu.VMEM((f, d), BF16)],
    )
    return pl.pallas_call(
        _gmm_kernel,
        grid_spec=grid_spec,
        out_shape=jax.ShapeDtypeStruct((n_rows, d), F32),
        compiler_params=_params("arbitrary"),
        name="moe_experts",
    )(blk_expert, n_valid, xs, w1, w3, w2)


def _combine_kernel(dest_ref, yb_ref, gate_ref, x_ref, ws13_ref, ws2_ref, g_ref, b_ref, o_ref,
                    ybuf_ref, sem, *, alpha, td):
    def row_copy(t, k, src):
        return pltpu.make_async_copy(yb_ref.at[pl.ds(src, 1), :], ybuf_ref.at[k, pl.ds(t, 1), :], sem)

    def start(t, carry):
        for k in range(TOP_K):
            row_copy(t, k, dest_ref[k * td + t]).start(priority=k % 2)
        return carry

    def wait(t, carry):
        for k in range(TOP_K):
            row_copy(t, k, dest_ref[k * td + t]).wait()
        return carry

    lax.fori_loop(0, td, start, 0, unroll=4)

    x = x_ref[...]
    f = ws2_ref.shape[0]
    h13 = jnp.dot(x.astype(BF16), ws13_ref[...], preferred_element_type=F32)
    hs = jax.nn.silu(h13[:, :f]) * h13[:, f:]
    shared = jnp.dot(hs.astype(BF16), ws2_ref[...], preferred_element_type=F32)

    lax.fori_loop(0, td, wait, 0, unroll=4)
    gate = gate_ref[...]
    routed = gate[:, 0:1] * ybuf_ref[0]
    for k in range(1, TOP_K):
        routed = routed + gate[:, k:k + 1] * ybuf_ref[k]
    o_ref[...] = _layer_norm(alpha * x + (routed + shared), g_ref[...], b_ref[...])


def _combine(yb, dest_flat, gate_tk, x1, ws1, ws3, ws2, g, b, alpha, td):
    t, d = x1.shape
    f = ws1.shape[1]
    ws13 = jnp.concatenate([ws1, ws3], axis=1).astype(BF16)
    const = lambda i: (0, 0)
    row = lambda v: v.reshape(1, d)
    return pl.pallas_call(
        functools.partial(_combine_kernel, alpha=alpha, td=td),
        grid=(t // td,),
        in_specs=[
            pl.BlockSpec((TOP_K * td,), lambda i: (i,), memory_space=pltpu.SMEM),
            pl.BlockSpec(memory_space=pl.ANY),
            pl.BlockSpec((td, TOP_K), lambda i: (i, 0)),
            pl.BlockSpec((td, d), lambda i: (i, 0)),
            pl.BlockSpec((d, 2 * f), const),
            pl.BlockSpec((f, d), const),
            pl.BlockSpec((1, d), const),
            pl.BlockSpec((1, d), const),
        ],
        out_specs=pl.BlockSpec((td, d), lambda i: (i, 0)),
        out_shape=jax.ShapeDtypeStruct((t, d), F32),
        scratch_shapes=[pltpu.VMEM((TOP_K, td, d), F32), pltpu.SemaphoreType.DMA],
        compiler_params=_params("arbitrary"),
        name="moe_combine",
    )(dest_flat, yb, gate_tk, x1, ws13, ws2.astype(BF16), row(g), row(b))


def _moe_layer(x1, layer, router_w, router_bias, w1, w3, w2, ws1, ws3, ws2, g, b, alpha):
    t, d = x1.shape
    td = min(DISPATCH_TD, t)
    bm = EXPERT_BM
    idx, gate, rank, counts = _route(x1, router_w, router_bias)

    cnt = counts[:, 0].astype(jnp.int32)
    padded = (cnt + bm - 1) // bm * bm
    pend = jnp.cumsum(padded)
    pstart = pend - padded
    nblk = (t * TOP_K) // bm + N_EXPERTS
    n_valid = pend[-1] // bm
    blk = jnp.arange(nblk, dtype=jnp.int32)
    blk_expert = jnp.minimum(jnp.sum((pend[None, :] <= (blk * bm)[:, None]).astype(jnp.int32), axis=1),
                             N_EXPERTS - 1)
    blk_expert = jnp.where(blk < n_valid, blk_expert, blk_expert[n_valid - 1]).astype(jnp.int32)
    experts = jnp.arange(N_EXPERTS, dtype=jnp.int32)[:, None, None]
    dest = rank + jnp.sum(jnp.where(idx[None] == experts, pstart[:, None, None], 0), axis=0)
    dest_flat = dest.reshape(TOP_K, t // td, td).transpose(1, 0, 2).reshape(-1)
    zrow = jnp.maximum(pend - bm, 0).astype(jnp.int32)
    zflag = (cnt > 0).astype(jnp.int32)

    xs = _scatter_rows(x1, dest_flat, zrow, zflag, nblk * bm, td, bm)
    yb = _gmm(xs, blk_expert, n_valid.reshape(1).astype(jnp.int32), w1, w3, w2, layer, bm)
    return _combine(yb, dest_flat, gate.T, x1, ws1, ws3, ws2, g, b, alpha, td)


def kernel(x, ln_gain, ln_bias, pool_w, pool_scale, attn_wqkv, attn_bqkv, attn_sinks, attn_wo, attn_bo,
           conv_w_in, conv_w, conv_w_out, router_w, router_bias, expert_w1, expert_w3, expert_w2,
           shared_w1, shared_w3, shared_w2):
    bsz, seq, d = x.shape
    depth = ln_gain.shape[0]
    alpha = float((2 * depth) ** 0.25)
    h = x.reshape(bsz * seq, d)
    for i in range(depth):
        kind = i % N_MIXERS
        j = i // N_MIXERS
        g1, b1 = ln_gain[i, 0], ln_bias[i, 0]
        if kind == 0:
            h = _pool_mixer(h, seq, pool_w[j], pool_scale[j], g1, b1, alpha)
        elif kind == 1:
            h = _swa_mixer(h, seq, attn_wqkv[j], attn_bqkv[j], attn_sinks[j], attn_wo[j], attn_bo[j],
                           g1, b1, alpha)
        else:
            h = _conv_mixer(h, seq, conv_w_in[j], conv_w[j], conv_w_out[j], g1, b1, alpha)
        h = _moe_layer(h, i, router_w[i], router_bias[i], expert_w1, expert_w3, expert_w2,
                       shared_w1[i], shared_w3[i], shared_w2[i], ln_gain[i, 1], ln_bias[i, 1], alpha)
    return h.reshape(bsz, seq, d)
```

```python
import functools

import jax
import jax.numpy as jnp
from jax import lax
from jax.experimental import pallas as pl
from jax.experimental.pallas import tpu as pltpu

POOL_WINDOWS = (2, 4, 8, 16)
N_MIXERS = 3
HEAD_DIM = 64
N_Q_HEADS = 16
N_KV_HEADS = 4
GQA_GROUP = N_Q_HEADS // N_KV_HEADS
WINDOW = 128
ROPE_THETA = 10000.0
N_EXPERTS = 64
TOP_K = 8
N_EXPERT_GROUPS = 8
GROUP_SIZE = N_EXPERTS // N_EXPERT_GROUPS
TOPK_GROUPS = 4
ROUTED_SCALE = 2.5
LN_EPS = 1e-5

LANES = 128
SUBLANES = 8
POOL_HALO = 16
VMEM_LIMIT = 48 * 1024 * 1024

POOL_TT = 512
QKV_TT = 512
ATTN_TQ = 256
CONV_TT = 256
GROUP_TG = 256
CHUNK_ALIGN = SUBLANES
SORT_CHUNK = 512
EXPERT_BM = 512
CHUNK_BITS = 6

U32 = jnp.uint32
HI_MASK = 0xFFFF0000

F32 = jnp.float32
BF16 = jnp.bfloat16
NEG_INF = float("-inf")


def _layer_norm(z, g, b):
    mu = jnp.mean(z, axis=-1, keepdims=True)
    zc = z - mu
    var = jnp.mean(zc * zc, axis=-1, keepdims=True)
    return zc * lax.rsqrt(var + LN_EPS) * g + b


def _params(*sem):
    return pltpu.CompilerParams(dimension_semantics=sem, vmem_limit_bytes=VMEM_LIMIT)


def _pool_kernel(x_ref, halo_ref, w_ref, scale_ref, g_ref, b_ref, o_ref, buf_ref, *, alpha, tt):
    i = pl.program_id(1)
    x = x_ref[...]
    ch = x.shape[1] // len(POOL_WINDOWS)
    buf_ref[0:POOL_HALO, :] = jnp.where(i > 0, halo_ref[...], 0.0)
    buf_ref[POOL_HALO:POOL_HALO + tt, :] = x
    pos = i * tt + lax.broadcasted_iota(jnp.int32, (tt, 1), 0)
    ys = []
    for g, w in enumerate(POOL_WINDOWS):
        c0 = g * ch
        xg = x[:, c0:c0 + ch]
        s = xg
        for j in range(1, w):
            s = s + buf_ref[POOL_HALO - j:POOL_HALO - j + tt, c0:c0 + ch]
        inv_cnt = 1.0 / jnp.minimum(pos + 1, w).astype(F32)
        pooled = s * inv_cnt - xg
        ys.append(jnp.dot(pooled.astype(BF16), w_ref[g], preferred_element_type=F32))
    y = jnp.concatenate(ys, axis=1) * scale_ref[...]
    o_ref[...] = _layer_norm(alpha * x + y, g_ref[...], b_ref[...])


def _pool_mixer(x2, seq, w_grp, scale, g, b, alpha):
    t, d = x2.shape
    tt = min(POOL_TT, seq)
    nt = seq // tt
    nb = t // seq
    row = lambda v: v.reshape(1, d)
    const2 = lambda bi, i: (0, 0)
    return pl.pallas_call(
        functools.partial(_pool_kernel, alpha=alpha, tt=tt),
        grid=(nb, nt),
        in_specs=[
            pl.BlockSpec((tt, d), lambda bi, i: (bi * nt + i, 0)),
            pl.BlockSpec((POOL_HALO, d),
                         lambda bi, i: (jnp.maximum((bi * seq + i * tt) // POOL_HALO - 1, 0), 0)),
            pl.BlockSpec(w_grp.shape, lambda bi, i: (0, 0, 0)),
            pl.BlockSpec((1, d), const2),
            pl.BlockSpec((1, d), const2),
            pl.BlockSpec((1, d), const2),
        ],
        out_specs=pl.BlockSpec((tt, d), lambda bi, i: (bi * nt + i, 0)),
        out_shape=jax.ShapeDtypeStruct((t, d), F32),
        scratch_shapes=[pltpu.VMEM((POOL_HALO + tt, d), F32)],
        compiler_params=_params("parallel", "parallel"),
        name="pool_mixer",
    )(x2, x2, w_grp.astype(BF16), row(scale), row(g), row(b))


def _qkv_kernel(x_ref, w_ref, b_ref, c_ref, s_ref, q_ref, k_ref, v_ref, *, nq, nk):
    y = jnp.dot(x_ref[...].astype(BF16), w_ref[...], preferred_element_type=F32) + b_ref[...]
    c = c_ref[...]
    s = s_ref[...]
    lane = lax.broadcasted_iota(jnp.int32, c.shape, 1)
    first_half = (lane % HEAD_DIM) < (HEAD_DIM // 2)

    def rope(v):
        partner = jnp.where(first_half, pltpu.roll(v, LANES - HEAD_DIM // 2, 1),
                            pltpu.roll(v, HEAD_DIM // 2, 1))
        return v * c + partner * s

    for j in range(nq // LANES):
        q_ref[:, j * LANES:(j + 1) * LANES] = (
            rope(y[:, j * LANES:(j + 1) * LANES]) * (HEAD_DIM ** -0.5)).astype(BF16)
    for j in range(nk // LANES):
        k_ref[:, j * LANES:(j + 1) * LANES] = rope(
            y[:, nq + j * LANES:nq + (j + 1) * LANES]).astype(BF16)
    v_ref[...] = y[:, nq + nk:].astype(BF16)


def _attn_kernel(sink_ref, q_ref, k_ref, kh_ref, v_ref, vh_ref, x_ref, wo_ref, bo_ref, g_ref, b_ref,
                 o_ref, *, alpha, tq):
    i = pl.program_id(1)
    tk = tq + WINDOW
    r = lax.broadcasted_iota(jnp.int32, (tq, tk), 0)
    c = lax.broadcasted_iota(jnp.int32, (tq, tk), 1)
    dist = r + WINDOW - c
    valid = (dist >= 0) & (dist < WINDOW) & ((c >= WINDOW) | (i > 0))
    lane = lax.broadcasted_iota(jnp.int32, (tq, LANES), 1)
    low = lane < HEAD_DIM
    dn = (((1,), (1,)), ((), ()))
    outs = []
    for kv in range(N_KV_HEADS):
        sl = slice(kv * LANES, (kv + 1) * LANES)
        kk = jnp.concatenate([kh_ref[:, sl], k_ref[:, sl]], axis=0)
        vv = jnp.concatenate([vh_ref[:, sl], v_ref[:, sl]], axis=0)
        for p in range(GQA_GROUP // 2):
            c0 = (kv * GQA_GROUP + 2 * p) * HEAD_DIM
            qp = q_ref[:, c0:c0 + LANES]
            res = []
            for half in range(2):
                keep = low if half == 0 else jnp.logical_not(low)
                qh = jnp.where(keep, qp, jnp.zeros_like(qp))
                sc = lax.dot_general(qh, kk, dn, preferred_element_type=F32)
                sc = jnp.where(valid, sc, NEG_INF)
                sink = sink_ref[kv * GQA_GROUP + 2 * p + half]
                m = jnp.maximum(jnp.max(sc, axis=1, keepdims=True), sink)
                pm = jnp.exp(sc - m)
                den = jnp.sum(pm, axis=1, keepdims=True) + jnp.exp(sink - m)
                res.append(jnp.dot(pm.astype(BF16), vv, preferred_element_type=F32) / den)
            outs.append(jnp.where(low, res[0], res[1]))
    o_all = jnp.concatenate(outs, axis=1).astype(BF16)
    mix = jnp.dot(o_all, wo_ref[...], preferred_element_type=F32) + bo_ref[...]
    o_ref[...] = _layer_norm(alpha * x_ref[...] + mix, g_ref[...], b_ref[...])


def _dup_heads(w, n_heads):
    lead = w.shape[:-1]
    w = w.reshape(lead + (n_heads, 1, HEAD_DIM))
    w = jnp.broadcast_to(w, lead + (n_heads, 2, HEAD_DIM))
    return w.reshape(lead + (n_heads * 2 * HEAD_DIM,))


def _swa_mixer(x2, seq, wqkv, bqkv, sinks, wo, bo, g, b, alpha):
    t, d = x2.shape
    nb = t // seq
    nq = N_Q_HEADS * HEAD_DIM
    nkv = N_KV_HEADS * HEAD_DIM
    nk2 = 2 * nkv
    w_ext = jnp.concatenate([wqkv[:, :nq], _dup_heads(wqkv[:, nq:nq + nkv], N_KV_HEADS),
                             _dup_heads(wqkv[:, nq + nkv:], N_KV_HEADS)], axis=1).astype(BF16)
    b_ext = jnp.concatenate([bqkv[:nq], _dup_heads(bqkv[nq:nq + nkv], N_KV_HEADS),
                             _dup_heads(bqkv[nq + nkv:], N_KV_HEADS)]).reshape(1, -1)
    n_ext = nq + 2 * nk2

    pos = jnp.arange(seq, dtype=F32)
    inv_freq = ROPE_THETA ** (-jnp.arange(0, HEAD_DIM, 2, dtype=F32) / HEAD_DIM)
    ang = pos[:, None] * inv_freq[None, :]
    cos, sin = jnp.cos(ang), jnp.sin(ang)
    cos_t = jnp.concatenate([cos, cos, cos, cos], axis=1)
    sin_t = jnp.concatenate([-sin, sin, -sin, sin], axis=1)

    tt = min(QKV_TT, seq)
    nt = seq // tt
    const2 = lambda i: (0, 0)
    q, k2, v2 = pl.pallas_call(
        functools.partial(_qkv_kernel, nq=nq, nk=nk2),
        grid=(t // tt,),
        in_specs=[
            pl.BlockSpec((tt, d), lambda i: (i, 0)),
            pl.BlockSpec((d, n_ext), const2),
            pl.BlockSpec((1, n_ext), const2),
            pl.BlockSpec((tt, LANES), lambda i: (i % nt, 0)),
            pl.BlockSpec((tt, LANES), lambda i: (i % nt, 0)),
        ],
        out_specs=[pl.BlockSpec((tt, nq), lambda i: (i, 0)),
                   pl.BlockSpec((tt, nk2), lambda i: (i, 0)),
                   pl.BlockSpec((tt, nk2), lambda i: (i, 0))],
        out_shape=[jax.ShapeDtypeStruct((t, nq), BF16),
                   jax.ShapeDtypeStruct((t, nk2), BF16),
                   jax.ShapeDtypeStruct((t, nk2), BF16)],
        compiler_params=_params("parallel"),
        name="swa_qkv",
    )(x2, w_ext, b_ext, cos_t, sin_t)

    tq = min(ATTN_TQ, seq)
    ntq = seq // tq
    main = lambda bi, i: (bi * ntq + i, 0)
    halo = lambda bi, i: (jnp.maximum((bi * seq + i * tq) // WINDOW - 1, 0), 0)
    const = lambda bi, i: (0, 0)
    row = lambda v: v.reshape(1, d)
    return pl.pallas_call(
        functools.partial(_attn_kernel, alpha=alpha, tq=tq),
        grid=(nb, ntq),
        in_specs=[
            pl.BlockSpec(memory_space=pltpu.SMEM),
            pl.BlockSpec((tq, nq), main),
            pl.BlockSpec((tq, nk2), main),
            pl.BlockSpec((WINDOW, nk2), halo),
            pl.BlockSpec((tq, nk2), main),
            pl.BlockSpec((WINDOW, nk2), halo),
            pl.BlockSpec((tq, d), main),
            pl.BlockSpec((nq, d), const),
            pl.BlockSpec((1, d), const),
            pl.BlockSpec((1, d), const),
            pl.BlockSpec((1, d), const),
        ],
        out_specs=pl.BlockSpec((tq, d), main),
        out_shape=jax.ShapeDtypeStruct((t, d), F32),
        compiler_params=_params("parallel", "parallel"),
        name="swa_attention",
    )(sinks, q, k2, k2, v2, v2, x2, wo.astype(BF16), row(bo), row(g), row(b))


def _conv_kernel(x_ref, win_ref, cw_ref, wout_ref, g_ref, b_ref, o_ref, ubuf_ref, *, alpha, tt):
    i = pl.program_id(1)
    d = x_ref.shape[1]

    @pl.when(i == 0)
    def _():
        ubuf_ref[0:SUBLANES, :] = jnp.zeros((SUBLANES, d), F32)

    x = x_ref[...]
    proj = jnp.dot(x.astype(BF16), win_ref[...], preferred_element_type=F32)
    gate_b = proj[:, :d]
    u = proj[:, d:2 * d] * proj[:, 2 * d:]
    ubuf_ref[SUBLANES:SUBLANES + tt, :] = u
    cw = cw_ref[...]
    conv = (cw[0:1, :] * ubuf_ref[SUBLANES - 2:SUBLANES - 2 + tt, :]
            + cw[1:2, :] * ubuf_ref[SUBLANES - 1:SUBLANES - 1 + tt, :]
            + cw[2:3, :] * u)
    ubuf_ref[0:SUBLANES, :] = u[tt - SUBLANES:tt, :]
    y = jnp.dot((gate_b * conv).astype(BF16), wout_ref[...], preferred_element_type=F32)
    o_ref[...] = _layer_norm(alpha * x + y, g_ref[...], b_ref[...])


def _conv_mixer(x2, seq, w_in, conv_w, w_out, g, b, alpha):
    t, d = x2.shape
    nb = t // seq
    tt = min(CONV_TT, seq)
    nt = seq // tt
    const = lambda bi, i: (0, 0)
    row = lambda v: v.reshape(1, d)
    return pl.pallas_call(
        functools.partial(_conv_kernel, alpha=alpha, tt=tt),
        grid=(nb, nt),
        in_specs=[
            pl.BlockSpec((tt, d), lambda bi, i: (bi * nt + i, 0)),
            pl.BlockSpec((d, 3 * d), const),
            pl.BlockSpec(conv_w.shape, const),
            pl.BlockSpec((d, d), const),
            pl.BlockSpec((1, d), const),
            pl.BlockSpec((1, d), const),
        ],
        out_specs=pl.BlockSpec((tt, d), lambda bi, i: (bi * nt + i, 0)),
        out_shape=jax.ShapeDtypeStruct((t, d), F32),
        scratch_shapes=[pltpu.VMEM((SUBLANES + tt, d), F32)],
        compiler_params=_params("arbitrary", "arbitrary"),
        name="conv_mixer",
    )(x2, w_in.astype(BF16), conv_w, w_out.astype(BF16), row(g), row(b))


def _first_argmax(v, rows, n):
    m = jnp.max(v, axis=0, keepdims=True)
    first = jnp.min(jnp.where(v == m, rows, n), axis=0, keepdims=True)
    return m, first


def _route_kernel(x_ref, wt_ref, bias_ref, pos_ref, gate_ref, n8_ref, *, tt):
    x = x_ref[...]
    xh = x.astype(BF16)
    xl = (x - xh.astype(F32)).astype(BF16)
    wt = wt_ref[...]
    wh = wt.astype(BF16)
    wl = (wt - wh.astype(F32)).astype(BF16)
    dn = (((1,), (1,)), ((), ()))
    logits = (lax.dot_general(wh, xh, dn, preferred_element_type=F32)
              + (lax.dot_general(wh, xl, dn, preferred_element_type=F32)
                 + lax.dot_general(wl, xh, dn, preferred_element_type=F32)))
    scores = jax.nn.sigmoid(logits)
    sel = scores + bias_ref[...]

    rows8 = lax.broadcasted_iota(jnp.int32, (GROUP_SIZE, tt), 0)
    gsc = []
    for gi in range(N_EXPERT_GROUPS):
        blk = sel[gi * GROUP_SIZE:(gi + 1) * GROUP_SIZE, :]
        m1, f1 = _first_argmax(blk, rows8, GROUP_SIZE)
        m2 = jnp.max(jnp.where(rows8 == f1, NEG_INF, blk), axis=0, keepdims=True)
        gsc.append(m1 + m2)
    cur = jnp.concatenate(gsc, axis=0)
    rows_g = lax.broadcasted_iota(jnp.int32, (N_EXPERT_GROUPS, tt), 0)
    gmask = jnp.zeros((N_EXPERT_GROUPS, tt), F32)
    for _ in range(TOPK_GROUPS):
        _, f = _first_argmax(cur, rows_g, N_EXPERT_GROUPS)
        hit = rows_g == f
        gmask = jnp.where(hit, 1.0, gmask)
        cur = jnp.where(hit, NEG_INF, cur)
    emask = jnp.concatenate(
        [jnp.broadcast_to(gmask[gi:gi + 1, :], (GROUP_SIZE, tt)) for gi in range(N_EXPERT_GROUPS)],
        axis=0)

    rows_e = lax.broadcasted_iota(jnp.int32, (N_EXPERTS, tt), 0)
    cur = jnp.where(emask > 0.0, sel, NEG_INF)
    idxs, gates = [], []
    member = jnp.zeros((N_EXPERTS, tt), F32)
    for _ in range(TOP_K):
        _, f = _first_argmax(cur, rows_e, N_EXPERTS)
        hit = rows_e == f
        idxs.append(f)
        gates.append(jnp.sum(jnp.where(hit, scores, 0.0), axis=0, keepdims=True))
        member = jnp.where(hit, 1.0, member)
        cur = jnp.where(hit, NEG_INF, cur)
    gate = jnp.concatenate(gates, axis=0)
    gate = gate / jnp.sum(gate, axis=0, keepdims=True) * ROUTED_SCALE

    ta = lax.broadcasted_iota(jnp.int32, (tt, tt), 0)
    tb = lax.broadcasted_iota(jnp.int32, (tt, tt), 1)
    earlier = (ta < tb).astype(BF16)
    before = jnp.dot(member.astype(BF16), earlier, preferred_element_type=F32)
    n = jnp.sum(member, axis=1, keepdims=True)
    n8 = jnp.floor((n + (CHUNK_ALIGN - 1)) * (1.0 / CHUNK_ALIGN)) * CHUNK_ALIGN
    n8_wide = jnp.broadcast_to(n8, (N_EXPERTS, LANES))
    ea = lax.broadcasted_iota(jnp.int32, (N_EXPERTS, N_EXPERTS), 0)
    eb = lax.broadcasted_iota(jnp.int32, (N_EXPERTS, N_EXPERTS), 1)
    lower = (eb < ea).astype(BF16)
    chunk_start = jnp.dot(lower, n8_wide.astype(BF16), preferred_element_type=F32)
    where_to = before + chunk_start[:, 0:1]
    for k in range(TOP_K):
        pk = jnp.sum(jnp.where(rows_e == idxs[k], where_to, 0.0), axis=0, keepdims=True)
        pos_ref[k:k + 1, :] = pk.astype(jnp.int32)
    gate_ref[...] = gate
    n8_ref[...] = n8_wide.astype(jnp.int32)


def _route(x1, router_w, router_bias, tg):
    t, d = x1.shape
    row_blk = lambda i: (0, i)
    return pl.pallas_call(
        functools.partial(_route_kernel, tt=tg),
        grid=(t // tg,),
        in_specs=[
            pl.BlockSpec((tg, d), lambda i: (i, 0)),
            pl.BlockSpec((N_EXPERTS, d), lambda i: (0, 0)),
            pl.BlockSpec((N_EXPERTS, 1), lambda i: (0, 0)),
        ],
        out_specs=[pl.BlockSpec((TOP_K, tg), row_blk),
                   pl.BlockSpec((TOP_K, tg), row_blk),
                   pl.BlockSpec((N_EXPERTS, LANES), lambda i: (i, 0))],
        out_shape=[jax.ShapeDtypeStruct((TOP_K, t), jnp.int32),
                   jax.ShapeDtypeStruct((TOP_K, t), F32),
                   jax.ShapeDtypeStruct((t // tg * N_EXPERTS, LANES), jnp.int32)],
        compiler_params=_params("parallel"),
        name="moe_router",
    )(x1, router_w.T, router_bias.reshape(N_EXPERTS, 1))


def _pack_halves(v):
    h = v.shape[1] // 2
    lo = lax.bitcast_convert_type(v[:, :h].astype(BF16).astype(F32), U32)
    hi = lax.bitcast_convert_type(v[:, h:].astype(BF16).astype(F32), U32)
    return (lo >> 16) | (hi & jnp.uint32(HI_MASK))


def _unpack_halves(w):
    lo = lax.bitcast_convert_type(w << 16, F32).astype(BF16)
    hi = lax.bitcast_convert_type(w & jnp.uint32(HI_MASK), F32).astype(BF16)
    return lo, hi


def _for_each_chunk_copy(n8_ref, goff_ref, gdest_ref, group, make_copy, fn):
    def body(e, carry):
        n8 = n8_ref[group * N_EXPERTS + e]
        src = goff_ref[group * N_EXPERTS + e]
        dst = gdest_ref[group * N_EXPERTS + e]
        for j in range(CHUNK_BITS):
            size = CHUNK_ALIGN << j

            @pl.when((n8 & size) != 0)
            def _():
                done = n8 & (size - 1)
                fn(make_copy(pl.multiple_of(src + done, CHUNK_ALIGN),
                             pl.multiple_of(dst + done, CHUNK_ALIGN), size))
        return carry

    lax.fori_loop(0, N_EXPERTS, body, 0)


def _dispatch_kernel(n8_ref, goff_ref, gdest_ref, zrow_ref, zflag_ref, pos_ref, x_ref, xs_ref,
                     sbuf_ref, zbuf_ref, sem, zsem, *, tg, ns, bm):
    g = pl.program_id(0)
    ng = pl.num_programs(0)
    slot = g % 2

    def zero_copy(e):
        row = pl.multiple_of(zrow_ref[e], bm)
        return pltpu.make_async_copy(zbuf_ref, xs_ref.at[pl.ds(row, bm), :], zsem)

    @pl.when(g == 0)
    def _():
        zbuf_ref[...] = jnp.zeros_like(zbuf_ref)

        def start(e, carry):
            @pl.when(zflag_ref[e] > 0)
            def _():
                zero_copy(e).start()
            return carry

        def wait(e, carry):
            @pl.when(zflag_ref[e] > 0)
            def _():
                zero_copy(e).wait()
            return carry

        lax.fori_loop(0, N_EXPERTS, start, 0)
        lax.fori_loop(0, N_EXPERTS, wait, 0)

    def copies(group, sl, fn):
        def make_copy(src, dst, size):
            return pltpu.make_async_copy(sbuf_ref.at[sl, pl.ds(src, size), :],
                                         xs_ref.at[pl.ds(dst, size), :], sem.at[sl])
        _for_each_chunk_copy(n8_ref, goff_ref, gdest_ref, group, make_copy, fn)

    @pl.when(g >= 2)
    def _():
        copies(g - 2, slot, lambda c: c.wait())

    xb = x_ref[...].astype(BF16)
    pos = pos_ref[...]
    for c in range(ns // SORT_CHUNK):
        rows = c * SORT_CHUNK + lax.broadcasted_iota(jnp.int32, (SORT_CHUNK, tg), 0)
        hit = rows == pos[0:1, :]
        for k in range(1, TOP_K):
            hit = jnp.logical_or(hit, rows == pos[k:k + 1, :])
        onehot = jnp.where(hit, 1.0, 0.0).astype(BF16)
        sorted_rows = jnp.dot(onehot, xb, preferred_element_type=F32)
        sbuf_ref[slot, c * SORT_CHUNK:(c + 1) * SORT_CHUNK, :] = _pack_halves(sorted_rows)

    copies(g, slot, lambda c: c.start())

    @pl.when(g == ng - 1)
    def _():
        @pl.when(g >= 1)
        def _():
            copies(g - 1, 1 - slot, lambda c: c.wait())
        copies(g, slot, lambda c: c.wait())


def _dispatch(x1, pos, n8, goff, gdest, zrow, zflag, n_rows, tg, ns, bm):
    t, d = x1.shape
    grid_spec = pltpu.PrefetchScalarGridSpec(
        num_scalar_prefetch=5,
        grid=(t // tg,),
        in_specs=[
            pl.BlockSpec((TOP_K, tg), lambda i, *_: (0, i)),
            pl.BlockSpec((tg, d), lambda i, *_: (i, 0)),
        ],
        out_specs=pl.BlockSpec(memory_space=pl.ANY),
        scratch_shapes=[pltpu.VMEM((2, ns, d // 2), U32), pltpu.VMEM((bm, d // 2), U32),
                        pltpu.SemaphoreType.DMA((2,)), pltpu.SemaphoreType.DMA],
    )
    return pl.pallas_call(
        functools.partial(_dispatch_kernel, tg=tg, ns=ns, bm=bm),
        grid_spec=grid_spec,
        out_shape=jax.ShapeDtypeStruct((n_rows, d // 2), U32),
        compiler_params=_params("arbitrary"),
        name="moe_dispatch",
    )(n8, goff, gdest, zrow, zflag, pos, x1)


def _gmm_kernel(be_ref, nv_ref, xs_ref, w1_ref, w3_ref, w2_ref, y_ref, w13_ref, w2b_ref):
    i = pl.program_id(0)
    f = w1_ref.shape[1]
    changed = jnp.logical_or(i == 0, be_ref[i] != be_ref[jnp.maximum(i - 1, 0)])

    @pl.when(changed)
    def _():
        w13_ref[:, :f] = w1_ref[...].astype(BF16)
        w13_ref[:, f:] = w3_ref[...].astype(BF16)
        w2b_ref[...] = w2_ref[...].astype(BF16)

    @pl.when(i < nv_ref[0])
    def _():
        half = xs_ref.shape[1]
        x_lo, x_hi = _unpack_halves(xs_ref[...])
        h13 = (jnp.dot(x_lo, w13_ref[:half, :], preferred_element_type=F32)
               + jnp.dot(x_hi, w13_ref[half:, :], preferred_element_type=F32))
        h = jax.nn.silu(h13[:, :f]) * h13[:, f:]
        y_ref[...] = _pack_halves(jnp.dot(h.astype(BF16), w2b_ref[...], preferred_element_type=F32))


def _gmm(xs, blk_expert, n_valid, w1, w3, w2, layer, bm):
    n_rows, half = xs.shape
    d = 2 * half
    f = w1.shape[-1]
    nblk = n_rows // bm
    xmap = lambda i, be, nv: (jnp.minimum(i, nv[0] - 1), 0)
    wmap = lambda i, be, nv: (layer, be[i], 0, 0)
    grid_spec = pltpu.PrefetchScalarGridSpec(
        num_scalar_prefetch=2,
        grid=(nblk,),
        in_specs=[
            pl.BlockSpec((bm, half), xmap),
            pl.BlockSpec((None, None, d, f), wmap),
            pl.BlockSpec((None, None, d, f), wmap),
            pl.BlockSpec((None, None, f, d), wmap),
        ],
        out_specs=pl.BlockSpec((bm, half), xmap),
        scratch_shapes=[pltpu.VMEM((d, 2 * f), BF16), pltpu.VMEM((f, d), BF16)],
    )
    return pl.pallas_call(
        _gmm_kernel,
        grid_spec=grid_spec,
        out_shape=jax.ShapeDtypeStruct((n_rows, half), U32),
        compiler_params=_params("arbitrary"),
        name="moe_experts",
    )(blk_expert, n_valid, xs, w1, w3, w2)


def _combine_kernel(n8_ref, goff_ref, gdest_ref, yb_ref, pos_ref, gate_ref, x_ref, ws13_ref, ws2_ref,
                    g_ref, b_ref, o_ref, ybuf_ref, sem, *, alpha, tg, ns):
    g = pl.program_id(0)
    ng = pl.num_programs(0)
    slot = g % 2

    def copies(group, sl, fn):
        def make_copy(buf_row, hbm_row, size):
            return pltpu.make_async_copy(yb_ref.at[pl.ds(hbm_row, size), :],
                                         ybuf_ref.at[sl, pl.ds(buf_row, size), :], sem.at[sl])
        _for_each_chunk_copy(n8_ref, goff_ref, gdest_ref, group, make_copy, fn)

    @pl.when(g == 0)
    def _():
        ybuf_ref[...] = jnp.zeros_like(ybuf_ref)
        copies(0, 0, lambda c: c.start())

    @pl.when(g + 1 < ng)
    def _():
        copies(g + 1, 1 - slot, lambda c: c.start())

    x = x_ref[...]
    f = ws2_ref.shape[0]
    h13 = jnp.dot(x.astype(BF16), ws13_ref[...], preferred_element_type=F32)
    hs = jax.nn.silu(h13[:, :f]) * h13[:, f:]
    shared = jnp.dot(hs.astype(BF16), ws2_ref[...], preferred_element_type=F32)

    copies(g, slot, lambda c: c.wait())

    pos = pos_ref[...]
    gate = gate_ref[...]
    half = ybuf_ref.shape[2]
    acc_lo = jnp.zeros((tg, half), F32)
    acc_hi = jnp.zeros((tg, half), F32)
    for c in range(ns // SORT_CHUNK):
        cols = c * SORT_CHUNK + lax.broadcasted_iota(jnp.int32, (tg, SORT_CHUNK), 1)
        wgt = jnp.zeros((tg, SORT_CHUNK), F32)
        for k in range(TOP_K):
            wgt = jnp.where(cols == pos[:, k:k + 1], gate[:, k:k + 1], wgt)
        wgt = wgt.astype(BF16)
        y_lo, y_hi = _unpack_halves(ybuf_ref[slot, c * SORT_CHUNK:(c + 1) * SORT_CHUNK, :])
        acc_lo = acc_lo + jnp.dot(wgt, y_lo, preferred_element_type=F32)
        acc_hi = acc_hi + jnp.dot(wgt, y_hi, preferred_element_type=F32)
    routed = jnp.concatenate([acc_lo, acc_hi], axis=1)
    o_ref[...] = _layer_norm(alpha * x + (routed + shared), g_ref[...], b_ref[...])


def _combine(yb, pos_tk, gate_tk, n8, goff, gdest, x1, ws1, ws3, ws2, g, b, alpha, tg, ns):
    t, d = x1.shape
    f = ws1.shape[1]
    ws13 = jnp.concatenate([ws1, ws3], axis=1).astype(BF16)
    const = lambda i, *_: (0, 0)
    tile = lambda i, *_: (i, 0)
    row = lambda v: v.reshape(1, d)
    grid_spec = pltpu.PrefetchScalarGridSpec(
        num_scalar_prefetch=3,
        grid=(t // tg,),
        in_specs=[
            pl.BlockSpec(memory_space=pl.ANY),
            pl.BlockSpec((tg, TOP_K), tile),
            pl.BlockSpec((tg, TOP_K), tile),
            pl.BlockSpec((tg, d), tile),
            pl.BlockSpec((d, 2 * f), const),
            pl.BlockSpec((f, d), const),
            pl.BlockSpec((1, d), const),
            pl.BlockSpec((1, d), const),
        ],
        out_specs=pl.BlockSpec((tg, d), tile),
        scratch_shapes=[pltpu.VMEM((2, ns, d // 2), U32), pltpu.SemaphoreType.DMA((2,))],
    )
    return pl.pallas_call(
        functools.partial(_combine_kernel, alpha=alpha, tg=tg, ns=ns),
        grid_spec=grid_spec,
        out_shape=jax.ShapeDtypeStruct((t, d), F32),
        compiler_params=_params("arbitrary"),
        name="moe_combine",
    )(n8, goff, gdest, yb, pos_tk, gate_tk, x1, ws13, ws2.astype(BF16), row(g), row(b))


def _moe_layer(x1, layer, router_w, router_bias, w1, w3, w2, ws1, ws3, ws2, g, b, alpha):
    t, d = x1.shape
    tg = min(GROUP_TG, t)
    ng = t // tg
    bm = EXPERT_BM
    ns = -(-(tg * TOP_K + N_EXPERTS * (CHUNK_ALIGN - 1)) // SORT_CHUNK) * SORT_CHUNK
    pos, gate, n8_wide = _route(x1, router_w, router_bias, tg)

    n8 = n8_wide[:, 0].reshape(ng, N_EXPERTS)
    goff = jnp.cumsum(n8, axis=1) - n8
    seg = jnp.sum(n8, axis=0)
    padded = (seg + bm - 1) // bm * bm
    pend = jnp.cumsum(padded)
    pstart = pend - padded
    gdest = pstart[None, :] + jnp.cumsum(n8, axis=0) - n8
    nblk = (ng * (tg * TOP_K + N_EXPERTS * (CHUNK_ALIGN - 1))) // bm + N_EXPERTS
    n_valid = pend[-1] // bm
    blk = jnp.arange(nblk, dtype=jnp.int32)
    blk_expert = jnp.minimum(jnp.sum((pend[None, :] <= (blk * bm)[:, None]).astype(jnp.int32), axis=1),
                             N_EXPERTS - 1)
    blk_expert = jnp.where(blk < n_valid, blk_expert, blk_expert[n_valid - 1]).astype(jnp.int32)
    zrow = jnp.maximum(pend - bm, 0).astype(jnp.int32)
    zflag = (seg > 0).astype(jnp.int32)
    n8f, gofff, gdestf = (a.reshape(-1).astype(jnp.int32) for a in (n8, goff, gdest))

    xs = _dispatch(x1, pos, n8f, gofff, gdestf, zrow, zflag, nblk * bm, tg, ns, bm)
    yb = _gmm(xs, blk_expert, n_valid.reshape(1).astype(jnp.int32), w1, w3, w2, layer, bm)
    return _combine(yb, pos.T, gate.T, n8f, gofff, gdestf, x1, ws1, ws3, ws2, g, b, alpha, tg, ns)


def kernel(x, ln_gain, ln_bias, pool_w, pool_scale, attn_wqkv, attn_bqkv, attn_sinks, attn_wo, attn_bo,
           conv_w_in, conv_w, conv_w_out, router_w, router_bias, expert_w1, expert_w3, expert_w2,
           shared_w1, shared_w3, shared_w2):
    bsz, seq, d = x.shape
    depth = ln_gain.shape[0]
    alpha = float((2 * depth) ** 0.25)
    h = x.reshape(bsz * seq, d)
    for i in range(depth):
        kind = i % N_MIXERS
        j = i // N_MIXERS
        g1, b1 = ln_gain[i, 0], ln_bias[i, 0]
        if kind == 0:
            h = _pool_mixer(h, seq, pool_w[j], pool_scale[j], g1, b1, alpha)
        elif kind == 1:
            h = _swa_mixer(h, seq, attn_wqkv[j], attn_bqkv[j], attn_sinks[j], attn_wo[j], attn_bo[j],
                           g1, b1, alpha)
        else:
            h = _conv_mixer(h, seq, conv_w_in[j], conv_w[j], conv_w_out[j], g1, b1, alpha)
        h = _moe_layer(h, i, router_w[i], router_bias[i], expert_w1, expert_w3, expert_w2,
                       shared_w1[i], shared_w3[i], shared_w2[i], ln_gain[i, 1], ln_bias[i, 1], alpha)
    return h.reshape(bsz, seq, d)
```

```python
import functools

import jax
import jax.numpy as jnp
from jax import lax
from jax.experimental import pallas as pl
from jax.experimental.pallas import tpu as pltpu

POOL_WINDOWS = (2, 4, 8, 16)
N_MIXERS = 3
HEAD_DIM = 64
N_Q_HEADS = 16
N_KV_HEADS = 4
GQA_GROUP = N_Q_HEADS // N_KV_HEADS
WINDOW = 128
ROPE_THETA = 10000.0
N_EXPERTS = 64
TOP_K = 8
N_EXPERT_GROUPS = 8
GROUP_SIZE = N_EXPERTS // N_EXPERT_GROUPS
TOPK_GROUPS = 4
ROUTED_SCALE = 2.5
LN_EPS = 1e-5

LANES = 128
SUBLANES = 8
POOL_HALO = 16
VMEM_LIMIT = 48 * 1024 * 1024

POOL_TT = 512
QKV_TT = 512
ATTN_TQ = 256
CONV_TT = 256
GROUP_TG = 256
CHUNK_ALIGN = SUBLANES
SORT_CHUNK = 512
EXPERT_BM = 512
CHUNK_BITS = 6
PLAN_ROWS = SUBLANES
PLAN_WORDS = PLAN_ROWS * LANES

U32 = jnp.uint32
HI_MASK = 0xFFFF0000

F32 = jnp.float32
BF16 = jnp.bfloat16
NEG_INF = float("-inf")


def _layer_norm(z, g, b):
    mu = jnp.mean(z, axis=-1, keepdims=True)
    zc = z - mu
    var = jnp.mean(zc * zc, axis=-1, keepdims=True)
    return zc * lax.rsqrt(var + LN_EPS) * g + b


def _params(*sem):
    return pltpu.CompilerParams(dimension_semantics=sem, vmem_limit_bytes=VMEM_LIMIT)


def _pool_kernel(x_ref, halo_ref, w_ref, scale_ref, g_ref, b_ref, o_ref, buf_ref, *, alpha, tt):
    i = pl.program_id(1)
    x = x_ref[...]
    ch = x.shape[1] // len(POOL_WINDOWS)
    buf_ref[0:POOL_HALO, :] = jnp.where(i > 0, halo_ref[...], 0.0)
    buf_ref[POOL_HALO:POOL_HALO + tt, :] = x
    pos = i * tt + lax.broadcasted_iota(jnp.int32, (tt, 1), 0)
    ys = []
    for g, w in enumerate(POOL_WINDOWS):
        c0 = g * ch
        xg = x[:, c0:c0 + ch]
        s = xg
        for j in range(1, w):
            s = s + buf_ref[POOL_HALO - j:POOL_HALO - j + tt, c0:c0 + ch]
        inv_cnt = 1.0 / jnp.minimum(pos + 1, w).astype(F32)
        pooled = s * inv_cnt - xg
        ys.append(jnp.dot(pooled.astype(BF16), w_ref[g], preferred_element_type=F32))
    y = jnp.concatenate(ys, axis=1) * scale_ref[...]
    o_ref[...] = _layer_norm(alpha * x + y, g_ref[...], b_ref[...])


def _pool_mixer(x2, seq, w_grp, scale, g, b, alpha):
    t, d = x2.shape
    tt = min(POOL_TT, seq)
    nt = seq // tt
    nb = t // seq
    row = lambda v: v.reshape(1, d)
    const2 = lambda bi, i: (0, 0)
    return pl.pallas_call(
        functools.partial(_pool_kernel, alpha=alpha, tt=tt),
        grid=(nb, nt),
        in_specs=[
            pl.BlockSpec((tt, d), lambda bi, i: (bi * nt + i, 0)),
            pl.BlockSpec((POOL_HALO, d),
                         lambda bi, i: (jnp.maximum((bi * seq + i * tt) // POOL_HALO - 1, 0), 0)),
            pl.BlockSpec(w_grp.shape, lambda bi, i: (0, 0, 0)),
            pl.BlockSpec((1, d), const2),
            pl.BlockSpec((1, d), const2),
            pl.BlockSpec((1, d), const2),
        ],
        out_specs=pl.BlockSpec((tt, d), lambda bi, i: (bi * nt + i, 0)),
        out_shape=jax.ShapeDtypeStruct((t, d), F32),
        scratch_shapes=[pltpu.VMEM((POOL_HALO + tt, d), F32)],
        compiler_params=_params("parallel", "parallel"),
        name="pool_mixer",
    )(x2, x2, w_grp.astype(BF16), row(scale), row(g), row(b))


def _qkv_kernel(x_ref, w_ref, b_ref, c_ref, s_ref, q_ref, k_ref, v_ref, *, nq, nk):
    y = jnp.dot(x_ref[...].astype(BF16), w_ref[...], preferred_element_type=F32) + b_ref[...]
    c = c_ref[...]
    s = s_ref[...]
    lane = lax.broadcasted_iota(jnp.int32, c.shape, 1)
    first_half = (lane % HEAD_DIM) < (HEAD_DIM // 2)

    def rope(v):
        partner = jnp.where(first_half, pltpu.roll(v, LANES - HEAD_DIM // 2, 1),
                            pltpu.roll(v, HEAD_DIM // 2, 1))
        return v * c + partner * s

    for j in range(nq // LANES):
        q_ref[:, j * LANES:(j + 1) * LANES] = (
            rope(y[:, j * LANES:(j + 1) * LANES]) * (HEAD_DIM ** -0.5)).astype(BF16)
    for j in range(nk // LANES):
        k_ref[:, j * LANES:(j + 1) * LANES] = rope(
            y[:, nq + j * LANES:nq + (j + 1) * LANES]).astype(BF16)
    v_ref[...] = y[:, nq + nk:].astype(BF16)


def _attn_kernel(sink_ref, q_ref, k_ref, kh_ref, v_ref, vh_ref, x_ref, wo_ref, bo_ref, g_ref, b_ref,
                 o_ref, *, alpha, tq):
    i = pl.program_id(1)
    tk = tq + WINDOW
    r = lax.broadcasted_iota(jnp.int32, (tq, tk), 0)
    c = lax.broadcasted_iota(jnp.int32, (tq, tk), 1)
    dist = r + WINDOW - c
    valid = (dist >= 0) & (dist < WINDOW) & ((c >= WINDOW) | (i > 0))
    lane = lax.broadcasted_iota(jnp.int32, (tq, LANES), 1)
    low = lane < HEAD_DIM
    dn = (((1,), (1,)), ((), ()))
    outs = []
    for kv in range(N_KV_HEADS):
        sl = slice(kv * LANES, (kv + 1) * LANES)
        kk = jnp.concatenate([kh_ref[:, sl], k_ref[:, sl]], axis=0)
        vv = jnp.concatenate([vh_ref[:, sl], v_ref[:, sl]], axis=0)
        for p in range(GQA_GROUP // 2):
            c0 = (kv * GQA_GROUP + 2 * p) * HEAD_DIM
            qp = q_ref[:, c0:c0 + LANES]
            res = []
            for half in range(2):
                keep = low if half == 0 else jnp.logical_not(low)
                qh = jnp.where(keep, qp, jnp.zeros_like(qp))
                sc = lax.dot_general(qh, kk, dn, preferred_element_type=F32)
                sc = jnp.where(valid, sc, NEG_INF)
                sink = sink_ref[kv * GQA_GROUP + 2 * p + half]
                m = jnp.maximum(jnp.max(sc, axis=1, keepdims=True), sink)
                pm = jnp.exp(sc - m)
                den = jnp.sum(pm, axis=1, keepdims=True) + jnp.exp(sink - m)
                res.append(jnp.dot(pm.astype(BF16), vv, preferred_element_type=F32) / den)
            outs.append(jnp.where(low, res[0], res[1]))
    o_all = jnp.concatenate(outs, axis=1).astype(BF16)
    mix = jnp.dot(o_all, wo_ref[...], preferred_element_type=F32) + bo_ref[...]
    o_ref[...] = _layer_norm(alpha * x_ref[...] + mix, g_ref[...], b_ref[...])


def _dup_heads(w, n_heads):
    lead = w.shape[:-1]
    w = w.reshape(lead + (n_heads, 1, HEAD_DIM))
    w = jnp.broadcast_to(w, lead + (n_heads, 2, HEAD_DIM))
    return w.reshape(lead + (n_heads * 2 * HEAD_DIM,))


def _swa_mixer(x2, seq, wqkv, bqkv, sinks, wo, bo, g, b, alpha):
    t, d = x2.shape
    nb = t // seq
    nq = N_Q_HEADS * HEAD_DIM
    nkv = N_KV_HEADS * HEAD_DIM
    nk2 = 2 * nkv
    w_ext = jnp.concatenate([wqkv[:, :nq], _dup_heads(wqkv[:, nq:nq + nkv], N_KV_HEADS),
                             _dup_heads(wqkv[:, nq + nkv:], N_KV_HEADS)], axis=1).astype(BF16)
    b_ext = jnp.concatenate([bqkv[:nq], _dup_heads(bqkv[nq:nq + nkv], N_KV_HEADS),
                             _dup_heads(bqkv[nq + nkv:], N_KV_HEADS)]).reshape(1, -1)
    n_ext = nq + 2 * nk2

    pos = jnp.arange(seq, dtype=F32)
    inv_freq = ROPE_THETA ** (-jnp.arange(0, HEAD_DIM, 2, dtype=F32) / HEAD_DIM)
    ang = pos[:, None] * inv_freq[None, :]
    cos, sin = jnp.cos(ang), jnp.sin(ang)
    cos_t = jnp.concatenate([cos, cos, cos, cos], axis=1)
    sin_t = jnp.concatenate([-sin, sin, -sin, sin], axis=1)

    tt = min(QKV_TT, seq)
    nt = seq // tt
    const2 = lambda i: (0, 0)
    q, k2, v2 = pl.pallas_call(
        functools.partial(_qkv_kernel, nq=nq, nk=nk2),
        grid=(t // tt,),
        in_specs=[
            pl.BlockSpec((tt, d), lambda i: (i, 0)),
            pl.BlockSpec((d, n_ext), const2),
            pl.BlockSpec((1, n_ext), const2),
            pl.BlockSpec((tt, LANES), lambda i: (i % nt, 0)),
            pl.BlockSpec((tt, LANES), lambda i: (i % nt, 0)),
        ],
        out_specs=[pl.BlockSpec((tt, nq), lambda i: (i, 0)),
                   pl.BlockSpec((tt, nk2), lambda i: (i, 0)),
                   pl.BlockSpec((tt, nk2), lambda i: (i, 0))],
        out_shape=[jax.ShapeDtypeStruct((t, nq), BF16),
                   jax.ShapeDtypeStruct((t, nk2), BF16),
                   jax.ShapeDtypeStruct((t, nk2), BF16)],
        compiler_params=_params("parallel"),
        name="swa_qkv",
    )(x2, w_ext, b_ext, cos_t, sin_t)

    tq = min(ATTN_TQ, seq)
    ntq = seq // tq
    main = lambda bi, i: (bi * ntq + i, 0)
    halo = lambda bi, i: (jnp.maximum((bi * seq + i * tq) // WINDOW - 1, 0), 0)
    const = lambda bi, i: (0, 0)
    row = lambda v: v.reshape(1, d)
    return pl.pallas_call(
        functools.partial(_attn_kernel, alpha=alpha, tq=tq),
        grid=(nb, ntq),
        in_specs=[
            pl.BlockSpec(memory_space=pltpu.SMEM),
            pl.BlockSpec((tq, nq), main),
            pl.BlockSpec((tq, nk2), main),
            pl.BlockSpec((WINDOW, nk2), halo),
            pl.BlockSpec((tq, nk2), main),
            pl.BlockSpec((WINDOW, nk2), halo),
            pl.BlockSpec((tq, d), main),
            pl.BlockSpec((nq, d), const),
            pl.BlockSpec((1, d), const),
            pl.BlockSpec((1, d), const),
            pl.BlockSpec((1, d), const),
        ],
        out_specs=pl.BlockSpec((tq, d), main),
        out_shape=jax.ShapeDtypeStruct((t, d), F32),
        compiler_params=_params("parallel", "parallel"),
        name="swa_attention",
    )(sinks, q, k2, k2, v2, v2, x2, wo.astype(BF16), row(bo), row(g), row(b))


def _conv_kernel(x_ref, win_ref, cw_ref, wout_ref, g_ref, b_ref, o_ref, ubuf_ref, *, alpha, tt):
    i = pl.program_id(1)
    d = x_ref.shape[1]

    @pl.when(i == 0)
    def _():
        ubuf_ref[0:SUBLANES, :] = jnp.zeros((SUBLANES, d), F32)

    x = x_ref[...]
    proj = jnp.dot(x.astype(BF16), win_ref[...], preferred_element_type=F32)
    gate_b = proj[:, :d]
    u = proj[:, d:2 * d] * proj[:, 2 * d:]
    ubuf_ref[SUBLANES:SUBLANES + tt, :] = u
    cw = cw_ref[...]
    conv = (cw[0:1, :] * ubuf_ref[SUBLANES - 2:SUBLANES - 2 + tt, :]
            + cw[1:2, :] * ubuf_ref[SUBLANES - 1:SUBLANES - 1 + tt, :]
            + cw[2:3, :] * u)
    ubuf_ref[0:SUBLANES, :] = u[tt - SUBLANES:tt, :]
    y = jnp.dot((gate_b * conv).astype(BF16), wout_ref[...], preferred_element_type=F32)
    o_ref[...] = _layer_norm(alpha * x + y, g_ref[...], b_ref[...])


def _conv_mixer(x2, seq, w_in, conv_w, w_out, g, b, alpha):
    t, d = x2.shape
    nb = t // seq
    tt = min(CONV_TT, seq)
    nt = seq // tt
    const = lambda bi, i: (0, 0)
    row = lambda v: v.reshape(1, d)
    return pl.pallas_call(
        functools.partial(_conv_kernel, alpha=alpha, tt=tt),
        grid=(nb, nt),
        in_specs=[
            pl.BlockSpec((tt, d), lambda bi, i: (bi * nt + i, 0)),
            pl.BlockSpec((d, 3 * d), const),
            pl.BlockSpec(conv_w.shape, const),
            pl.BlockSpec((d, d), const),
            pl.BlockSpec((1, d), const),
            pl.BlockSpec((1, d), const),
        ],
        out_specs=pl.BlockSpec((tt, d), lambda bi, i: (bi * nt + i, 0)),
        out_shape=jax.ShapeDtypeStruct((t, d), F32),
        scratch_shapes=[pltpu.VMEM((SUBLANES + tt, d), F32)],
        compiler_params=_params("arbitrary", "arbitrary"),
        name="conv_mixer",
    )(x2, w_in.astype(BF16), conv_w, w_out.astype(BF16), row(g), row(b))


def _first_argmax(v, rows, n):
    m = jnp.max(v, axis=0, keepdims=True)
    first = jnp.min(jnp.where(v == m, rows, n), axis=0, keepdims=True)
    return m, first


def _route_kernel(x_ref, wt_ref, bias_ref, pos_ref, gate_ref, n8_ref, plan_ref, *, tt):
    x = x_ref[...]
    xh = x.astype(BF16)
    xl = (x - xh.astype(F32)).astype(BF16)
    wt = wt_ref[...]
    wh = wt.astype(BF16)
    wl = (wt - wh.astype(F32)).astype(BF16)
    dn = (((1,), (1,)), ((), ()))
    logits = (lax.dot_general(wh, xh, dn, preferred_element_type=F32)
              + (lax.dot_general(wh, xl, dn, preferred_element_type=F32)
                 + lax.dot_general(wl, xh, dn, preferred_element_type=F32)))
    scores = jax.nn.sigmoid(logits)
    sel = scores + bias_ref[...]

    rows8 = lax.broadcasted_iota(jnp.int32, (GROUP_SIZE, tt), 0)
    gsc = []
    for gi in range(N_EXPERT_GROUPS):
        blk = sel[gi * GROUP_SIZE:(gi + 1) * GROUP_SIZE, :]
        m1, f1 = _first_argmax(blk, rows8, GROUP_SIZE)
        m2 = jnp.max(jnp.where(rows8 == f1, NEG_INF, blk), axis=0, keepdims=True)
        gsc.append(m1 + m2)
    cur = jnp.concatenate(gsc, axis=0)
    rows_g = lax.broadcasted_iota(jnp.int32, (N_EXPERT_GROUPS, tt), 0)
    gmask = jnp.zeros((N_EXPERT_GROUPS, tt), F32)
    for _ in range(TOPK_GROUPS):
        _, f = _first_argmax(cur, rows_g, N_EXPERT_GROUPS)
        hit = rows_g == f
        gmask = jnp.where(hit, 1.0, gmask)
        cur = jnp.where(hit, NEG_INF, cur)
    emask = jnp.concatenate(
        [jnp.broadcast_to(gmask[gi:gi + 1, :], (GROUP_SIZE, tt)) for gi in range(N_EXPERT_GROUPS)],
        axis=0)

    rows_e = lax.broadcasted_iota(jnp.int32, (N_EXPERTS, tt), 0)
    cur = jnp.where(emask > 0.0, sel, NEG_INF)
    idxs, gates = [], []
    member = jnp.zeros((N_EXPERTS, tt), F32)
    for _ in range(TOP_K):
        _, f = _first_argmax(cur, rows_e, N_EXPERTS)
        hit = rows_e == f
        idxs.append(f)
        gates.append(jnp.sum(jnp.where(hit, scores, 0.0), axis=0, keepdims=True))
        member = jnp.where(hit, 1.0, member)
        cur = jnp.where(hit, NEG_INF, cur)
    gate = jnp.concatenate(gates, axis=0)
    gate = gate / jnp.sum(gate, axis=0, keepdims=True) * ROUTED_SCALE

    ta = lax.broadcasted_iota(jnp.int32, (tt, tt), 0)
    tb = lax.broadcasted_iota(jnp.int32, (tt, tt), 1)
    earlier = (ta < tb).astype(BF16)
    before = jnp.dot(member.astype(BF16), earlier, preferred_element_type=F32)
    n = jnp.sum(member, axis=1, keepdims=True)
    n8 = jnp.floor((n + (CHUNK_ALIGN - 1)) * (1.0 / CHUNK_ALIGN)) * CHUNK_ALIGN
    n8_wide = jnp.broadcast_to(n8, (N_EXPERTS, LANES))
    ea = lax.broadcasted_iota(jnp.int32, (N_EXPERTS, N_EXPERTS), 0)
    eb = lax.broadcasted_iota(jnp.int32, (N_EXPERTS, N_EXPERTS), 1)
    lower = (eb < ea).astype(BF16)
    chunk_start = jnp.dot(lower, n8_wide.astype(BF16), preferred_element_type=F32)
    where_to = before + chunk_start[:, 0:1]
    for k in range(TOP_K):
        pk = jnp.sum(jnp.where(rows_e == idxs[k], where_to, 0.0), axis=0, keepdims=True)
        pos_ref[k:k + 1, :] = pk.astype(jnp.int32)
    gate_ref[...] = gate
    n8_int = n8_wide.astype(jnp.int32)
    n8_ref[...] = n8_int

    lane = lax.broadcasted_iota(jnp.int32, (N_EXPERTS, LANES), 1)
    flag = jnp.zeros((N_EXPERTS, LANES), F32)
    for j in range(CHUNK_BITS):
        flag = jnp.where((lane == j) & ((n8_int & (CHUNK_ALIGN << j)) != 0), 1.0, flag)
    rank = jnp.dot(lower, flag.astype(BF16), preferred_element_type=F32)
    e_id = lax.broadcasted_iota(jnp.int32, (N_EXPERTS, LANES), 0)
    for j in range(CHUNK_BITS):
        chosen = (rank[:, j:j + 1] == lane.astype(F32)) & (flag[:, j:j + 1] > 0.0)
        plan_ref[j:j + 1, :] = jnp.sum(jnp.where(chosen, e_id, 0), axis=0, keepdims=True)
    plan_ref[CHUNK_BITS:CHUNK_BITS + 1, :] = jnp.sum(flag, axis=0, keepdims=True).astype(jnp.int32)
    plan_ref[CHUNK_BITS + 1:CHUNK_BITS + 2, :] = jnp.sum(n8_int, axis=0, keepdims=True)


def _route(x1, router_w, router_bias, tg):
    t, d = x1.shape
    row_blk = lambda i: (0, i)
    return pl.pallas_call(
        functools.partial(_route_kernel, tt=tg),
        grid=(t // tg,),
        in_specs=[
            pl.BlockSpec((tg, d), lambda i: (i, 0)),
            pl.BlockSpec((N_EXPERTS, d), lambda i: (0, 0)),
            pl.BlockSpec((N_EXPERTS, 1), lambda i: (0, 0)),
        ],
        out_specs=[pl.BlockSpec((TOP_K, tg), row_blk),
                   pl.BlockSpec((TOP_K, tg), row_blk),
                   pl.BlockSpec((N_EXPERTS, LANES), lambda i: (i, 0)),
                   pl.BlockSpec((PLAN_ROWS, LANES), lambda i: (i, 0))],
        out_shape=[jax.ShapeDtypeStruct((TOP_K, t), jnp.int32),
                   jax.ShapeDtypeStruct((TOP_K, t), F32),
                   jax.ShapeDtypeStruct((t // tg * N_EXPERTS, LANES), jnp.int32),
                   jax.ShapeDtypeStruct((t // tg * PLAN_ROWS, LANES), jnp.int32)],
        compiler_params=_params("parallel"),
        name="moe_router",
    )(x1, router_w.T, router_bias.reshape(N_EXPERTS, 1))


def _pack_halves(v):
    h = v.shape[1] // 2
    lo = lax.bitcast_convert_type(v[:, :h].astype(BF16).astype(F32), U32)
    hi = lax.bitcast_convert_type(v[:, h:].astype(BF16).astype(F32), U32)
    return (lo >> 16) | (hi & jnp.uint32(HI_MASK))


def _unpack_halves(w):
    lo = lax.bitcast_convert_type(w << 16, F32).astype(BF16)
    hi = lax.bitcast_convert_type(w & jnp.uint32(HI_MASK), F32).astype(BF16)
    return lo, hi


def _start_group_copies(plan_ref, n8_ref, goff_ref, gdest_ref, group, make_copy):
    for j in range(CHUNK_BITS):
        size = CHUNK_ALIGN << j

        def body(k, carry):
            e = plan_ref[j * LANES + k]
            n8 = n8_ref[group * N_EXPERTS + e]
            done = n8 & (size - 1)
            make_copy(pl.multiple_of(goff_ref[group * N_EXPERTS + e] + done, CHUNK_ALIGN),
                      pl.multiple_of(gdest_ref[group * N_EXPERTS + e] + done, CHUNK_ALIGN), size).start()
            return carry

        lax.fori_loop(0, plan_ref[CHUNK_BITS * LANES + j], body, 0)


def _wait_group_rows(total_rows, ns, make_copy):
    for j in range(ns.bit_length()):
        size = CHUNK_ALIGN << j
        if size > ns:
            break

        @pl.when((total_rows & size) != 0)
        def _():
            make_copy(0, 0, size).wait()


def _dispatch_kernel(n8_ref, goff_ref, gdest_ref, tot_ref, zrow_ref, zflag_ref, plan_ref, pos_ref, x_ref,
                     xs_ref, sbuf_ref, zbuf_ref, sem, zsem, *, tg, ns, bm):
    g = pl.program_id(0)
    ng = pl.num_programs(0)
    slot = g % 2

    def zero_copy(e):
        row = pl.multiple_of(zrow_ref[e], bm)
        return pltpu.make_async_copy(zbuf_ref, xs_ref.at[pl.ds(row, bm), :], zsem)

    @pl.when(g == 0)
    def _():
        zbuf_ref[...] = jnp.zeros_like(zbuf_ref)

        def start(e, carry):
            @pl.when(zflag_ref[e] > 0)
            def _():
                zero_copy(e).start()
            return carry

        def wait(e, carry):
            @pl.when(zflag_ref[e] > 0)
            def _():
                zero_copy(e).wait()
            return carry

        lax.fori_loop(0, N_EXPERTS, start, 0)
        lax.fori_loop(0, N_EXPERTS, wait, 0)

    def copy_from(sl):
        def make_copy(src, dst, size):
            return pltpu.make_async_copy(sbuf_ref.at[sl, pl.ds(src, size), :],
                                         xs_ref.at[pl.ds(dst, size), :], sem.at[sl])
        return make_copy

    @pl.when(g >= 2)
    def _():
        _wait_group_rows(tot_ref[g - 2], ns, copy_from(slot))

    xb = x_ref[...].astype(BF16)
    pos = pos_ref[...]
    for c in range(ns // SORT_CHUNK):
        rows = c * SORT_CHUNK + lax.broadcasted_iota(jnp.int32, (SORT_CHUNK, tg), 0)
        hit = rows == pos[0:1, :]
        for k in range(1, TOP_K):
            hit = jnp.logical_or(hit, rows == pos[k:k + 1, :])
        onehot = jnp.where(hit, 1.0, 0.0).astype(BF16)
        sorted_rows = jnp.dot(onehot, xb, preferred_element_type=F32)
        sbuf_ref[slot, c * SORT_CHUNK:(c + 1) * SORT_CHUNK, :] = _pack_halves(sorted_rows)

    _start_group_copies(plan_ref, n8_ref, goff_ref, gdest_ref, g, copy_from(slot))

    @pl.when(g == ng - 1)
    def _():
        @pl.when(g >= 1)
        def _():
            _wait_group_rows(tot_ref[g - 1], ns, copy_from(1 - slot))
        _wait_group_rows(tot_ref[g], ns, copy_from(slot))


def _dispatch(x1, pos, plan, n8, goff, gdest, tot, zrow, zflag, n_rows, tg, ns, bm):
    t, d = x1.shape
    grid_spec = pltpu.PrefetchScalarGridSpec(
        num_scalar_prefetch=6,
        grid=(t // tg,),
        in_specs=[
            pl.BlockSpec((PLAN_WORDS,), lambda i, *_: (i,), memory_space=pltpu.SMEM),
            pl.BlockSpec((TOP_K, tg), lambda i, *_: (0, i)),
            pl.BlockSpec((tg, d), lambda i, *_: (i, 0)),
        ],
        out_specs=pl.BlockSpec(memory_space=pl.ANY),
        scratch_shapes=[pltpu.VMEM((2, ns, d // 2), U32), pltpu.VMEM((bm, d // 2), U32),
                        pltpu.SemaphoreType.DMA((2,)), pltpu.SemaphoreType.DMA],
    )
    return pl.pallas_call(
        functools.partial(_dispatch_kernel, tg=tg, ns=ns, bm=bm),
        grid_spec=grid_spec,
        out_shape=jax.ShapeDtypeStruct((n_rows, d // 2), U32),
        compiler_params=_params("arbitrary"),
        name="moe_dispatch",
    )(n8, goff, gdest, tot, zrow, zflag, plan, pos, x1)


def _gmm_kernel(be_ref, nv_ref, xs_ref, w1_ref, w3_ref, w2_ref, y_ref, w13_ref, w2b_ref):
    i = pl.program_id(0)
    f = w1_ref.shape[1]
    changed = jnp.logical_or(i == 0, be_ref[i] != be_ref[jnp.maximum(i - 1, 0)])

    @pl.when(changed)
    def _():
        w13_ref[:, :f] = w1_ref[...].astype(BF16)
        w13_ref[:, f:] = w3_ref[...].astype(BF16)
        w2b_ref[...] = w2_ref[...].astype(BF16)

    @pl.when(i < nv_ref[0])
    def _():
        half = xs_ref.shape[1]
        x_lo, x_hi = _unpack_halves(xs_ref[...])
        h13 = (jnp.dot(x_lo, w13_ref[:half, :], preferred_element_type=F32)
               + jnp.dot(x_hi, w13_ref[half:, :], preferred_element_type=F32))
        h = jax.nn.silu(h13[:, :f]) * h13[:, f:]
        y_ref[...] = _pack_halves(jnp.dot(h.astype(BF16), w2b_ref[...], preferred_element_type=F32))


def _gmm(xs, blk_expert, n_valid, w1, w3, w2, layer, bm):
    n_rows, half = xs.shape
    d = 2 * half
    f = w1.shape[-1]
    nblk = n_rows // bm
    xmap = lambda i, be, nv: (jnp.minimum(i, nv[0] - 1), 0)
    wmap = lambda i, be, nv: (layer, be[i], 0, 0)
    grid_spec = pltpu.PrefetchScalarGridSpec(
        num_scalar_prefetch=2,
        grid=(nblk,),
        in_specs=[
            pl.BlockSpec((bm, half), xmap),
            pl.BlockSpec((None, None, d, f), wmap),
            pl.BlockSpec((None, None, d, f), wmap),
            pl.BlockSpec((None, None, f, d), wmap),
        ],
        out_specs=pl.BlockSpec((bm, half), xmap),
        scratch_shapes=[pltpu.VMEM((d, 2 * f), BF16), pltpu.VMEM((f, d), BF16)],
    )
    return pl.pallas_call(
        _gmm_kernel,
        grid_spec=grid_spec,
        out_shape=jax.ShapeDtypeStruct((n_rows, half), U32),
        compiler_params=_params("arbitrary"),
        name="moe_experts",
    )(blk_expert, n_valid, xs, w1, w3, w2)


def _combine_kernel(n8_ref, goff_ref, gdest_ref, tot_ref, plan_ref, plan_next_ref, yb_ref, pos_ref, gate_ref,
                    x_ref, ws13_ref, ws2_ref, g_ref, b_ref, o_ref, ybuf_ref, sem, *, alpha, tg, ns):
    g = pl.program_id(0)
    ng = pl.num_programs(0)
    slot = g % 2

    def copy_into(sl):
        def make_copy(buf_row, hbm_row, size):
            return pltpu.make_async_copy(yb_ref.at[pl.ds(hbm_row, size), :],
                                         ybuf_ref.at[sl, pl.ds(buf_row, size), :], sem.at[sl])
        return make_copy

    @pl.when(g == 0)
    def _():
        ybuf_ref[...] = jnp.zeros_like(ybuf_ref)
        _start_group_copies(plan_ref, n8_ref, goff_ref, gdest_ref, 0, copy_into(0))

    @pl.when(g + 1 < ng)
    def _():
        _start_group_copies(plan_next_ref, n8_ref, goff_ref, gdest_ref, g + 1, copy_into(1 - slot))

    x = x_ref[...]
    f = ws2_ref.shape[0]
    h13 = jnp.dot(x.astype(BF16), ws13_ref[...], preferred_element_type=F32)
    hs = jax.nn.silu(h13[:, :f]) * h13[:, f:]
    shared = jnp.dot(hs.astype(BF16), ws2_ref[...], preferred_element_type=F32)

    _wait_group_rows(tot_ref[g], ns, copy_into(slot))

    pos = pos_ref[...]
    gate = gate_ref[...]
    half = ybuf_ref.shape[2]
    acc_lo = jnp.zeros((tg, half), F32)
    acc_hi = jnp.zeros((tg, half), F32)
    for c in range(ns // SORT_CHUNK):
        cols = c * SORT_CHUNK + lax.broadcasted_iota(jnp.int32, (tg, SORT_CHUNK), 1)
        wgt = jnp.zeros((tg, SORT_CHUNK), F32)
        for k in range(TOP_K):
            wgt = jnp.where(cols == pos[:, k:k + 1], gate[:, k:k + 1], wgt)
        wgt = wgt.astype(BF16)
        y_lo, y_hi = _unpack_halves(ybuf_ref[slot, c * SORT_CHUNK:(c + 1) * SORT_CHUNK, :])
        acc_lo = acc_lo + jnp.dot(wgt, y_lo, preferred_element_type=F32)
        acc_hi = acc_hi + jnp.dot(wgt, y_hi, preferred_element_type=F32)
    routed = jnp.concatenate([acc_lo, acc_hi], axis=1)
    o_ref[...] = _layer_norm(alpha * x + (routed + shared), g_ref[...], b_ref[...])


def _combine(yb, pos_tk, gate_tk, plan, n8, goff, gdest, tot, x1, ws1, ws3, ws2, g, b, alpha, tg, ns):
    t, d = x1.shape
    ng = t // tg
    f = ws1.shape[1]
    ws13 = jnp.concatenate([ws1, ws3], axis=1).astype(BF16)
    const = lambda i, *_: (0, 0)
    tile = lambda i, *_: (i, 0)
    row = lambda v: v.reshape(1, d)
    grid_spec = pltpu.PrefetchScalarGridSpec(
        num_scalar_prefetch=4,
        grid=(ng,),
        in_specs=[
            pl.BlockSpec((PLAN_WORDS,), lambda i, *_: (i,), memory_space=pltpu.SMEM),
            pl.BlockSpec((PLAN_WORDS,), lambda i, *_: (jnp.minimum(i + 1, ng - 1),),
                         memory_space=pltpu.SMEM),
            pl.BlockSpec(memory_space=pl.ANY),
            pl.BlockSpec((tg, TOP_K), tile),
            pl.BlockSpec((tg, TOP_K), tile),
            pl.BlockSpec((tg, d), tile),
            pl.BlockSpec((d, 2 * f), const),
            pl.BlockSpec((f, d), const),
            pl.BlockSpec((1, d), const),
            pl.BlockSpec((1, d), const),
        ],
        out_specs=pl.BlockSpec((tg, d), tile),
        scratch_shapes=[pltpu.VMEM((2, ns, d // 2), U32), pltpu.SemaphoreType.DMA((2,))],
    )
    return pl.pallas_call(
        functools.partial(_combine_kernel, alpha=alpha, tg=tg, ns=ns),
        grid_spec=grid_spec,
        out_shape=jax.ShapeDtypeStruct((t, d), F32),
        compiler_params=_params("arbitrary"),
        name="moe_combine",
    )(n8, goff, gdest, tot, plan, plan, yb, pos_tk, gate_tk, x1, ws13, ws2.astype(BF16), row(g), row(b))


def _moe_layer(x1, layer, router_w, router_bias, w1, w3, w2, ws1, ws3, ws2, g, b, alpha):
    t, d = x1.shape
    tg = min(GROUP_TG, t)
    ng = t // tg
    bm = EXPERT_BM
    ns = -(-(tg * TOP_K + N_EXPERTS * (CHUNK_ALIGN - 1)) // SORT_CHUNK) * SORT_CHUNK
    pos, gate, n8_wide, plan_wide = _route(x1, router_w, router_bias, tg)
    plan = plan_wide.reshape(-1)
    tot = plan_wide.reshape(ng, PLAN_ROWS, LANES)[:, CHUNK_BITS + 1, 0]

    n8 = n8_wide[:, 0].reshape(ng, N_EXPERTS)
    goff = jnp.cumsum(n8, axis=1) - n8
    seg = jnp.sum(n8, axis=0)
    padded = (seg + bm - 1) // bm * bm
    pend = jnp.cumsum(padded)
    pstart = pend - padded
    gdest = pstart[None, :] + jnp.cumsum(n8, axis=0) - n8
    nblk = (ng * (tg * TOP_K + N_EXPERTS * (CHUNK_ALIGN - 1))) // bm + N_EXPERTS
    n_valid = pend[-1] // bm
    blk = jnp.arange(nblk, dtype=jnp.int32)
    blk_expert = jnp.minimum(jnp.sum((pend[None, :] <= (blk * bm)[:, None]).astype(jnp.int32), axis=1),
                             N_EXPERTS - 1)
    blk_expert = jnp.where(blk < n_valid, blk_expert, blk_expert[n_valid - 1]).astype(jnp.int32)
    zrow = jnp.maximum(pend - bm, 0).astype(jnp.int32)
    zflag = (seg > 0).astype(jnp.int32)
    n8f, gofff, gdestf = (a.reshape(-1).astype(jnp.int32) for a in (n8, goff, gdest))

    xs = _dispatch(x1, pos, plan, n8f, gofff, gdestf, tot, zrow, zflag, nblk * bm, tg, ns, bm)
    yb = _gmm(xs, blk_expert, n_valid.reshape(1).astype(jnp.int32), w1, w3, w2, layer, bm)
    return _combine(yb, pos.T, gate.T, plan, n8f, gofff, gdestf, tot, x1, ws1, ws3, ws2, g, b, alpha,
                    tg, ns)


def kernel(x, ln_gain, ln_bias, pool_w, pool_scale, attn_wqkv, attn_bqkv, attn_sinks, attn_wo, attn_bo,
           conv_w_in, conv_w, conv_w_out, router_w, router_bias, expert_w1, expert_w3, expert_w2,
           shared_w1, shared_w3, shared_w2):
    bsz, seq, d = x.shape
    depth = ln_gain.shape[0]
    alpha = float((2 * depth) ** 0.25)
    h = x.reshape(bsz * seq, d)
    for i in range(depth):
        kind = i % N_MIXERS
        j = i // N_MIXERS
        g1, b1 = ln_gain[i, 0], ln_bias[i, 0]
        if kind == 0:
            h = _pool_mixer(h, seq, pool_w[j], pool_scale[j], g1, b1, alpha)
        elif kind == 1:
            h = _swa_mixer(h, seq, attn_wqkv[j], attn_bqkv[j], attn_sinks[j], attn_wo[j], attn_bo[j],
                           g1, b1, alpha)
        else:
            h = _conv_mixer(h, seq, conv_w_in[j], conv_w[j], conv_w_out[j], g1, b1, alpha)
        h = _moe_layer(h, i, router_w[i], router_bias[i], expert_w1, expert_w3, expert_w2,
                       shared_w1[i], shared_w3[i], shared_w2[i], ln_gain[i, 1], ln_bias[i, 1], alpha)
    return h.reshape(bsz, seq, d)
```

```python
import functools

import jax
import jax.numpy as jnp
from jax import lax
from jax.experimental import pallas as pl
from jax.experimental.pallas import tpu as pltpu

POOL_WINDOWS = (2, 4, 8, 16)
N_MIXERS = 3
HEAD_DIM = 64
N_Q_HEADS = 16
N_KV_HEADS = 4
GQA_GROUP = N_Q_HEADS // N_KV_HEADS
WINDOW = 128
ROPE_THETA = 10000.0
N_EXPERTS = 64
TOP_K = 8
N_EXPERT_GROUPS = 8
GROUP_SIZE = N_EXPERTS // N_EXPERT_GROUPS
TOPK_GROUPS = 4
ROUTED_SCALE = 2.5
LN_EPS = 1e-5

LANES = 128
SUBLANES = 8
POOL_HALO = 16
VMEM_LIMIT = 48 * 1024 * 1024

POOL_TT = 512
QKV_TT = 512
ATTN_TQ = 256
CONV_TT = 256
GROUP_TG = 256
CHUNK_ALIGN = SUBLANES
SORT_CHUNK = 256
EXPERT_BM = 512
CHUNK_BITS = 6
PLAN_ROWS = SUBLANES
PLAN_WORDS = 2 * PLAN_ROWS * LANES

U32 = jnp.uint32
HI_MASK = 0xFFFF0000

F32 = jnp.float32
BF16 = jnp.bfloat16
NEG_INF = float("-inf")


def _layer_norm(z, g, b):
    mu = jnp.mean(z, axis=-1, keepdims=True)
    zc = z - mu
    var = jnp.mean(zc * zc, axis=-1, keepdims=True)
    return zc * lax.rsqrt(var + LN_EPS) * g + b


def _params(*sem):
    return pltpu.CompilerParams(dimension_semantics=sem, vmem_limit_bytes=VMEM_LIMIT)


def _pool_kernel(x_ref, halo_ref, w_ref, scale_ref, g_ref, b_ref, o_ref, buf_ref, *, alpha, tt):
    i = pl.program_id(1)
    x = x_ref[...]
    ch = x.shape[1] // len(POOL_WINDOWS)
    buf_ref[0:POOL_HALO, :] = jnp.where(i > 0, halo_ref[...], 0.0)
    buf_ref[POOL_HALO:POOL_HALO + tt, :] = x
    pos = i * tt + lax.broadcasted_iota(jnp.int32, (tt, 1), 0)
    ys = []
    for g, w in enumerate(POOL_WINDOWS):
        c0 = g * ch
        xg = x[:, c0:c0 + ch]
        s = xg
        for j in range(1, w):
            s = s + buf_ref[POOL_HALO - j:POOL_HALO - j + tt, c0:c0 + ch]
        inv_cnt = 1.0 / jnp.minimum(pos + 1, w).astype(F32)
        pooled = s * inv_cnt - xg
        ys.append(jnp.dot(pooled.astype(BF16), w_ref[g], preferred_element_type=F32))
    y = jnp.concatenate(ys, axis=1) * scale_ref[...]
    o_ref[...] = _layer_norm(alpha * x + y, g_ref[...], b_ref[...])


def _pool_mixer(x2, seq, w_grp, scale, g, b, alpha):
    t, d = x2.shape
    tt = min(POOL_TT, seq)
    nt = seq // tt
    nb = t // seq
    row = lambda v: v.reshape(1, d)
    const2 = lambda bi, i: (0, 0)
    return pl.pallas_call(
        functools.partial(_pool_kernel, alpha=alpha, tt=tt),
        grid=(nb, nt),
        in_specs=[
            pl.BlockSpec((tt, d), lambda bi, i: (bi * nt + i, 0)),
            pl.BlockSpec((POOL_HALO, d),
                         lambda bi, i: (jnp.maximum((bi * seq + i * tt) // POOL_HALO - 1, 0), 0)),
            pl.BlockSpec(w_grp.shape, lambda bi, i: (0, 0, 0)),
            pl.BlockSpec((1, d), const2),
            pl.BlockSpec((1, d), const2),
            pl.BlockSpec((1, d), const2),
        ],
        out_specs=pl.BlockSpec((tt, d), lambda bi, i: (bi * nt + i, 0)),
        out_shape=jax.ShapeDtypeStruct((t, d), F32),
        scratch_shapes=[pltpu.VMEM((POOL_HALO + tt, d), F32)],
        compiler_params=_params("parallel", "parallel"),
        name="pool_mixer",
    )(x2, x2, w_grp.astype(BF16), row(scale), row(g), row(b))


def _qkv_kernel(x_ref, w_ref, b_ref, c_ref, s_ref, q_ref, k_ref, v_ref, *, nq, nk):
    y = jnp.dot(x_ref[...].astype(BF16), w_ref[...], preferred_element_type=F32) + b_ref[...]
    c = c_ref[...]
    s = s_ref[...]
    lane = lax.broadcasted_iota(jnp.int32, c.shape, 1)
    first_half = (lane % HEAD_DIM) < (HEAD_DIM // 2)

    def rope(v):
        partner = jnp.where(first_half, pltpu.roll(v, LANES - HEAD_DIM // 2, 1),
                            pltpu.roll(v, HEAD_DIM // 2, 1))
        return v * c + partner * s

    for j in range(nq // LANES):
        q_ref[:, j * LANES:(j + 1) * LANES] = (
            rope(y[:, j * LANES:(j + 1) * LANES]) * (HEAD_DIM ** -0.5)).astype(BF16)
    for j in range(nk // LANES):
        k_ref[:, j * LANES:(j + 1) * LANES] = rope(
            y[:, nq + j * LANES:nq + (j + 1) * LANES]).astype(BF16)
    v_ref[...] = y[:, nq + nk:].astype(BF16)


def _attn_kernel(sink_ref, q_ref, k_ref, kh_ref, v_ref, vh_ref, x_ref, wo_ref, bo_ref, g_ref, b_ref,
                 o_ref, *, alpha, tq):
    i = pl.program_id(1)
    tk = tq + WINDOW
    r = lax.broadcasted_iota(jnp.int32, (tq, tk), 0)
    c = lax.broadcasted_iota(jnp.int32, (tq, tk), 1)
    dist = r + WINDOW - c
    valid = (dist >= 0) & (dist < WINDOW) & ((c >= WINDOW) | (i > 0))
    lane = lax.broadcasted_iota(jnp.int32, (tq, LANES), 1)
    low = lane < HEAD_DIM
    dn = (((1,), (1,)), ((), ()))
    outs = []
    for kv in range(N_KV_HEADS):
        sl = slice(kv * LANES, (kv + 1) * LANES)
        kk = jnp.concatenate([kh_ref[:, sl], k_ref[:, sl]], axis=0)
        vv = jnp.concatenate([vh_ref[:, sl], v_ref[:, sl]], axis=0)
        for p in range(GQA_GROUP // 2):
            c0 = (kv * GQA_GROUP + 2 * p) * HEAD_DIM
            qp = q_ref[:, c0:c0 + LANES]
            res = []
            for half in range(2):
                keep = low if half == 0 else jnp.logical_not(low)
                qh = jnp.where(keep, qp, jnp.zeros_like(qp))
                sc = lax.dot_general(qh, kk, dn, preferred_element_type=F32)
                sc = jnp.where(valid, sc, NEG_INF)
                sink = sink_ref[kv * GQA_GROUP + 2 * p + half]
                m = jnp.maximum(jnp.max(sc, axis=1, keepdims=True), sink)
                pm = jnp.exp(sc - m)
                den = jnp.sum(pm, axis=1, keepdims=True) + jnp.exp(sink - m)
                res.append(jnp.dot(pm.astype(BF16), vv, preferred_element_type=F32) / den)
            outs.append(jnp.where(low, res[0], res[1]))
    o_all = jnp.concatenate(outs, axis=1).astype(BF16)
    mix = jnp.dot(o_all, wo_ref[...], preferred_element_type=F32) + bo_ref[...]
    o_ref[...] = _layer_norm(alpha * x_ref[...] + mix, g_ref[...], b_ref[...])


def _dup_heads(w, n_heads):
    lead = w.shape[:-1]
    w = w.reshape(lead + (n_heads, 1, HEAD_DIM))
    w = jnp.broadcast_to(w, lead + (n_heads, 2, HEAD_DIM))
    return w.reshape(lead + (n_heads * 2 * HEAD_DIM,))


def _swa_mixer(x2, seq, wqkv, bqkv, sinks, wo, bo, g, b, alpha):
    t, d = x2.shape
    nb = t // seq
    nq = N_Q_HEADS * HEAD_DIM
    nkv = N_KV_HEADS * HEAD_DIM
    nk2 = 2 * nkv
    w_ext = jnp.concatenate([wqkv[:, :nq], _dup_heads(wqkv[:, nq:nq + nkv], N_KV_HEADS),
                             _dup_heads(wqkv[:, nq + nkv:], N_KV_HEADS)], axis=1).astype(BF16)
    b_ext = jnp.concatenate([bqkv[:nq], _dup_heads(bqkv[nq:nq + nkv], N_KV_HEADS),
                             _dup_heads(bqkv[nq + nkv:], N_KV_HEADS)]).reshape(1, -1)
    n_ext = nq + 2 * nk2

    pos = jnp.arange(seq, dtype=F32)
    inv_freq = ROPE_THETA ** (-jnp.arange(0, HEAD_DIM, 2, dtype=F32) / HEAD_DIM)
    ang = pos[:, None] * inv_freq[None, :]
    cos, sin = jnp.cos(ang), jnp.sin(ang)
    cos_t = jnp.concatenate([cos, cos, cos, cos], axis=1)
    sin_t = jnp.concatenate([-sin, sin, -sin, sin], axis=1)

    tt = min(QKV_TT, seq)
    nt = seq // tt
    const2 = lambda i: (0, 0)
    q, k2, v2 = pl.pallas_call(
        functools.partial(_qkv_kernel, nq=nq, nk=nk2),
        grid=(t // tt,),
        in_specs=[
            pl.BlockSpec((tt, d), lambda i: (i, 0)),
            pl.BlockSpec((d, n_ext), const2),
            pl.BlockSpec((1, n_ext), const2),
            pl.BlockSpec((tt, LANES), lambda i: (i % nt, 0)),
            pl.BlockSpec((tt, LANES), lambda i: (i % nt, 0)),
        ],
        out_specs=[pl.BlockSpec((tt, nq), lambda i: (i, 0)),
                   pl.BlockSpec((tt, nk2), lambda i: (i, 0)),
                   pl.BlockSpec((tt, nk2), lambda i: (i, 0))],
        out_shape=[jax.ShapeDtypeStruct((t, nq), BF16),
                   jax.ShapeDtypeStruct((t, nk2), BF16),
                   jax.ShapeDtypeStruct((t, nk2), BF16)],
        compiler_params=_params("parallel"),
        name="swa_qkv",
    )(x2, w_ext, b_ext, cos_t, sin_t)

    tq = min(ATTN_TQ, seq)
    ntq = seq // tq
    main = lambda bi, i: (bi * ntq + i, 0)
    halo = lambda bi, i: (jnp.maximum((bi * seq + i * tq) // WINDOW - 1, 0), 0)
    const = lambda bi, i: (0, 0)
    row = lambda v: v.reshape(1, d)
    return pl.pallas_call(
        functools.partial(_attn_kernel, alpha=alpha, tq=tq),
        grid=(nb, ntq),
        in_specs=[
            pl.BlockSpec(memory_space=pltpu.SMEM),
            pl.BlockSpec((tq, nq), main),
            pl.BlockSpec((tq, nk2), main),
            pl.BlockSpec((WINDOW, nk2), halo),
            pl.BlockSpec((tq, nk2), main),
            pl.BlockSpec((WINDOW, nk2), halo),
            pl.BlockSpec((tq, d), main),
            pl.BlockSpec((nq, d), const),
            pl.BlockSpec((1, d), const),
            pl.BlockSpec((1, d), const),
            pl.BlockSpec((1, d), const),
        ],
        out_specs=pl.BlockSpec((tq, d), main),
        out_shape=jax.ShapeDtypeStruct((t, d), F32),
        compiler_params=_params("parallel", "parallel"),
        name="swa_attention",
    )(sinks, q, k2, k2, v2, v2, x2, wo.astype(BF16), row(bo), row(g), row(b))


def _conv_kernel(x_ref, win_ref, cw_ref, wout_ref, g_ref, b_ref, o_ref, ubuf_ref, *, alpha, tt):
    i = pl.program_id(1)
    d = x_ref.shape[1]

    @pl.when(i == 0)
    def _():
        ubuf_ref[0:SUBLANES, :] = jnp.zeros((SUBLANES, d), F32)

    x = x_ref[...]
    proj = jnp.dot(x.astype(BF16), win_ref[...], preferred_element_type=F32)
    gate_b = proj[:, :d]
    u = proj[:, d:2 * d] * proj[:, 2 * d:]
    ubuf_ref[SUBLANES:SUBLANES + tt, :] = u
    cw = cw_ref[...]
    conv = (cw[0:1, :] * ubuf_ref[SUBLANES - 2:SUBLANES - 2 + tt, :]
            + cw[1:2, :] * ubuf_ref[SUBLANES - 1:SUBLANES - 1 + tt, :]
            + cw[2:3, :] * u)
    ubuf_ref[0:SUBLANES, :] = u[tt - SUBLANES:tt, :]
    y = jnp.dot((gate_b * conv).astype(BF16), wout_ref[...], preferred_element_type=F32)
    o_ref[...] = _layer_norm(alpha * x + y, g_ref[...], b_ref[...])


def _conv_mixer(x2, seq, w_in, conv_w, w_out, g, b, alpha):
    t, d = x2.shape
    nb = t // seq
    tt = min(CONV_TT, seq)
    nt = seq // tt
    const = lambda bi, i: (0, 0)
    row = lambda v: v.reshape(1, d)
    return pl.pallas_call(
        functools.partial(_conv_kernel, alpha=alpha, tt=tt),
        grid=(nb, nt),
        in_specs=[
            pl.BlockSpec((tt, d), lambda bi, i: (bi * nt + i, 0)),
            pl.BlockSpec((d, 3 * d), const),
            pl.BlockSpec(conv_w.shape, const),
            pl.BlockSpec((d, d), const),
            pl.BlockSpec((1, d), const),
            pl.BlockSpec((1, d), const),
        ],
        out_specs=pl.BlockSpec((tt, d), lambda bi, i: (bi * nt + i, 0)),
        out_shape=jax.ShapeDtypeStruct((t, d), F32),
        scratch_shapes=[pltpu.VMEM((SUBLANES + tt, d), F32)],
        compiler_params=_params("arbitrary", "arbitrary"),
        name="conv_mixer",
    )(x2, w_in.astype(BF16), conv_w, w_out.astype(BF16), row(g), row(b))


def _first_argmax(v, rows, n):
    m = jnp.max(v, axis=0, keepdims=True)
    first = jnp.min(jnp.where(v == m, rows, n), axis=0, keepdims=True)
    return m, first


def _route_kernel(x_ref, wt_ref, bias_ref, pos_ref, gate_ref, n8_ref, plan_ref, *, tt):
    x = x_ref[...]
    xh = x.astype(BF16)
    xl = (x - xh.astype(F32)).astype(BF16)
    wt = wt_ref[...]
    wh = wt.astype(BF16)
    wl = (wt - wh.astype(F32)).astype(BF16)
    dn = (((1,), (1,)), ((), ()))
    logits = (lax.dot_general(wh, xh, dn, preferred_element_type=F32)
              + (lax.dot_general(wh, xl, dn, preferred_element_type=F32)
                 + lax.dot_general(wl, xh, dn, preferred_element_type=F32)))
    scores = jax.nn.sigmoid(logits)
    sel = scores + bias_ref[...]

    rows8 = lax.broadcasted_iota(jnp.int32, (GROUP_SIZE, tt), 0)
    gsc = []
    for gi in range(N_EXPERT_GROUPS):
        blk = sel[gi * GROUP_SIZE:(gi + 1) * GROUP_SIZE, :]
        m1, f1 = _first_argmax(blk, rows8, GROUP_SIZE)
        m2 = jnp.max(jnp.where(rows8 == f1, NEG_INF, blk), axis=0, keepdims=True)
        gsc.append(m1 + m2)
    cur = jnp.concatenate(gsc, axis=0)
    rows_g = lax.broadcasted_iota(jnp.int32, (N_EXPERT_GROUPS, tt), 0)
    gmask = jnp.zeros((N_EXPERT_GROUPS, tt), F32)
    for _ in range(TOPK_GROUPS):
        _, f = _first_argmax(cur, rows_g, N_EXPERT_GROUPS)
        hit = rows_g == f
        gmask = jnp.where(hit, 1.0, gmask)
        cur = jnp.where(hit, NEG_INF, cur)
    emask = jnp.concatenate(
        [jnp.broadcast_to(gmask[gi:gi + 1, :], (GROUP_SIZE, tt)) for gi in range(N_EXPERT_GROUPS)],
        axis=0)

    rows_e = lax.broadcasted_iota(jnp.int32, (N_EXPERTS, tt), 0)
    cur = jnp.where(emask > 0.0, sel, NEG_INF)
    idxs, gates = [], []
    member = jnp.zeros((N_EXPERTS, tt), F32)
    for _ in range(TOP_K):
        _, f = _first_argmax(cur, rows_e, N_EXPERTS)
        hit = rows_e == f
        idxs.append(f)
        gates.append(jnp.sum(jnp.where(hit, scores, 0.0), axis=0, keepdims=True))
        member = jnp.where(hit, 1.0, member)
        cur = jnp.where(hit, NEG_INF, cur)
    gate = jnp.concatenate(gates, axis=0)
    gate = gate / jnp.sum(gate, axis=0, keepdims=True) * ROUTED_SCALE

    ta = lax.broadcasted_iota(jnp.int32, (tt, tt), 0)
    tb = lax.broadcasted_iota(jnp.int32, (tt, tt), 1)
    earlier = (ta < tb).astype(BF16)
    before = jnp.dot(member.astype(BF16), earlier, preferred_element_type=F32)
    n = jnp.sum(member, axis=1, keepdims=True)
    n8 = jnp.floor((n + (CHUNK_ALIGN - 1)) * (1.0 / CHUNK_ALIGN)) * CHUNK_ALIGN
    n8_wide = jnp.broadcast_to(n8, (N_EXPERTS, LANES))
    ea = lax.broadcasted_iota(jnp.int32, (N_EXPERTS, N_EXPERTS), 0)
    eb = lax.broadcasted_iota(jnp.int32, (N_EXPERTS, N_EXPERTS), 1)
    lower = (eb < ea).astype(BF16)
    chunk_start = jnp.dot(lower, n8_wide.astype(BF16), preferred_element_type=F32)
    where_to = before + chunk_start[:, 0:1]
    for k in range(TOP_K):
        pk = jnp.sum(jnp.where(rows_e == idxs[k], where_to, 0.0), axis=0, keepdims=True)
        pos_ref[k:k + 1, :] = pk.astype(jnp.int32)
    gate_ref[...] = gate
    n8_int = n8_wide.astype(jnp.int32)
    n8_ref[...] = n8_int

    lane = lax.broadcasted_iota(jnp.int32, (N_EXPERTS, LANES), 1)
    flag = jnp.zeros((N_EXPERTS, LANES), F32)
    for j in range(CHUNK_BITS):
        flag = jnp.where((lane == j) & ((n8_int & (CHUNK_ALIGN << j)) != 0), 1.0, flag)
    rank = jnp.dot(lower, flag.astype(BF16), preferred_element_type=F32)
    e_id = lax.broadcasted_iota(jnp.int32, (N_EXPERTS, LANES), 0)
    for j in range(CHUNK_BITS):
        chosen = (rank[:, j:j + 1] == lane.astype(F32)) & (flag[:, j:j + 1] > 0.0)
        plan_ref[j:j + 1, :] = jnp.sum(jnp.where(chosen, e_id, 0), axis=0, keepdims=True)
    plan_ref[CHUNK_BITS:CHUNK_BITS + 1, :] = jnp.sum(flag, axis=0, keepdims=True).astype(jnp.int32)
    plan_ref[CHUNK_BITS + 1:CHUNK_BITS + 2, :] = jnp.sum(n8_int, axis=0, keepdims=True)


def _route(x1, router_w, router_bias, tg):
    t, d = x1.shape
    row_blk = lambda i: (0, i)
    return pl.pallas_call(
        functools.partial(_route_kernel, tt=tg),
        grid=(t // tg,),
        in_specs=[
            pl.BlockSpec((tg, d), lambda i: (i, 0)),
            pl.BlockSpec((N_EXPERTS, d), lambda i: (0, 0)),
            pl.BlockSpec((N_EXPERTS, 1), lambda i: (0, 0)),
        ],
        out_specs=[pl.BlockSpec((TOP_K, tg), row_blk),
                   pl.BlockSpec((TOP_K, tg), row_blk),
                   pl.BlockSpec((N_EXPERTS, LANES), lambda i: (i, 0)),
                   pl.BlockSpec((PLAN_ROWS, LANES), lambda i: (i, 0))],
        out_shape=[jax.ShapeDtypeStruct((TOP_K, t), jnp.int32),
                   jax.ShapeDtypeStruct((TOP_K, t), F32),
                   jax.ShapeDtypeStruct((t // tg * N_EXPERTS, LANES), jnp.int32),
                   jax.ShapeDtypeStruct((t // tg * PLAN_ROWS, LANES), jnp.int32)],
        compiler_params=_params("parallel"),
        name="moe_router",
    )(x1, router_w.T, router_bias.reshape(N_EXPERTS, 1))


def _pack_halves(v, exact=False):
    h = v.shape[1] // 2
    return _pack_pair(v[:, :h], v[:, h:], exact)


def _pack_pair(lo, hi, exact=False):
    if not exact:
        lo, hi = lo.astype(BF16).astype(F32), hi.astype(BF16).astype(F32)
    lo = lax.bitcast_convert_type(lo, U32)
    hi = lax.bitcast_convert_type(hi, U32)
    return (lo >> 16) | (hi & jnp.uint32(HI_MASK))


def _chunk_relative(pos, c):
    rel = pos - c * SORT_CHUNK
    inside = (rel >= 0) & (rel < SORT_CHUNK)
    return jnp.where(inside, rel, -1).astype(F32).astype(BF16)


def _unpack_halves(w):
    lo = lax.bitcast_convert_type(w << 16, F32).astype(BF16)
    hi = lax.bitcast_convert_type(w & jnp.uint32(HI_MASK), F32).astype(BF16)
    return lo, hi


def _start_group_copies(plan_ref, make_copy):
    for j in range(CHUNK_BITS):
        size = CHUNK_ALIGN << j

        def body(k, carry):
            make_copy(pl.multiple_of(plan_ref[j * LANES + k], CHUNK_ALIGN),
                      pl.multiple_of(plan_ref[(PLAN_ROWS + j) * LANES + k], CHUNK_ALIGN), size).start()
            return carry

        lax.fori_loop(0, plan_ref[CHUNK_BITS * LANES + j], body, 0)


def _wait_group_rows(total_rows, ns, make_copy):
    for j in range(ns.bit_length()):
        size = CHUNK_ALIGN << j
        if size > ns:
            break

        @pl.when((total_rows & size) != 0)
        def _():
            make_copy(0, 0, size).wait()


def _dispatch_kernel(tot_ref, zrow_ref, zflag_ref, plan_ref, pos_ref, x_ref, xs_ref, sbuf_ref, zbuf_ref,
                     onehot_ref, sem, zsem, *, tg, ns, bm):
    g = pl.program_id(0)
    ng = pl.num_programs(0)
    slot = g % 2

    def zero_copy(e):
        row = pl.multiple_of(zrow_ref[e], bm)
        return pltpu.make_async_copy(zbuf_ref, xs_ref.at[pl.ds(row, bm), :], zsem)

    @pl.when(g == 0)
    def _():
        zbuf_ref[...] = jnp.zeros_like(zbuf_ref)

        def start(e, carry):
            @pl.when(zflag_ref[e] > 0)
            def _():
                zero_copy(e).start()
            return carry

        def wait(e, carry):
            @pl.when(zflag_ref[e] > 0)
            def _():
                zero_copy(e).wait()
            return carry

        lax.fori_loop(0, N_EXPERTS, start, 0)
        lax.fori_loop(0, N_EXPERTS, wait, 0)

    def copy_from(sl):
        def make_copy(src, dst, size):
            return pltpu.make_async_copy(sbuf_ref.at[sl, pl.ds(src, size), :],
                                         xs_ref.at[pl.ds(dst, size), :], sem.at[sl])
        return make_copy

    @pl.when(g >= 2)
    def _():
        _wait_group_rows(tot_ref[g - 2], ns, copy_from(slot))

    xb = x_ref[...].astype(BF16)
    pos = pos_ref[...]
    local_rows = lax.broadcasted_iota(jnp.int32, (SORT_CHUNK, tg), 0).astype(F32).astype(BF16)
    one = jnp.ones((SORT_CHUNK, tg), BF16)
    zero = jnp.zeros((SORT_CHUNK, tg), BF16)
    for c in range(ns // SORT_CHUNK):
        rel = _chunk_relative(pos, c)
        hit = local_rows == rel[0:1, :]
        for k in range(1, TOP_K):
            hit = jnp.logical_or(hit, local_rows == rel[k:k + 1, :])
        onehot_ref[c * SORT_CHUNK:(c + 1) * SORT_CHUNK, :] = jnp.where(hit, one, zero)
    onehot = onehot_ref[...]
    half = x_ref.shape[1] // 2
    for n in range(half // SORT_CHUNK):
        lo = jnp.dot(onehot, xb[:, n * SORT_CHUNK:(n + 1) * SORT_CHUNK], preferred_element_type=F32)
        hi = jnp.dot(onehot, xb[:, half + n * SORT_CHUNK:half + (n + 1) * SORT_CHUNK],
                     preferred_element_type=F32)
        sbuf_ref[slot, :, n * SORT_CHUNK:(n + 1) * SORT_CHUNK] = _pack_pair(lo, hi, exact=True)

    _start_group_copies(plan_ref, copy_from(slot))

    @pl.when(g == ng - 1)
    def _():
        @pl.when(g >= 1)
        def _():
            _wait_group_rows(tot_ref[g - 1], ns, copy_from(1 - slot))
        _wait_group_rows(tot_ref[g], ns, copy_from(slot))


def _dispatch(x1, pos, plan, tot, zrow, zflag, n_rows, tg, ns, bm):
    t, d = x1.shape
    grid_spec = pltpu.PrefetchScalarGridSpec(
        num_scalar_prefetch=3,
        grid=(t // tg,),
        in_specs=[
            pl.BlockSpec((PLAN_WORDS,), lambda i, *_: (i,), memory_space=pltpu.SMEM),
            pl.BlockSpec((TOP_K, tg), lambda i, *_: (0, i)),
            pl.BlockSpec((tg, d), lambda i, *_: (i, 0)),
        ],
        out_specs=pl.BlockSpec(memory_space=pl.ANY),
        scratch_shapes=[pltpu.VMEM((2, ns, d // 2), U32), pltpu.VMEM((bm, d // 2), U32),
                        pltpu.VMEM((ns, tg), BF16),
                        pltpu.SemaphoreType.DMA((2,)), pltpu.SemaphoreType.DMA],
    )
    return pl.pallas_call(
        functools.partial(_dispatch_kernel, tg=tg, ns=ns, bm=bm),
        grid_spec=grid_spec,
        out_shape=jax.ShapeDtypeStruct((n_rows, d // 2), U32),
        compiler_params=_params("arbitrary"),
        name="moe_dispatch",
    )(tot, zrow, zflag, plan, pos, x1)


def _gmm_kernel(be_ref, nv_ref, xs_ref, w1_ref, w3_ref, w2_ref, y_ref, w13_ref, w2b_ref):
    i = pl.program_id(0)
    f = w1_ref.shape[1]
    changed = jnp.logical_or(i == 0, be_ref[i] != be_ref[jnp.maximum(i - 1, 0)])

    @pl.when(changed)
    def _():
        w13_ref[:, :f] = w1_ref[...].astype(BF16)
        w13_ref[:, f:] = w3_ref[...].astype(BF16)
        w2b_ref[...] = w2_ref[...].astype(BF16)

    @pl.when(i < nv_ref[0])
    def _():
        half = xs_ref.shape[1]
        x_lo, x_hi = _unpack_halves(xs_ref[...])
        h13 = (jnp.dot(x_lo, w13_ref[:half, :], preferred_element_type=F32)
               + jnp.dot(x_hi, w13_ref[half:, :], preferred_element_type=F32))
        h = jax.nn.silu(h13[:, :f]) * h13[:, f:]
        y_ref[...] = _pack_halves(jnp.dot(h.astype(BF16), w2b_ref[...], preferred_element_type=F32))


def _gmm(xs, blk_expert, n_valid, w1, w3, w2, layer, bm):
    n_rows, half = xs.shape
    d = 2 * half
    f = w1.shape[-1]
    nblk = n_rows // bm
    xmap = lambda i, be, nv: (jnp.minimum(i, nv[0] - 1), 0)
    wmap = lambda i, be, nv: (layer, be[i], 0, 0)
    grid_spec = pltpu.PrefetchScalarGridSpec(
        num_scalar_prefetch=2,
        grid=(nblk,),
        in_specs=[
            pl.BlockSpec((bm, half), xmap),
            pl.BlockSpec((None, None, d, f), wmap),
            pl.BlockSpec((None, None, d, f), wmap),
            pl.BlockSpec((None, None, f, d), wmap),
        ],
        out_specs=pl.BlockSpec((bm, half), xmap),
        scratch_shapes=[pltpu.VMEM((d, 2 * f), BF16), pltpu.VMEM((f, d), BF16)],
    )
    return pl.pallas_call(
        _gmm_kernel,
        grid_spec=grid_spec,
        out_shape=jax.ShapeDtypeStruct((n_rows, half), U32),
        compiler_params=_params("arbitrary"),
        name="moe_experts",
    )(blk_expert, n_valid, xs, w1, w3, w2)


def _combine_kernel(tot_ref, plan_ref, plan_next_ref, yb_ref, pos_ref, gate_ref, x_ref, ws13_ref, ws2_ref,
                    g_ref, b_ref, o_ref, ybuf_ref, sem, *, alpha, tg, ns):
    g = pl.program_id(0)
    ng = pl.num_programs(0)
    slot = g % 2

    def copy_into(sl):
        def make_copy(buf_row, hbm_row, size):
            return pltpu.make_async_copy(yb_ref.at[pl.ds(hbm_row, size), :],
                                         ybuf_ref.at[sl, pl.ds(buf_row, size), :], sem.at[sl])
        return make_copy

    @pl.when(g == 0)
    def _():
        ybuf_ref[...] = jnp.zeros_like(ybuf_ref)
        _start_group_copies(plan_ref, copy_into(0))

    @pl.when(g + 1 < ng)
    def _():
        _start_group_copies(plan_next_ref, copy_into(1 - slot))

    x = x_ref[...]
    f = ws2_ref.shape[0]
    h13 = jnp.dot(x.astype(BF16), ws13_ref[...], preferred_element_type=F32)
    hs = jax.nn.silu(h13[:, :f]) * h13[:, f:]
    shared = jnp.dot(hs.astype(BF16), ws2_ref[...], preferred_element_type=F32)

    _wait_group_rows(tot_ref[g], ns, copy_into(slot))

    pos = pos_ref[...]
    gate = gate_ref[...]
    half = ybuf_ref.shape[2]
    acc_lo = jnp.zeros((tg, half), F32)
    acc_hi = jnp.zeros((tg, half), F32)
    local_cols = lax.broadcasted_iota(jnp.int32, (tg, SORT_CHUNK), 1).astype(F32).astype(BF16)
    gate_b = gate.astype(BF16)
    for c in range(ns // SORT_CHUNK):
        rel = _chunk_relative(pos, c)
        wgt = jnp.zeros((tg, SORT_CHUNK), BF16)
        for k in range(TOP_K):
            wgt = jnp.where(local_cols == rel[:, k:k + 1], gate_b[:, k:k + 1], wgt)
        y_lo, y_hi = _unpack_halves(ybuf_ref[slot, c * SORT_CHUNK:(c + 1) * SORT_CHUNK, :])
        acc_lo = acc_lo + jnp.dot(wgt, y_lo, preferred_element_type=F32)
        acc_hi = acc_hi + jnp.dot(wgt, y_hi, preferred_element_type=F32)
    routed = jnp.concatenate([acc_lo, acc_hi], axis=1)
    o_ref[...] = _layer_norm(alpha * x + (routed + shared), g_ref[...], b_ref[...])


def _combine(yb, pos_tk, gate_tk, plan, tot, x1, ws1, ws3, ws2, g, b, alpha, tg, ns):
    t, d = x1.shape
    ng = t // tg
    f = ws1.shape[1]
    ws13 = jnp.concatenate([ws1, ws3], axis=1).astype(BF16)
    const = lambda i, *_: (0, 0)
    tile = lambda i, *_: (i, 0)
    row = lambda v: v.reshape(1, d)
    grid_spec = pltpu.PrefetchScalarGridSpec(
        num_scalar_prefetch=1,
        grid=(ng,),
        in_specs=[
            pl.BlockSpec((PLAN_WORDS,), lambda i, *_: (i,), memory_space=pltpu.SMEM),
            pl.BlockSpec((PLAN_WORDS,), lambda i, *_: (jnp.minimum(i + 1, ng - 1),),
                         memory_space=pltpu.SMEM),
            pl.BlockSpec(memory_space=pl.ANY),
            pl.BlockSpec((tg, TOP_K), tile),
            pl.BlockSpec((tg, TOP_K), tile),
            pl.BlockSpec((tg, d), tile),
            pl.BlockSpec((d, 2 * f), const),
            pl.BlockSpec((f, d), const),
            pl.BlockSpec((1, d), const),
            pl.BlockSpec((1, d), const),
        ],
        out_specs=pl.BlockSpec((tg, d), tile),
        scratch_shapes=[pltpu.VMEM((2, ns, d // 2), U32), pltpu.SemaphoreType.DMA((2,))],
    )
    return pl.pallas_call(
        functools.partial(_combine_kernel, alpha=alpha, tg=tg, ns=ns),
        grid_spec=grid_spec,
        out_shape=jax.ShapeDtypeStruct((t, d), F32),
        compiler_params=_params("arbitrary"),
        name="moe_combine",
    )(tot, plan, plan, yb, pos_tk, gate_tk, x1, ws13, ws2.astype(BF16), row(g), row(b))


def _moe_layer(x1, layer, router_w, router_bias, w1, w3, w2, ws1, ws3, ws2, g, b, alpha):
    t, d = x1.shape
    tg = min(GROUP_TG, t)
    ng = t // tg
    bm = EXPERT_BM
    ns = -(-(tg * TOP_K + N_EXPERTS * (CHUNK_ALIGN - 1)) // SORT_CHUNK) * SORT_CHUNK
    pos, gate, n8_wide, plan_wide = _route(x1, router_w, router_bias, tg)
    plan3 = plan_wide.reshape(ng, PLAN_ROWS, LANES)
    tot = plan3[:, CHUNK_BITS + 1, 0]

    n8 = n8_wide[:, 0].reshape(ng, N_EXPERTS)
    goff = jnp.cumsum(n8, axis=1) - n8
    seg = jnp.sum(n8, axis=0)
    padded = (seg + bm - 1) // bm * bm
    pend = jnp.cumsum(padded)
    pstart = pend - padded
    gdest = pstart[None, :] + jnp.cumsum(n8, axis=0) - n8
    nblk = (ng * (tg * TOP_K + N_EXPERTS * (CHUNK_ALIGN - 1))) // bm + N_EXPERTS
    n_valid = pend[-1] // bm
    blk = jnp.arange(nblk, dtype=jnp.int32)
    blk_expert = jnp.minimum(jnp.sum((pend[None, :] <= (blk * bm)[:, None]).astype(jnp.int32), axis=1),
                             N_EXPERTS - 1)
    blk_expert = jnp.where(blk < n_valid, blk_expert, blk_expert[n_valid - 1]).astype(jnp.int32)
    zrow = jnp.maximum(pend - bm, 0).astype(jnp.int32)
    zflag = (seg > 0).astype(jnp.int32)

    lists = plan3[:, :CHUNK_BITS, :N_EXPERTS]
    sizes = CHUNK_ALIGN << jnp.arange(CHUNK_BITS, dtype=jnp.int32)
    done = n8[:, None, :] & (sizes[None, :, None] - 1)
    pick = lists[..., None] == jnp.arange(N_EXPERTS, dtype=jnp.int32)
    src = jnp.sum(jnp.where(pick, (goff[:, None, :] + done)[:, :, None, :], 0), axis=-1)
    dst = jnp.sum(jnp.where(pick, (gdest[:, None, :] + done)[:, :, None, :], 0), axis=-1)
    widen = lambda a: jnp.pad(a, ((0, 0), (0, 0), (0, LANES - N_EXPERTS)))
    plan = jnp.concatenate([widen(src), plan3[:, CHUNK_BITS:], widen(dst),
                            jnp.zeros((ng, PLAN_ROWS - CHUNK_BITS, LANES), jnp.int32)],
                           axis=1).astype(jnp.int32).reshape(-1)

    xs = _dispatch(x1, pos, plan, tot, zrow, zflag, nblk * bm, tg, ns, bm)
    yb = _gmm(xs, blk_expert, n_valid.reshape(1).astype(jnp.int32), w1, w3, w2, layer, bm)
    return _combine(yb, pos.T, gate.T, plan, tot, x1, ws1, ws3, ws2, g, b, alpha, tg, ns)


def kernel(x, ln_gain, ln_bias, pool_w, pool_scale, attn_wqkv, attn_bqkv, attn_sinks, attn_wo, attn_bo,
           conv_w_in, conv_w, conv_w_out, router_w, router_bias, expert_w1, expert_w3, expert_w2,
           shared_w1, shared_w3, shared_w2):
    bsz, seq, d = x.shape
    depth = ln_gain.shape[0]
    alpha = float((2 * depth) ** 0.25)
    h = x.reshape(bsz * seq, d)
    for i in range(depth):
        kind = i % N_MIXERS
        j = i // N_MIXERS
        g1, b1 = ln_gain[i, 0], ln_bias[i, 0]
        if kind == 0:
            h = _pool_mixer(h, seq, pool_w[j], pool_scale[j], g1, b1, alpha)
        elif kind == 1:
            h = _swa_mixer(h, seq, attn_wqkv[j], attn_bqkv[j], attn_sinks[j], attn_wo[j], attn_bo[j],
                           g1, b1, alpha)
        else:
            h = _conv_mixer(h, seq, conv_w_in[j], conv_w[j], conv_w_out[j], g1, b1, alpha)
        h = _moe_layer(h, i, router_w[i], router_bias[i], expert_w1, expert_w3, expert_w2,
                       shared_w1[i], shared_w3[i], shared_w2[i], ln_gain[i, 1], ln_bias[i, 1], alpha)
    return h.reshape(bsz, seq, d)
```

```python
import functools

import jax
import jax.numpy as jnp
from jax import lax
from jax.experimental import pallas as pl
from jax.experimental.pallas import tpu as pltpu

POOL_WINDOWS = (2, 4, 8, 16)
N_MIXERS = 3
HEAD_DIM = 64
N_Q_HEADS = 16
N_KV_HEADS = 4
GQA_GROUP = N_Q_HEADS // N_KV_HEADS
WINDOW = 128
ROPE_THETA = 10000.0
N_EXPERTS = 64
TOP_K = 8
N_EXPERT_GROUPS = 8
GROUP_SIZE = N_EXPERTS // N_EXPERT_GROUPS
TOPK_GROUPS = 4
ROUTED_SCALE = 2.5
LN_EPS = 1e-5

LANES = 128
SUBLANES = 8
POOL_HALO = 16
VMEM_LIMIT = 48 * 1024 * 1024

POOL_TT = 512
QKV_TT = 512
ATTN_TQ = 256
CONV_TT = 256
GROUP_TG = 256
CHUNK_ALIGN = SUBLANES
SORT_CHUNK = 256
EXPERT_BM = 512
GMM_X_SLOTS = 3
CHUNK_BITS = 6
PLAN_ROWS = SUBLANES
PLAN_WORDS = 2 * PLAN_ROWS * LANES

U32 = jnp.uint32
HI_MASK = 0xFFFF0000

F32 = jnp.float32
BF16 = jnp.bfloat16
NEG_INF = float("-inf")


def _layer_norm(z, g, b):
    mu = jnp.mean(z, axis=-1, keepdims=True)
    zc = z - mu
    var = jnp.mean(zc * zc, axis=-1, keepdims=True)
    return zc * lax.rsqrt(var + LN_EPS) * g + b


def _params(*sem):
    return pltpu.CompilerParams(dimension_semantics=sem, vmem_limit_bytes=VMEM_LIMIT)


def _pool_kernel(x_ref, halo_ref, w_ref, scale_ref, g_ref, b_ref, o_ref, buf_ref, *, alpha, tt):
    i = pl.program_id(1)
    x = x_ref[...]
    ch = x.shape[1] // len(POOL_WINDOWS)
    buf_ref[0:POOL_HALO, :] = jnp.where(i > 0, halo_ref[...], 0.0)
    buf_ref[POOL_HALO:POOL_HALO + tt, :] = x
    pos = i * tt + lax.broadcasted_iota(jnp.int32, (tt, 1), 0)
    ys = []
    for g, w in enumerate(POOL_WINDOWS):
        c0 = g * ch
        xg = x[:, c0:c0 + ch]
        s = xg
        for j in range(1, w):
            s = s + buf_ref[POOL_HALO - j:POOL_HALO - j + tt, c0:c0 + ch]
        inv_cnt = 1.0 / jnp.minimum(pos + 1, w).astype(F32)
        pooled = s * inv_cnt - xg
        ys.append(jnp.dot(pooled.astype(BF16), w_ref[g], preferred_element_type=F32))
    y = jnp.concatenate(ys, axis=1) * scale_ref[...]
    o_ref[...] = _layer_norm(alpha * x + y, g_ref[...], b_ref[...])


def _pool_mixer(x2, seq, w_grp, scale, g, b, alpha):
    t, d = x2.shape
    tt = min(POOL_TT, seq)
    nt = seq // tt
    nb = t // seq
    row = lambda v: v.reshape(1, d)
    const2 = lambda bi, i: (0, 0)
    return pl.pallas_call(
        functools.partial(_pool_kernel, alpha=alpha, tt=tt),
        grid=(nb, nt),
        in_specs=[
            pl.BlockSpec((tt, d), lambda bi, i: (bi * nt + i, 0)),
            pl.BlockSpec((POOL_HALO, d),
                         lambda bi, i: (jnp.maximum((bi * seq + i * tt) // POOL_HALO - 1, 0), 0)),
            pl.BlockSpec(w_grp.shape, lambda bi, i: (0, 0, 0)),
            pl.BlockSpec((1, d), const2),
            pl.BlockSpec((1, d), const2),
            pl.BlockSpec((1, d), const2),
        ],
        out_specs=pl.BlockSpec((tt, d), lambda bi, i: (bi * nt + i, 0)),
        out_shape=jax.ShapeDtypeStruct((t, d), F32),
        scratch_shapes=[pltpu.VMEM((POOL_HALO + tt, d), F32)],
        compiler_params=_params("parallel", "parallel"),
        name="pool_mixer",
    )(x2, x2, w_grp.astype(BF16), row(scale), row(g), row(b))


def _qkv_kernel(x_ref, w_ref, b_ref, c_ref, s_ref, q_ref, k_ref, v_ref, *, nq, nk):
    y = jnp.dot(x_ref[...].astype(BF16), w_ref[...], preferred_element_type=F32) + b_ref[...]
    c = c_ref[...]
    s = s_ref[...]
    lane = lax.broadcasted_iota(jnp.int32, c.shape, 1)
    first_half = (lane % HEAD_DIM) < (HEAD_DIM // 2)

    def rope(v):
        partner = jnp.where(first_half, pltpu.roll(v, LANES - HEAD_DIM // 2, 1),
                            pltpu.roll(v, HEAD_DIM // 2, 1))
        return v * c + partner * s

    for j in range(nq // LANES):
        q_ref[:, j * LANES:(j + 1) * LANES] = (
            rope(y[:, j * LANES:(j + 1) * LANES]) * (HEAD_DIM ** -0.5)).astype(BF16)
    for j in range(nk // LANES):
        k_ref[:, j * LANES:(j + 1) * LANES] = rope(
            y[:, nq + j * LANES:nq + (j + 1) * LANES]).astype(BF16)
    v_ref[...] = y[:, nq + nk:].astype(BF16)


def _attn_kernel(sink_ref, q_ref, k_ref, kh_ref, v_ref, vh_ref, x_ref, wo_ref, bo_ref, g_ref, b_ref,
                 o_ref, *, alpha, tq):
    i = pl.program_id(1)
    tk = tq + WINDOW
    r = lax.broadcasted_iota(jnp.int32, (tq, tk), 0)
    c = lax.broadcasted_iota(jnp.int32, (tq, tk), 1)
    dist = r + WINDOW - c
    valid = (dist >= 0) & (dist < WINDOW) & ((c >= WINDOW) | (i > 0))
    lane = lax.broadcasted_iota(jnp.int32, (tq, LANES), 1)
    low = lane < HEAD_DIM
    dn = (((1,), (1,)), ((), ()))
    outs = []
    for kv in range(N_KV_HEADS):
        sl = slice(kv * LANES, (kv + 1) * LANES)
        kk = jnp.concatenate([kh_ref[:, sl], k_ref[:, sl]], axis=0)
        vv = jnp.concatenate([vh_ref[:, sl], v_ref[:, sl]], axis=0)
        for p in range(GQA_GROUP // 2):
            c0 = (kv * GQA_GROUP + 2 * p) * HEAD_DIM
            qp = q_ref[:, c0:c0 + LANES]
            res = []
            for half in range(2):
                keep = low if half == 0 else jnp.logical_not(low)
                qh = jnp.where(keep, qp, jnp.zeros_like(qp))
                sc = lax.dot_general(qh, kk, dn, preferred_element_type=F32)
                sc = jnp.where(valid, sc, NEG_INF)
                sink = sink_ref[kv * GQA_GROUP + 2 * p + half]
                m = jnp.maximum(jnp.max(sc, axis=1, keepdims=True), sink)
                pm = jnp.exp(sc - m)
                den = jnp.sum(pm, axis=1, keepdims=True) + jnp.exp(sink - m)
                res.append(jnp.dot(pm.astype(BF16), vv, preferred_element_type=F32) / den)
            outs.append(jnp.where(low, res[0], res[1]))
    o_all = jnp.concatenate(outs, axis=1).astype(BF16)
    mix = jnp.dot(o_all, wo_ref[...], preferred_element_type=F32) + bo_ref[...]
    o_ref[...] = _layer_norm(alpha * x_ref[...] + mix, g_ref[...], b_ref[...])


def _dup_heads(w, n_heads):
    lead = w.shape[:-1]
    w = w.reshape(lead + (n_heads, 1, HEAD_DIM))
    w = jnp.broadcast_to(w, lead + (n_heads, 2, HEAD_DIM))
    return w.reshape(lead + (n_heads * 2 * HEAD_DIM,))


def _swa_mixer(x2, seq, wqkv, bqkv, sinks, wo, bo, g, b, alpha):
    t, d = x2.shape
    nb = t // seq
    nq = N_Q_HEADS * HEAD_DIM
    nkv = N_KV_HEADS * HEAD_DIM
    nk2 = 2 * nkv
    w_ext = jnp.concatenate([wqkv[:, :nq], _dup_heads(wqkv[:, nq:nq + nkv], N_KV_HEADS),
                             _dup_heads(wqkv[:, nq + nkv:], N_KV_HEADS)], axis=1).astype(BF16)
    b_ext = jnp.concatenate([bqkv[:nq], _dup_heads(bqkv[nq:nq + nkv], N_KV_HEADS),
                             _dup_heads(bqkv[nq + nkv:], N_KV_HEADS)]).reshape(1, -1)
    n_ext = nq + 2 * nk2

    pos = jnp.arange(seq, dtype=F32)
    inv_freq = ROPE_THETA ** (-jnp.arange(0, HEAD_DIM, 2, dtype=F32) / HEAD_DIM)
    ang = pos[:, None] * inv_freq[None, :]
    cos, sin = jnp.cos(ang), jnp.sin(ang)
    cos_t = jnp.concatenate([cos, cos, cos, cos], axis=1)
    sin_t = jnp.concatenate([-sin, sin, -sin, sin], axis=1)

    tt = min(QKV_TT, seq)
    nt = seq // tt
    const2 = lambda i: (0, 0)
    q, k2, v2 = pl.pallas_call(
        functools.partial(_qkv_kernel, nq=nq, nk=nk2),
        grid=(t // tt,),
        in_specs=[
            pl.BlockSpec((tt, d), lambda i: (i, 0)),
            pl.BlockSpec((d, n_ext), const2),
            pl.BlockSpec((1, n_ext), const2),
            pl.BlockSpec((tt, LANES), lambda i: (i % nt, 0)),
            pl.BlockSpec((tt, LANES), lambda i: (i % nt, 0)),
        ],
        out_specs=[pl.BlockSpec((tt, nq), lambda i: (i, 0)),
                   pl.BlockSpec((tt, nk2), lambda i: (i, 0)),
                   pl.BlockSpec((tt, nk2), lambda i: (i, 0))],
        out_shape=[jax.ShapeDtypeStruct((t, nq), BF16),
                   jax.ShapeDtypeStruct((t, nk2), BF16),
                   jax.ShapeDtypeStruct((t, nk2), BF16)],
        compiler_params=_params("parallel"),
        name="swa_qkv",
    )(x2, w_ext, b_ext, cos_t, sin_t)

    tq = min(ATTN_TQ, seq)
    ntq = seq // tq
    main = lambda bi, i: (bi * ntq + i, 0)
    halo = lambda bi, i: (jnp.maximum((bi * seq + i * tq) // WINDOW - 1, 0), 0)
    const = lambda bi, i: (0, 0)
    row = lambda v: v.reshape(1, d)
    return pl.pallas_call(
        functools.partial(_attn_kernel, alpha=alpha, tq=tq),
        grid=(nb, ntq),
        in_specs=[
            pl.BlockSpec(memory_space=pltpu.SMEM),
            pl.BlockSpec((tq, nq), main),
            pl.BlockSpec((tq, nk2), main),
            pl.BlockSpec((WINDOW, nk2), halo),
            pl.BlockSpec((tq, nk2), main),
            pl.BlockSpec((WINDOW, nk2), halo),
            pl.BlockSpec((tq, d), main),
            pl.BlockSpec((nq, d), const),
            pl.BlockSpec((1, d), const),
            pl.BlockSpec((1, d), const),
            pl.BlockSpec((1, d), const),
        ],
        out_specs=pl.BlockSpec((tq, d), main),
        out_shape=jax.ShapeDtypeStruct((t, d), F32),
        compiler_params=_params("parallel", "parallel"),
        name="swa_attention",
    )(sinks, q, k2, k2, v2, v2, x2, wo.astype(BF16), row(bo), row(g), row(b))


def _conv_kernel(x_ref, win_ref, cw_ref, wout_ref, g_ref, b_ref, o_ref, ubuf_ref, *, alpha, tt):
    i = pl.program_id(1)
    d = x_ref.shape[1]

    @pl.when(i == 0)
    def _():
        ubuf_ref[0:SUBLANES, :] = jnp.zeros((SUBLANES, d), F32)

    x = x_ref[...]
    proj = jnp.dot(x.astype(BF16), win_ref[...], preferred_element_type=F32)
    gate_b = proj[:, :d]
    u = proj[:, d:2 * d] * proj[:, 2 * d:]
    ubuf_ref[SUBLANES:SUBLANES + tt, :] = u
    cw = cw_ref[...]
    conv = (cw[0:1, :] * ubuf_ref[SUBLANES - 2:SUBLANES - 2 + tt, :]
            + cw[1:2, :] * ubuf_ref[SUBLANES - 1:SUBLANES - 1 + tt, :]
            + cw[2:3, :] * u)
    ubuf_ref[0:SUBLANES, :] = u[tt - SUBLANES:tt, :]
    y = jnp.dot((gate_b * conv).astype(BF16), wout_ref[...], preferred_element_type=F32)
    o_ref[...] = _layer_norm(alpha * x + y, g_ref[...], b_ref[...])


def _conv_mixer(x2, seq, w_in, conv_w, w_out, g, b, alpha):
    t, d = x2.shape
    nb = t // seq
    tt = min(CONV_TT, seq)
    nt = seq // tt
    const = lambda bi, i: (0, 0)
    row = lambda v: v.reshape(1, d)
    return pl.pallas_call(
        functools.partial(_conv_kernel, alpha=alpha, tt=tt),
        grid=(nb, nt),
        in_specs=[
            pl.BlockSpec((tt, d), lambda bi, i: (bi * nt + i, 0)),
            pl.BlockSpec((d, 3 * d), const),
            pl.BlockSpec(conv_w.shape, const),
            pl.BlockSpec((d, d), const),
            pl.BlockSpec((1, d), const),
            pl.BlockSpec((1, d), const),
        ],
        out_specs=pl.BlockSpec((tt, d), lambda bi, i: (bi * nt + i, 0)),
        out_shape=jax.ShapeDtypeStruct((t, d), F32),
        scratch_shapes=[pltpu.VMEM((SUBLANES + tt, d), F32)],
        compiler_params=_params("arbitrary", "arbitrary"),
        name="conv_mixer",
    )(x2, w_in.astype(BF16), conv_w, w_out.astype(BF16), row(g), row(b))


def _first_argmax(v, rows, n):
    m = jnp.max(v, axis=0, keepdims=True)
    first = jnp.min(jnp.where(v == m, rows, n), axis=0, keepdims=True)
    return m, first


def _route_kernel(x_ref, wt_ref, bias_ref, pos_ref, gate_ref, n8_ref, plan_ref, *, tt):
    x = x_ref[...]
    xh = x.astype(BF16)
    xl = (x - xh.astype(F32)).astype(BF16)
    wt = wt_ref[...]
    wh = wt.astype(BF16)
    wl = (wt - wh.astype(F32)).astype(BF16)
    dn = (((1,), (1,)), ((), ()))
    logits = (lax.dot_general(wh, xh, dn, preferred_element_type=F32)
              + (lax.dot_general(wh, xl, dn, preferred_element_type=F32)
                 + lax.dot_general(wl, xh, dn, preferred_element_type=F32)))
    scores = jax.nn.sigmoid(logits)
    sel = scores + bias_ref[...]

    rows8 = lax.broadcasted_iota(jnp.int32, (GROUP_SIZE, tt), 0)
    gsc = []
    for gi in range(N_EXPERT_GROUPS):
        blk = sel[gi * GROUP_SIZE:(gi + 1) * GROUP_SIZE, :]
        m1, f1 = _first_argmax(blk, rows8, GROUP_SIZE)
        m2 = jnp.max(jnp.where(rows8 == f1, NEG_INF, blk), axis=0, keepdims=True)
        gsc.append(m1 + m2)
    cur = jnp.concatenate(gsc, axis=0)
    rows_g = lax.broadcasted_iota(jnp.int32, (N_EXPERT_GROUPS, tt), 0)
    gmask = jnp.zeros((N_EXPERT_GROUPS, tt), F32)
    for _ in range(TOPK_GROUPS):
        _, f = _first_argmax(cur, rows_g, N_EXPERT_GROUPS)
        hit = rows_g == f
        gmask = jnp.where(hit, 1.0, gmask)
        cur = jnp.where(hit, NEG_INF, cur)
    emask = jnp.concatenate(
        [jnp.broadcast_to(gmask[gi:gi + 1, :], (GROUP_SIZE, tt)) for gi in range(N_EXPERT_GROUPS)],
        axis=0)

    rows_e = lax.broadcasted_iota(jnp.int32, (N_EXPERTS, tt), 0)
    cur = jnp.where(emask > 0.0, sel, NEG_INF)
    idxs, gates = [], []
    member = jnp.zeros((N_EXPERTS, tt), F32)
    for _ in range(TOP_K):
        _, f = _first_argmax(cur, rows_e, N_EXPERTS)
        hit = rows_e == f
        idxs.append(f)
        gates.append(jnp.sum(jnp.where(hit, scores, 0.0), axis=0, keepdims=True))
        member = jnp.where(hit, 1.0, member)
        cur = jnp.where(hit, NEG_INF, cur)
    gate = jnp.concatenate(gates, axis=0)
    gate = gate / jnp.sum(gate, axis=0, keepdims=True) * ROUTED_SCALE

    ta = lax.broadcasted_iota(jnp.int32, (tt, tt), 0)
    tb = lax.broadcasted_iota(jnp.int32, (tt, tt), 1)
    earlier = (ta < tb).astype(BF16)
    before = jnp.dot(member.astype(BF16), earlier, preferred_element_type=F32)
    n = jnp.sum(member, axis=1, keepdims=True)
    n8 = jnp.floor((n + (CHUNK_ALIGN - 1)) * (1.0 / CHUNK_ALIGN)) * CHUNK_ALIGN
    n8_wide = jnp.broadcast_to(n8, (N_EXPERTS, LANES))
    ea = lax.broadcasted_iota(jnp.int32, (N_EXPERTS, N_EXPERTS), 0)
    eb = lax.broadcasted_iota(jnp.int32, (N_EXPERTS, N_EXPERTS), 1)
    lower = (eb < ea).astype(BF16)
    chunk_start = jnp.dot(lower, n8_wide.astype(BF16), preferred_element_type=F32)
    where_to = before + chunk_start[:, 0:1]
    for k in range(TOP_K):
        pk = jnp.sum(jnp.where(rows_e == idxs[k], where_to, 0.0), axis=0, keepdims=True)
        pos_ref[k:k + 1, :] = pk.astype(jnp.int32)
    gate_ref[...] = gate
    n8_int = n8_wide.astype(jnp.int32)
    n8_ref[...] = n8_int

    lane = lax.broadcasted_iota(jnp.int32, (N_EXPERTS, LANES), 1)
    flag = jnp.zeros((N_EXPERTS, LANES), F32)
    for j in range(CHUNK_BITS):
        flag = jnp.where((lane == j) & ((n8_int & (CHUNK_ALIGN << j)) != 0), 1.0, flag)
    rank = jnp.dot(lower, flag.astype(BF16), preferred_element_type=F32)
    e_id = lax.broadcasted_iota(jnp.int32, (N_EXPERTS, LANES), 0)
    for j in range(CHUNK_BITS):
        chosen = (rank[:, j:j + 1] == lane.astype(F32)) & (flag[:, j:j + 1] > 0.0)
        plan_ref[j:j + 1, :] = jnp.sum(jnp.where(chosen, e_id, 0), axis=0, keepdims=True)
    plan_ref[CHUNK_BITS:CHUNK_BITS + 1, :] = jnp.sum(flag, axis=0, keepdims=True).astype(jnp.int32)
    plan_ref[CHUNK_BITS + 1:CHUNK_BITS + 2, :] = jnp.sum(n8_int, axis=0, keepdims=True)


def _route(x1, router_w, router_bias, tg):
    t, d = x1.shape
    row_blk = lambda i: (0, i)
    return pl.pallas_call(
        functools.partial(_route_kernel, tt=tg),
        grid=(t // tg,),
        in_specs=[
            pl.BlockSpec((tg, d), lambda i: (i, 0)),
            pl.BlockSpec((N_EXPERTS, d), lambda i: (0, 0)),
            pl.BlockSpec((N_EXPERTS, 1), lambda i: (0, 0)),
        ],
        out_specs=[pl.BlockSpec((TOP_K, tg), row_blk),
                   pl.BlockSpec((TOP_K, tg), row_blk),
                   pl.BlockSpec((N_EXPERTS, LANES), lambda i: (i, 0)),
                   pl.BlockSpec((PLAN_ROWS, LANES), lambda i: (i, 0))],
        out_shape=[jax.ShapeDtypeStruct((TOP_K, t), jnp.int32),
                   jax.ShapeDtypeStruct((TOP_K, t), F32),
                   jax.ShapeDtypeStruct((t // tg * N_EXPERTS, LANES), jnp.int32),
                   jax.ShapeDtypeStruct((t // tg * PLAN_ROWS, LANES), jnp.int32)],
        compiler_params=_params("parallel"),
        name="moe_router",
    )(x1, router_w.T, router_bias.reshape(N_EXPERTS, 1))


def _pack_halves(v, exact=False):
    h = v.shape[1] // 2
    return _pack_pair(v[:, :h], v[:, h:], exact)


def _pack_pair(lo, hi, exact=False):
    if not exact:
        lo, hi = lo.astype(BF16).astype(F32), hi.astype(BF16).astype(F32)
    lo = lax.bitcast_convert_type(lo, U32)
    hi = lax.bitcast_convert_type(hi, U32)
    return (lo >> 16) | (hi & jnp.uint32(HI_MASK))


def _chunk_relative(pos, c):
    rel = pos - c * SORT_CHUNK
    inside = (rel >= 0) & (rel < SORT_CHUNK)
    return jnp.where(inside, rel, -1).astype(F32).astype(BF16)


def _unpack_halves(w):
    lo = lax.bitcast_convert_type(w << 16, F32).astype(BF16)
    hi = lax.bitcast_convert_type(w & jnp.uint32(HI_MASK), F32).astype(BF16)
    return lo, hi


def _start_group_copies(plan_ref, make_copy):
    for j in range(CHUNK_BITS):
        size = CHUNK_ALIGN << j

        def body(k, carry):
            make_copy(pl.multiple_of(plan_ref[j * LANES + k], CHUNK_ALIGN),
                      pl.multiple_of(plan_ref[(PLAN_ROWS + j) * LANES + k], CHUNK_ALIGN), size).start()
            return carry

        lax.fori_loop(0, plan_ref[CHUNK_BITS * LANES + j], body, 0)


def _wait_group_rows(total_rows, ns, make_copy):
    for j in range(ns.bit_length()):
        size = CHUNK_ALIGN << j
        if size > ns:
            break

        @pl.when((total_rows & size) != 0)
        def _():
            make_copy(0, 0, size).wait()


def _dispatch_kernel(tot_ref, zrow_ref, plan_ref, pos_ref, x_ref, xs_ref, sbuf_ref, zbuf_ref,
                     onehot_ref, sem, zsem, *, tg, ns, bm):
    g = pl.program_id(0)
    ng = pl.num_programs(0)
    slot = g % 2

    @pl.when(g == 0)
    def _():
        zbuf_ref[...] = jnp.zeros_like(zbuf_ref)
        row = pl.multiple_of(zrow_ref[0], CHUNK_ALIGN)
        tail = pltpu.make_async_copy(zbuf_ref, xs_ref.at[pl.ds(row, bm), :], zsem)
        tail.start()
        tail.wait()

    def copy_from(sl):
        def make_copy(src, dst, size):
            return pltpu.make_async_copy(sbuf_ref.at[sl, pl.ds(src, size), :],
                                         xs_ref.at[pl.ds(dst, size), :], sem.at[sl])
        return make_copy

    @pl.when(g >= 2)
    def _():
        _wait_group_rows(tot_ref[g - 2], ns, copy_from(slot))

    xb = x_ref[...].astype(BF16)
    pos = pos_ref[...]
    local_rows = lax.broadcasted_iota(jnp.int32, (SORT_CHUNK, tg), 0).astype(F32).astype(BF16)
    one = jnp.ones((SORT_CHUNK, tg), BF16)
    zero = jnp.zeros((SORT_CHUNK, tg), BF16)
    for c in range(ns // SORT_CHUNK):
        rel = _chunk_relative(pos, c)
        hit = local_rows == rel[0:1, :]
        for k in range(1, TOP_K):
            hit = jnp.logical_or(hit, local_rows == rel[k:k + 1, :])
        onehot_ref[c * SORT_CHUNK:(c + 1) * SORT_CHUNK, :] = jnp.where(hit, one, zero)
    onehot = onehot_ref[...]
    half = x_ref.shape[1] // 2
    for n in range(half // SORT_CHUNK):
        lo = jnp.dot(onehot, xb[:, n * SORT_CHUNK:(n + 1) * SORT_CHUNK], preferred_element_type=F32)
        hi = jnp.dot(onehot, xb[:, half + n * SORT_CHUNK:half + (n + 1) * SORT_CHUNK],
                     preferred_element_type=F32)
        sbuf_ref[slot, :, n * SORT_CHUNK:(n + 1) * SORT_CHUNK] = _pack_pair(lo, hi, exact=True)

    _start_group_copies(plan_ref, copy_from(slot))

    @pl.when(g == ng - 1)
    def _():
        @pl.when(g >= 1)
        def _():
            _wait_group_rows(tot_ref[g - 1], ns, copy_from(1 - slot))
        _wait_group_rows(tot_ref[g], ns, copy_from(slot))


def _dispatch(x1, pos, plan, tot, zrow, n_rows, tg, ns, bm):
    t, d = x1.shape
    grid_spec = pltpu.PrefetchScalarGridSpec(
        num_scalar_prefetch=2,
        grid=(t // tg,),
        in_specs=[
            pl.BlockSpec((PLAN_WORDS,), lambda i, *_: (i,), memory_space=pltpu.SMEM),
            pl.BlockSpec((TOP_K, tg), lambda i, *_: (0, i)),
            pl.BlockSpec((tg, d), lambda i, *_: (i, 0)),
        ],
        out_specs=pl.BlockSpec(memory_space=pl.ANY),
        scratch_shapes=[pltpu.VMEM((2, ns, d // 2), U32), pltpu.VMEM((bm, d // 2), U32),
                        pltpu.VMEM((ns, tg), BF16),
                        pltpu.SemaphoreType.DMA((2,)), pltpu.SemaphoreType.DMA],
    )
    return pl.pallas_call(
        functools.partial(_dispatch_kernel, tg=tg, ns=ns, bm=bm),
        grid_spec=grid_spec,
        out_shape=jax.ShapeDtypeStruct((n_rows, d // 2), U32),
        compiler_params=_params("arbitrary"),
        name="moe_dispatch",
    )(tot, zrow, plan, pos, x1)


def _gmm_kernel(nq_ref, brow_ref, bval_ref, bord_ref, bfirst_ref, aexp_ref,
                xs_ref, w1_ref, w3_ref, w2_ref, y_ref,
                xbuf_ref, ybuf_ref, w1s_ref, w3s_ref, w2s_ref, w13_ref, w2b_ref, xsem, ysem, wsem,
                *, layer, bm):
    nq = nq_ref[0]
    na = nq_ref[1]
    f = w1s_ref.shape[2]
    half = xbuf_ref.shape[2]
    nx = xbuf_ref.shape[0]

    def x_copy(q, slot):
        row = pl.multiple_of(brow_ref[q], CHUNK_ALIGN)
        return pltpu.make_async_copy(xs_ref.at[pl.ds(row, bm), :], xbuf_ref.at[slot], xsem.at[slot])

    def w_copies(j, slot):
        e = aexp_ref[j]
        return (pltpu.make_async_copy(w1_ref.at[layer, e], w1s_ref.at[slot], wsem.at[slot]),
                pltpu.make_async_copy(w3_ref.at[layer, e], w3s_ref.at[slot], wsem.at[slot]),
                pltpu.make_async_copy(w2_ref.at[layer, e], w2s_ref.at[slot], wsem.at[slot]))

    def y_copy(slot, row, off, size):
        return pltpu.make_async_copy(ybuf_ref.at[slot, pl.ds(off, size), :],
                                     y_ref.at[pl.ds(pl.multiple_of(row + off, CHUNK_ALIGN), size), :],
                                     ysem.at[slot])

    def for_each_y_copy(slot, row, valid, fn):
        @pl.when(valid == bm)
        def _():
            fn(y_copy(slot, row, 0, bm))

        @pl.when(valid < bm)
        def _():
            for j in range((bm // CHUNK_ALIGN).bit_length() - 1):
                size = CHUNK_ALIGN << j

                @pl.when((valid & size) != 0)
                def _():
                    fn(y_copy(slot, row, pl.multiple_of(valid & (size - 1), CHUNK_ALIGN), size))

    for c in w_copies(0, 0):
        c.start()
    for q0 in range(nx - 1):
        @pl.when(q0 < nq)
        def _():
            x_copy(q0, q0).start()

    def body(q, carry):
        xslot = q % nx
        yslot = q % 2

        @pl.when(q + (nx - 1) < nq)
        def _():
            x_copy(q + (nx - 1), (q + (nx - 1)) % nx).start()

        @pl.when(bfirst_ref[q] == 1)
        def _():
            j = bord_ref[q]
            ws = j % 2
            for c in w_copies(j, ws):
                c.wait()

            @pl.when(j + 1 < na)
            def _():
                for c in w_copies(j + 1, 1 - ws):
                    c.start()
            w13_ref[:, :f] = w1s_ref[ws].astype(BF16)
            w13_ref[:, f:] = w3s_ref[ws].astype(BF16)
            w2b_ref[...] = w2s_ref[ws].astype(BF16)

        x_copy(q, xslot).wait()

        @pl.when(q >= 2)
        def _():
            for_each_y_copy(yslot, 0, bval_ref[jnp.maximum(q - 2, 0)], lambda c: c.wait())

        x_lo, x_hi = _unpack_halves(xbuf_ref[xslot])
        h13 = (jnp.dot(x_lo, w13_ref[:half, :], preferred_element_type=F32)
               + jnp.dot(x_hi, w13_ref[half:, :], preferred_element_type=F32))
        h = jax.nn.silu(h13[:, :f]) * h13[:, f:]
        ybuf_ref[yslot] = _pack_halves(jnp.dot(h.astype(BF16), w2b_ref[...], preferred_element_type=F32))
        for_each_y_copy(yslot, brow_ref[q], bval_ref[q], lambda c: c.start())
        return carry

    lax.fori_loop(0, nq, body, 0)

    @pl.when(nq >= 2)
    def _():
        for_each_y_copy(nq % 2, 0, bval_ref[jnp.maximum(nq - 2, 0)], lambda c: c.wait())
    for_each_y_copy((nq - 1) % 2, 0, bval_ref[nq - 1], lambda c: c.wait())


def _gmm(xs, tables, w1, w3, w2, layer, bm):
    n_rows, half = xs.shape
    d = 2 * half
    f = w1.shape[-1]
    any_spec = pl.BlockSpec(memory_space=pl.ANY)
    grid_spec = pltpu.PrefetchScalarGridSpec(
        num_scalar_prefetch=len(tables),
        grid=(1,),
        in_specs=[any_spec, any_spec, any_spec, any_spec],
        out_specs=any_spec,
        scratch_shapes=[pltpu.VMEM((GMM_X_SLOTS, bm, half), U32), pltpu.VMEM((2, bm, half), U32),
                        pltpu.VMEM((2, d, f), F32), pltpu.VMEM((2, d, f), F32), pltpu.VMEM((2, f, d), F32),
                        pltpu.VMEM((d, 2 * f), BF16), pltpu.VMEM((f, d), BF16),
                        pltpu.SemaphoreType.DMA((GMM_X_SLOTS,)), pltpu.SemaphoreType.DMA((2,)),
                        pltpu.SemaphoreType.DMA((2,))],
    )
    return pl.pallas_call(
        functools.partial(_gmm_kernel, layer=layer, bm=bm),
        grid_spec=grid_spec,
        out_shape=jax.ShapeDtypeStruct((n_rows, half), U32),
        compiler_params=_params("arbitrary"),
        name="moe_experts",
    )(*tables, xs, w1, w3, w2)


def _combine_kernel(tot_ref, plan_ref, plan_next_ref, yb_ref, pos_ref, gate_ref, x_ref, ws13_ref, ws2_ref,
                    g_ref, b_ref, o_ref, ybuf_ref, sem, *, alpha, tg, ns):
    g = pl.program_id(0)
    ng = pl.num_programs(0)
    slot = g % 2

    def copy_into(sl):
        def make_copy(buf_row, hbm_row, size):
            return pltpu.make_async_copy(yb_ref.at[pl.ds(hbm_row, size), :],
                                         ybuf_ref.at[sl, pl.ds(buf_row, size), :], sem.at[sl])
        return make_copy

    @pl.when(g == 0)
    def _():
        ybuf_ref[...] = jnp.zeros_like(ybuf_ref)
        _start_group_copies(plan_ref, copy_into(0))

    @pl.when(g + 1 < ng)
    def _():
        _start_group_copies(plan_next_ref, copy_into(1 - slot))

    x = x_ref[...]
    f = ws2_ref.shape[0]
    h13 = jnp.dot(x.astype(BF16), ws13_ref[...], preferred_element_type=F32)
    hs = jax.nn.silu(h13[:, :f]) * h13[:, f:]
    shared = jnp.dot(hs.astype(BF16), ws2_ref[...], preferred_element_type=F32)

    _wait_group_rows(tot_ref[g], ns, copy_into(slot))

    pos = pos_ref[...]
    gate = gate_ref[...]
    half = ybuf_ref.shape[2]
    acc_lo = jnp.zeros((tg, half), F32)
    acc_hi = jnp.zeros((tg, half), F32)
    local_cols = lax.broadcasted_iota(jnp.int32, (tg, SORT_CHUNK), 1).astype(F32).astype(BF16)
    gate_b = gate.astype(BF16)
    for c in range(ns // SORT_CHUNK):
        rel = _chunk_relative(pos, c)
        wgt = jnp.zeros((tg, SORT_CHUNK), BF16)
        for k in range(TOP_K):
            wgt = jnp.where(local_cols == rel[:, k:k + 1], gate_b[:, k:k + 1], wgt)
        y_lo, y_hi = _unpack_halves(ybuf_ref[slot, c * SORT_CHUNK:(c + 1) * SORT_CHUNK, :])
        acc_lo = acc_lo + jnp.dot(wgt, y_lo, preferred_element_type=F32)
        acc_hi = acc_hi + jnp.dot(wgt, y_hi, preferred_element_type=F32)
    routed = jnp.concatenate([acc_lo, acc_hi], axis=1)
    o_ref[...] = _layer_norm(alpha * x + (routed + shared), g_ref[...], b_ref[...])


def _combine(yb, pos_tk, gate_tk, plan, tot, x1, ws1, ws3, ws2, g, b, alpha, tg, ns):
    t, d = x1.shape
    ng = t // tg
    f = ws1.shape[1]
    ws13 = jnp.concatenate([ws1, ws3], axis=1).astype(BF16)
    const = lambda i, *_: (0, 0)
    tile = lambda i, *_: (i, 0)
    row = lambda v: v.reshape(1, d)
    grid_spec = pltpu.PrefetchScalarGridSpec(
        num_scalar_prefetch=1,
        grid=(ng,),
        in_specs=[
            pl.BlockSpec((PLAN_WORDS,), lambda i, *_: (i,), memory_space=pltpu.SMEM),
            pl.BlockSpec((PLAN_WORDS,), lambda i, *_: (jnp.minimum(i + 1, ng - 1),),
                         memory_space=pltpu.SMEM),
            pl.BlockSpec(memory_space=pl.ANY),
            pl.BlockSpec((tg, TOP_K), tile),
            pl.BlockSpec((tg, TOP_K), tile),
            pl.BlockSpec((tg, d), tile),
            pl.BlockSpec((d, 2 * f), const),
            pl.BlockSpec((f, d), const),
            pl.BlockSpec((1, d), const),
            pl.BlockSpec((1, d), const),
        ],
        out_specs=pl.BlockSpec((tg, d), tile),
        scratch_shapes=[pltpu.VMEM((2, ns, d // 2), U32), pltpu.SemaphoreType.DMA((2,))],
    )
    return pl.pallas_call(
        functools.partial(_combine_kernel, alpha=alpha, tg=tg, ns=ns),
        grid_spec=grid_spec,
        out_shape=jax.ShapeDtypeStruct((t, d), F32),
        compiler_params=_params("arbitrary"),
        name="moe_combine",
    )(tot, plan, plan, yb, pos_tk, gate_tk, x1, ws13, ws2.astype(BF16), row(g), row(b))


def _moe_layer(x1, layer, router_w, router_bias, w1, w3, w2, ws1, ws3, ws2, g, b, alpha):
    t, d = x1.shape
    tg = min(GROUP_TG, t)
    ng = t // tg
    bm = EXPERT_BM
    ns = -(-(tg * TOP_K + N_EXPERTS * (CHUNK_ALIGN - 1)) // SORT_CHUNK) * SORT_CHUNK
    pos, gate, n8_wide, plan_wide = _route(x1, router_w, router_bias, tg)
    plan3 = plan_wide.reshape(ng, PLAN_ROWS, LANES)
    tot = plan3[:, CHUNK_BITS + 1, 0]

    n8 = n8_wide[:, 0].reshape(ng, N_EXPERTS)
    goff = jnp.cumsum(n8, axis=1) - n8
    seg = jnp.sum(n8, axis=0)
    pend = jnp.cumsum(seg)
    pstart = pend - seg
    gdest = pstart[None, :] + jnp.cumsum(n8, axis=0) - n8
    max_rows = ng * (tg * TOP_K + N_EXPERTS * (CHUNK_ALIGN - 1))
    n_rows = max_rows + bm

    experts = jnp.arange(N_EXPERTS, dtype=jnp.int32)
    nb = (seg + bm - 1) // bm
    cb = jnp.cumsum(nb)
    blk = jnp.arange(max_rows // bm + N_EXPERTS, dtype=jnp.int32)
    blk_expert = jnp.minimum(jnp.sum((cb[None, :] <= blk[:, None]).astype(jnp.int32), axis=1), N_EXPERTS - 1)
    mine = blk_expert[:, None] == experts[None, :]
    of_expert = lambda v: jnp.sum(jnp.where(mine, v[None, :], 0), axis=1)
    local = blk - of_expert(cb - nb)
    active = seg > 0
    ordinal = jnp.cumsum(active.astype(jnp.int32)) - 1
    active_experts = jnp.sum(jnp.where((ordinal[None, :] == experts[:, None]) & active[None, :],
                                       experts[None, :], 0), axis=1)
    tables = [jnp.stack([cb[-1], jnp.sum(active.astype(jnp.int32))]),
              of_expert(pstart) + local * bm,
              jnp.clip(of_expert(seg) - local * bm, 0, bm),
              of_expert(ordinal),
              (local == 0).astype(jnp.int32),
              active_experts]
    tables = [a.astype(jnp.int32) for a in tables]
    zrow = pend[-1:].astype(jnp.int32)

    lists = plan3[:, :CHUNK_BITS, :N_EXPERTS]
    sizes = CHUNK_ALIGN << jnp.arange(CHUNK_BITS, dtype=jnp.int32)
    done = n8[:, None, :] & (sizes[None, :, None] - 1)
    pick = lists[..., None] == jnp.arange(N_EXPERTS, dtype=jnp.int32)
    src = jnp.sum(jnp.where(pick, (goff[:, None, :] + done)[:, :, None, :], 0), axis=-1)
    dst = jnp.sum(jnp.where(pick, (gdest[:, None, :] + done)[:, :, None, :], 0), axis=-1)
    widen = lambda a: jnp.pad(a, ((0, 0), (0, 0), (0, LANES - N_EXPERTS)))
    plan = jnp.concatenate([widen(src), plan3[:, CHUNK_BITS:], widen(dst),
                            jnp.zeros((ng, PLAN_ROWS - CHUNK_BITS, LANES), jnp.int32)],
                           axis=1).astype(jnp.int32).reshape(-1)

    xs = _dispatch(x1, pos, plan, tot, zrow, n_rows, tg, ns, bm)
    yb = _gmm(xs, tables, w1, w3, w2, layer, bm)
    return _combine(yb, pos.T, gate.T, plan, tot, x1, ws1, ws3, ws2, g, b, alpha, tg, ns)


def kernel(x, ln_gain, ln_bias, pool_w, pool_scale, attn_wqkv, attn_bqkv, attn_sinks, attn_wo, attn_bo,
           conv_w_in, conv_w, conv_w_out, router_w, router_bias, expert_w1, expert_w3, expert_w2,
           shared_w1, shared_w3, shared_w2):
    bsz, seq, d = x.shape
    depth = ln_gain.shape[0]
    alpha = float((2 * depth) ** 0.25)
    h = x.reshape(bsz * seq, d)
    for i in range(depth):
        kind = i % N_MIXERS
        j = i // N_MIXERS
        g1, b1 = ln_gain[i, 0], ln_bias[i, 0]
        if kind == 0:
            h = _pool_mixer(h, seq, pool_w[j], pool_scale[j], g1, b1, alpha)
        elif kind == 1:
            h = _swa_mixer(h, seq, attn_wqkv[j], attn_bqkv[j], attn_sinks[j], attn_wo[j], attn_bo[j],
                           g1, b1, alpha)
        else:
            h = _conv_mixer(h, seq, conv_w_in[j], conv_w[j], conv_w_out[j], g1, b1, alpha)
        h = _moe_layer(h, i, router_w[i], router_bias[i], expert_w1, expert_w3, expert_w2,
                       shared_w1[i], shared_w3[i], shared_w2[i], ln_gain[i, 1], ln_bias[i, 1], alpha)
    return h.reshape(bsz, seq, d)
```

```python
import functools

import jax
import jax.numpy as jnp
from jax import lax
from jax.experimental import pallas as pl
from jax.experimental.pallas import tpu as pltpu

POOL_WINDOWS = (2, 4, 8, 16)
N_MIXERS = 3
HEAD_DIM = 64
N_Q_HEADS = 16
N_KV_HEADS = 4
GQA_GROUP = N_Q_HEADS // N_KV_HEADS
WINDOW = 128
ROPE_THETA = 10000.0
N_EXPERTS = 64
TOP_K = 8
N_EXPERT_GROUPS = 8
GROUP_SIZE = N_EXPERTS // N_EXPERT_GROUPS
TOPK_GROUPS = 4
ROUTED_SCALE = 2.5
LN_EPS = 1e-5

LANES = 128
SUBLANES = 8
POOL_HALO = 16
VMEM_LIMIT = 48 * 1024 * 1024

POOL_TT = 512
QKV_TT = 512
ATTN_TQ = 256
CONV_TT = 256
GROUP_TG = 256
CHUNK_ALIGN = SUBLANES
SORT_CHUNK = 256
EXPERT_BM = 512
GMM_X_SLOTS = 3
CHUNK_BITS = 6
PLAN_ROWS = SUBLANES
PLAN_WORDS = 2 * PLAN_ROWS * LANES

U32 = jnp.uint32
HI_MASK = 0xFFFF0000

F32 = jnp.float32
BF16 = jnp.bfloat16
NEG_INF = float("-inf")


def _layer_norm(z, g, b):
    mu = jnp.mean(z, axis=-1, keepdims=True)
    zc = z - mu
    var = jnp.mean(zc * zc, axis=-1, keepdims=True)
    return zc * lax.rsqrt(var + LN_EPS) * g + b


def _params(*sem):
    return pltpu.CompilerParams(dimension_semantics=sem, vmem_limit_bytes=VMEM_LIMIT)


def _pool_kernel(x_ref, halo_ref, w_ref, scale_ref, g_ref, b_ref, o_ref, buf_ref, *, alpha, tt):
    i = pl.program_id(1)
    x = x_ref[...]
    ch = x.shape[1] // len(POOL_WINDOWS)
    buf_ref[0:POOL_HALO, :] = jnp.where(i > 0, halo_ref[...], 0.0)
    buf_ref[POOL_HALO:POOL_HALO + tt, :] = x
    pos = i * tt + lax.broadcasted_iota(jnp.int32, (tt, 1), 0)
    ys = []
    for g, w in enumerate(POOL_WINDOWS):
        c0 = g * ch
        xg = x[:, c0:c0 + ch]
        s = xg
        for j in range(1, w):
            s = s + buf_ref[POOL_HALO - j:POOL_HALO - j + tt, c0:c0 + ch]
        inv_cnt = 1.0 / jnp.minimum(pos + 1, w).astype(F32)
        pooled = s * inv_cnt - xg
        ys.append(jnp.dot(pooled.astype(BF16), w_ref[g], preferred_element_type=F32))
    y = jnp.concatenate(ys, axis=1) * scale_ref[...]
    o_ref[...] = _layer_norm(alpha * x + y, g_ref[...], b_ref[...])


def _pool_mixer(x2, seq, w_grp, scale, g, b, alpha):
    t, d = x2.shape
    tt = min(POOL_TT, seq)
    nt = seq // tt
    nb = t // seq
    row = lambda v: v.reshape(1, d)
    const2 = lambda bi, i: (0, 0)
    return pl.pallas_call(
        functools.partial(_pool_kernel, alpha=alpha, tt=tt),
        grid=(nb, nt),
        in_specs=[
            pl.BlockSpec((tt, d), lambda bi, i: (bi * nt + i, 0)),
            pl.BlockSpec((POOL_HALO, d),
                         lambda bi, i: (jnp.maximum((bi * seq + i * tt) // POOL_HALO - 1, 0), 0)),
            pl.BlockSpec(w_grp.shape, lambda bi, i: (0, 0, 0)),
            pl.BlockSpec((1, d), const2),
            pl.BlockSpec((1, d), const2),
            pl.BlockSpec((1, d), const2),
        ],
        out_specs=pl.BlockSpec((tt, d), lambda bi, i: (bi * nt + i, 0)),
        out_shape=jax.ShapeDtypeStruct((t, d), F32),
        scratch_shapes=[pltpu.VMEM((POOL_HALO + tt, d), F32)],
        compiler_params=_params("parallel", "parallel"),
        name="pool_mixer",
    )(x2, x2, w_grp.astype(BF16), row(scale), row(g), row(b))


def _qkv_kernel(x_ref, w_ref, b_ref, c_ref, s_ref, q_ref, k_ref, v_ref, *, nq, nk):
    y = jnp.dot(x_ref[...].astype(BF16), w_ref[...], preferred_element_type=F32) + b_ref[...]
    c = c_ref[...]
    s = s_ref[...]
    lane = lax.broadcasted_iota(jnp.int32, c.shape, 1)
    first_half = (lane % HEAD_DIM) < (HEAD_DIM // 2)

    def rope(v):
        partner = jnp.where(first_half, pltpu.roll(v, LANES - HEAD_DIM // 2, 1),
                            pltpu.roll(v, HEAD_DIM // 2, 1))
        return v * c + partner * s

    for j in range(nq // LANES):
        q_ref[:, j * LANES:(j + 1) * LANES] = (
            rope(y[:, j * LANES:(j + 1) * LANES]) * (HEAD_DIM ** -0.5)).astype(BF16)
    for j in range(nk // LANES):
        k_ref[:, j * LANES:(j + 1) * LANES] = rope(
            y[:, nq + j * LANES:nq + (j + 1) * LANES]).astype(BF16)
    v_ref[...] = y[:, nq + nk:].astype(BF16)


def _attn_kernel(sink_ref, q_ref, k_ref, kh_ref, v_ref, vh_ref, x_ref, wo_ref, bo_ref, g_ref, b_ref,
                 o_ref, *, alpha, tq):
    i = pl.program_id(1)
    r = lax.broadcasted_iota(jnp.int32, (WINDOW, 2 * WINDOW), 0)
    c = lax.broadcasted_iota(jnp.int32, (WINDOW, 2 * WINDOW), 1)
    dist = r + WINDOW - c
    band = (dist >= 0) & (dist < WINDOW)
    band_at_start = band & ((c >= WINDOW) | (i > 0))
    lane = lax.broadcasted_iota(jnp.int32, (tq, LANES), 1)
    low = lane < HEAD_DIM
    dn = (((1,), (1,)), ((), ()))
    outs = []
    for kv in range(N_KV_HEADS):
        sl = slice(kv * LANES, (kv + 1) * LANES)
        kk = jnp.concatenate([kh_ref[:, sl], k_ref[:, sl]], axis=0)
        vv = jnp.concatenate([vh_ref[:, sl], v_ref[:, sl]], axis=0)
        for p in range(GQA_GROUP // 2):
            c0 = (kv * GQA_GROUP + 2 * p) * HEAD_DIM
            qp = q_ref[:, c0:c0 + LANES]
            res = []
            for half in range(2):
                keep = low if half == 0 else jnp.logical_not(low)
                qh = jnp.where(keep, qp, jnp.zeros_like(qp))
                sc_all = lax.dot_general(qh, kk, dn, preferred_element_type=F32)
                sink = sink_ref[kv * GQA_GROUP + 2 * p + half]
                blocks = []
                for qb in range(tq // WINDOW):
                    r0 = qb * WINDOW
                    sc = sc_all[r0:r0 + WINDOW, r0:r0 + 2 * WINDOW]
                    sc = jnp.where(band_at_start if qb == 0 else band, sc, NEG_INF)
                    m = jnp.maximum(jnp.max(sc, axis=1, keepdims=True), sink)
                    pm = jnp.exp(sc - m)
                    den = jnp.sum(pm, axis=1, keepdims=True) + jnp.exp(sink - m)
                    blocks.append(jnp.dot(pm.astype(BF16), vv[r0:r0 + 2 * WINDOW, :],
                                          preferred_element_type=F32) / den)
                res.append(jnp.concatenate(blocks, axis=0))
            outs.append(jnp.where(low, res[0], res[1]))
    o_all = jnp.concatenate(outs, axis=1).astype(BF16)
    mix = jnp.dot(o_all, wo_ref[...], preferred_element_type=F32) + bo_ref[...]
    o_ref[...] = _layer_norm(alpha * x_ref[...] + mix, g_ref[...], b_ref[...])


def _dup_heads(w, n_heads):
    lead = w.shape[:-1]
    w = w.reshape(lead + (n_heads, 1, HEAD_DIM))
    w = jnp.broadcast_to(w, lead + (n_heads, 2, HEAD_DIM))
    return w.reshape(lead + (n_heads * 2 * HEAD_DIM,))


def _swa_mixer(x2, seq, wqkv, bqkv, sinks, wo, bo, g, b, alpha):
    t, d = x2.shape
    nb = t // seq
    nq = N_Q_HEADS * HEAD_DIM
    nkv = N_KV_HEADS * HEAD_DIM
    nk2 = 2 * nkv
    w_ext = jnp.concatenate([wqkv[:, :nq], _dup_heads(wqkv[:, nq:nq + nkv], N_KV_HEADS),
                             _dup_heads(wqkv[:, nq + nkv:], N_KV_HEADS)], axis=1).astype(BF16)
    b_ext = jnp.concatenate([bqkv[:nq], _dup_heads(bqkv[nq:nq + nkv], N_KV_HEADS),
                             _dup_heads(bqkv[nq + nkv:], N_KV_HEADS)]).reshape(1, -1)
    n_ext = nq + 2 * nk2

    pos = jnp.arange(seq, dtype=F32)
    inv_freq = ROPE_THETA ** (-jnp.arange(0, HEAD_DIM, 2, dtype=F32) / HEAD_DIM)
    ang = pos[:, None] * inv_freq[None, :]
    cos, sin = jnp.cos(ang), jnp.sin(ang)
    cos_t = jnp.concatenate([cos, cos, cos, cos], axis=1)
    sin_t = jnp.concatenate([-sin, sin, -sin, sin], axis=1)

    tt = min(QKV_TT, seq)
    nt = seq // tt
    const2 = lambda i: (0, 0)
    q, k2, v2 = pl.pallas_call(
        functools.partial(_qkv_kernel, nq=nq, nk=nk2),
        grid=(t // tt,),
        in_specs=[
            pl.BlockSpec((tt, d), lambda i: (i, 0)),
            pl.BlockSpec((d, n_ext), const2),
            pl.BlockSpec((1, n_ext), const2),
            pl.BlockSpec((tt, LANES), lambda i: (i % nt, 0)),
            pl.BlockSpec((tt, LANES), lambda i: (i % nt, 0)),
        ],
        out_specs=[pl.BlockSpec((tt, nq), lambda i: (i, 0)),
                   pl.BlockSpec((tt, nk2), lambda i: (i, 0)),
                   pl.BlockSpec((tt, nk2), lambda i: (i, 0))],
        out_shape=[jax.ShapeDtypeStruct((t, nq), BF16),
                   jax.ShapeDtypeStruct((t, nk2), BF16),
                   jax.ShapeDtypeStruct((t, nk2), BF16)],
        compiler_params=_params("parallel"),
        name="swa_qkv",
    )(x2, w_ext, b_ext, cos_t, sin_t)

    tq = min(ATTN_TQ, seq)
    ntq = seq // tq
    main = lambda bi, i: (bi * ntq + i, 0)
    halo = lambda bi, i: (jnp.maximum((bi * seq + i * tq) // WINDOW - 1, 0), 0)
    const = lambda bi, i: (0, 0)
    row = lambda v: v.reshape(1, d)
    return pl.pallas_call(
        functools.partial(_attn_kernel, alpha=alpha, tq=tq),
        grid=(nb, ntq),
        in_specs=[
            pl.BlockSpec(memory_space=pltpu.SMEM),
            pl.BlockSpec((tq, nq), main),
            pl.BlockSpec((tq, nk2), main),
            pl.BlockSpec((WINDOW, nk2), halo),
            pl.BlockSpec((tq, nk2), main),
            pl.BlockSpec((WINDOW, nk2), halo),
            pl.BlockSpec((tq, d), main),
            pl.BlockSpec((nq, d), const),
            pl.BlockSpec((1, d), const),
            pl.BlockSpec((1, d), const),
            pl.BlockSpec((1, d), const),
        ],
        out_specs=pl.BlockSpec((tq, d), main),
        out_shape=jax.ShapeDtypeStruct((t, d), F32),
        compiler_params=_params("parallel", "parallel"),
        name="swa_attention",
    )(sinks, q, k2, k2, v2, v2, x2, wo.astype(BF16), row(bo), row(g), row(b))


def _conv_kernel(x_ref, win_ref, cw_ref, wout_ref, g_ref, b_ref, o_ref, ubuf_ref, *, alpha, tt):
    i = pl.program_id(1)
    d = x_ref.shape[1]

    @pl.when(i == 0)
    def _():
        ubuf_ref[0:SUBLANES, :] = jnp.zeros((SUBLANES, d), F32)

    x = x_ref[...]
    proj = jnp.dot(x.astype(BF16), win_ref[...], preferred_element_type=F32)
    gate_b = proj[:, :d]
    u = proj[:, d:2 * d] * proj[:, 2 * d:]
    ubuf_ref[SUBLANES:SUBLANES + tt, :] = u
    cw = cw_ref[...]
    conv = (cw[0:1, :] * ubuf_ref[SUBLANES - 2:SUBLANES - 2 + tt, :]
            + cw[1:2, :] * ubuf_ref[SUBLANES - 1:SUBLANES - 1 + tt, :]
            + cw[2:3, :] * u)
    ubuf_ref[0:SUBLANES, :] = u[tt - SUBLANES:tt, :]
    y = jnp.dot((gate_b * conv).astype(BF16), wout_ref[...], preferred_element_type=F32)
    o_ref[...] = _layer_norm(alpha * x + y, g_ref[...], b_ref[...])


def _conv_mixer(x2, seq, w_in, conv_w, w_out, g, b, alpha):
    t, d = x2.shape
    nb = t // seq
    tt = min(CONV_TT, seq)
    nt = seq // tt
    const = lambda bi, i: (0, 0)
    row = lambda v: v.reshape(1, d)
    return pl.pallas_call(
        functools.partial(_conv_kernel, alpha=alpha, tt=tt),
        grid=(nb, nt),
        in_specs=[
            pl.BlockSpec((tt, d), lambda bi, i: (bi * nt + i, 0)),
            pl.BlockSpec((d, 3 * d), const),
            pl.BlockSpec(conv_w.shape, const),
            pl.BlockSpec((d, d), const),
            pl.BlockSpec((1, d), const),
            pl.BlockSpec((1, d), const),
        ],
        out_specs=pl.BlockSpec((tt, d), lambda bi, i: (bi * nt + i, 0)),
        out_shape=jax.ShapeDtypeStruct((t, d), F32),
        scratch_shapes=[pltpu.VMEM((SUBLANES + tt, d), F32)],
        compiler_params=_params("arbitrary", "arbitrary"),
        name="conv_mixer",
    )(x2, w_in.astype(BF16), conv_w, w_out.astype(BF16), row(g), row(b))


def _first_argmax(v, rows, n):
    m = jnp.max(v, axis=0, keepdims=True)
    first = jnp.min(jnp.where(v == m, rows, n), axis=0, keepdims=True)
    return m, first


def _route_kernel(x_ref, wt_ref, bias_ref, pos_ref, gate_ref, n8_ref, plan_ref, *, tt):
    x = x_ref[...]
    xh = x.astype(BF16)
    xl = (x - xh.astype(F32)).astype(BF16)
    wt = wt_ref[...]
    wh = wt.astype(BF16)
    wl = (wt - wh.astype(F32)).astype(BF16)
    dn = (((1,), (1,)), ((), ()))
    logits = (lax.dot_general(wh, xh, dn, preferred_element_type=F32)
              + (lax.dot_general(wh, xl, dn, preferred_element_type=F32)
                 + lax.dot_general(wl, xh, dn, preferred_element_type=F32)))
    scores = jax.nn.sigmoid(logits)
    sel = scores + bias_ref[...]

    rows8 = lax.broadcasted_iota(jnp.int32, (GROUP_SIZE, tt), 0)
    gsc = []
    for gi in range(N_EXPERT_GROUPS):
        blk = sel[gi * GROUP_SIZE:(gi + 1) * GROUP_SIZE, :]
        m1, f1 = _first_argmax(blk, rows8, GROUP_SIZE)
        m2 = jnp.max(jnp.where(rows8 == f1, NEG_INF, blk), axis=0, keepdims=True)
        gsc.append(m1 + m2)
    cur = jnp.concatenate(gsc, axis=0)
    rows_g = lax.broadcasted_iota(jnp.int32, (N_EXPERT_GROUPS, tt), 0)
    gmask = jnp.zeros((N_EXPERT_GROUPS, tt), F32)
    for _ in range(TOPK_GROUPS):
        _, f = _first_argmax(cur, rows_g, N_EXPERT_GROUPS)
        hit = rows_g == f
        gmask = jnp.where(hit, 1.0, gmask)
        cur = jnp.where(hit, NEG_INF, cur)
    emask = jnp.concatenate(
        [jnp.broadcast_to(gmask[gi:gi + 1, :], (GROUP_SIZE, tt)) for gi in range(N_EXPERT_GROUPS)],
        axis=0)

    rows_e = lax.broadcasted_iota(jnp.int32, (N_EXPERTS, tt), 0)
    cur = jnp.where(emask > 0.0, sel, NEG_INF)
    idxs, gates = [], []
    member = jnp.zeros((N_EXPERTS, tt), F32)
    for _ in range(TOP_K):
        _, f = _first_argmax(cur, rows_e, N_EXPERTS)
        hit = rows_e == f
        idxs.append(f)
        gates.append(jnp.sum(jnp.where(hit, scores, 0.0), axis=0, keepdims=True))
        member = jnp.where(hit, 1.0, member)
        cur = jnp.where(hit, NEG_INF, cur)
    gate = jnp.concatenate(gates, axis=0)
    gate = gate / jnp.sum(gate, axis=0, keepdims=True) * ROUTED_SCALE

    ta = lax.broadcasted_iota(jnp.int32, (tt, tt), 0)
    tb = lax.broadcasted_iota(jnp.int32, (tt, tt), 1)
    earlier = (ta < tb).astype(BF16)
    before = jnp.dot(member.astype(BF16), earlier, preferred_element_type=F32)
    n = jnp.sum(member, axis=1, keepdims=True)
    n8 = jnp.floor((n + (CHUNK_ALIGN - 1)) * (1.0 / CHUNK_ALIGN)) * CHUNK_ALIGN
    n8_wide = jnp.broadcast_to(n8, (N_EXPERTS, LANES))
    ea = lax.broadcasted_iota(jnp.int32, (N_EXPERTS, N_EXPERTS), 0)
    eb = lax.broadcasted_iota(jnp.int32, (N_EXPERTS, N_EXPERTS), 1)
    lower = (eb < ea).astype(BF16)
    chunk_start = jnp.dot(lower, n8_wide.astype(BF16), preferred_element_type=F32)
    where_to = before + chunk_start[:, 0:1]
    for k in range(TOP_K):
        pk = jnp.sum(jnp.where(rows_e == idxs[k], where_to, 0.0), axis=0, keepdims=True)
        pos_ref[k:k + 1, :] = pk.astype(jnp.int32)
    gate_ref[...] = gate
    n8_int = n8_wide.astype(jnp.int32)
    n8_ref[...] = n8_int

    lane = lax.broadcasted_iota(jnp.int32, (N_EXPERTS, LANES), 1)
    flag = jnp.zeros((N_EXPERTS, LANES), F32)
    for j in range(CHUNK_BITS):
        flag = jnp.where((lane == j) & ((n8_int & (CHUNK_ALIGN << j)) != 0), 1.0, flag)
    rank = jnp.dot(lower, flag.astype(BF16), preferred_element_type=F32)
    e_id = lax.broadcasted_iota(jnp.int32, (N_EXPERTS, LANES), 0)
    for j in range(CHUNK_BITS):
        chosen = (rank[:, j:j + 1] == lane.astype(F32)) & (flag[:, j:j + 1] > 0.0)
        plan_ref[j:j + 1, :] = jnp.sum(jnp.where(chosen, e_id, 0), axis=0, keepdims=True)
    plan_ref[CHUNK_BITS:CHUNK_BITS + 1, :] = jnp.sum(flag, axis=0, keepdims=True).astype(jnp.int32)
    plan_ref[CHUNK_BITS + 1:CHUNK_BITS + 2, :] = jnp.sum(n8_int, axis=0, keepdims=True)


def _route(x1, router_w, router_bias, tg):
    t, d = x1.shape
    row_blk = lambda i: (0, i)
    return pl.pallas_call(
        functools.partial(_route_kernel, tt=tg),
        grid=(t // tg,),
        in_specs=[
            pl.BlockSpec((tg, d), lambda i: (i, 0)),
            pl.BlockSpec((N_EXPERTS, d), lambda i: (0, 0)),
            pl.BlockSpec((N_EXPERTS, 1), lambda i: (0, 0)),
        ],
        out_specs=[pl.BlockSpec((TOP_K, tg), row_blk),
                   pl.BlockSpec((TOP_K, tg), row_blk),
                   pl.BlockSpec((N_EXPERTS, LANES), lambda i: (i, 0)),
                   pl.BlockSpec((PLAN_ROWS, LANES), lambda i: (i, 0))],
        out_shape=[jax.ShapeDtypeStruct((TOP_K, t), jnp.int32),
                   jax.ShapeDtypeStruct((TOP_K, t), F32),
                   jax.ShapeDtypeStruct((t // tg * N_EXPERTS, LANES), jnp.int32),
                   jax.ShapeDtypeStruct((t // tg * PLAN_ROWS, LANES), jnp.int32)],
        compiler_params=_params("parallel"),
        name="moe_router",
    )(x1, router_w.T, router_bias.reshape(N_EXPERTS, 1))


def _pack_halves(v, exact=False):
    h = v.shape[1] // 2
    return _pack_pair(v[:, :h], v[:, h:], exact)


def _pack_pair(lo, hi, exact=False):
    if not exact:
        lo, hi = lo.astype(BF16).astype(F32), hi.astype(BF16).astype(F32)
    lo = lax.bitcast_convert_type(lo, U32)
    hi = lax.bitcast_convert_type(hi, U32)
    return (lo >> 16) | (hi & jnp.uint32(HI_MASK))


def _chunk_relative(pos, c):
    rel = pos - c * SORT_CHUNK
    inside = (rel >= 0) & (rel < SORT_CHUNK)
    return jnp.where(inside, rel, -1).astype(F32).astype(BF16)


def _unpack_halves(w):
    lo = lax.bitcast_convert_type(w << 16, F32).astype(BF16)
    hi = lax.bitcast_convert_type(w & jnp.uint32(HI_MASK), F32).astype(BF16)
    return lo, hi


def _start_group_copies(plan_ref, make_copy):
    for j in range(CHUNK_BITS):
        size = CHUNK_ALIGN << j

        def body(k, carry):
            make_copy(pl.multiple_of(plan_ref[j * LANES + k], CHUNK_ALIGN),
                      pl.multiple_of(plan_ref[(PLAN_ROWS + j) * LANES + k], CHUNK_ALIGN), size).start()
            return carry

        lax.fori_loop(0, plan_ref[CHUNK_BITS * LANES + j], body, 0)


def _wait_group_rows(total_rows, ns, make_copy):
    for j in range(ns.bit_length()):
        size = CHUNK_ALIGN << j
        if size > ns:
            break

        @pl.when((total_rows & size) != 0)
        def _():
            make_copy(0, 0, size).wait()


def _dispatch_kernel(tot_ref, zrow_ref, plan_ref, pos_ref, x_ref, xs_ref, sbuf_ref, zbuf_ref,
                     onehot_ref, sem, zsem, *, tg, ns, bm):
    g = pl.program_id(0)
    ng = pl.num_programs(0)
    slot = g % 2

    @pl.when(g == 0)
    def _():
        zbuf_ref[...] = jnp.zeros_like(zbuf_ref)
        row = pl.multiple_of(zrow_ref[0], CHUNK_ALIGN)
        tail = pltpu.make_async_copy(zbuf_ref, xs_ref.at[pl.ds(row, bm), :], zsem)
        tail.start()
        tail.wait()

    def copy_from(sl):
        def make_copy(src, dst, size):
            return pltpu.make_async_copy(sbuf_ref.at[sl, pl.ds(src, size), :],
                                         xs_ref.at[pl.ds(dst, size), :], sem.at[sl])
        return make_copy

    @pl.when(g >= 2)
    def _():
        _wait_group_rows(tot_ref[g - 2], ns, copy_from(slot))

    xb = x_ref[...].astype(BF16)
    pos = pos_ref[...]
    local_rows = lax.broadcasted_iota(jnp.int32, (SORT_CHUNK, tg), 0).astype(F32).astype(BF16)
    one = jnp.ones((SORT_CHUNK, tg), BF16)
    zero = jnp.zeros((SORT_CHUNK, tg), BF16)
    for c in range(ns // SORT_CHUNK):
        rel = _chunk_relative(pos, c)
        hit = local_rows == rel[0:1, :]
        for k in range(1, TOP_K):
            hit = jnp.logical_or(hit, local_rows == rel[k:k + 1, :])
        onehot_ref[c * SORT_CHUNK:(c + 1) * SORT_CHUNK, :] = jnp.where(hit, one, zero)
    onehot = onehot_ref[...]
    half = x_ref.shape[1] // 2
    for n in range(half // SORT_CHUNK):
        lo = jnp.dot(onehot, xb[:, n * SORT_CHUNK:(n + 1) * SORT_CHUNK], preferred_element_type=F32)
        hi = jnp.dot(onehot, xb[:, half + n * SORT_CHUNK:half + (n + 1) * SORT_CHUNK],
                     preferred_element_type=F32)
        sbuf_ref[slot, :, n * SORT_CHUNK:(n + 1) * SORT_CHUNK] = _pack_pair(lo, hi, exact=True)

    _start_group_copies(plan_ref, copy_from(slot))

    @pl.when(g == ng - 1)
    def _():
        @pl.when(g >= 1)
        def _():
            _wait_group_rows(tot_ref[g - 1], ns, copy_from(1 - slot))
        _wait_group_rows(tot_ref[g], ns, copy_from(slot))


def _dispatch(x1, pos, plan, tot, zrow, n_rows, tg, ns, bm):
    t, d = x1.shape
    grid_spec = pltpu.PrefetchScalarGridSpec(
        num_scalar_prefetch=2,
        grid=(t // tg,),
        in_specs=[
            pl.BlockSpec((PLAN_WORDS,), lambda i, *_: (i,), memory_space=pltpu.SMEM),
            pl.BlockSpec((TOP_K, tg), lambda i, *_: (0, i)),
            pl.BlockSpec((tg, d), lambda i, *_: (i, 0)),
        ],
        out_specs=pl.BlockSpec(memory_space=pl.ANY),
        scratch_shapes=[pltpu.VMEM((2, ns, d // 2), U32), pltpu.VMEM((bm, d // 2), U32),
                        pltpu.VMEM((ns, tg), BF16),
                        pltpu.SemaphoreType.DMA((2,)), pltpu.SemaphoreType.DMA],
    )
    return pl.pallas_call(
        functools.partial(_dispatch_kernel, tg=tg, ns=ns, bm=bm),
        grid_spec=grid_spec,
        out_shape=jax.ShapeDtypeStruct((n_rows, d // 2), U32),
        compiler_params=_params("arbitrary"),
        name="moe_dispatch",
    )(tot, zrow, plan, pos, x1)


def _gmm_kernel(nq_ref, brow_ref, bval_ref, bord_ref, bfirst_ref, aexp_ref,
                xs_ref, w1_ref, w3_ref, w2_ref, y_ref,
                xbuf_ref, ybuf_ref, w1s_ref, w3s_ref, w2s_ref, w13_ref, w2b_ref, xsem, ysem, wsem,
                *, layer, bm):
    nq = nq_ref[0]
    na = nq_ref[1]
    f = w1s_ref.shape[2]
    half = xbuf_ref.shape[2]
    nx = xbuf_ref.shape[0]

    def x_copy(q, slot):
        row = pl.multiple_of(brow_ref[q], CHUNK_ALIGN)
        return pltpu.make_async_copy(xs_ref.at[pl.ds(row, bm), :], xbuf_ref.at[slot], xsem.at[slot])

    def w_copies(j, slot):
        e = aexp_ref[j]
        return (pltpu.make_async_copy(w1_ref.at[layer, e], w1s_ref.at[slot], wsem.at[slot]),
                pltpu.make_async_copy(w3_ref.at[layer, e], w3s_ref.at[slot], wsem.at[slot]),
                pltpu.make_async_copy(w2_ref.at[layer, e], w2s_ref.at[slot], wsem.at[slot]))

    def y_copy(slot, row, off, size):
        return pltpu.make_async_copy(ybuf_ref.at[slot, pl.ds(off, size), :],
                                     y_ref.at[pl.ds(pl.multiple_of(row + off, CHUNK_ALIGN), size), :],
                                     ysem.at[slot])

    def for_each_y_copy(slot, row, valid, fn):
        @pl.when(valid == bm)
        def _():
            fn(y_copy(slot, row, 0, bm))

        @pl.when(valid < bm)
        def _():
            for j in range((bm // CHUNK_ALIGN).bit_length() - 1):
                size = CHUNK_ALIGN << j

                @pl.when((valid & size) != 0)
                def _():
                    fn(y_copy(slot, row, pl.multiple_of(valid & (size - 1), CHUNK_ALIGN), size))

    for c in w_copies(0, 0):
        c.start()
    for q0 in range(nx - 1):
        @pl.when(q0 < nq)
        def _():
            x_copy(q0, q0).start()

    def body(q, carry):
        xslot = q % nx
        yslot = q % 2

        @pl.when(q + (nx - 1) < nq)
        def _():
            x_copy(q + (nx - 1), (q + (nx - 1)) % nx).start()

        @pl.when(bfirst_ref[q] == 1)
        def _():
            j = bord_ref[q]
            ws = j % 2
            for c in w_copies(j, ws):
                c.wait()

            @pl.when(j + 1 < na)
            def _():
                for c in w_copies(j + 1, 1 - ws):
                    c.start()
            w13_ref[:, :f] = w1s_ref[ws].astype(BF16)
            w13_ref[:, f:] = w3s_ref[ws].astype(BF16)
            w2b_ref[...] = w2s_ref[ws].astype(BF16)

        x_copy(q, xslot).wait()

        @pl.when(q >= 2)
        def _():
            for_each_y_copy(yslot, 0, bval_ref[jnp.maximum(q - 2, 0)], lambda c: c.wait())

        x_lo, x_hi = _unpack_halves(xbuf_ref[xslot])
        h13 = (jnp.dot(x_lo, w13_ref[:half, :], preferred_element_type=F32)
               + jnp.dot(x_hi, w13_ref[half:, :], preferred_element_type=F32))
        h = jax.nn.silu(h13[:, :f]) * h13[:, f:]
        ybuf_ref[yslot] = _pack_halves(jnp.dot(h.astype(BF16), w2b_ref[...], preferred_element_type=F32))
        for_each_y_copy(yslot, brow_ref[q], bval_ref[q], lambda c: c.start())
        return carry

    lax.fori_loop(0, nq, body, 0)

    @pl.when(nq >= 2)
    def _():
        for_each_y_copy(nq % 2, 0, bval_ref[jnp.maximum(nq - 2, 0)], lambda c: c.wait())
    for_each_y_copy((nq - 1) % 2, 0, bval_ref[nq - 1], lambda c: c.wait())


def _gmm(xs, tables, w1, w3, w2, layer, bm):
    n_rows, half = xs.shape
    d = 2 * half
    f = w1.shape[-1]
    any_spec = pl.BlockSpec(memory_space=pl.ANY)
    grid_spec = pltpu.PrefetchScalarGridSpec(
        num_scalar_prefetch=len(tables),
        grid=(1,),
        in_specs=[any_spec, any_spec, any_spec, any_spec],
        out_specs=any_spec,
        scratch_shapes=[pltpu.VMEM((GMM_X_SLOTS, bm, half), U32), pltpu.VMEM((2, bm, half), U32),
                        pltpu.VMEM((2, d, f), F32), pltpu.VMEM((2, d, f), F32), pltpu.VMEM((2, f, d), F32),
                        pltpu.VMEM((d, 2 * f), BF16), pltpu.VMEM((f, d), BF16),
                        pltpu.SemaphoreType.DMA((GMM_X_SLOTS,)), pltpu.SemaphoreType.DMA((2,)),
                        pltpu.SemaphoreType.DMA((2,))],
    )
    return pl.pallas_call(
        functools.partial(_gmm_kernel, layer=layer, bm=bm),
        grid_spec=grid_spec,
        out_shape=jax.ShapeDtypeStruct((n_rows, half), U32),
        compiler_params=_params("arbitrary"),
        name="moe_experts",
    )(*tables, xs, w1, w3, w2)


def _combine_kernel(tot_ref, plan_ref, plan_next_ref, yb_ref, pos_ref, gate_ref, x_ref, ws13_ref, ws2_ref,
                    g_ref, b_ref, o_ref, ybuf_ref, wgt_ref, ylo_ref, yhi_ref, sem, *, alpha, tg, ns):
    g = pl.program_id(0)
    ng = pl.num_programs(0)
    slot = g % 2

    def copy_into(sl):
        def make_copy(buf_row, hbm_row, size):
            return pltpu.make_async_copy(yb_ref.at[pl.ds(hbm_row, size), :],
                                         ybuf_ref.at[sl, pl.ds(buf_row, size), :], sem.at[sl])
        return make_copy

    @pl.when(g == 0)
    def _():
        ybuf_ref[...] = jnp.zeros_like(ybuf_ref)
        _start_group_copies(plan_ref, copy_into(0))

    @pl.when(g + 1 < ng)
    def _():
        _start_group_copies(plan_next_ref, copy_into(1 - slot))

    x = x_ref[...]
    f = ws2_ref.shape[0]
    h13 = jnp.dot(x.astype(BF16), ws13_ref[...], preferred_element_type=F32)
    hs = jax.nn.silu(h13[:, :f]) * h13[:, f:]
    shared = jnp.dot(hs.astype(BF16), ws2_ref[...], preferred_element_type=F32)

    _wait_group_rows(tot_ref[g], ns, copy_into(slot))

    pos = pos_ref[...]
    gate_b = gate_ref[...].astype(BF16)
    local_rows = lax.broadcasted_iota(jnp.int32, (SORT_CHUNK, tg), 0).astype(F32).astype(BF16)
    for c in range(ns // SORT_CHUNK):
        rel = _chunk_relative(pos, c)
        wgt = jnp.zeros((SORT_CHUNK, tg), BF16)
        for k in range(TOP_K):
            wgt = jnp.where(local_rows == rel[k:k + 1, :], gate_b[k:k + 1, :], wgt)
        wgt_ref[c * SORT_CHUNK:(c + 1) * SORT_CHUNK, :] = wgt
        y_lo, y_hi = _unpack_halves(ybuf_ref[slot, c * SORT_CHUNK:(c + 1) * SORT_CHUNK, :])
        ylo_ref[c * SORT_CHUNK:(c + 1) * SORT_CHUNK, :] = y_lo
        yhi_ref[c * SORT_CHUNK:(c + 1) * SORT_CHUNK, :] = y_hi
    wgt_all = wgt_ref[...]
    dn = (((0,), (0,)), ((), ()))
    routed = jnp.concatenate([lax.dot_general(wgt_all, ylo_ref[...], dn, preferred_element_type=F32),
                              lax.dot_general(wgt_all, yhi_ref[...], dn, preferred_element_type=F32)],
                             axis=1)
    o_ref[...] = _layer_norm(alpha * x + (routed + shared), g_ref[...], b_ref[...])


def _combine(yb, pos_kt, gate_kt, plan, tot, x1, ws1, ws3, ws2, g, b, alpha, tg, ns):
    t, d = x1.shape
    ng = t // tg
    f = ws1.shape[1]
    ws13 = jnp.concatenate([ws1, ws3], axis=1).astype(BF16)
    const = lambda i, *_: (0, 0)
    tile = lambda i, *_: (i, 0)
    row = lambda v: v.reshape(1, d)
    grid_spec = pltpu.PrefetchScalarGridSpec(
        num_scalar_prefetch=1,
        grid=(ng,),
        in_specs=[
            pl.BlockSpec((PLAN_WORDS,), lambda i, *_: (i,), memory_space=pltpu.SMEM),
            pl.BlockSpec((PLAN_WORDS,), lambda i, *_: (jnp.minimum(i + 1, ng - 1),),
                         memory_space=pltpu.SMEM),
            pl.BlockSpec(memory_space=pl.ANY),
            pl.BlockSpec((TOP_K, tg), lambda i, *_: (0, i)),
            pl.BlockSpec((TOP_K, tg), lambda i, *_: (0, i)),
            pl.BlockSpec((tg, d), tile),
            pl.BlockSpec((d, 2 * f), const),
            pl.BlockSpec((f, d), const),
            pl.BlockSpec((1, d), const),
            pl.BlockSpec((1, d), const),
        ],
        out_specs=pl.BlockSpec((tg, d), tile),
        scratch_shapes=[pltpu.VMEM((2, ns, d // 2), U32), pltpu.VMEM((ns, tg), BF16),
                        pltpu.VMEM((ns, d // 2), BF16), pltpu.VMEM((ns, d // 2), BF16),
                        pltpu.SemaphoreType.DMA((2,))],
    )
    return pl.pallas_call(
        functools.partial(_combine_kernel, alpha=alpha, tg=tg, ns=ns),
        grid_spec=grid_spec,
        out_shape=jax.ShapeDtypeStruct((t, d), F32),
        compiler_params=_params("arbitrary"),
        name="moe_combine",
    )(tot, plan, plan, yb, pos_kt, gate_kt, x1, ws13, ws2.astype(BF16), row(g), row(b))


def _moe_layer(x1, layer, router_w, router_bias, w1, w3, w2, ws1, ws3, ws2, g, b, alpha):
    t, d = x1.shape
    tg = min(GROUP_TG, t)
    ng = t // tg
    bm = EXPERT_BM
    ns = -(-(tg * TOP_K + N_EXPERTS * (CHUNK_ALIGN - 1)) // SORT_CHUNK) * SORT_CHUNK
    pos, gate, n8_wide, plan_wide = _route(x1, router_w, router_bias, tg)
    plan3 = plan_wide.reshape(ng, PLAN_ROWS, LANES)
    tot = plan3[:, CHUNK_BITS + 1, 0]

    n8 = n8_wide[:, 0].reshape(ng, N_EXPERTS)
    goff = jnp.cumsum(n8, axis=1) - n8
    seg = jnp.sum(n8, axis=0)
    pend = jnp.cumsum(seg)
    pstart = pend - seg
    gdest = pstart[None, :] + jnp.cumsum(n8, axis=0) - n8
    max_rows = ng * (tg * TOP_K + N_EXPERTS * (CHUNK_ALIGN - 1))
    n_rows = max_rows + bm

    experts = jnp.arange(N_EXPERTS, dtype=jnp.int32)
    nb = (seg + bm - 1) // bm
    cb = jnp.cumsum(nb)
    blk = jnp.arange(max_rows // bm + N_EXPERTS, dtype=jnp.int32)
    blk_expert = jnp.minimum(jnp.sum((cb[None, :] <= blk[:, None]).astype(jnp.int32), axis=1), N_EXPERTS - 1)
    mine = blk_expert[:, None] == experts[None, :]
    of_expert = lambda v: jnp.sum(jnp.where(mine, v[None, :], 0), axis=1)
    local = blk - of_expert(cb - nb)
    active = seg > 0
    ordinal = jnp.cumsum(active.astype(jnp.int32)) - 1
    active_experts = jnp.sum(jnp.where((ordinal[None, :] == experts[:, None]) & active[None, :],
                                       experts[None, :], 0), axis=1)
    tables = [jnp.stack([cb[-1], jnp.sum(active.astype(jnp.int32))]),
              of_expert(pstart) + local * bm,
              jnp.clip(of_expert(seg) - local * bm, 0, bm),
              of_expert(ordinal),
              (local == 0).astype(jnp.int32),
              active_experts]
    tables = [a.astype(jnp.int32) for a in tables]
    zrow = pend[-1:].astype(jnp.int32)

    lists = plan3[:, :CHUNK_BITS, :N_EXPERTS]
    sizes = CHUNK_ALIGN << jnp.arange(CHUNK_BITS, dtype=jnp.int32)
    done = n8[:, None, :] & (sizes[None, :, None] - 1)
    pick = lists[..., None] == jnp.arange(N_EXPERTS, dtype=jnp.int32)
    src = jnp.sum(jnp.where(pick, (goff[:, None, :] + done)[:, :, None, :], 0), axis=-1)
    dst = jnp.sum(jnp.where(pick, (gdest[:, None, :] + done)[:, :, None, :], 0), axis=-1)
    widen = lambda a: jnp.pad(a, ((0, 0), (0, 0), (0, LANES - N_EXPERTS)))
    plan = jnp.concatenate([widen(src), plan3[:, CHUNK_BITS:], widen(dst),
                            jnp.zeros((ng, PLAN_ROWS - CHUNK_BITS, LANES), jnp.int32)],
                           axis=1).astype(jnp.int32).reshape(-1)

    xs = _dispatch(x1, pos, plan, tot, zrow, n_rows, tg, ns, bm)
    yb = _gmm(xs, tables, w1, w3, w2, layer, bm)
    return _combine(yb, pos, gate, plan, tot, x1, ws1, ws3, ws2, g, b, alpha, tg, ns)


def kernel(x, ln_gain, ln_bias, pool_w, pool_scale, attn_wqkv, attn_bqkv, attn_sinks, attn_wo, attn_bo,
           conv_w_in, conv_w, conv_w_out, router_w, router_bias, expert_w1, expert_w3, expert_w2,
           shared_w1, shared_w3, shared_w2):
    bsz, seq, d = x.shape
    depth = ln_gain.shape[0]
    alpha = float((2 * depth) ** 0.25)
    h = x.reshape(bsz * seq, d)
    for i in range(depth):
        kind = i % N_MIXERS
        j = i // N_MIXERS
        g1, b1 = ln_gain[i, 0], ln_bias[i, 0]
        if kind == 0:
            h = _pool_mixer(h, seq, pool_w[j], pool_scale[j], g1, b1, alpha)
        elif kind == 1:
            h = _swa_mixer(h, seq, attn_wqkv[j], attn_bqkv[j], attn_sinks[j], attn_wo[j], attn_bo[j],
                           g1, b1, alpha)
        else:
            h = _conv_mixer(h, seq, conv_w_in[j], conv_w[j], conv_w_out[j], g1, b1, alpha)
        h = _moe_layer(h, i, router_w[i], router_bias[i], expert_w1, expert_w3, expert_w2,
                       shared_w1[i], shared_w3[i], shared_w2[i], ln_gain[i, 1], ln_bias[i, 1], alpha)
    return h.reshape(bsz, seq, d)
```

```python
import functools

import jax
import jax.numpy as jnp
from jax import lax
from jax.experimental import pallas as pl
from jax.experimental.pallas import tpu as pltpu

POOL_WINDOWS = (2, 4, 8, 16)
N_MIXERS = 3
HEAD_DIM = 64
N_Q_HEADS = 16
N_KV_HEADS = 4
GQA_GROUP = N_Q_HEADS // N_KV_HEADS
WINDOW = 128
ROPE_THETA = 10000.0
N_EXPERTS = 64
TOP_K = 8
N_EXPERT_GROUPS = 8
GROUP_SIZE = N_EXPERTS // N_EXPERT_GROUPS
TOPK_GROUPS = 4
ROUTED_SCALE = 2.5
LN_EPS = 1e-5

LANES = 128
SUBLANES = 8
POOL_HALO = 16
VMEM_LIMIT = 48 * 1024 * 1024

POOL_TT = 512
QKV_TT = 512
ATTN_TQ = 512
CONV_TT = 512
GROUP_TG = 256
ROUTE_GROUPS = 1
CHUNK_ALIGN = SUBLANES
SORT_CHUNK = 256
EXPERT_BM = 512
GMM_X_SLOTS = 3
CHUNK_BITS = 6
PLAN_ROWS = SUBLANES
PLAN_WORDS = 2 * PLAN_ROWS * LANES

U32 = jnp.uint32
HI_MASK = 0xFFFF0000

F32 = jnp.float32
BF16 = jnp.bfloat16
NEG_INF = float("-inf")


def _layer_norm(z, g, b):
    mu = jnp.mean(z, axis=-1, keepdims=True)
    zc = z - mu
    var = jnp.mean(zc * zc, axis=-1, keepdims=True)
    return zc * lax.rsqrt(var + LN_EPS) * g + b


def _params(*sem):
    return pltpu.CompilerParams(dimension_semantics=sem, vmem_limit_bytes=VMEM_LIMIT)


def _pool_kernel(x_ref, halo_ref, w_ref, scale_ref, g_ref, b_ref, o_ref, buf_ref, *, alpha, tt):
    i = pl.program_id(1)
    x = x_ref[...]
    ch = x.shape[1] // len(POOL_WINDOWS)
    buf_ref[0:POOL_HALO, :] = jnp.where(i > 0, halo_ref[...], 0.0)
    buf_ref[POOL_HALO:POOL_HALO + tt, :] = x
    pos = i * tt + lax.broadcasted_iota(jnp.int32, (tt, 1), 0)
    ys = []
    for g, w in enumerate(POOL_WINDOWS):
        c0 = g * ch
        xg = x[:, c0:c0 + ch]
        s = xg
        for j in range(1, w):
            s = s + buf_ref[POOL_HALO - j:POOL_HALO - j + tt, c0:c0 + ch]
        inv_cnt = 1.0 / jnp.minimum(pos + 1, w).astype(F32)
        pooled = s * inv_cnt - xg
        ys.append(jnp.dot(pooled.astype(BF16), w_ref[g], preferred_element_type=F32))
    y = jnp.concatenate(ys, axis=1) * scale_ref[...]
    o_ref[...] = _layer_norm(alpha * x + y, g_ref[...], b_ref[...])


def _pool_mixer(x2, seq, w_grp, scale, g, b, alpha):
    t, d = x2.shape
    tt = min(POOL_TT, seq)
    nt = seq // tt
    nb = t // seq
    row = lambda v: v.reshape(1, d)
    const2 = lambda bi, i: (0, 0)
    return pl.pallas_call(
        functools.partial(_pool_kernel, alpha=alpha, tt=tt),
        grid=(nb, nt),
        in_specs=[
            pl.BlockSpec((tt, d), lambda bi, i: (bi * nt + i, 0)),
            pl.BlockSpec((POOL_HALO, d),
                         lambda bi, i: (jnp.maximum((bi * seq + i * tt) // POOL_HALO - 1, 0), 0)),
            pl.BlockSpec(w_grp.shape, lambda bi, i: (0, 0, 0)),
            pl.BlockSpec((1, d), const2),
            pl.BlockSpec((1, d), const2),
            pl.BlockSpec((1, d), const2),
        ],
        out_specs=pl.BlockSpec((tt, d), lambda bi, i: (bi * nt + i, 0)),
        out_shape=jax.ShapeDtypeStruct((t, d), F32),
        scratch_shapes=[pltpu.VMEM((POOL_HALO + tt, d), F32)],
        compiler_params=_params("parallel", "parallel"),
        name="pool_mixer",
    )(x2, x2, w_grp.astype(BF16), row(scale), row(g), row(b))


def _qkv_kernel(x_ref, w_ref, b_ref, c_ref, s_ref, q_ref, k_ref, v_ref, *, nq, nk):
    y = jnp.dot(x_ref[...].astype(BF16), w_ref[...], preferred_element_type=F32) + b_ref[...]
    c = c_ref[...]
    s = s_ref[...]
    lane = lax.broadcasted_iota(jnp.int32, c.shape, 1)
    first_half = (lane % HEAD_DIM) < (HEAD_DIM // 2)

    def rope(v):
        partner = jnp.where(first_half, pltpu.roll(v, LANES - HEAD_DIM // 2, 1),
                            pltpu.roll(v, HEAD_DIM // 2, 1))
        return v * c + partner * s

    for j in range(nq // LANES):
        q_ref[:, j * LANES:(j + 1) * LANES] = (
            rope(y[:, j * LANES:(j + 1) * LANES]) * (HEAD_DIM ** -0.5)).astype(BF16)
    for j in range(nk // LANES):
        k_ref[:, j * LANES:(j + 1) * LANES] = rope(
            y[:, nq + j * LANES:nq + (j + 1) * LANES]).astype(BF16)
    v_ref[...] = y[:, nq + nk:].astype(BF16)


def _attn_kernel(sink_ref, q_ref, k_ref, kh_ref, v_ref, vh_ref, x_ref, wo_ref, bo_ref, g_ref, b_ref,
                 o_ref, *, alpha, tq):
    i = pl.program_id(1)
    nqb = tq // WINDOW
    r = lax.broadcasted_iota(jnp.int32, (tq, 2 * WINDOW), 0)
    c = lax.broadcasted_iota(jnp.int32, (tq, 2 * WINDOW), 1)
    dist = (r & (WINDOW - 1)) + WINDOW - c
    valid = (dist >= 0) & (dist < WINDOW) & ((c >= WINDOW) | (r >= WINDOW) | (i > 0))
    lane = lax.broadcasted_iota(jnp.int32, (tq, LANES), 1)
    low = lane < HEAD_DIM
    dn = (((1,), (1,)), ((), ()))
    outs = []
    for kv in range(N_KV_HEADS):
        sl = slice(kv * LANES, (kv + 1) * LANES)
        kk = jnp.concatenate([kh_ref[:, sl], k_ref[:, sl]], axis=0)
        vv = jnp.concatenate([vh_ref[:, sl], v_ref[:, sl]], axis=0)
        for p in range(GQA_GROUP // 2):
            c0 = (kv * GQA_GROUP + 2 * p) * HEAD_DIM
            qp = q_ref[:, c0:c0 + LANES]
            res = []
            for half in range(2):
                keep = low if half == 0 else jnp.logical_not(low)
                qh = jnp.where(keep, qp, jnp.zeros_like(qp))
                sc_all = lax.dot_general(qh, kk, dn, preferred_element_type=F32)
                sink = sink_ref[kv * GQA_GROUP + 2 * p + half]
                sc = jnp.concatenate([sc_all[qb * WINDOW:(qb + 1) * WINDOW, qb * WINDOW:(qb + 2) * WINDOW]
                                      for qb in range(nqb)], axis=0)
                sc = jnp.where(valid, sc, NEG_INF)
                m = jnp.maximum(jnp.max(sc, axis=1, keepdims=True), sink)
                pm = jnp.exp(sc - m)
                rden = 1.0 / (jnp.sum(pm, axis=1, keepdims=True) + jnp.exp(sink - m))
                pm = pm.astype(BF16)
                res.append(jnp.concatenate(
                    [jnp.dot(pm[qb * WINDOW:(qb + 1) * WINDOW, :], vv[qb * WINDOW:(qb + 2) * WINDOW, :],
                             preferred_element_type=F32) for qb in range(nqb)], axis=0) * rden)
            outs.append(jnp.where(low, res[0], res[1]))
    o_all = jnp.concatenate(outs, axis=1).astype(BF16)
    mix = jnp.dot(o_all, wo_ref[...], preferred_element_type=F32) + bo_ref[...]
    o_ref[...] = _layer_norm(alpha * x_ref[...] + mix, g_ref[...], b_ref[...])


def _dup_heads(w, n_heads):
    lead = w.shape[:-1]
    w = w.reshape(lead + (n_heads, 1, HEAD_DIM))
    w = jnp.broadcast_to(w, lead + (n_heads, 2, HEAD_DIM))
    return w.reshape(lead + (n_heads * 2 * HEAD_DIM,))


def _swa_mixer(x2, seq, wqkv, bqkv, sinks, wo, bo, g, b, alpha):
    t, d = x2.shape
    nb = t // seq
    nq = N_Q_HEADS * HEAD_DIM
    nkv = N_KV_HEADS * HEAD_DIM
    nk2 = 2 * nkv
    w_ext = jnp.concatenate([wqkv[:, :nq], _dup_heads(wqkv[:, nq:nq + nkv], N_KV_HEADS),
                             _dup_heads(wqkv[:, nq + nkv:], N_KV_HEADS)], axis=1).astype(BF16)
    b_ext = jnp.concatenate([bqkv[:nq], _dup_heads(bqkv[nq:nq + nkv], N_KV_HEADS),
                             _dup_heads(bqkv[nq + nkv:], N_KV_HEADS)]).reshape(1, -1)
    n_ext = nq + 2 * nk2

    pos = jnp.arange(seq, dtype=F32)
    inv_freq = ROPE_THETA ** (-jnp.arange(0, HEAD_DIM, 2, dtype=F32) / HEAD_DIM)
    ang = pos[:, None] * inv_freq[None, :]
    cos, sin = jnp.cos(ang), jnp.sin(ang)
    cos_t = jnp.concatenate([cos, cos, cos, cos], axis=1)
    sin_t = jnp.concatenate([-sin, sin, -sin, sin], axis=1)

    tt = min(QKV_TT, seq)
    nt = seq // tt
    const2 = lambda i: (0, 0)
    q, k2, v2 = pl.pallas_call(
        functools.partial(_qkv_kernel, nq=nq, nk=nk2),
        grid=(t // tt,),
        in_specs=[
            pl.BlockSpec((tt, d), lambda i: (i, 0)),
            pl.BlockSpec((d, n_ext), const2),
            pl.BlockSpec((1, n_ext), const2),
            pl.BlockSpec((tt, LANES), lambda i: (i % nt, 0)),
            pl.BlockSpec((tt, LANES), lambda i: (i % nt, 0)),
        ],
        out_specs=[pl.BlockSpec((tt, nq), lambda i: (i, 0)),
                   pl.BlockSpec((tt, nk2), lambda i: (i, 0)),
                   pl.BlockSpec((tt, nk2), lambda i: (i, 0))],
        out_shape=[jax.ShapeDtypeStruct((t, nq), BF16),
                   jax.ShapeDtypeStruct((t, nk2), BF16),
                   jax.ShapeDtypeStruct((t, nk2), BF16)],
        compiler_params=_params("parallel"),
        name="swa_qkv",
    )(x2, w_ext, b_ext, cos_t, sin_t)

    tq = min(ATTN_TQ, seq)
    ntq = seq // tq
    main = lambda bi, i: (bi * ntq + i, 0)
    halo = lambda bi, i: (jnp.maximum((bi * seq + i * tq) // WINDOW - 1, 0), 0)
    const = lambda bi, i: (0, 0)
    row = lambda v: v.reshape(1, d)
    return pl.pallas_call(
        functools.partial(_attn_kernel, alpha=alpha, tq=tq),
        grid=(nb, ntq),
        in_specs=[
            pl.BlockSpec(memory_space=pltpu.SMEM),
            pl.BlockSpec((tq, nq), main),
            pl.BlockSpec((tq, nk2), main),
            pl.BlockSpec((WINDOW, nk2), halo),
            pl.BlockSpec((tq, nk2), main),
            pl.BlockSpec((WINDOW, nk2), halo),
            pl.BlockSpec((tq, d), main),
            pl.BlockSpec((nq, d), const),
            pl.BlockSpec((1, d), const),
            pl.BlockSpec((1, d), const),
            pl.BlockSpec((1, d), const),
        ],
        out_specs=pl.BlockSpec((tq, d), main),
        out_shape=jax.ShapeDtypeStruct((t, d), F32),
        compiler_params=_params("parallel", "parallel"),
        name="swa_attention",
    )(sinks, q, k2, k2, v2, v2, x2, wo.astype(BF16), row(bo), row(g), row(b))


def _conv_kernel(x_ref, win_ref, cw_ref, wout_ref, g_ref, b_ref, o_ref, ubuf_ref, *, alpha, tt):
    i = pl.program_id(1)
    d = x_ref.shape[1]

    @pl.when(i == 0)
    def _():
        ubuf_ref[0:SUBLANES, :] = jnp.zeros((SUBLANES, d), F32)

    x = x_ref[...]
    proj = jnp.dot(x.astype(BF16), win_ref[...], preferred_element_type=F32)
    gate_b = proj[:, :d]
    u = proj[:, d:2 * d] * proj[:, 2 * d:]
    ubuf_ref[SUBLANES:SUBLANES + tt, :] = u
    cw = cw_ref[...]
    conv = (cw[0:1, :] * ubuf_ref[SUBLANES - 2:SUBLANES - 2 + tt, :]
            + cw[1:2, :] * ubuf_ref[SUBLANES - 1:SUBLANES - 1 + tt, :]
            + cw[2:3, :] * u)
    ubuf_ref[0:SUBLANES, :] = u[tt - SUBLANES:tt, :]
    y = jnp.dot((gate_b * conv).astype(BF16), wout_ref[...], preferred_element_type=F32)
    o_ref[...] = _layer_norm(alpha * x + y, g_ref[...], b_ref[...])


def _conv_mixer(x2, seq, w_in, conv_w, w_out, g, b, alpha):
    t, d = x2.shape
    nb = t // seq
    tt = min(CONV_TT, seq)
    nt = seq // tt
    const = lambda bi, i: (0, 0)
    row = lambda v: v.reshape(1, d)
    return pl.pallas_call(
        functools.partial(_conv_kernel, alpha=alpha, tt=tt),
        grid=(nb, nt),
        in_specs=[
            pl.BlockSpec((tt, d), lambda bi, i: (bi * nt + i, 0)),
            pl.BlockSpec((d, 3 * d), const),
            pl.BlockSpec(conv_w.shape, const),
            pl.BlockSpec((d, d), const),
            pl.BlockSpec((1, d), const),
            pl.BlockSpec((1, d), const),
        ],
        out_specs=pl.BlockSpec((tt, d), lambda bi, i: (bi * nt + i, 0)),
        out_shape=jax.ShapeDtypeStruct((t, d), F32),
        scratch_shapes=[pltpu.VMEM((SUBLANES + tt, d), F32)],
        compiler_params=_params("arbitrary", "arbitrary"),
        name="conv_mixer",
    )(x2, w_in.astype(BF16), conv_w, w_out.astype(BF16), row(g), row(b))


def _first_argmax(v, rows, n):
    m = jnp.max(v, axis=0, keepdims=True)
    first = jnp.min(jnp.where(v == m, rows, n), axis=0, keepdims=True)
    return m, first


def _route_kernel(x_ref, wt_ref, bias_ref, pos_ref, gate_ref, n8_ref, plan_ref, *, tt, groups):
    for s in range(groups):
        _route_group(x_ref.at[pl.ds(s * tt, tt), :], wt_ref, bias_ref,
                     pos_ref.at[:, pl.ds(s * tt, tt)], gate_ref.at[:, pl.ds(s * tt, tt)],
                     n8_ref.at[pl.ds(s * N_EXPERTS, N_EXPERTS), :],
                     plan_ref.at[pl.ds(s * PLAN_ROWS, PLAN_ROWS), :], tt=tt)


def _route_group(x_ref, wt_ref, bias_ref, pos_ref, gate_ref, n8_ref, plan_ref, *, tt):
    x = x_ref[...]
    xh = x.astype(BF16)
    xl = (x - xh.astype(F32)).astype(BF16)
    wt = wt_ref[...]
    wh = wt.astype(BF16)
    wl = (wt - wh.astype(F32)).astype(BF16)
    dn = (((1,), (1,)), ((), ()))
    logits = (lax.dot_general(wh, xh, dn, preferred_element_type=F32)
              + (lax.dot_general(wh, xl, dn, preferred_element_type=F32)
                 + lax.dot_general(wl, xh, dn, preferred_element_type=F32)))
    scores = jax.nn.sigmoid(logits)
    sel = scores + bias_ref[...]

    rows8 = lax.broadcasted_iota(jnp.int32, (GROUP_SIZE, tt), 0)
    gsc = []
    for gi in range(N_EXPERT_GROUPS):
        blk = sel[gi * GROUP_SIZE:(gi + 1) * GROUP_SIZE, :]
        m1, f1 = _first_argmax(blk, rows8, GROUP_SIZE)
        m2 = jnp.max(jnp.where(rows8 == f1, NEG_INF, blk), axis=0, keepdims=True)
        gsc.append(m1 + m2)
    cur = jnp.concatenate(gsc, axis=0)
    rows_g = lax.broadcasted_iota(jnp.int32, (N_EXPERT_GROUPS, tt), 0)
    gmask = jnp.zeros((N_EXPERT_GROUPS, tt), F32)
    for _ in range(TOPK_GROUPS):
        _, f = _first_argmax(cur, rows_g, N_EXPERT_GROUPS)
        hit = rows_g == f
        gmask = jnp.where(hit, 1.0, gmask)
        cur = jnp.where(hit, NEG_INF, cur)
    emask = jnp.concatenate(
        [jnp.broadcast_to(gmask[gi:gi + 1, :], (GROUP_SIZE, tt)) for gi in range(N_EXPERT_GROUPS)],
        axis=0)

    rows_e = lax.broadcasted_iota(jnp.int32, (N_EXPERTS, tt), 0)
    cur = jnp.where(emask > 0.0, sel, NEG_INF)
    idxs, gates = [], []
    member = jnp.zeros((N_EXPERTS, tt), F32)
    for _ in range(TOP_K):
        _, f = _first_argmax(cur, rows_e, N_EXPERTS)
        hit = rows_e == f
        idxs.append(f)
        gates.append(jnp.sum(jnp.where(hit, scores, 0.0), axis=0, keepdims=True))
        member = jnp.where(hit, 1.0, member)
        cur = jnp.where(hit, NEG_INF, cur)
    gate = jnp.concatenate(gates, axis=0)
    gate = gate / jnp.sum(gate, axis=0, keepdims=True) * ROUTED_SCALE

    ta = lax.broadcasted_iota(jnp.int32, (tt, tt), 0)
    tb = lax.broadcasted_iota(jnp.int32, (tt, tt), 1)
    earlier = (ta < tb).astype(BF16)
    before = jnp.dot(member.astype(BF16), earlier, preferred_element_type=F32)
    n = jnp.sum(member, axis=1, keepdims=True)
    n8 = jnp.floor((n + (CHUNK_ALIGN - 1)) * (1.0 / CHUNK_ALIGN)) * CHUNK_ALIGN
    n8_wide = jnp.broadcast_to(n8, (N_EXPERTS, LANES))
    ea = lax.broadcasted_iota(jnp.int32, (N_EXPERTS, N_EXPERTS), 0)
    eb = lax.broadcasted_iota(jnp.int32, (N_EXPERTS, N_EXPERTS), 1)
    lower = (eb < ea).astype(BF16)
    chunk_start = jnp.dot(lower, n8_wide.astype(BF16), preferred_element_type=F32)
    where_to = before + chunk_start[:, 0:1]
    for k in range(TOP_K):
        pk = jnp.sum(jnp.where(rows_e == idxs[k], where_to, 0.0), axis=0, keepdims=True)
        pos_ref[k:k + 1, :] = pk.astype(jnp.int32)
    gate_ref[...] = gate
    n8_int = n8_wide.astype(jnp.int32)
    n8_ref[...] = n8_int

    lane = lax.broadcasted_iota(jnp.int32, (N_EXPERTS, LANES), 1)
    flag = jnp.zeros((N_EXPERTS, LANES), F32)
    for j in range(CHUNK_BITS):
        flag = jnp.where((lane == j) & ((n8_int & (CHUNK_ALIGN << j)) != 0), 1.0, flag)
    rank = jnp.dot(lower, flag.astype(BF16), preferred_element_type=F32)
    e_id = lax.broadcasted_iota(jnp.int32, (N_EXPERTS, LANES), 0)
    for j in range(CHUNK_BITS):
        chosen = (rank[:, j:j + 1] == lane.astype(F32)) & (flag[:, j:j + 1] > 0.0)
        plan_ref[j:j + 1, :] = jnp.sum(jnp.where(chosen, e_id, 0), axis=0, keepdims=True)
    plan_ref[CHUNK_BITS:CHUNK_BITS + 1, :] = jnp.sum(flag, axis=0, keepdims=True).astype(jnp.int32)
    plan_ref[CHUNK_BITS + 1:CHUNK_BITS + 2, :] = jnp.sum(n8_int, axis=0, keepdims=True)


def _route(x1, router_w, router_bias, tg):
    t, d = x1.shape
    row_blk = lambda i: (0, i)
    ng = t // tg
    groups = ROUTE_GROUPS if ng % ROUTE_GROUPS == 0 else 1
    return pl.pallas_call(
        functools.partial(_route_kernel, tt=tg, groups=groups),
        grid=(ng // groups,),
        in_specs=[
            pl.BlockSpec((groups * tg, d), lambda i: (i, 0)),
            pl.BlockSpec((N_EXPERTS, d), lambda i: (0, 0)),
            pl.BlockSpec((N_EXPERTS, 1), lambda i: (0, 0)),
        ],
        out_specs=[pl.BlockSpec((TOP_K, groups * tg), row_blk),
                   pl.BlockSpec((TOP_K, groups * tg), row_blk),
                   pl.BlockSpec((groups * N_EXPERTS, LANES), lambda i: (i, 0)),
                   pl.BlockSpec((groups * PLAN_ROWS, LANES), lambda i: (i, 0))],
        out_shape=[jax.ShapeDtypeStruct((TOP_K, t), jnp.int32),
                   jax.ShapeDtypeStruct((TOP_K, t), F32),
                   jax.ShapeDtypeStruct((t // tg * N_EXPERTS, LANES), jnp.int32),
                   jax.ShapeDtypeStruct((t // tg * PLAN_ROWS, LANES), jnp.int32)],
        compiler_params=_params("parallel"),
        name="moe_router",
    )(x1, router_w.T, router_bias.reshape(N_EXPERTS, 1))


def _pack_halves(v, exact=False):
    h = v.shape[1] // 2
    return _pack_pair(v[:, :h], v[:, h:], exact)


def _pack_pair(lo, hi, exact=False):
    if not exact:
        lo, hi = lo.astype(BF16).astype(F32), hi.astype(BF16).astype(F32)
    lo = lax.bitcast_convert_type(lo, U32)
    hi = lax.bitcast_convert_type(hi, U32)
    return (lo >> 16) | (hi & jnp.uint32(HI_MASK))


def _chunk_relative(pos, c):
    rel = pos - c * SORT_CHUNK
    inside = (rel >= 0) & (rel < SORT_CHUNK)
    return jnp.where(inside, rel, -1).astype(F32).astype(BF16)


def _unpack_halves(w):
    lo = lax.bitcast_convert_type(w << 16, F32).astype(BF16)
    hi = lax.bitcast_convert_type(w & jnp.uint32(HI_MASK), F32).astype(BF16)
    return lo, hi


def _start_group_copies(plan_ref, make_copy):
    for j in range(CHUNK_BITS):
        size = CHUNK_ALIGN << j

        def body(k, carry):
            make_copy(pl.multiple_of(plan_ref[j * LANES + k], CHUNK_ALIGN),
                      pl.multiple_of(plan_ref[(PLAN_ROWS + j) * LANES + k], CHUNK_ALIGN), size).start()
            return carry

        lax.fori_loop(0, plan_ref[CHUNK_BITS * LANES + j], body, 0)


def _wait_group_rows(total_rows, ns, make_copy):
    for j in range(ns.bit_length()):
        size = CHUNK_ALIGN << j
        if size > ns:
            break

        @pl.when((total_rows & size) != 0)
        def _():
            make_copy(0, 0, size).wait()


def _dispatch_kernel(tot_ref, zrow_ref, plan_ref, pos_ref, x_ref, xs_ref, sbuf_ref, zbuf_ref,
                     onehot_ref, sem, zsem, *, tg, ns, bm):
    g = pl.program_id(0)
    ng = pl.num_programs(0)
    slot = g % 2

    @pl.when(g == 0)
    def _():
        zbuf_ref[...] = jnp.zeros_like(zbuf_ref)
        row = pl.multiple_of(zrow_ref[0], CHUNK_ALIGN)
        tail = pltpu.make_async_copy(zbuf_ref, xs_ref.at[pl.ds(row, bm), :], zsem)
        tail.start()
        tail.wait()

    def copy_from(sl):
        def make_copy(src, dst, size):
            return pltpu.make_async_copy(sbuf_ref.at[sl, pl.ds(src, size), :],
                                         xs_ref.at[pl.ds(dst, size), :], sem.at[sl])
        return make_copy

    @pl.when(g >= 2)
    def _():
        _wait_group_rows(tot_ref[g - 2], ns, copy_from(slot))

    xb = x_ref[...].astype(BF16)
    pos = pos_ref[...]
    local_rows = lax.broadcasted_iota(jnp.int32, (SORT_CHUNK, tg), 0).astype(F32).astype(BF16)
    one = jnp.ones((SORT_CHUNK, tg), BF16)
    zero = jnp.zeros((SORT_CHUNK, tg), BF16)
    for c in range(ns // SORT_CHUNK):
        rel = _chunk_relative(pos, c)
        hit = local_rows == rel[0:1, :]
        for k in range(1, TOP_K):
            hit = jnp.logical_or(hit, local_rows == rel[k:k + 1, :])
        onehot_ref[c * SORT_CHUNK:(c + 1) * SORT_CHUNK, :] = jnp.where(hit, one, zero)
    onehot = onehot_ref[...]
    half = x_ref.shape[1] // 2
    for n in range(half // SORT_CHUNK):
        lo = jnp.dot(onehot, xb[:, n * SORT_CHUNK:(n + 1) * SORT_CHUNK], preferred_element_type=F32)
        hi = jnp.dot(onehot, xb[:, half + n * SORT_CHUNK:half + (n + 1) * SORT_CHUNK],
                     preferred_element_type=F32)
        sbuf_ref[slot, :, n * SORT_CHUNK:(n + 1) * SORT_CHUNK] = _pack_pair(lo, hi, exact=True)

    _start_group_copies(plan_ref, copy_from(slot))

    @pl.when(g == ng - 1)
    def _():
        @pl.when(g >= 1)
        def _():
            _wait_group_rows(tot_ref[g - 1], ns, copy_from(1 - slot))
        _wait_group_rows(tot_ref[g], ns, copy_from(slot))


def _dispatch(x1, pos, plan, tot, zrow, n_rows, tg, ns, bm):
    t, d = x1.shape
    grid_spec = pltpu.PrefetchScalarGridSpec(
        num_scalar_prefetch=2,
        grid=(t // tg,),
        in_specs=[
            pl.BlockSpec((PLAN_WORDS,), lambda i, *_: (i,), memory_space=pltpu.SMEM),
            pl.BlockSpec((TOP_K, tg), lambda i, *_: (0, i)),
            pl.BlockSpec((tg, d), lambda i, *_: (i, 0)),
        ],
        out_specs=pl.BlockSpec(memory_space=pl.ANY),
        scratch_shapes=[pltpu.VMEM((2, ns, d // 2), U32), pltpu.VMEM((bm, d // 2), U32),
                        pltpu.VMEM((ns, tg), BF16),
                        pltpu.SemaphoreType.DMA((2,)), pltpu.SemaphoreType.DMA],
    )
    return pl.pallas_call(
        functools.partial(_dispatch_kernel, tg=tg, ns=ns, bm=bm),
        grid_spec=grid_spec,
        out_shape=jax.ShapeDtypeStruct((n_rows, d // 2), U32),
        compiler_params=_params("arbitrary"),
        name="moe_dispatch",
    )(tot, zrow, plan, pos, x1)


def _gmm_kernel(nq_ref, brow_ref, bval_ref, bord_ref, bfirst_ref, aexp_ref,
                xs_ref, w1_ref, w3_ref, w2_ref, y_ref,
                xbuf_ref, ybuf_ref, w1s_ref, w3s_ref, w2s_ref, w13_ref, w2b_ref, xsem, ysem, wsem,
                *, layer, bm):
    nq = nq_ref[0]
    na = nq_ref[1]
    f = w1s_ref.shape[2]
    half = xbuf_ref.shape[2]
    nx = xbuf_ref.shape[0]

    def x_copy(q, slot):
        row = pl.multiple_of(brow_ref[q], CHUNK_ALIGN)
        return pltpu.make_async_copy(xs_ref.at[pl.ds(row, bm), :], xbuf_ref.at[slot], xsem.at[slot])

    def w_copies(j, slot):
        e = aexp_ref[j]
        return (pltpu.make_async_copy(w1_ref.at[layer, e], w1s_ref.at[slot], wsem.at[slot]),
                pltpu.make_async_copy(w3_ref.at[layer, e], w3s_ref.at[slot], wsem.at[slot]),
                pltpu.make_async_copy(w2_ref.at[layer, e], w2s_ref.at[slot], wsem.at[slot]))

    def y_copy(slot, row, off, size):
        return pltpu.make_async_copy(ybuf_ref.at[slot, pl.ds(off, size), :],
                                     y_ref.at[pl.ds(pl.multiple_of(row + off, CHUNK_ALIGN), size), :],
                                     ysem.at[slot])

    def for_each_y_copy(slot, row, valid, fn):
        @pl.when(valid == bm)
        def _():
            fn(y_copy(slot, row, 0, bm))

        @pl.when(valid < bm)
        def _():
            for j in range((bm // CHUNK_ALIGN).bit_length() - 1):
                size = CHUNK_ALIGN << j

                @pl.when((valid & size) != 0)
                def _():
                    fn(y_copy(slot, row, pl.multiple_of(valid & (size - 1), CHUNK_ALIGN), size))

    for c in w_copies(0, 0):
        c.start()
    for q0 in range(nx - 1):
        @pl.when(q0 < nq)
        def _():
            x_copy(q0, q0).start()

    def body(q, carry):
        xslot = q % nx
        yslot = q % 2

        @pl.when(q + (nx - 1) < nq)
        def _():
            x_copy(q + (nx - 1), (q + (nx - 1)) % nx).start()

        @pl.when(bfirst_ref[q] == 1)
        def _():
            j = bord_ref[q]
            ws = j % 2
            for c in w_copies(j, ws):
                c.wait()

            @pl.when(j + 1 < na)
            def _():
                for c in w_copies(j + 1, 1 - ws):
                    c.start()
            w13_ref[:, :f] = w1s_ref[ws].astype(BF16)
            w13_ref[:, f:] = w3s_ref[ws].astype(BF16)
            w2b_ref[...] = w2s_ref[ws].astype(BF16)

        x_copy(q, xslot).wait()

        @pl.when(q >= 2)
        def _():
            for_each_y_copy(yslot, 0, bval_ref[jnp.maximum(q - 2, 0)], lambda c: c.wait())

        x_lo, x_hi = _unpack_halves(xbuf_ref[xslot])
        h13 = (jnp.dot(x_lo, w13_ref[:half, :], preferred_element_type=F32)
               + jnp.dot(x_hi, w13_ref[half:, :], preferred_element_type=F32))
        h = jax.nn.silu(h13[:, :f]) * h13[:, f:]
        ybuf_ref[yslot] = _pack_halves(jnp.dot(h.astype(BF16), w2b_ref[...], preferred_element_type=F32))
        for_each_y_copy(yslot, brow_ref[q], bval_ref[q], lambda c: c.start())
        return carry

    lax.fori_loop(0, nq, body, 0)

    @pl.when(nq >= 2)
    def _():
        for_each_y_copy(nq % 2, 0, bval_ref[jnp.maximum(nq - 2, 0)], lambda c: c.wait())
    for_each_y_copy((nq - 1) % 2, 0, bval_ref[nq - 1], lambda c: c.wait())


def _gmm(xs, tables, w1, w3, w2, layer, bm):
    n_rows, half = xs.shape
    d = 2 * half
    f = w1.shape[-1]
    any_spec = pl.BlockSpec(memory_space=pl.ANY)
    grid_spec = pltpu.PrefetchScalarGridSpec(
        num_scalar_prefetch=len(tables),
        grid=(1,),
        in_specs=[any_spec, any_spec, any_spec, any_spec],
        out_specs=any_spec,
        scratch_shapes=[pltpu.VMEM((GMM_X_SLOTS, bm, half), U32), pltpu.VMEM((2, bm, half), U32),
                        pltpu.VMEM((2, d, f), F32), pltpu.VMEM((2, d, f), F32), pltpu.VMEM((2, f, d), F32),
                        pltpu.VMEM((d, 2 * f), BF16), pltpu.VMEM((f, d), BF16),
                        pltpu.SemaphoreType.DMA((GMM_X_SLOTS,)), pltpu.SemaphoreType.DMA((2,)),
                        pltpu.SemaphoreType.DMA((2,))],
    )
    return pl.pallas_call(
        functools.partial(_gmm_kernel, layer=layer, bm=bm),
        grid_spec=grid_spec,
        out_shape=jax.ShapeDtypeStruct((n_rows, half), U32),
        compiler_params=_params("arbitrary"),
        name="moe_experts",
    )(*tables, xs, w1, w3, w2)


def _combine_kernel(tot_ref, plan_ref, plan_next_ref, yb_ref, pos_ref, gate_ref, x_ref, ws13_ref, ws2_ref,
                    g_ref, b_ref, o_ref, ybuf_ref, wgt_ref, ylo_ref, yhi_ref, sem, *, alpha, tg, ns):
    g = pl.program_id(0)
    ng = pl.num_programs(0)
    slot = g % 2

    def copy_into(sl):
        def make_copy(buf_row, hbm_row, size):
            return pltpu.make_async_copy(yb_ref.at[pl.ds(hbm_row, size), :],
                                         ybuf_ref.at[sl, pl.ds(buf_row, size), :], sem.at[sl])
        return make_copy

    @pl.when(g == 0)
    def _():
        ybuf_ref[...] = jnp.zeros_like(ybuf_ref)
        _start_group_copies(plan_ref, copy_into(0))

    @pl.when(g + 1 < ng)
    def _():
        _start_group_copies(plan_next_ref, copy_into(1 - slot))

    x = x_ref[...]
    f = ws2_ref.shape[0]
    h13 = jnp.dot(x.astype(BF16), ws13_ref[...], preferred_element_type=F32)
    hs = jax.nn.silu(h13[:, :f]) * h13[:, f:]
    shared = jnp.dot(hs.astype(BF16), ws2_ref[...], preferred_element_type=F32)

    _wait_group_rows(tot_ref[g], ns, copy_into(slot))

    pos = pos_ref[...]
    gate_b = gate_ref[...].astype(BF16)
    local_rows = lax.broadcasted_iota(jnp.int32, (SORT_CHUNK, tg), 0).astype(F32).astype(BF16)
    for c in range(ns // SORT_CHUNK):
        rel = _chunk_relative(pos, c)
        wgt = jnp.zeros((SORT_CHUNK, tg), BF16)
        for k in range(TOP_K):
            wgt = jnp.where(local_rows == rel[k:k + 1, :], gate_b[k:k + 1, :], wgt)
        wgt_ref[c * SORT_CHUNK:(c + 1) * SORT_CHUNK, :] = wgt
        y_lo, y_hi = _unpack_halves(ybuf_ref[slot, c * SORT_CHUNK:(c + 1) * SORT_CHUNK, :])
        ylo_ref[c * SORT_CHUNK:(c + 1) * SORT_CHUNK, :] = y_lo
        yhi_ref[c * SORT_CHUNK:(c + 1) * SORT_CHUNK, :] = y_hi
    wgt_all = wgt_ref[...]
    dn = (((0,), (0,)), ((), ()))
    routed = jnp.concatenate([lax.dot_general(wgt_all, ylo_ref[...], dn, preferred_element_type=F32),
                              lax.dot_general(wgt_all, yhi_ref[...], dn, preferred_element_type=F32)],
                             axis=1)
    o_ref[...] = _layer_norm(alpha * x + (routed + shared), g_ref[...], b_ref[...])


def _combine(yb, pos_kt, gate_kt, plan, tot, x1, ws1, ws3, ws2, g, b, alpha, tg, ns):
    t, d = x1.shape
    ng = t // tg
    f = ws1.shape[1]
    ws13 = jnp.concatenate([ws1, ws3], axis=1).astype(BF16)
    const = lambda i, *_: (0, 0)
    tile = lambda i, *_: (i, 0)
    row = lambda v: v.reshape(1, d)
    grid_spec = pltpu.PrefetchScalarGridSpec(
        num_scalar_prefetch=1,
        grid=(ng,),
        in_specs=[
            pl.BlockSpec((PLAN_WORDS,), lambda i, *_: (i,), memory_space=pltpu.SMEM),
            pl.BlockSpec((PLAN_WORDS,), lambda i, *_: (jnp.minimum(i + 1, ng - 1),),
                         memory_space=pltpu.SMEM),
            pl.BlockSpec(memory_space=pl.ANY),
            pl.BlockSpec((TOP_K, tg), lambda i, *_: (0, i)),
            pl.BlockSpec((TOP_K, tg), lambda i, *_: (0, i)),
            pl.BlockSpec((tg, d), tile),
            pl.BlockSpec((d, 2 * f), const),
            pl.BlockSpec((f, d), const),
            pl.BlockSpec((1, d), const),
            pl.BlockSpec((1, d), const),
        ],
        out_specs=pl.BlockSpec((tg, d), tile),
        scratch_shapes=[pltpu.VMEM((2, ns, d // 2), U32), pltpu.VMEM((ns, tg), BF16),
                        pltpu.VMEM((ns, d // 2), BF16), pltpu.VMEM((ns, d // 2), BF16),
                        pltpu.SemaphoreType.DMA((2,))],
    )
    return pl.pallas_call(
        functools.partial(_combine_kernel, alpha=alpha, tg=tg, ns=ns),
        grid_spec=grid_spec,
        out_shape=jax.ShapeDtypeStruct((t, d), F32),
        compiler_params=_params("arbitrary"),
        name="moe_combine",
    )(tot, plan, plan, yb, pos_kt, gate_kt, x1, ws13, ws2.astype(BF16), row(g), row(b))


def _moe_layer(x1, layer, router_w, router_bias, w1, w3, w2, ws1, ws3, ws2, g, b, alpha):
    t, d = x1.shape
    tg = min(GROUP_TG, t)
    ng = t // tg
    bm = EXPERT_BM
    ns = -(-(tg * TOP_K + N_EXPERTS * (CHUNK_ALIGN - 1)) // SORT_CHUNK) * SORT_CHUNK
    pos, gate, n8_wide, plan_wide = _route(x1, router_w, router_bias, tg)
    plan3 = plan_wide.reshape(ng, PLAN_ROWS, LANES)
    tot = plan3[:, CHUNK_BITS + 1, 0]

    n8 = n8_wide[:, 0].reshape(ng, N_EXPERTS)
    goff = jnp.cumsum(n8, axis=1) - n8
    seg = jnp.sum(n8, axis=0)
    pend = jnp.cumsum(seg)
    pstart = pend - seg
    gdest = pstart[None, :] + jnp.cumsum(n8, axis=0) - n8
    max_rows = ng * (tg * TOP_K + N_EXPERTS * (CHUNK_ALIGN - 1))
    n_rows = max_rows + bm

    experts = jnp.arange(N_EXPERTS, dtype=jnp.int32)
    nb = (seg + bm - 1) // bm
    cb = jnp.cumsum(nb)
    blk = jnp.arange(max_rows // bm + N_EXPERTS, dtype=jnp.int32)
    blk_expert = jnp.minimum(jnp.sum((cb[None, :] <= blk[:, None]).astype(jnp.int32), axis=1), N_EXPERTS - 1)
    mine = blk_expert[:, None] == experts[None, :]
    of_expert = lambda v: jnp.sum(jnp.where(mine, v[None, :], 0), axis=1)
    local = blk - of_expert(cb - nb)
    active = seg > 0
    ordinal = jnp.cumsum(active.astype(jnp.int32)) - 1
    active_experts = jnp.sum(jnp.where((ordinal[None, :] == experts[:, None]) & active[None, :],
                                       experts[None, :], 0), axis=1)
    tables = [jnp.stack([cb[-1], jnp.sum(active.astype(jnp.int32))]),
              of_expert(pstart) + local * bm,
              jnp.clip(of_expert(seg) - local * bm, 0, bm),
              of_expert(ordinal),
              (local == 0).astype(jnp.int32),
              active_experts]
    tables = [a.astype(jnp.int32) for a in tables]
    zrow = pend[-1:].astype(jnp.int32)

    lists = plan3[:, :CHUNK_BITS, :N_EXPERTS]
    sizes = CHUNK_ALIGN << jnp.arange(CHUNK_BITS, dtype=jnp.int32)
    done = n8[:, None, :] & (sizes[None, :, None] - 1)
    pick = lists[..., None] == jnp.arange(N_EXPERTS, dtype=jnp.int32)
    src = jnp.sum(jnp.where(pick, (goff[:, None, :] + done)[:, :, None, :], 0), axis=-1)
    dst = jnp.sum(jnp.where(pick, (gdest[:, None, :] + done)[:, :, None, :], 0), axis=-1)
    widen = lambda a: jnp.pad(a, ((0, 0), (0, 0), (0, LANES - N_EXPERTS)))
    plan = jnp.concatenate([widen(src), plan3[:, CHUNK_BITS:], widen(dst),
                            jnp.zeros((ng, PLAN_ROWS - CHUNK_BITS, LANES), jnp.int32)],
                           axis=1).astype(jnp.int32).reshape(-1)

    xs = _dispatch(x1, pos, plan, tot, zrow, n_rows, tg, ns, bm)
    yb = _gmm(xs, tables, w1, w3, w2, layer, bm)
    return _combine(yb, pos, gate, plan, tot, x1, ws1, ws3, ws2, g, b, alpha, tg, ns)


def kernel(x, ln_gain, ln_bias, pool_w, pool_scale, attn_wqkv, attn_bqkv, attn_sinks, attn_wo, attn_bo,
           conv_w_in, conv_w, conv_w_out, router_w, router_bias, expert_w1, expert_w3, expert_w2,
           shared_w1, shared_w3, shared_w2):
    bsz, seq, d = x.shape
    depth = ln_gain.shape[0]
    alpha = float((2 * depth) ** 0.25)
    h = x.reshape(bsz * seq, d)
    for i in range(depth):
        kind = i % N_MIXERS
        j = i // N_MIXERS
        g1, b1 = ln_gain[i, 0], ln_bias[i, 0]
        if kind == 0:
            h = _pool_mixer(h, seq, pool_w[j], pool_scale[j], g1, b1, alpha)
        elif kind == 1:
            h = _swa_mixer(h, seq, attn_wqkv[j], attn_bqkv[j], attn_sinks[j], attn_wo[j], attn_bo[j],
                           g1, b1, alpha)
        else:
            h = _conv_mixer(h, seq, conv_w_in[j], conv_w[j], conv_w_out[j], g1, b1, alpha)
        h = _moe_layer(h, i, router_w[i], router_bias[i], expert_w1, expert_w3, expert_w2,
                       shared_w1[i], shared_w3[i], shared_w2[i], ln_gain[i, 1], ln_bias[i, 1], alpha)
    return h.reshape(bsz, seq, d)
```

```python
import functools

import jax
import jax.numpy as jnp
from jax import lax
from jax.experimental import pallas as pl
from jax.experimental.pallas import tpu as pltpu

POOL_WINDOWS = (2, 4, 8, 16)
N_MIXERS = 3
HEAD_DIM = 64
N_Q_HEADS = 16
N_KV_HEADS = 4
GQA_GROUP = N_Q_HEADS // N_KV_HEADS
WINDOW = 128
ROPE_THETA = 10000.0
N_EXPERTS = 64
TOP_K = 8
N_EXPERT_GROUPS = 8
GROUP_SIZE = N_EXPERTS // N_EXPERT_GROUPS
TOPK_GROUPS = 4
ROUTED_SCALE = 2.5
LN_EPS = 1e-5

LANES = 128
SUBLANES = 8
POOL_HALO = 16
VMEM_LIMIT = 48 * 1024 * 1024

POOL_TT = 512
QKV_TT = 512
ATTN_TQ = 512
CONV_TT = 512
GROUP_TG = 256
CHUNK_ALIGN = SUBLANES
SORT_CHUNK = 256
EXPERT_BM = 512
GMM_X_SLOTS = 3
CHUNK_BITS = 6
PLAN_ROWS = SUBLANES
PLAN_WORDS = 2 * PLAN_ROWS * LANES

U32 = jnp.uint32
HI_MASK = 0xFFFF0000

F32 = jnp.float32
BF16 = jnp.bfloat16
NEG_INF = float("-inf")


def _layer_norm(z, g, b):
    mu = jnp.mean(z, axis=-1, keepdims=True)
    zc = z - mu
    var = jnp.mean(zc * zc, axis=-1, keepdims=True)
    return zc * lax.rsqrt(var + LN_EPS) * g + b


def _params(*sem):
    return pltpu.CompilerParams(dimension_semantics=sem, vmem_limit_bytes=VMEM_LIMIT)


def _route_tile(o_ref, router_refs, route_out_refs, tg):
    wt_ref, bias_ref = router_refs
    pos_ref, gate_ref, n8_ref, plan_ref = route_out_refs
    for s in range(o_ref.shape[0] // tg):
        _route_group(o_ref.at[pl.ds(s * tg, tg), :], wt_ref, bias_ref,
                     pos_ref.at[:, pl.ds(s * tg, tg)], gate_ref.at[:, pl.ds(s * tg, tg)],
                     n8_ref.at[pl.ds(s * N_EXPERTS, N_EXPERTS), :],
                     plan_ref.at[pl.ds(s * PLAN_ROWS, PLAN_ROWS), :], tt=tg)


def _router_plumbing(router_w, router_bias, t, d, tt, tg, tile_index):
    groups = tt // tg
    const = lambda *ids: (0, 0)
    by_cols = lambda *ids: (0, tile_index(*ids))
    by_rows = lambda *ids: (tile_index(*ids), 0)
    operands = [router_w.T, router_bias.reshape(N_EXPERTS, 1)]
    in_specs = [pl.BlockSpec((N_EXPERTS, d), const), pl.BlockSpec((N_EXPERTS, 1), const)]
    out_specs = [pl.BlockSpec((TOP_K, tt), by_cols), pl.BlockSpec((TOP_K, tt), by_cols),
                 pl.BlockSpec((groups * N_EXPERTS, LANES), by_rows),
                 pl.BlockSpec((groups * PLAN_ROWS, LANES), by_rows)]
    out_shapes = [jax.ShapeDtypeStruct((TOP_K, t), jnp.int32), jax.ShapeDtypeStruct((TOP_K, t), F32),
                  jax.ShapeDtypeStruct((t // tg * N_EXPERTS, LANES), jnp.int32),
                  jax.ShapeDtypeStruct((t // tg * PLAN_ROWS, LANES), jnp.int32)]
    return operands, in_specs, out_specs, out_shapes


def _pool_kernel(x_ref, halo_ref, w_ref, scale_ref, g_ref, b_ref, wt_ref, rb_ref, o_ref, pos_ref, gate_ref,
                 n8_ref, plan_ref, buf_ref, *, alpha, tt, tg):
    i = pl.program_id(1)
    x = x_ref[...]
    ch = x.shape[1] // len(POOL_WINDOWS)
    buf_ref[0:POOL_HALO, :] = jnp.where(i > 0, halo_ref[...], 0.0)
    buf_ref[POOL_HALO:POOL_HALO + tt, :] = x
    pos = i * tt + lax.broadcasted_iota(jnp.int32, (tt, 1), 0)
    ys = []
    for g, w in enumerate(POOL_WINDOWS):
        c0 = g * ch
        xg = x[:, c0:c0 + ch]
        s = xg
        for j in range(1, w):
            s = s + buf_ref[POOL_HALO - j:POOL_HALO - j + tt, c0:c0 + ch]
        inv_cnt = 1.0 / jnp.minimum(pos + 1, w).astype(F32)
        pooled = s * inv_cnt - xg
        ys.append(jnp.dot(pooled.astype(BF16), w_ref[g], preferred_element_type=F32))
    y = jnp.concatenate(ys, axis=1) * scale_ref[...]
    o_ref[...] = _layer_norm(alpha * x + y, g_ref[...], b_ref[...])
    _route_tile(o_ref, (wt_ref, rb_ref), (pos_ref, gate_ref, n8_ref, plan_ref), tg)


def _pool_mixer(x2, seq, w_grp, scale, g, b, alpha, router_w, router_bias, tg):
    t, d = x2.shape
    tt = min(POOL_TT, seq)
    nt = seq // tt
    nb = t // seq
    row = lambda v: v.reshape(1, d)
    const2 = lambda bi, i: (0, 0)
    tile = lambda bi, i: bi * nt + i
    r_ops, r_in, r_out, r_shapes = _router_plumbing(router_w, router_bias, t, d, tt, tg, tile)
    return pl.pallas_call(
        functools.partial(_pool_kernel, alpha=alpha, tt=tt, tg=tg),
        grid=(nb, nt),
        in_specs=[
            pl.BlockSpec((tt, d), lambda bi, i: (bi * nt + i, 0)),
            pl.BlockSpec((POOL_HALO, d),
                         lambda bi, i: (jnp.maximum((bi * seq + i * tt) // POOL_HALO - 1, 0), 0)),
            pl.BlockSpec(w_grp.shape, lambda bi, i: (0, 0, 0)),
            pl.BlockSpec((1, d), const2),
            pl.BlockSpec((1, d), const2),
            pl.BlockSpec((1, d), const2),
        ] + r_in,
        out_specs=[pl.BlockSpec((tt, d), lambda bi, i: (bi * nt + i, 0))] + r_out,
        out_shape=[jax.ShapeDtypeStruct((t, d), F32)] + r_shapes,
        scratch_shapes=[pltpu.VMEM((POOL_HALO + tt, d), F32)],
        compiler_params=_params("parallel", "parallel"),
        name="pool_mixer",
    )(x2, x2, w_grp.astype(BF16), row(scale), row(g), row(b), *r_ops)


def _qkv_kernel(x_ref, w_ref, b_ref, c_ref, s_ref, q_ref, k_ref, v_ref, *, nq, nk):
    y = jnp.dot(x_ref[...].astype(BF16), w_ref[...], preferred_element_type=F32) + b_ref[...]
    c = c_ref[...]
    s = s_ref[...]
    lane = lax.broadcasted_iota(jnp.int32, c.shape, 1)
    first_half = (lane % HEAD_DIM) < (HEAD_DIM // 2)

    def rope(v):
        partner = jnp.where(first_half, pltpu.roll(v, LANES - HEAD_DIM // 2, 1),
                            pltpu.roll(v, HEAD_DIM // 2, 1))
        return v * c + partner * s

    for j in range(nq // LANES):
        q_ref[:, j * LANES:(j + 1) * LANES] = (
            rope(y[:, j * LANES:(j + 1) * LANES]) * (HEAD_DIM ** -0.5)).astype(BF16)
    for j in range(nk // LANES):
        k_ref[:, j * LANES:(j + 1) * LANES] = rope(
            y[:, nq + j * LANES:nq + (j + 1) * LANES]).astype(BF16)
    v_ref[...] = y[:, nq + nk:].astype(BF16)


def _attn_kernel(sink_ref, q_ref, k_ref, kh_ref, v_ref, vh_ref, x_ref, wo_ref, bo_ref, g_ref, b_ref,
                 wt_ref, rb_ref, o_ref, pos_ref, gate_ref, n8_ref, plan_ref, *, alpha, tq, tg):
    i = pl.program_id(1)
    nqb = tq // WINDOW
    r = lax.broadcasted_iota(jnp.int32, (tq, 2 * WINDOW), 0)
    c = lax.broadcasted_iota(jnp.int32, (tq, 2 * WINDOW), 1)
    dist = (r & (WINDOW - 1)) + WINDOW - c
    valid = (dist >= 0) & (dist < WINDOW) & ((c >= WINDOW) | (r >= WINDOW) | (i > 0))
    lane = lax.broadcasted_iota(jnp.int32, (tq, LANES), 1)
    low = lane < HEAD_DIM
    dn = (((1,), (1,)), ((), ()))
    outs = []
    for kv in range(N_KV_HEADS):
        sl = slice(kv * LANES, (kv + 1) * LANES)
        kk = jnp.concatenate([kh_ref[:, sl], k_ref[:, sl]], axis=0)
        vv = jnp.concatenate([vh_ref[:, sl], v_ref[:, sl]], axis=0)
        for p in range(GQA_GROUP // 2):
            c0 = (kv * GQA_GROUP + 2 * p) * HEAD_DIM
            qp = q_ref[:, c0:c0 + LANES]
            res = []
            for half in range(2):
                keep = low if half == 0 else jnp.logical_not(low)
                qh = jnp.where(keep, qp, jnp.zeros_like(qp))
                sc_all = lax.dot_general(qh, kk, dn, preferred_element_type=F32)
                sink = sink_ref[kv * GQA_GROUP + 2 * p + half]
                sc = jnp.concatenate([sc_all[qb * WINDOW:(qb + 1) * WINDOW, qb * WINDOW:(qb + 2) * WINDOW]
                                      for qb in range(nqb)], axis=0)
                sc = jnp.where(valid, sc, NEG_INF)
                m = jnp.maximum(jnp.max(sc, axis=1, keepdims=True), sink)
                pm = jnp.exp(sc - m)
                rden = 1.0 / (jnp.sum(pm, axis=1, keepdims=True) + jnp.exp(sink - m))
                pm = pm.astype(BF16)
                res.append(jnp.concatenate(
                    [jnp.dot(pm[qb * WINDOW:(qb + 1) * WINDOW, :], vv[qb * WINDOW:(qb + 2) * WINDOW, :],
                             preferred_element_type=F32) for qb in range(nqb)], axis=0) * rden)
            outs.append(jnp.where(low, res[0], res[1]))
    o_all = jnp.concatenate(outs, axis=1).astype(BF16)
    mix = jnp.dot(o_all, wo_ref[...], preferred_element_type=F32) + bo_ref[...]
    o_ref[...] = _layer_norm(alpha * x_ref[...] + mix, g_ref[...], b_ref[...])
    _route_tile(o_ref, (wt_ref, rb_ref), (pos_ref, gate_ref, n8_ref, plan_ref), tg)


def _dup_heads(w, n_heads):
    lead = w.shape[:-1]
    w = w.reshape(lead + (n_heads, 1, HEAD_DIM))
    w = jnp.broadcast_to(w, lead + (n_heads, 2, HEAD_DIM))
    return w.reshape(lead + (n_heads * 2 * HEAD_DIM,))


def _swa_mixer(x2, seq, wqkv, bqkv, sinks, wo, bo, g, b, alpha, router_w, router_bias, tg):
    t, d = x2.shape
    nb = t // seq
    nq = N_Q_HEADS * HEAD_DIM
    nkv = N_KV_HEADS * HEAD_DIM
    nk2 = 2 * nkv
    w_ext = jnp.concatenate([wqkv[:, :nq], _dup_heads(wqkv[:, nq:nq + nkv], N_KV_HEADS),
                             _dup_heads(wqkv[:, nq + nkv:], N_KV_HEADS)], axis=1).astype(BF16)
    b_ext = jnp.concatenate([bqkv[:nq], _dup_heads(bqkv[nq:nq + nkv], N_KV_HEADS),
                             _dup_heads(bqkv[nq + nkv:], N_KV_HEADS)]).reshape(1, -1)
    n_ext = nq + 2 * nk2

    pos = jnp.arange(seq, dtype=F32)
    inv_freq = ROPE_THETA ** (-jnp.arange(0, HEAD_DIM, 2, dtype=F32) / HEAD_DIM)
    ang = pos[:, None] * inv_freq[None, :]
    cos, sin = jnp.cos(ang), jnp.sin(ang)
    cos_t = jnp.concatenate([cos, cos, cos, cos], axis=1)
    sin_t = jnp.concatenate([-sin, sin, -sin, sin], axis=1)

    tt = min(QKV_TT, seq)
    nt = seq // tt
    const2 = lambda i: (0, 0)
    q, k2, v2 = pl.pallas_call(
        functools.partial(_qkv_kernel, nq=nq, nk=nk2),
        grid=(t // tt,),
        in_specs=[
            pl.BlockSpec((tt, d), lambda i: (i, 0)),
            pl.BlockSpec((d, n_ext), const2),
            pl.BlockSpec((1, n_ext), const2),
            pl.BlockSpec((tt, LANES), lambda i: (i % nt, 0)),
            pl.BlockSpec((tt, LANES), lambda i: (i % nt, 0)),
        ],
        out_specs=[pl.BlockSpec((tt, nq), lambda i: (i, 0)),
                   pl.BlockSpec((tt, nk2), lambda i: (i, 0)),
                   pl.BlockSpec((tt, nk2), lambda i: (i, 0))],
        out_shape=[jax.ShapeDtypeStruct((t, nq), BF16),
                   jax.ShapeDtypeStruct((t, nk2), BF16),
                   jax.ShapeDtypeStruct((t, nk2), BF16)],
        compiler_params=_params("parallel"),
        name="swa_qkv",
    )(x2, w_ext, b_ext, cos_t, sin_t)

    tq = min(ATTN_TQ, seq)
    ntq = seq // tq
    main = lambda bi, i: (bi * ntq + i, 0)
    halo = lambda bi, i: (jnp.maximum((bi * seq + i * tq) // WINDOW - 1, 0), 0)
    const = lambda bi, i: (0, 0)
    row = lambda v: v.reshape(1, d)
    r_ops, r_in, r_out, r_shapes = _router_plumbing(router_w, router_bias, t, d, tq, tg,
                                                    lambda bi, i: bi * ntq + i)
    return pl.pallas_call(
        functools.partial(_attn_kernel, alpha=alpha, tq=tq, tg=tg),
        grid=(nb, ntq),
        in_specs=[
            pl.BlockSpec(memory_space=pltpu.SMEM),
            pl.BlockSpec((tq, nq), main),
            pl.BlockSpec((tq, nk2), main),
            pl.BlockSpec((WINDOW, nk2), halo),
            pl.BlockSpec((tq, nk2), main),
            pl.BlockSpec((WINDOW, nk2), halo),
            pl.BlockSpec((tq, d), main),
            pl.BlockSpec((nq, d), const),
            pl.BlockSpec((1, d), const),
            pl.BlockSpec((1, d), const),
            pl.BlockSpec((1, d), const),
        ] + r_in,
        out_specs=[pl.BlockSpec((tq, d), main)] + r_out,
        out_shape=[jax.ShapeDtypeStruct((t, d), F32)] + r_shapes,
        compiler_params=_params("parallel", "parallel"),
        name="swa_attention",
    )(sinks, q, k2, k2, v2, v2, x2, wo.astype(BF16), row(bo), row(g), row(b), *r_ops)


def _conv_kernel(x_ref, win_ref, cw_ref, wout_ref, g_ref, b_ref, wt_ref, rb_ref, o_ref, pos_ref, gate_ref,
                 n8_ref, plan_ref, ubuf_ref, *, alpha, tt, tg):
    i = pl.program_id(1)
    d = x_ref.shape[1]

    @pl.when(i == 0)
    def _():
        ubuf_ref[0:SUBLANES, :] = jnp.zeros((SUBLANES, d), F32)

    x = x_ref[...]
    proj = jnp.dot(x.astype(BF16), win_ref[...], preferred_element_type=F32)
    gate_b = proj[:, :d]
    u = proj[:, d:2 * d] * proj[:, 2 * d:]
    ubuf_ref[SUBLANES:SUBLANES + tt, :] = u
    cw = cw_ref[...]
    conv = (cw[0:1, :] * ubuf_ref[SUBLANES - 2:SUBLANES - 2 + tt, :]
            + cw[1:2, :] * ubuf_ref[SUBLANES - 1:SUBLANES - 1 + tt, :]
            + cw[2:3, :] * u)
    ubuf_ref[0:SUBLANES, :] = u[tt - SUBLANES:tt, :]
    y = jnp.dot((gate_b * conv).astype(BF16), wout_ref[...], preferred_element_type=F32)
    o_ref[...] = _layer_norm(alpha * x + y, g_ref[...], b_ref[...])
    _route_tile(o_ref, (wt_ref, rb_ref), (pos_ref, gate_ref, n8_ref, plan_ref), tg)


def _conv_mixer(x2, seq, w_in, conv_w, w_out, g, b, alpha, router_w, router_bias, tg):
    t, d = x2.shape
    nb = t // seq
    tt = min(CONV_TT, seq)
    nt = seq // tt
    const = lambda bi, i: (0, 0)
    row = lambda v: v.reshape(1, d)
    r_ops, r_in, r_out, r_shapes = _router_plumbing(router_w, router_bias, t, d, tt, tg,
                                                    lambda bi, i: bi * nt + i)
    return pl.pallas_call(
        functools.partial(_conv_kernel, alpha=alpha, tt=tt, tg=tg),
        grid=(nb, nt),
        in_specs=[
            pl.BlockSpec((tt, d), lambda bi, i: (bi * nt + i, 0)),
            pl.BlockSpec((d, 3 * d), const),
            pl.BlockSpec(conv_w.shape, const),
            pl.BlockSpec((d, d), const),
            pl.BlockSpec((1, d), const),
            pl.BlockSpec((1, d), const),
        ] + r_in,
        out_specs=[pl.BlockSpec((tt, d), lambda bi, i: (bi * nt + i, 0))] + r_out,
        out_shape=[jax.ShapeDtypeStruct((t, d), F32)] + r_shapes,
        scratch_shapes=[pltpu.VMEM((SUBLANES + tt, d), F32)],
        compiler_params=_params("arbitrary", "arbitrary"),
        name="conv_mixer",
    )(x2, w_in.astype(BF16), conv_w, w_out.astype(BF16), row(g), row(b), *r_ops)


def _first_argmax(v, rows, n):
    m = jnp.max(v, axis=0, keepdims=True)
    first = jnp.min(jnp.where(v == m, rows, n), axis=0, keepdims=True)
    return m, first


def _route_group(x_ref, wt_ref, bias_ref, pos_ref, gate_ref, n8_ref, plan_ref, *, tt):
    x = x_ref[...]
    xh = x.astype(BF16)
    xl = (x - xh.astype(F32)).astype(BF16)
    wt = wt_ref[...]
    wh = wt.astype(BF16)
    wl = (wt - wh.astype(F32)).astype(BF16)
    dn = (((1,), (1,)), ((), ()))
    logits = (lax.dot_general(wh, xh, dn, preferred_element_type=F32)
              + (lax.dot_general(wh, xl, dn, preferred_element_type=F32)
                 + lax.dot_general(wl, xh, dn, preferred_element_type=F32)))
    scores = jax.nn.sigmoid(logits)
    sel = scores + bias_ref[...]

    rows8 = lax.broadcasted_iota(jnp.int32, (GROUP_SIZE, tt), 0)
    gsc = []
    for gi in range(N_EXPERT_GROUPS):
        blk = sel[gi * GROUP_SIZE:(gi + 1) * GROUP_SIZE, :]
        m1, f1 = _first_argmax(blk, rows8, GROUP_SIZE)
        m2 = jnp.max(jnp.where(rows8 == f1, NEG_INF, blk), axis=0, keepdims=True)
        gsc.append(m1 + m2)
    cur = jnp.concatenate(gsc, axis=0)
    rows_g = lax.broadcasted_iota(jnp.int32, (N_EXPERT_GROUPS, tt), 0)
    gmask = jnp.zeros((N_EXPERT_GROUPS, tt), F32)
    for _ in range(TOPK_GROUPS):
        _, f = _first_argmax(cur, rows_g, N_EXPERT_GROUPS)
        hit = rows_g == f
        gmask = jnp.where(hit, 1.0, gmask)
        cur = jnp.where(hit, NEG_INF, cur)
    emask = jnp.concatenate(
        [jnp.broadcast_to(gmask[gi:gi + 1, :], (GROUP_SIZE, tt)) for gi in range(N_EXPERT_GROUPS)],
        axis=0)

    rows_e = lax.broadcasted_iota(jnp.int32, (N_EXPERTS, tt), 0)
    cur = jnp.where(emask > 0.0, sel, NEG_INF)
    idxs, gates = [], []
    member = jnp.zeros((N_EXPERTS, tt), F32)
    for _ in range(TOP_K):
        _, f = _first_argmax(cur, rows_e, N_EXPERTS)
        hit = rows_e == f
        idxs.append(f)
        gates.append(jnp.sum(jnp.where(hit, scores, 0.0), axis=0, keepdims=True))
        member = jnp.where(hit, 1.0, member)
        cur = jnp.where(hit, NEG_INF, cur)
    gate = jnp.concatenate(gates, axis=0)
    gate = gate / jnp.sum(gate, axis=0, keepdims=True) * ROUTED_SCALE

    ta = lax.broadcasted_iota(jnp.int32, (tt, tt), 0)
    tb = lax.broadcasted_iota(jnp.int32, (tt, tt), 1)
    earlier = (ta < tb).astype(BF16)
    before = jnp.dot(member.astype(BF16), earlier, preferred_element_type=F32)
    n = jnp.sum(member, axis=1, keepdims=True)
    n8 = jnp.floor((n + (CHUNK_ALIGN - 1)) * (1.0 / CHUNK_ALIGN)) * CHUNK_ALIGN
    n8_wide = jnp.broadcast_to(n8, (N_EXPERTS, LANES))
    ea = lax.broadcasted_iota(jnp.int32, (N_EXPERTS, N_EXPERTS), 0)
    eb = lax.broadcasted_iota(jnp.int32, (N_EXPERTS, N_EXPERTS), 1)
    lower = (eb < ea).astype(BF16)
    chunk_start = jnp.dot(lower, n8_wide.astype(BF16), preferred_element_type=F32)
    where_to = before + chunk_start[:, 0:1]
    for k in range(TOP_K):
        pk = jnp.sum(jnp.where(rows_e == idxs[k], where_to, 0.0), axis=0, keepdims=True)
        pos_ref[k:k + 1, :] = pk.astype(jnp.int32)
    gate_ref[...] = gate
    n8_int = n8_wide.astype(jnp.int32)
    n8_ref[...] = n8_int

    lane = lax.broadcasted_iota(jnp.int32, (N_EXPERTS, LANES), 1)
    flag = jnp.zeros((N_EXPERTS, LANES), F32)
    for j in range(CHUNK_BITS):
        flag = jnp.where((lane == j) & ((n8_int & (CHUNK_ALIGN << j)) != 0), 1.0, flag)
    rank = jnp.dot(lower, flag.astype(BF16), preferred_element_type=F32)
    e_id = lax.broadcasted_iota(jnp.int32, (N_EXPERTS, LANES), 0)
    for j in range(CHUNK_BITS):
        chosen = (rank[:, j:j + 1] == lane.astype(F32)) & (flag[:, j:j + 1] > 0.0)
        plan_ref[j:j + 1, :] = jnp.sum(jnp.where(chosen, e_id, 0), axis=0, keepdims=True)
    plan_ref[CHUNK_BITS:CHUNK_BITS + 1, :] = jnp.sum(flag, axis=0, keepdims=True).astype(jnp.int32)
    plan_ref[CHUNK_BITS + 1:CHUNK_BITS + 2, :] = jnp.sum(n8_int, axis=0, keepdims=True)


def _pack_halves(v, exact=False):
    h = v.shape[1] // 2
    return _pack_pair(v[:, :h], v[:, h:], exact)


def _pack_pair(lo, hi, exact=False):
    if not exact:
        lo, hi = lo.astype(BF16).astype(F32), hi.astype(BF16).astype(F32)
    lo = lax.bitcast_convert_type(lo, U32)
    hi = lax.bitcast_convert_type(hi, U32)
    return (lo >> 16) | (hi & jnp.uint32(HI_MASK))


def _chunk_relative(pos, c):
    rel = pos - c * SORT_CHUNK
    inside = (rel >= 0) & (rel < SORT_CHUNK)
    return jnp.where(inside, rel, -1).astype(F32).astype(BF16)


def _unpack_halves(w):
    lo = lax.bitcast_convert_type(w << 16, F32).astype(BF16)
    hi = lax.bitcast_convert_type(w & jnp.uint32(HI_MASK), F32).astype(BF16)
    return lo, hi


def _start_group_copies(plan_ref, make_copy):
    for j in range(CHUNK_BITS):
        size = CHUNK_ALIGN << j

        def body(k, carry):
            make_copy(pl.multiple_of(plan_ref[j * LANES + k], CHUNK_ALIGN),
                      pl.multiple_of(plan_ref[(PLAN_ROWS + j) * LANES + k], CHUNK_ALIGN), size).start()
            return carry

        lax.fori_loop(0, plan_ref[CHUNK_BITS * LANES + j], body, 0)


def _wait_group_rows(total_rows, ns, make_copy):
    for j in range(ns.bit_length()):
        size = CHUNK_ALIGN << j
        if size > ns:
            break

        @pl.when((total_rows & size) != 0)
        def _():
            make_copy(0, 0, size).wait()


def _dispatch_kernel(tot_ref, zrow_ref, plan_ref, pos_ref, x_ref, xs_ref, sbuf_ref, zbuf_ref,
                     onehot_ref, sem, zsem, *, tg, ns, bm):
    g = pl.program_id(0)
    ng = pl.num_programs(0)
    slot = g % 2

    @pl.when(g == 0)
    def _():
        zbuf_ref[...] = jnp.zeros_like(zbuf_ref)
        row = pl.multiple_of(zrow_ref[0], CHUNK_ALIGN)
        tail = pltpu.make_async_copy(zbuf_ref, xs_ref.at[pl.ds(row, bm), :], zsem)
        tail.start()
        tail.wait()

    def copy_from(sl):
        def make_copy(src, dst, size):
            return pltpu.make_async_copy(sbuf_ref.at[sl, pl.ds(src, size), :],
                                         xs_ref.at[pl.ds(dst, size), :], sem.at[sl])
        return make_copy

    @pl.when(g >= 2)
    def _():
        _wait_group_rows(tot_ref[g - 2], ns, copy_from(slot))

    xb = x_ref[...].astype(BF16)
    pos = pos_ref[...]
    local_rows = lax.broadcasted_iota(jnp.int32, (SORT_CHUNK, tg), 0).astype(F32).astype(BF16)
    one = jnp.ones((SORT_CHUNK, tg), BF16)
    zero = jnp.zeros((SORT_CHUNK, tg), BF16)
    for c in range(ns // SORT_CHUNK):
        rel = _chunk_relative(pos, c)
        hit = local_rows == rel[0:1, :]
        for k in range(1, TOP_K):
            hit = jnp.logical_or(hit, local_rows == rel[k:k + 1, :])
        onehot_ref[c * SORT_CHUNK:(c + 1) * SORT_CHUNK, :] = jnp.where(hit, one, zero)
    onehot = onehot_ref[...]
    half = x_ref.shape[1] // 2
    for n in range(half // SORT_CHUNK):
        lo = jnp.dot(onehot, xb[:, n * SORT_CHUNK:(n + 1) * SORT_CHUNK], preferred_element_type=F32)
        hi = jnp.dot(onehot, xb[:, half + n * SORT_CHUNK:half + (n + 1) * SORT_CHUNK],
                     preferred_element_type=F32)
        sbuf_ref[slot, :, n * SORT_CHUNK:(n + 1) * SORT_CHUNK] = _pack_pair(lo, hi, exact=True)

    _start_group_copies(plan_ref, copy_from(slot))

    @pl.when(g == ng - 1)
    def _():
        @pl.when(g >= 1)
        def _():
            _wait_group_rows(tot_ref[g - 1], ns, copy_from(1 - slot))
        _wait_group_rows(tot_ref[g], ns, copy_from(slot))


def _dispatch(x1, pos, plan, tot, zrow, n_rows, tg, ns, bm):
    t, d = x1.shape
    grid_spec = pltpu.PrefetchScalarGridSpec(
        num_scalar_prefetch=2,
        grid=(t // tg,),
        in_specs=[
            pl.BlockSpec((PLAN_WORDS,), lambda i, *_: (i,), memory_space=pltpu.SMEM),
            pl.BlockSpec((TOP_K, tg), lambda i, *_: (0, i)),
            pl.BlockSpec((tg, d), lambda i, *_: (i, 0)),
        ],
        out_specs=pl.BlockSpec(memory_space=pl.ANY),
        scratch_shapes=[pltpu.VMEM((2, ns, d // 2), U32), pltpu.VMEM((bm, d // 2), U32),
                        pltpu.VMEM((ns, tg), BF16),
                        pltpu.SemaphoreType.DMA((2,)), pltpu.SemaphoreType.DMA],
    )
    return pl.pallas_call(
        functools.partial(_dispatch_kernel, tg=tg, ns=ns, bm=bm),
        grid_spec=grid_spec,
        out_shape=jax.ShapeDtypeStruct((n_rows, d // 2), U32),
        compiler_params=_params("arbitrary"),
        name="moe_dispatch",
    )(tot, zrow, plan, pos, x1)


def _gmm_kernel(nq_ref, brow_ref, bval_ref, bord_ref, bfirst_ref, aexp_ref,
                xs_ref, w1_ref, w3_ref, w2_ref, y_ref,
                xbuf_ref, ybuf_ref, w1s_ref, w3s_ref, w2s_ref, w13_ref, w2b_ref, xsem, ysem, wsem,
                *, layer, bm):
    nq = nq_ref[0]
    na = nq_ref[1]
    f = w1s_ref.shape[2]
    half = xbuf_ref.shape[2]
    nx = xbuf_ref.shape[0]

    def x_copy(q, slot):
        row = pl.multiple_of(brow_ref[q], CHUNK_ALIGN)
        return pltpu.make_async_copy(xs_ref.at[pl.ds(row, bm), :], xbuf_ref.at[slot], xsem.at[slot])

    def w_copies(j, slot):
        e = aexp_ref[j]
        return (pltpu.make_async_copy(w1_ref.at[layer, e], w1s_ref.at[slot], wsem.at[slot]),
                pltpu.make_async_copy(w3_ref.at[layer, e], w3s_ref.at[slot], wsem.at[slot]),
                pltpu.make_async_copy(w2_ref.at[layer, e], w2s_ref.at[slot], wsem.at[slot]))

    def y_copy(slot, row, off, size):
        return pltpu.make_async_copy(ybuf_ref.at[slot, pl.ds(off, size), :],
                                     y_ref.at[pl.ds(pl.multiple_of(row + off, CHUNK_ALIGN), size), :],
                                     ysem.at[slot])

    def for_each_y_copy(slot, row, valid, fn):
        @pl.when(valid == bm)
        def _():
            fn(y_copy(slot, row, 0, bm))

        @pl.when(valid < bm)
        def _():
            for j in range((bm // CHUNK_ALIGN).bit_length() - 1):
                size = CHUNK_ALIGN << j

                @pl.when((valid & size) != 0)
                def _():
                    fn(y_copy(slot, row, pl.multiple_of(valid & (size - 1), CHUNK_ALIGN), size))

    for c in w_copies(0, 0):
        c.start()
    for q0 in range(nx - 1):
        @pl.when(q0 < nq)
        def _():
            x_copy(q0, q0).start()

    def body(q, carry):
        xslot = q % nx
        yslot = q % 2

        @pl.when(q + (nx - 1) < nq)
        def _():
            x_copy(q + (nx - 1), (q + (nx - 1)) % nx).start()

        @pl.when(bfirst_ref[q] == 1)
        def _():
            j = bord_ref[q]
            ws = j % 2
            for c in w_copies(j, ws):
                c.wait()

            @pl.when(j + 1 < na)
            def _():
                for c in w_copies(j + 1, 1 - ws):
                    c.start()
            w13_ref[:, :f] = w1s_ref[ws].astype(BF16)
            w13_ref[:, f:] = w3s_ref[ws].astype(BF16)
            w2b_ref[...] = w2s_ref[ws].astype(BF16)

        x_copy(q, xslot).wait()

        @pl.when(q >= 2)
        def _():
            for_each_y_copy(yslot, 0, bval_ref[jnp.maximum(q - 2, 0)], lambda c: c.wait())

        def swiglu(rows):
            x_lo, x_hi = _unpack_halves(xbuf_ref[xslot, 0:rows, :])
            h13 = (jnp.dot(x_lo, w13_ref[:half, :], preferred_element_type=F32)
                   + jnp.dot(x_hi, w13_ref[half:, :], preferred_element_type=F32))
            h = jax.nn.silu(h13[:, :f]) * h13[:, f:]
            ybuf_ref[yslot, 0:rows, :] = _pack_halves(
                jnp.dot(h.astype(BF16), w2b_ref[...], preferred_element_type=F32))

        @pl.when(bval_ref[q] > bm // 2)
        def _():
            swiglu(bm)

        @pl.when(bval_ref[q] <= bm // 2)
        def _():
            swiglu(bm // 2)
        for_each_y_copy(yslot, brow_ref[q], bval_ref[q], lambda c: c.start())
        return carry

    lax.fori_loop(0, nq, body, 0)

    @pl.when(nq >= 2)
    def _():
        for_each_y_copy(nq % 2, 0, bval_ref[jnp.maximum(nq - 2, 0)], lambda c: c.wait())
    for_each_y_copy((nq - 1) % 2, 0, bval_ref[nq - 1], lambda c: c.wait())


def _gmm(xs, tables, w1, w3, w2, layer, bm):
    n_rows, half = xs.shape
    d = 2 * half
    f = w1.shape[-1]
    any_spec = pl.BlockSpec(memory_space=pl.ANY)
    grid_spec = pltpu.PrefetchScalarGridSpec(
        num_scalar_prefetch=len(tables),
        grid=(1,),
        in_specs=[any_spec, any_spec, any_spec, any_spec],
        out_specs=any_spec,
        scratch_shapes=[pltpu.VMEM((GMM_X_SLOTS, bm, half), U32), pltpu.VMEM((2, bm, half), U32),
                        pltpu.VMEM((2, d, f), F32), pltpu.VMEM((2, d, f), F32), pltpu.VMEM((2, f, d), F32),
                        pltpu.VMEM((d, 2 * f), BF16), pltpu.VMEM((f, d), BF16),
                        pltpu.SemaphoreType.DMA((GMM_X_SLOTS,)), pltpu.SemaphoreType.DMA((2,)),
                        pltpu.SemaphoreType.DMA((2,))],
    )
    return pl.pallas_call(
        functools.partial(_gmm_kernel, layer=layer, bm=bm),
        grid_spec=grid_spec,
        out_shape=jax.ShapeDtypeStruct((n_rows, half), U32),
        compiler_params=_params("arbitrary"),
        name="moe_experts",
    )(*tables, xs, w1, w3, w2)


def _combine_kernel(tot_ref, plan_ref, plan_next_ref, yb_ref, pos_ref, gate_ref, x_ref, ws13_ref, ws2_ref,
                    g_ref, b_ref, o_ref, ybuf_ref, wgt_ref, ylo_ref, yhi_ref, sem, *, alpha, tg, ns):
    g = pl.program_id(0)
    ng = pl.num_programs(0)
    slot = g % 2

    def copy_into(sl):
        def make_copy(buf_row, hbm_row, size):
            return pltpu.make_async_copy(yb_ref.at[pl.ds(hbm_row, size), :],
                                         ybuf_ref.at[sl, pl.ds(buf_row, size), :], sem.at[sl])
        return make_copy

    @pl.when(g == 0)
    def _():
        ybuf_ref[...] = jnp.zeros_like(ybuf_ref)
        _start_group_copies(plan_ref, copy_into(0))

    @pl.when(g + 1 < ng)
    def _():
        _start_group_copies(plan_next_ref, copy_into(1 - slot))

    x = x_ref[...]
    f = ws2_ref.shape[0]
    h13 = jnp.dot(x.astype(BF16), ws13_ref[...], preferred_element_type=F32)
    hs = jax.nn.silu(h13[:, :f]) * h13[:, f:]
    shared = jnp.dot(hs.astype(BF16), ws2_ref[...], preferred_element_type=F32)

    _wait_group_rows(tot_ref[g], ns, copy_into(slot))

    pos = pos_ref[...]
    gate_b = gate_ref[...].astype(BF16)
    local_rows = lax.broadcasted_iota(jnp.int32, (SORT_CHUNK, tg), 0).astype(F32).astype(BF16)
    for c in range(ns // SORT_CHUNK):
        rel = _chunk_relative(pos, c)
        wgt = jnp.zeros((SORT_CHUNK, tg), BF16)
        for k in range(TOP_K):
            wgt = jnp.where(local_rows == rel[k:k + 1, :], gate_b[k:k + 1, :], wgt)
        wgt_ref[c * SORT_CHUNK:(c + 1) * SORT_CHUNK, :] = wgt
        y_lo, y_hi = _unpack_halves(ybuf_ref[slot, c * SORT_CHUNK:(c + 1) * SORT_CHUNK, :])
        ylo_ref[c * SORT_CHUNK:(c + 1) * SORT_CHUNK, :] = y_lo
        yhi_ref[c * SORT_CHUNK:(c + 1) * SORT_CHUNK, :] = y_hi
    wgt_all = wgt_ref[...]
    dn = (((0,), (0,)), ((), ()))
    routed = jnp.concatenate([lax.dot_general(wgt_all, ylo_ref[...], dn, preferred_element_type=F32),
                              lax.dot_general(wgt_all, yhi_ref[...], dn, preferred_element_type=F32)],
                             axis=1)
    o_ref[...] = _layer_norm(alpha * x + (routed + shared), g_ref[...], b_ref[...])


def _combine(yb, pos_kt, gate_kt, plan, tot, x1, ws1, ws3, ws2, g, b, alpha, tg, ns):
    t, d = x1.shape
    ng = t // tg
    f = ws1.shape[1]
    ws13 = jnp.concatenate([ws1, ws3], axis=1).astype(BF16)
    const = lambda i, *_: (0, 0)
    tile = lambda i, *_: (i, 0)
    row = lambda v: v.reshape(1, d)
    grid_spec = pltpu.PrefetchScalarGridSpec(
        num_scalar_prefetch=1,
        grid=(ng,),
        in_specs=[
            pl.BlockSpec((PLAN_WORDS,), lambda i, *_: (i,), memory_space=pltpu.SMEM),
            pl.BlockSpec((PLAN_WORDS,), lambda i, *_: (jnp.minimum(i + 1, ng - 1),),
                         memory_space=pltpu.SMEM),
            pl.BlockSpec(memory_space=pl.ANY),
            pl.BlockSpec((TOP_K, tg), lambda i, *_: (0, i)),
            pl.BlockSpec((TOP_K, tg), lambda i, *_: (0, i)),
            pl.BlockSpec((tg, d), tile),
            pl.BlockSpec((d, 2 * f), const),
            pl.BlockSpec((f, d), const),
            pl.BlockSpec((1, d), const),
            pl.BlockSpec((1, d), const),
        ],
        out_specs=pl.BlockSpec((tg, d), tile),
        scratch_shapes=[pltpu.VMEM((2, ns, d // 2), U32), pltpu.VMEM((ns, tg), BF16),
                        pltpu.VMEM((ns, d // 2), BF16), pltpu.VMEM((ns, d // 2), BF16),
                        pltpu.SemaphoreType.DMA((2,))],
    )
    return pl.pallas_call(
        functools.partial(_combine_kernel, alpha=alpha, tg=tg, ns=ns),
        grid_spec=grid_spec,
        out_shape=jax.ShapeDtypeStruct((t, d), F32),
        compiler_params=_params("arbitrary"),
        name="moe_combine",
    )(tot, plan, plan, yb, pos_kt, gate_kt, x1, ws13, ws2.astype(BF16), row(g), row(b))


def _moe_layer(x1, routing, tg, layer, w1, w3, w2, ws1, ws3, ws2, g, b, alpha):
    t, d = x1.shape
    ng = t // tg
    bm = EXPERT_BM
    ns = -(-(tg * TOP_K + N_EXPERTS * (CHUNK_ALIGN - 1)) // SORT_CHUNK) * SORT_CHUNK
    pos, gate, n8_wide, plan_wide = routing
    plan3 = plan_wide.reshape(ng, PLAN_ROWS, LANES)
    tot = plan3[:, CHUNK_BITS + 1, 0]

    n8 = n8_wide[:, 0].reshape(ng, N_EXPERTS)
    goff = jnp.cumsum(n8, axis=1) - n8
    seg = jnp.sum(n8, axis=0)
    pend = jnp.cumsum(seg)
    pstart = pend - seg
    gdest = pstart[None, :] + jnp.cumsum(n8, axis=0) - n8
    max_rows = ng * (tg * TOP_K + N_EXPERTS * (CHUNK_ALIGN - 1))
    n_rows = max_rows + bm

    experts = jnp.arange(N_EXPERTS, dtype=jnp.int32)
    nb = (seg + bm - 1) // bm
    cb = jnp.cumsum(nb)
    blk = jnp.arange(max_rows // bm + N_EXPERTS, dtype=jnp.int32)
    blk_expert = jnp.minimum(jnp.sum((cb[None, :] <= blk[:, None]).astype(jnp.int32), axis=1), N_EXPERTS - 1)
    mine = blk_expert[:, None] == experts[None, :]
    of_expert = lambda v: jnp.sum(jnp.where(mine, v[None, :], 0), axis=1)
    local = blk - of_expert(cb - nb)
    active = seg > 0
    ordinal = jnp.cumsum(active.astype(jnp.int32)) - 1
    active_experts = jnp.sum(jnp.where((ordinal[None, :] == experts[:, None]) & active[None, :],
                                       experts[None, :], 0), axis=1)
    tables = [jnp.stack([cb[-1], jnp.sum(active.astype(jnp.int32))]),
              of_expert(pstart) + local * bm,
              jnp.clip(of_expert(seg) - local * bm, 0, bm),
              of_expert(ordinal),
              (local == 0).astype(jnp.int32),
              active_experts]
    tables = [a.astype(jnp.int32) for a in tables]
    zrow = pend[-1:].astype(jnp.int32)

    lists = plan3[:, :CHUNK_BITS, :N_EXPERTS]
    sizes = CHUNK_ALIGN << jnp.arange(CHUNK_BITS, dtype=jnp.int32)
    done = n8[:, None, :] & (sizes[None, :, None] - 1)
    pick = lists[..., None] == jnp.arange(N_EXPERTS, dtype=jnp.int32)
    src = jnp.sum(jnp.where(pick, (goff[:, None, :] + done)[:, :, None, :], 0), axis=-1)
    dst = jnp.sum(jnp.where(pick, (gdest[:, None, :] + done)[:, :, None, :], 0), axis=-1)
    widen = lambda a: jnp.pad(a, ((0, 0), (0, 0), (0, LANES - N_EXPERTS)))
    plan = jnp.concatenate([widen(src), plan3[:, CHUNK_BITS:], widen(dst),
                            jnp.zeros((ng, PLAN_ROWS - CHUNK_BITS, LANES), jnp.int32)],
                           axis=1).astype(jnp.int32).reshape(-1)

    xs = _dispatch(x1, pos, plan, tot, zrow, n_rows, tg, ns, bm)
    yb = _gmm(xs, tables, w1, w3, w2, layer, bm)
    return _combine(yb, pos, gate, plan, tot, x1, ws1, ws3, ws2, g, b, alpha, tg, ns)


def kernel(x, ln_gain, ln_bias, pool_w, pool_scale, attn_wqkv, attn_bqkv, attn_sinks, attn_wo, attn_bo,
           conv_w_in, conv_w, conv_w_out, router_w, router_bias, expert_w1, expert_w3, expert_w2,
           shared_w1, shared_w3, shared_w2):
    bsz, seq, d = x.shape
    depth = ln_gain.shape[0]
    alpha = float((2 * depth) ** 0.25)
    h = x.reshape(bsz * seq, d)
    tg = min(GROUP_TG, seq)
    for i in range(depth):
        kind = i % N_MIXERS
        j = i // N_MIXERS
        g1, b1 = ln_gain[i, 0], ln_bias[i, 0]
        route = (router_w[i], router_bias[i], tg)
        if kind == 0:
            h, *routing = _pool_mixer(h, seq, pool_w[j], pool_scale[j], g1, b1, alpha, *route)
        elif kind == 1:
            h, *routing = _swa_mixer(h, seq, attn_wqkv[j], attn_bqkv[j], attn_sinks[j], attn_wo[j],
                                     attn_bo[j], g1, b1, alpha, *route)
        else:
            h, *routing = _conv_mixer(h, seq, conv_w_in[j], conv_w[j], conv_w_out[j], g1, b1, alpha, *route)
        h = _moe_layer(h, routing, tg, i, expert_w1, expert_w3, expert_w2,
                       shared_w1[i], shared_w3[i], shared_w2[i], ln_gain[i, 1], ln_bias[i, 1], alpha)
    return h.reshape(bsz, seq, d)
```

```python
import functools

import jax
import jax.numpy as jnp
from jax import lax
from jax.experimental import pallas as pl
from jax.experimental.pallas import tpu as pltpu

POOL_WINDOWS = (2, 4, 8, 16)
N_MIXERS = 3
HEAD_DIM = 64
N_Q_HEADS = 16
N_KV_HEADS = 4
GQA_GROUP = N_Q_HEADS // N_KV_HEADS
WINDOW = 128
ROPE_THETA = 10000.0
N_EXPERTS = 64
TOP_K = 8
N_EXPERT_GROUPS = 8
GROUP_SIZE = N_EXPERTS // N_EXPERT_GROUPS
TOPK_GROUPS = 4
ROUTED_SCALE = 2.5
LN_EPS = 1e-5

LANES = 128
SUBLANES = 8
POOL_HALO = 16
VMEM_LIMIT = 48 * 1024 * 1024

POOL_TT = 512
QKV_TT = 512
ATTN_TQ = 512
CONV_TT = 512
GROUP_TG = 256
CHUNK_ALIGN = SUBLANES
SORT_CHUNK = 256
EXPERT_BM = 512
GMM_X_SLOTS = 3
CHUNK_BITS = 6
PLAN_ROWS = SUBLANES
PLAN_WORDS = 2 * PLAN_ROWS * LANES

U32 = jnp.uint32
HI_MASK = 0xFFFF0000

F32 = jnp.float32
BF16 = jnp.bfloat16
NEG_INF = float("-inf")


def _layer_norm(z, g, b):
    mu = jnp.mean(z, axis=-1, keepdims=True)
    zc = z - mu
    var = jnp.mean(zc * zc, axis=-1, keepdims=True)
    return zc * lax.rsqrt(var + LN_EPS) * g + b


def _params(*sem):
    return pltpu.CompilerParams(dimension_semantics=sem, vmem_limit_bytes=VMEM_LIMIT)


def _route_tile(o_ref, router_refs, route_out_refs, tg):
    wt_ref, bias_ref = router_refs
    pos_ref, gate_ref, n8_ref, plan_ref = route_out_refs
    for s in range(o_ref.shape[0] // tg):
        _route_group(o_ref.at[pl.ds(s * tg, tg), :], wt_ref, bias_ref,
                     pos_ref.at[:, pl.ds(s * tg, tg)], gate_ref.at[:, pl.ds(s * tg, tg)],
                     n8_ref.at[pl.ds(s * N_EXPERTS, N_EXPERTS), :],
                     plan_ref.at[pl.ds(s * PLAN_ROWS, PLAN_ROWS), :], tt=tg)


def _router_plumbing(router_w, router_bias, t, d, tt, tg, tile_index):
    groups = tt // tg
    const = lambda *ids: (0, 0)
    by_cols = lambda *ids: (0, tile_index(*ids))
    by_rows = lambda *ids: (tile_index(*ids), 0)
    operands = [router_w.T, router_bias.reshape(N_EXPERTS, 1)]
    in_specs = [pl.BlockSpec((N_EXPERTS, d), const), pl.BlockSpec((N_EXPERTS, 1), const)]
    out_specs = [pl.BlockSpec((TOP_K, tt), by_cols), pl.BlockSpec((TOP_K, tt), by_cols),
                 pl.BlockSpec((groups * N_EXPERTS, LANES), by_rows),
                 pl.BlockSpec((groups * PLAN_ROWS, LANES), by_rows)]
    out_shapes = [jax.ShapeDtypeStruct((TOP_K, t), jnp.int32), jax.ShapeDtypeStruct((TOP_K, t), F32),
                  jax.ShapeDtypeStruct((t // tg * N_EXPERTS, LANES), jnp.int32),
                  jax.ShapeDtypeStruct((t // tg * PLAN_ROWS, LANES), jnp.int32)]
    return operands, in_specs, out_specs, out_shapes


def _pool_kernel(x_ref, halo_ref, w_ref, scale_ref, g_ref, b_ref, wt_ref, rb_ref, o_ref, pos_ref, gate_ref,
                 n8_ref, plan_ref, buf_ref, *, alpha, tt, tg):
    i = pl.program_id(1)
    x = x_ref[...]
    ch = x.shape[1] // len(POOL_WINDOWS)
    buf_ref[0:POOL_HALO, :] = jnp.where(i > 0, halo_ref[...], 0.0)
    buf_ref[POOL_HALO:POOL_HALO + tt, :] = x
    pos = i * tt + lax.broadcasted_iota(jnp.int32, (tt, 1), 0)
    ys = []
    for g, w in enumerate(POOL_WINDOWS):
        c0 = g * ch
        xg = x[:, c0:c0 + ch]
        s = xg
        for j in range(1, w):
            s = s + buf_ref[POOL_HALO - j:POOL_HALO - j + tt, c0:c0 + ch]
        inv_cnt = 1.0 / jnp.minimum(pos + 1, w).astype(F32)
        pooled = s * inv_cnt - xg
        ys.append(jnp.dot(pooled.astype(BF16), w_ref[g], preferred_element_type=F32))
    y = jnp.concatenate(ys, axis=1) * scale_ref[...]
    o_ref[...] = _layer_norm(alpha * x + y, g_ref[...], b_ref[...])
    _route_tile(o_ref, (wt_ref, rb_ref), (pos_ref, gate_ref, n8_ref, plan_ref), tg)


def _pool_mixer(x2, seq, w_grp, scale, g, b, alpha, router_w, router_bias, tg):
    t, d = x2.shape
    tt = min(POOL_TT, seq)
    nt = seq // tt
    nb = t // seq
    row = lambda v: v.reshape(1, d)
    const2 = lambda bi, i: (0, 0)
    tile = lambda bi, i: bi * nt + i
    r_ops, r_in, r_out, r_shapes = _router_plumbing(router_w, router_bias, t, d, tt, tg, tile)
    return pl.pallas_call(
        functools.partial(_pool_kernel, alpha=alpha, tt=tt, tg=tg),
        grid=(nb, nt),
        in_specs=[
            pl.BlockSpec((tt, d), lambda bi, i: (bi * nt + i, 0)),
            pl.BlockSpec((POOL_HALO, d),
                         lambda bi, i: (jnp.maximum((bi * seq + i * tt) // POOL_HALO - 1, 0), 0)),
            pl.BlockSpec(w_grp.shape, lambda bi, i: (0, 0, 0)),
            pl.BlockSpec((1, d), const2),
            pl.BlockSpec((1, d), const2),
            pl.BlockSpec((1, d), const2),
        ] + r_in,
        out_specs=[pl.BlockSpec((tt, d), lambda bi, i: (bi * nt + i, 0))] + r_out,
        out_shape=[jax.ShapeDtypeStruct((t, d), F32)] + r_shapes,
        scratch_shapes=[pltpu.VMEM((POOL_HALO + tt, d), F32)],
        compiler_params=_params("parallel", "parallel"),
        name="pool_mixer",
    )(x2, x2, w_grp.astype(BF16), row(scale), row(g), row(b), *r_ops)


def _qkv_kernel(x_ref, w_ref, b_ref, c_ref, s_ref, q_ref, k_ref, v_ref, *, nq, nk):
    y = jnp.dot(x_ref[...].astype(BF16), w_ref[...], preferred_element_type=F32) + b_ref[...]
    c = c_ref[...]
    s = s_ref[...]
    lane = lax.broadcasted_iota(jnp.int32, c.shape, 1)
    first_half = (lane % HEAD_DIM) < (HEAD_DIM // 2)

    def rope(v):
        partner = jnp.where(first_half, pltpu.roll(v, LANES - HEAD_DIM // 2, 1),
                            pltpu.roll(v, HEAD_DIM // 2, 1))
        return v * c + partner * s

    for j in range(nq // LANES):
        q_ref[:, j * LANES:(j + 1) * LANES] = (
            rope(y[:, j * LANES:(j + 1) * LANES]) * (HEAD_DIM ** -0.5)).astype(BF16)
    for j in range(nk // LANES):
        k_ref[:, j * LANES:(j + 1) * LANES] = rope(
            y[:, nq + j * LANES:nq + (j + 1) * LANES]).astype(BF16)
    v_ref[...] = y[:, nq + nk:].astype(BF16)


def _attn_kernel(sink_ref, q_ref, k_ref, kh_ref, v_ref, vh_ref, x_ref, wo_ref, bo_ref, g_ref, b_ref,
                 wt_ref, rb_ref, o_ref, pos_ref, gate_ref, n8_ref, plan_ref, *, alpha, tq, tg):
    i = pl.program_id(1)
    nqb = tq // WINDOW
    r = lax.broadcasted_iota(jnp.int32, (tq, 2 * WINDOW), 0)
    c = lax.broadcasted_iota(jnp.int32, (tq, 2 * WINDOW), 1)
    dist = (r & (WINDOW - 1)) + WINDOW - c
    valid = (dist >= 0) & (dist < WINDOW) & ((c >= WINDOW) | (r >= WINDOW) | (i > 0))
    lane = lax.broadcasted_iota(jnp.int32, (tq, LANES), 1)
    low = lane < HEAD_DIM
    dn = (((1,), (1,)), ((), ()))
    outs = []
    for kv in range(N_KV_HEADS):
        sl = slice(kv * LANES, (kv + 1) * LANES)
        kk = jnp.concatenate([kh_ref[:, sl], k_ref[:, sl]], axis=0)
        vv = jnp.concatenate([vh_ref[:, sl], v_ref[:, sl]], axis=0)
        for p in range(GQA_GROUP // 2):
            c0 = (kv * GQA_GROUP + 2 * p) * HEAD_DIM
            qp = q_ref[:, c0:c0 + LANES]
            res = []
            for half in range(2):
                keep = low if half == 0 else jnp.logical_not(low)
                qh = jnp.where(keep, qp, jnp.zeros_like(qp))
                sc_all = lax.dot_general(qh, kk, dn, preferred_element_type=F32)
                sink = sink_ref[kv * GQA_GROUP + 2 * p + half]
                sc = jnp.concatenate([sc_all[qb * WINDOW:(qb + 1) * WINDOW, qb * WINDOW:(qb + 2) * WINDOW]
                                      for qb in range(nqb)], axis=0)
                sc = jnp.where(valid, sc, NEG_INF)
                m = jnp.maximum(jnp.max(sc, axis=1, keepdims=True), sink)
                pm = jnp.exp(sc - m)
                rden = 1.0 / (jnp.sum(pm, axis=1, keepdims=True) + jnp.exp(sink - m))
                pm = pm.astype(BF16)
                res.append(jnp.concatenate(
                    [jnp.dot(pm[qb * WINDOW:(qb + 1) * WINDOW, :], vv[qb * WINDOW:(qb + 2) * WINDOW, :],
                             preferred_element_type=F32) for qb in range(nqb)], axis=0) * rden)
            outs.append(jnp.where(low, res[0], res[1]))
    o_all = jnp.concatenate(outs, axis=1).astype(BF16)
    mix = jnp.dot(o_all, wo_ref[...], preferred_element_type=F32) + bo_ref[...]
    o_ref[...] = _layer_norm(alpha * x_ref[...] + mix, g_ref[...], b_ref[...])
    _route_tile(o_ref, (wt_ref, rb_ref), (pos_ref, gate_ref, n8_ref, plan_ref), tg)


def _dup_heads(w, n_heads):
    lead = w.shape[:-1]
    w = w.reshape(lead + (n_heads, 1, HEAD_DIM))
    w = jnp.broadcast_to(w, lead + (n_heads, 2, HEAD_DIM))
    return w.reshape(lead + (n_heads * 2 * HEAD_DIM,))


def _swa_mixer(x2, seq, wqkv, bqkv, sinks, wo, bo, g, b, alpha, router_w, router_bias, tg):
    t, d = x2.shape
    nb = t // seq
    nq = N_Q_HEADS * HEAD_DIM
    nkv = N_KV_HEADS * HEAD_DIM
    nk2 = 2 * nkv
    w_ext = jnp.concatenate([wqkv[:, :nq], _dup_heads(wqkv[:, nq:nq + nkv], N_KV_HEADS),
                             _dup_heads(wqkv[:, nq + nkv:], N_KV_HEADS)], axis=1).astype(BF16)
    b_ext = jnp.concatenate([bqkv[:nq], _dup_heads(bqkv[nq:nq + nkv], N_KV_HEADS),
                             _dup_heads(bqkv[nq + nkv:], N_KV_HEADS)]).reshape(1, -1)
    n_ext = nq + 2 * nk2

    pos = jnp.arange(seq, dtype=F32)
    inv_freq = ROPE_THETA ** (-jnp.arange(0, HEAD_DIM, 2, dtype=F32) / HEAD_DIM)
    ang = pos[:, None] * inv_freq[None, :]
    cos, sin = jnp.cos(ang), jnp.sin(ang)
    cos_t = jnp.concatenate([cos, cos, cos, cos], axis=1)
    sin_t = jnp.concatenate([-sin, sin, -sin, sin], axis=1)

    tt = min(QKV_TT, seq)
    nt = seq // tt
    const2 = lambda i: (0, 0)
    q, k2, v2 = pl.pallas_call(
        functools.partial(_qkv_kernel, nq=nq, nk=nk2),
        grid=(t // tt,),
        in_specs=[
            pl.BlockSpec((tt, d), lambda i: (i, 0)),
            pl.BlockSpec((d, n_ext), const2),
            pl.BlockSpec((1, n_ext), const2),
            pl.BlockSpec((tt, LANES), lambda i: (i % nt, 0)),
            pl.BlockSpec((tt, LANES), lambda i: (i % nt, 0)),
        ],
        out_specs=[pl.BlockSpec((tt, nq), lambda i: (i, 0)),
                   pl.BlockSpec((tt, nk2), lambda i: (i, 0)),
                   pl.BlockSpec((tt, nk2), lambda i: (i, 0))],
        out_shape=[jax.ShapeDtypeStruct((t, nq), BF16),
                   jax.ShapeDtypeStruct((t, nk2), BF16),
                   jax.ShapeDtypeStruct((t, nk2), BF16)],
        compiler_params=_params("parallel"),
        name="swa_qkv",
    )(x2, w_ext, b_ext, cos_t, sin_t)

    tq = min(ATTN_TQ, seq)
    ntq = seq // tq
    main = lambda bi, i: (bi * ntq + i, 0)
    halo = lambda bi, i: (jnp.maximum((bi * seq + i * tq) // WINDOW - 1, 0), 0)
    const = lambda bi, i: (0, 0)
    row = lambda v: v.reshape(1, d)
    r_ops, r_in, r_out, r_shapes = _router_plumbing(router_w, router_bias, t, d, tq, tg,
                                                    lambda bi, i: bi * ntq + i)
    return pl.pallas_call(
        functools.partial(_attn_kernel, alpha=alpha, tq=tq, tg=tg),
        grid=(nb, ntq),
        in_specs=[
            pl.BlockSpec(memory_space=pltpu.SMEM),
            pl.BlockSpec((tq, nq), main),
            pl.BlockSpec((tq, nk2), main),
            pl.BlockSpec((WINDOW, nk2), halo),
            pl.BlockSpec((tq, nk2), main),
            pl.BlockSpec((WINDOW, nk2), halo),
            pl.BlockSpec((tq, d), main),
            pl.BlockSpec((nq, d), const),
            pl.BlockSpec((1, d), const),
            pl.BlockSpec((1, d), const),
            pl.BlockSpec((1, d), const),
        ] + r_in,
        out_specs=[pl.BlockSpec((tq, d), main)] + r_out,
        out_shape=[jax.ShapeDtypeStruct((t, d), F32)] + r_shapes,
        compiler_params=_params("parallel", "parallel"),
        name="swa_attention",
    )(sinks, q, k2, k2, v2, v2, x2, wo.astype(BF16), row(bo), row(g), row(b), *r_ops)


def _conv_kernel(x_ref, win_ref, cw_ref, wout_ref, g_ref, b_ref, wt_ref, rb_ref, o_ref, pos_ref, gate_ref,
                 n8_ref, plan_ref, ubuf_ref, *, alpha, tt, tg):
    i = pl.program_id(1)
    d = x_ref.shape[1]

    @pl.when(i == 0)
    def _():
        ubuf_ref[0:SUBLANES, :] = jnp.zeros((SUBLANES, d), F32)

    x = x_ref[...]
    proj = jnp.dot(x.astype(BF16), win_ref[...], preferred_element_type=F32)
    gate_b = proj[:, :d]
    u = proj[:, d:2 * d] * proj[:, 2 * d:]
    ubuf_ref[SUBLANES:SUBLANES + tt, :] = u
    cw = cw_ref[...]
    conv = (cw[0:1, :] * ubuf_ref[SUBLANES - 2:SUBLANES - 2 + tt, :]
            + cw[1:2, :] * ubuf_ref[SUBLANES - 1:SUBLANES - 1 + tt, :]
            + cw[2:3, :] * u)
    ubuf_ref[0:SUBLANES, :] = u[tt - SUBLANES:tt, :]
    y = jnp.dot((gate_b * conv).astype(BF16), wout_ref[...], preferred_element_type=F32)
    o_ref[...] = _layer_norm(alpha * x + y, g_ref[...], b_ref[...])
    _route_tile(o_ref, (wt_ref, rb_ref), (pos_ref, gate_ref, n8_ref, plan_ref), tg)


def _conv_mixer(x2, seq, w_in, conv_w, w_out, g, b, alpha, router_w, router_bias, tg):
    t, d = x2.shape
    nb = t // seq
    tt = min(CONV_TT, seq)
    nt = seq // tt
    const = lambda bi, i: (0, 0)
    row = lambda v: v.reshape(1, d)
    r_ops, r_in, r_out, r_shapes = _router_plumbing(router_w, router_bias, t, d, tt, tg,
                                                    lambda bi, i: bi * nt + i)
    return pl.pallas_call(
        functools.partial(_conv_kernel, alpha=alpha, tt=tt, tg=tg),
        grid=(nb, nt),
        in_specs=[
            pl.BlockSpec((tt, d), lambda bi, i: (bi * nt + i, 0)),
            pl.BlockSpec((d, 3 * d), const),
            pl.BlockSpec(conv_w.shape, const),
            pl.BlockSpec((d, d), const),
            pl.BlockSpec((1, d), const),
            pl.BlockSpec((1, d), const),
        ] + r_in,
        out_specs=[pl.BlockSpec((tt, d), lambda bi, i: (bi * nt + i, 0))] + r_out,
        out_shape=[jax.ShapeDtypeStruct((t, d), F32)] + r_shapes,
        scratch_shapes=[pltpu.VMEM((SUBLANES + tt, d), F32)],
        compiler_params=_params("arbitrary", "arbitrary"),
        name="conv_mixer",
    )(x2, w_in.astype(BF16), conv_w, w_out.astype(BF16), row(g), row(b), *r_ops)


def _first_argmax(v, rows, n):
    m = jnp.max(v, axis=0, keepdims=True)
    first = jnp.min(jnp.where(v == m, rows, n), axis=0, keepdims=True)
    return m, first


def _route_group(x_ref, wt_ref, bias_ref, pos_ref, gate_ref, n8_ref, plan_ref, *, tt):
    x = x_ref[...]
    xh = x.astype(BF16)
    xl = (x - xh.astype(F32)).astype(BF16)
    wt = wt_ref[...]
    wh = wt.astype(BF16)
    wl = (wt - wh.astype(F32)).astype(BF16)
    dn = (((1,), (1,)), ((), ()))
    logits = (lax.dot_general(wh, xh, dn, preferred_element_type=F32)
              + (lax.dot_general(wh, xl, dn, preferred_element_type=F32)
                 + lax.dot_general(wl, xh, dn, preferred_element_type=F32)))
    scores = jax.nn.sigmoid(logits)
    sel = scores + bias_ref[...]

    rows8 = lax.broadcasted_iota(jnp.int32, (GROUP_SIZE, tt), 0)
    gsc = []
    for gi in range(N_EXPERT_GROUPS):
        blk = sel[gi * GROUP_SIZE:(gi + 1) * GROUP_SIZE, :]
        m1, f1 = _first_argmax(blk, rows8, GROUP_SIZE)
        m2 = jnp.max(jnp.where(rows8 == f1, NEG_INF, blk), axis=0, keepdims=True)
        gsc.append(m1 + m2)
    cur = jnp.concatenate(gsc, axis=0)
    rows_g = lax.broadcasted_iota(jnp.int32, (N_EXPERT_GROUPS, tt), 0)
    gmask = jnp.zeros((N_EXPERT_GROUPS, tt), F32)
    for _ in range(TOPK_GROUPS):
        _, f = _first_argmax(cur, rows_g, N_EXPERT_GROUPS)
        hit = rows_g == f
        gmask = jnp.where(hit, 1.0, gmask)
        cur = jnp.where(hit, NEG_INF, cur)
    emask = jnp.concatenate(
        [jnp.broadcast_to(gmask[gi:gi + 1, :], (GROUP_SIZE, tt)) for gi in range(N_EXPERT_GROUPS)],
        axis=0)

    rows_e = lax.broadcasted_iota(jnp.int32, (N_EXPERTS, tt), 0)
    cur = jnp.where(emask > 0.0, sel, NEG_INF)
    idxs, gates = [], []
    member = jnp.zeros((N_EXPERTS, tt), F32)
    for _ in range(TOP_K):
        _, f = _first_argmax(cur, rows_e, N_EXPERTS)
        hit = rows_e == f
        idxs.append(f)
        gates.append(jnp.sum(jnp.where(hit, scores, 0.0), axis=0, keepdims=True))
        member = jnp.where(hit, 1.0, member)
        cur = jnp.where(hit, NEG_INF, cur)
    gate = jnp.concatenate(gates, axis=0)
    gate = gate / jnp.sum(gate, axis=0, keepdims=True) * ROUTED_SCALE

    ta = lax.broadcasted_iota(jnp.int32, (tt, tt), 0)
    tb = lax.broadcasted_iota(jnp.int32, (tt, tt), 1)
    earlier = (ta < tb).astype(BF16)
    before = jnp.dot(member.astype(BF16), earlier, preferred_element_type=F32)
    n = jnp.sum(member, axis=1, keepdims=True)
    n8 = jnp.floor((n + (CHUNK_ALIGN - 1)) * (1.0 / CHUNK_ALIGN)) * CHUNK_ALIGN
    n8_wide = jnp.broadcast_to(n8, (N_EXPERTS, LANES))
    ea = lax.broadcasted_iota(jnp.int32, (N_EXPERTS, N_EXPERTS), 0)
    eb = lax.broadcasted_iota(jnp.int32, (N_EXPERTS, N_EXPERTS), 1)
    lower = (eb < ea).astype(BF16)
    chunk_start = jnp.dot(lower, n8_wide.astype(BF16), preferred_element_type=F32)
    where_to = before + chunk_start[:, 0:1]
    for k in range(TOP_K):
        pk = jnp.sum(jnp.where(rows_e == idxs[k], where_to, 0.0), axis=0, keepdims=True)
        pos_ref[k:k + 1, :] = pk.astype(jnp.int32)
    gate_ref[...] = gate
    n8_int = n8_wide.astype(jnp.int32)
    n8_ref[...] = n8_int

    lane = lax.broadcasted_iota(jnp.int32, (N_EXPERTS, LANES), 1)
    flag = jnp.zeros((N_EXPERTS, LANES), F32)
    for j in range(CHUNK_BITS):
        flag = jnp.where((lane == j) & ((n8_int & (CHUNK_ALIGN << j)) != 0), 1.0, flag)
    rank = jnp.dot(lower, flag.astype(BF16), preferred_element_type=F32)
    e_id = lax.broadcasted_iota(jnp.int32, (N_EXPERTS, LANES), 0)
    for j in range(CHUNK_BITS):
        chosen = (rank[:, j:j + 1] == lane.astype(F32)) & (flag[:, j:j + 1] > 0.0)
        plan_ref[j:j + 1, :] = jnp.sum(jnp.where(chosen, e_id, 0), axis=0, keepdims=True)
    plan_ref[CHUNK_BITS:CHUNK_BITS + 1, :] = jnp.sum(flag, axis=0, keepdims=True).astype(jnp.int32)
    plan_ref[CHUNK_BITS + 1:CHUNK_BITS + 2, :] = jnp.sum(n8_int, axis=0, keepdims=True)


def _pack_halves(v, exact=False):
    h = v.shape[1] // 2
    return _pack_pair(v[:, :h], v[:, h:], exact)


def _pack_pair(lo, hi, exact=False):
    if not exact:
        lo, hi = lo.astype(BF16).astype(F32), hi.astype(BF16).astype(F32)
    lo = lax.bitcast_convert_type(lo, U32)
    hi = lax.bitcast_convert_type(hi, U32)
    return (lo >> 16) | (hi & jnp.uint32(HI_MASK))


def _chunk_relative(pos, c):
    rel = pos - c * SORT_CHUNK
    inside = (rel >= 0) & (rel < SORT_CHUNK)
    return jnp.where(inside, rel, -1).astype(F32).astype(BF16)


def _unpack_halves(w):
    lo = lax.bitcast_convert_type(w << 16, F32).astype(BF16)
    hi = lax.bitcast_convert_type(w & jnp.uint32(HI_MASK), F32).astype(BF16)
    return lo, hi


def _start_group_copies(plan_ref, make_copy):
    for j in range(CHUNK_BITS):
        size = CHUNK_ALIGN << j

        def body(k, carry):
            make_copy(pl.multiple_of(plan_ref[j * LANES + k], CHUNK_ALIGN),
                      pl.multiple_of(plan_ref[(PLAN_ROWS + j) * LANES + k], CHUNK_ALIGN), size).start()
            return carry

        lax.fori_loop(0, plan_ref[CHUNK_BITS * LANES + j], body, 0)


def _wait_group_rows(total_rows, ns, make_copy):
    for j in range(ns.bit_length()):
        size = CHUNK_ALIGN << j
        if size > ns:
            break

        @pl.when((total_rows & size) != 0)
        def _():
            make_copy(0, 0, size).wait()


def _dispatch_kernel(tot_ref, zrow_ref, plan_ref, pos_ref, x_ref, xs_ref, sbuf_ref, zbuf_ref,
                     onehot_ref, sem, zsem, *, tg, ns, bm):
    g = pl.program_id(0)
    ng = pl.num_programs(0)
    slot = g % 2

    @pl.when(g == 0)
    def _():
        zbuf_ref[...] = jnp.zeros_like(zbuf_ref)
        row = pl.multiple_of(zrow_ref[0], CHUNK_ALIGN)
        tail = pltpu.make_async_copy(zbuf_ref, xs_ref.at[pl.ds(row, bm), :], zsem)
        tail.start()
        tail.wait()

    def copy_from(sl):
        def make_copy(src, dst, size):
            return pltpu.make_async_copy(sbuf_ref.at[sl, pl.ds(src, size), :],
                                         xs_ref.at[pl.ds(dst, size), :], sem.at[sl])
        return make_copy

    @pl.when(g >= 2)
    def _():
        _wait_group_rows(tot_ref[g - 2], ns, copy_from(slot))

    xb = x_ref[...].astype(BF16)
    pos = pos_ref[...]
    local_rows = lax.broadcasted_iota(jnp.int32, (SORT_CHUNK, tg), 0).astype(F32).astype(BF16)
    one = jnp.ones((SORT_CHUNK, tg), BF16)
    zero = jnp.zeros((SORT_CHUNK, tg), BF16)
    for c in range(ns // SORT_CHUNK):
        rel = _chunk_relative(pos, c)
        hit = local_rows == rel[0:1, :]
        for k in range(1, TOP_K):
            hit = jnp.logical_or(hit, local_rows == rel[k:k + 1, :])
        onehot_ref[c * SORT_CHUNK:(c + 1) * SORT_CHUNK, :] = jnp.where(hit, one, zero)
    onehot = onehot_ref[...]
    half = x_ref.shape[1] // 2
    for n in range(half // SORT_CHUNK):
        lo = jnp.dot(onehot, xb[:, n * SORT_CHUNK:(n + 1) * SORT_CHUNK], preferred_element_type=F32)
        hi = jnp.dot(onehot, xb[:, half + n * SORT_CHUNK:half + (n + 1) * SORT_CHUNK],
                     preferred_element_type=F32)
        sbuf_ref[slot, :, n * SORT_CHUNK:(n + 1) * SORT_CHUNK] = _pack_pair(lo, hi, exact=True)

    _start_group_copies(plan_ref, copy_from(slot))

    @pl.when(g == ng - 1)
    def _():
        @pl.when(g >= 1)
        def _():
            _wait_group_rows(tot_ref[g - 1], ns, copy_from(1 - slot))
        _wait_group_rows(tot_ref[g], ns, copy_from(slot))


def _dispatch(x1, pos, plan, tot, zrow, n_rows, tg, ns, bm):
    t, d = x1.shape
    grid_spec = pltpu.PrefetchScalarGridSpec(
        num_scalar_prefetch=2,
        grid=(t // tg,),
        in_specs=[
            pl.BlockSpec((PLAN_WORDS,), lambda i, *_: (i,), memory_space=pltpu.SMEM),
            pl.BlockSpec((TOP_K, tg), lambda i, *_: (0, i)),
            pl.BlockSpec((tg, d), lambda i, *_: (i, 0)),
        ],
        out_specs=pl.BlockSpec(memory_space=pl.ANY),
        scratch_shapes=[pltpu.VMEM((2, ns, d // 2), U32), pltpu.VMEM((bm, d // 2), U32),
                        pltpu.VMEM((ns, tg), BF16),
                        pltpu.SemaphoreType.DMA((2,)), pltpu.SemaphoreType.DMA],
    )
    return pl.pallas_call(
        functools.partial(_dispatch_kernel, tg=tg, ns=ns, bm=bm),
        grid_spec=grid_spec,
        out_shape=jax.ShapeDtypeStruct((n_rows, d // 2), U32),
        compiler_params=_params("arbitrary"),
        name="moe_dispatch",
    )(tot, zrow, plan, pos, x1)


def _gmm_kernel(nq_ref, brow_ref, bval_ref, bord_ref, bfirst_ref, aexp_ref,
                xs_ref, w1_ref, w3_ref, w2_ref, y_ref,
                xbuf_ref, ybuf_ref, w1s_ref, w3s_ref, w2s_ref, w13_ref, w2b_ref, xsem, ysem, wsem,
                *, layer, bm):
    nq = nq_ref[0]
    na = nq_ref[1]
    f = w1s_ref.shape[2]
    half = xbuf_ref.shape[2]
    nx = xbuf_ref.shape[0]

    def x_copy(q, slot):
        row = pl.multiple_of(brow_ref[q], CHUNK_ALIGN)
        return pltpu.make_async_copy(xs_ref.at[pl.ds(row, bm), :], xbuf_ref.at[slot], xsem.at[slot])

    def w_copies(j, slot):
        e = aexp_ref[j]
        return (pltpu.make_async_copy(w1_ref.at[layer, e], w1s_ref.at[slot], wsem.at[slot]),
                pltpu.make_async_copy(w3_ref.at[layer, e], w3s_ref.at[slot], wsem.at[slot]),
                pltpu.make_async_copy(w2_ref.at[layer, e], w2s_ref.at[slot], wsem.at[slot]))

    def y_copy(slot, row, off, size):
        return pltpu.make_async_copy(ybuf_ref.at[slot, pl.ds(off, size), :],
                                     y_ref.at[pl.ds(pl.multiple_of(row + off, CHUNK_ALIGN), size), :],
                                     ysem.at[slot])

    def for_each_y_copy(slot, row, valid, fn):
        @pl.when(valid == bm)
        def _():
            fn(y_copy(slot, row, 0, bm))

        @pl.when(valid < bm)
        def _():
            for j in range((bm // CHUNK_ALIGN).bit_length() - 1):
                size = CHUNK_ALIGN << j

                @pl.when((valid & size) != 0)
                def _():
                    fn(y_copy(slot, row, pl.multiple_of(valid & (size - 1), CHUNK_ALIGN), size))

    for c in w_copies(0, 0):
        c.start()
    for q0 in range(nx - 1):
        @pl.when(q0 < nq)
        def _():
            x_copy(q0, q0).start()

    def body(q, carry):
        xslot = q % nx
        yslot = q % 2

        @pl.when(q + (nx - 1) < nq)
        def _():
            x_copy(q + (nx - 1), (q + (nx - 1)) % nx).start()

        @pl.when(bfirst_ref[q] == 1)
        def _():
            j = bord_ref[q]
            ws = j % 2
            for c in w_copies(j, ws):
                c.wait()

            @pl.when(j + 1 < na)
            def _():
                for c in w_copies(j + 1, 1 - ws):
                    c.start()
            w13_ref[:, :f] = w1s_ref[ws].astype(BF16)
            w13_ref[:, f:] = w3s_ref[ws].astype(BF16)
            w2b_ref[...] = w2s_ref[ws].astype(BF16)

        x_copy(q, xslot).wait()

        @pl.when(q >= 2)
        def _():
            for_each_y_copy(yslot, 0, bval_ref[jnp.maximum(q - 2, 0)], lambda c: c.wait())

        x_lo, x_hi = _unpack_halves(xbuf_ref[xslot])
        h13 = (jnp.dot(x_lo, w13_ref[:half, :], preferred_element_type=F32)
               + jnp.dot(x_hi, w13_ref[half:, :], preferred_element_type=F32))
        h = jax.nn.silu(h13[:, :f]) * h13[:, f:]
        ybuf_ref[yslot] = _pack_halves(jnp.dot(h.astype(BF16), w2b_ref[...], preferred_element_type=F32))
        for_each_y_copy(yslot, brow_ref[q], bval_ref[q], lambda c: c.start())
        return carry

    lax.fori_loop(0, nq, body, 0)

    @pl.when(nq >= 2)
    def _():
        for_each_y_copy(nq % 2, 0, bval_ref[jnp.maximum(nq - 2, 0)], lambda c: c.wait())
    for_each_y_copy((nq - 1) % 2, 0, bval_ref[nq - 1], lambda c: c.wait())


def _gmm(xs, tables, w1, w3, w2, layer, bm):
    n_rows, half = xs.shape
    d = 2 * half
    f = w1.shape[-1]
    any_spec = pl.BlockSpec(memory_space=pl.ANY)
    grid_spec = pltpu.PrefetchScalarGridSpec(
        num_scalar_prefetch=len(tables),
        grid=(1,),
        in_specs=[any_spec, any_spec, any_spec, any_spec],
        out_specs=any_spec,
        scratch_shapes=[pltpu.VMEM((GMM_X_SLOTS, bm, half), U32), pltpu.VMEM((2, bm, half), U32),
                        pltpu.VMEM((2, d, f), F32), pltpu.VMEM((2, d, f), F32), pltpu.VMEM((2, f, d), F32),
                        pltpu.VMEM((d, 2 * f), BF16), pltpu.VMEM((f, d), BF16),
                        pltpu.SemaphoreType.DMA((GMM_X_SLOTS,)), pltpu.SemaphoreType.DMA((2,)),
                        pltpu.SemaphoreType.DMA((2,))],
    )
    return pl.pallas_call(
        functools.partial(_gmm_kernel, layer=layer, bm=bm),
        grid_spec=grid_spec,
        out_shape=jax.ShapeDtypeStruct((n_rows, half), U32),
        compiler_params=_params("arbitrary"),
        name="moe_experts",
    )(*tables, xs, w1, w3, w2)


def _combine_kernel(tot_ref, plan_ref, plan_next_ref, yb_ref, pos_ref, gate_ref, x_ref, ws13_ref, ws2_ref,
                    g_ref, b_ref, o_ref, ybuf_ref, wgt_ref, ylo_ref, yhi_ref, sem, *, alpha, tg, ns):
    g = pl.program_id(0)
    ng = pl.num_programs(0)
    slot = g % 2

    def copy_into(sl):
        def make_copy(buf_row, hbm_row, size):
            return pltpu.make_async_copy(yb_ref.at[pl.ds(hbm_row, size), :],
                                         ybuf_ref.at[sl, pl.ds(buf_row, size), :], sem.at[sl])
        return make_copy

    @pl.when(g == 0)
    def _():
        ybuf_ref[...] = jnp.zeros_like(ybuf_ref)
        _start_group_copies(plan_ref, copy_into(0))

    @pl.when(g + 1 < ng)
    def _():
        _start_group_copies(plan_next_ref, copy_into(1 - slot))

    x = x_ref[...]
    f = ws2_ref.shape[0]
    h13 = jnp.dot(x.astype(BF16), ws13_ref[...], preferred_element_type=F32)
    hs = jax.nn.silu(h13[:, :f]) * h13[:, f:]
    shared = jnp.dot(hs.astype(BF16), ws2_ref[...], preferred_element_type=F32)

    _wait_group_rows(tot_ref[g], ns, copy_into(slot))

    pos = pos_ref[...]
    gate_b = gate_ref[...].astype(BF16)
    local_rows = lax.broadcasted_iota(jnp.int32, (SORT_CHUNK, tg), 0).astype(F32).astype(BF16)
    for c in range(ns // SORT_CHUNK):
        rel = _chunk_relative(pos, c)
        wgt = jnp.zeros((SORT_CHUNK, tg), BF16)
        for k in range(TOP_K):
            wgt = jnp.where(local_rows == rel[k:k + 1, :], gate_b[k:k + 1, :], wgt)
        wgt_ref[c * SORT_CHUNK:(c + 1) * SORT_CHUNK, :] = wgt
        y_lo, y_hi = _unpack_halves(ybuf_ref[slot, c * SORT_CHUNK:(c + 1) * SORT_CHUNK, :])
        ylo_ref[c * SORT_CHUNK:(c + 1) * SORT_CHUNK, :] = y_lo
        yhi_ref[c * SORT_CHUNK:(c + 1) * SORT_CHUNK, :] = y_hi
    wgt_all = wgt_ref[...]
    dn = (((0,), (0,)), ((), ()))
    routed = jnp.concatenate([lax.dot_general(wgt_all, ylo_ref[...], dn, preferred_element_type=F32),
                              lax.dot_general(wgt_all, yhi_ref[...], dn, preferred_element_type=F32)],
                             axis=1)
    o_ref[...] = _layer_norm(alpha * x + (routed + shared), g_ref[...], b_ref[...])


def _combine(yb, pos_kt, gate_kt, plan, tot, x1, ws1, ws3, ws2, g, b, alpha, tg, ns):
    t, d = x1.shape
    ng = t // tg
    f = ws1.shape[1]
    ws13 = jnp.concatenate([ws1, ws3], axis=1).astype(BF16)
    const = lambda i, *_: (0, 0)
    tile = lambda i, *_: (i, 0)
    row = lambda v: v.reshape(1, d)
    grid_spec = pltpu.PrefetchScalarGridSpec(
        num_scalar_prefetch=1,
        grid=(ng,),
        in_specs=[
            pl.BlockSpec((PLAN_WORDS,), lambda i, *_: (i,), memory_space=pltpu.SMEM),
            pl.BlockSpec((PLAN_WORDS,), lambda i, *_: (jnp.minimum(i + 1, ng - 1),),
                         memory_space=pltpu.SMEM),
            pl.BlockSpec(memory_space=pl.ANY),
            pl.BlockSpec((TOP_K, tg), lambda i, *_: (0, i)),
            pl.BlockSpec((TOP_K, tg), lambda i, *_: (0, i)),
            pl.BlockSpec((tg, d), tile),
            pl.BlockSpec((d, 2 * f), const),
            pl.BlockSpec((f, d), const),
            pl.BlockSpec((1, d), const),
            pl.BlockSpec((1, d), const),
        ],
        out_specs=pl.BlockSpec((tg, d), tile),
        scratch_shapes=[pltpu.VMEM((2, ns, d // 2), U32), pltpu.VMEM((ns, tg), BF16),
                        pltpu.VMEM((ns, d // 2), BF16), pltpu.VMEM((ns, d // 2), BF16),
                        pltpu.SemaphoreType.DMA((2,))],
    )
    return pl.pallas_call(
        functools.partial(_combine_kernel, alpha=alpha, tg=tg, ns=ns),
        grid_spec=grid_spec,
        out_shape=jax.ShapeDtypeStruct((t, d), F32),
        compiler_params=_params("arbitrary"),
        name="moe_combine",
    )(tot, plan, plan, yb, pos_kt, gate_kt, x1, ws13, ws2.astype(BF16), row(g), row(b))


def _moe_layer(x1, routing, tg, layer, w1, w3, w2, ws1, ws3, ws2, g, b, alpha):
    t, d = x1.shape
    ng = t // tg
    bm = EXPERT_BM
    ns = -(-(tg * TOP_K + N_EXPERTS * (CHUNK_ALIGN - 1)) // SORT_CHUNK) * SORT_CHUNK
    pos, gate, n8_wide, plan_wide = routing
    plan3 = plan_wide.reshape(ng, PLAN_ROWS, LANES)
    tot = plan3[:, CHUNK_BITS + 1, 0]

    n8 = n8_wide[:, 0].reshape(ng, N_EXPERTS)
    goff = jnp.cumsum(n8, axis=1) - n8
    seg = jnp.sum(n8, axis=0)
    pend = jnp.cumsum(seg)
    pstart = pend - seg
    gdest = pstart[None, :] + jnp.cumsum(n8, axis=0) - n8
    max_rows = ng * (tg * TOP_K + N_EXPERTS * (CHUNK_ALIGN - 1))
    n_rows = max_rows + bm

    experts = jnp.arange(N_EXPERTS, dtype=jnp.int32)
    nb = (seg + bm - 1) // bm
    cb = jnp.cumsum(nb)
    blk = jnp.arange(max_rows // bm + N_EXPERTS, dtype=jnp.int32)
    blk_expert = jnp.minimum(jnp.sum((cb[None, :] <= blk[:, None]).astype(jnp.int32), axis=1), N_EXPERTS - 1)
    mine = blk_expert[:, None] == experts[None, :]
    of_expert = lambda v: jnp.sum(jnp.where(mine, v[None, :], 0), axis=1)
    local = blk - of_expert(cb - nb)
    active = seg > 0
    ordinal = jnp.cumsum(active.astype(jnp.int32)) - 1
    active_experts = jnp.sum(jnp.where((ordinal[None, :] == experts[:, None]) & active[None, :],
                                       experts[None, :], 0), axis=1)
    tables = [jnp.stack([cb[-1], jnp.sum(active.astype(jnp.int32))]),
              of_expert(pstart) + local * bm,
              jnp.clip(of_expert(seg) - local * bm, 0, bm),
              of_expert(ordinal),
              (local == 0).astype(jnp.int32),
              active_experts]
    tables = [a.astype(jnp.int32) for a in tables]
    zrow = pend[-1:].astype(jnp.int32)

    lists = plan3[:, :CHUNK_BITS, :N_EXPERTS]
    sizes = CHUNK_ALIGN << jnp.arange(CHUNK_BITS, dtype=jnp.int32)
    done = n8[:, None, :] & (sizes[None, :, None] - 1)
    pick = lists[..., None] == jnp.arange(N_EXPERTS, dtype=jnp.int32)
    src = jnp.sum(jnp.where(pick, (goff[:, None, :] + done)[:, :, None, :], 0), axis=-1)
    dst = jnp.sum(jnp.where(pick, (gdest[:, None, :] + done)[:, :, None, :], 0), axis=-1)
    widen = lambda a: jnp.pad(a, ((0, 0), (0, 0), (0, LANES - N_EXPERTS)))
    plan = jnp.concatenate([widen(src), plan3[:, CHUNK_BITS:], widen(dst),
                            jnp.zeros((ng, PLAN_ROWS - CHUNK_BITS, LANES), jnp.int32)],
                           axis=1).astype(jnp.int32).reshape(-1)

    xs = _dispatch(x1, pos, plan, tot, zrow, n_rows, tg, ns, bm)
    yb = _gmm(xs, tables, w1, w3, w2, layer, bm)
    return _combine(yb, pos, gate, plan, tot, x1, ws1, ws3, ws2, g, b, alpha, tg, ns)


def kernel(x, ln_gain, ln_bias, pool_w, pool_scale, attn_wqkv, attn_bqkv, attn_sinks, attn_wo, attn_bo,
           conv_w_in, conv_w, conv_w_out, router_w, router_bias, expert_w1, expert_w3, expert_w2,
           shared_w1, shared_w3, shared_w2):
    bsz, seq, d = x.shape
    depth = ln_gain.shape[0]
    alpha = float((2 * depth) ** 0.25)
    h = x.reshape(bsz * seq, d)
    tg = min(GROUP_TG, seq)
    for i in range(depth):
        kind = i % N_MIXERS
        j = i // N_MIXERS
        g1, b1 = ln_gain[i, 0], ln_bias[i, 0]
        route = (router_w[i], router_bias[i], tg)
        if kind == 0:
            h, *routing = _pool_mixer(h, seq, pool_w[j], pool_scale[j], g1, b1, alpha, *route)
        elif kind == 1:
            h, *routing = _swa_mixer(h, seq, attn_wqkv[j], attn_bqkv[j], attn_sinks[j], attn_wo[j],
                                     attn_bo[j], g1, b1, alpha, *route)
        else:
            h, *routing = _conv_mixer(h, seq, conv_w_in[j], conv_w[j], conv_w_out[j], g1, b1, alpha, *route)
        h = _moe_layer(h, routing, tg, i, expert_w1, expert_w3, expert_w2,
                       shared_w1[i], shared_w3[i], shared_w2[i], ln_gain[i, 1], ln_bias[i, 1], alpha)
    return h.reshape(bsz, seq, d)
```

```python
import functools

import jax
import jax.numpy as jnp
from jax import lax
from jax.experimental import pallas as pl
from jax.experimental.pallas import tpu as pltpu

POOL_WINDOWS = (2, 4, 8, 16)
N_MIXERS = 3
HEAD_DIM = 64
N_Q_HEADS = 16
N_KV_HEADS = 4
GQA_GROUP = N_Q_HEADS // N_KV_HEADS
WINDOW = 128
ROPE_THETA = 10000.0
N_EXPERTS = 64
TOP_K = 8
N_EXPERT_GROUPS = 8
GROUP_SIZE = N_EXPERTS // N_EXPERT_GROUPS
TOPK_GROUPS = 4
ROUTED_SCALE = 2.5
LN_EPS = 1e-5

LANES = 128
SUBLANES = 8
POOL_HALO = 16
VMEM_LIMIT = 48 * 1024 * 1024

POOL_TT = 512
QKV_TT = 512
ATTN_TQ = 512
CONV_TT = 512
GROUP_TG = 256
CHUNK_ALIGN = SUBLANES
SORT_CHUNK = 256
EXPERT_BM = 512
GMM_X_SLOTS = 3
CHUNK_BITS = 6
PLAN_ROWS = SUBLANES
PLAN_WORDS = 2 * PLAN_ROWS * LANES

U32 = jnp.uint32
HI_MASK = 0xFFFF0000

F32 = jnp.float32
BF16 = jnp.bfloat16
NEG_INF = float("-inf")


def _layer_norm(z, g, b):
    mu = jnp.mean(z, axis=-1, keepdims=True)
    zc = z - mu
    var = jnp.mean(zc * zc, axis=-1, keepdims=True)
    return zc * lax.rsqrt(var + LN_EPS) * g + b


def _params(*sem):
    return pltpu.CompilerParams(dimension_semantics=sem, vmem_limit_bytes=VMEM_LIMIT)


def _route_tile(o_ref, router_refs, route_out_refs, tg):
    wt_ref, bias_ref = router_refs
    pos_ref, gate_ref, n8_ref, plan_ref = route_out_refs
    for s in range(o_ref.shape[0] // tg):
        _route_group(o_ref.at[pl.ds(s * tg, tg), :], wt_ref, bias_ref,
                     pos_ref.at[:, pl.ds(s * tg, tg)], gate_ref.at[:, pl.ds(s * tg, tg)],
                     n8_ref.at[pl.ds(s * N_EXPERTS, N_EXPERTS), :],
                     plan_ref.at[pl.ds(s * PLAN_ROWS, PLAN_ROWS), :], tt=tg)


def _router_plumbing(router_w, router_bias, t, d, tt, tg, tile_index):
    groups = tt // tg
    const = lambda *ids: (0, 0)
    by_cols = lambda *ids: (0, tile_index(*ids))
    by_rows = lambda *ids: (tile_index(*ids), 0)
    operands = [router_w.T, router_bias.reshape(N_EXPERTS, 1)]
    in_specs = [pl.BlockSpec((N_EXPERTS, d), const), pl.BlockSpec((N_EXPERTS, 1), const)]
    out_specs = [pl.BlockSpec((TOP_K, tt), by_cols), pl.BlockSpec((TOP_K, tt), by_cols),
                 pl.BlockSpec((groups * N_EXPERTS, LANES), by_rows),
                 pl.BlockSpec((groups * PLAN_ROWS, LANES), by_rows)]
    out_shapes = [jax.ShapeDtypeStruct((TOP_K, t), jnp.int32), jax.ShapeDtypeStruct((TOP_K, t), F32),
                  jax.ShapeDtypeStruct((t // tg * N_EXPERTS, LANES), jnp.int32),
                  jax.ShapeDtypeStruct((t // tg * PLAN_ROWS, LANES), jnp.int32)]
    return operands, in_specs, out_specs, out_shapes


def _pool_kernel(x_ref, halo_ref, w_ref, scale_ref, g_ref, b_ref, wt_ref, rb_ref, o_ref, pos_ref, gate_ref,
                 n8_ref, plan_ref, buf_ref, *, alpha, tt, tg):
    i = pl.program_id(1)
    x = x_ref[...]
    d = x.shape[1]
    ch = d // len(POOL_WINDOWS)
    top = SUBLANES
    first = top + POOL_HALO
    n = first + tt
    for k in range(3):
        buf_ref[k, 0:top, :] = jnp.zeros((top, d), F32)
    buf_ref[0, top:first, :] = jnp.where(i > 0, halo_ref[...], 0.0)
    buf_ref[0, first:n, :] = x
    pos = i * tt + lax.broadcasted_iota(jnp.int32, (tt, 1), 0)
    ys = []
    for g, w in enumerate(POOL_WINDOWS):
        c0 = g * ch
        xg = x[:, c0:c0 + ch]
        src, span = 0, 1
        while span < w:
            dst = 1 if src != 1 else 2
            buf_ref[dst, top:n, c0:c0 + ch] = (buf_ref[src, top:n, c0:c0 + ch]
                                               + buf_ref[src, top - span:n - span, c0:c0 + ch])
            src, span = dst, 2 * span
        s = buf_ref[src, first:n, c0:c0 + ch]
        inv_cnt = 1.0 / jnp.minimum(pos + 1, w).astype(F32)
        pooled = s * inv_cnt - xg
        ys.append(jnp.dot(pooled.astype(BF16), w_ref[g], preferred_element_type=F32))
    y = jnp.concatenate(ys, axis=1) * scale_ref[...]
    o_ref[...] = _layer_norm(alpha * x + y, g_ref[...], b_ref[...])
    _route_tile(o_ref, (wt_ref, rb_ref), (pos_ref, gate_ref, n8_ref, plan_ref), tg)


def _pool_mixer(x2, seq, w_grp, scale, g, b, alpha, router_w, router_bias, tg):
    t, d = x2.shape
    tt = min(POOL_TT, seq)
    nt = seq // tt
    nb = t // seq
    row = lambda v: v.reshape(1, d)
    const2 = lambda bi, i: (0, 0)
    tile = lambda bi, i: bi * nt + i
    r_ops, r_in, r_out, r_shapes = _router_plumbing(router_w, router_bias, t, d, tt, tg, tile)
    return pl.pallas_call(
        functools.partial(_pool_kernel, alpha=alpha, tt=tt, tg=tg),
        grid=(nb, nt),
        in_specs=[
            pl.BlockSpec((tt, d), lambda bi, i: (bi * nt + i, 0)),
            pl.BlockSpec((POOL_HALO, d),
                         lambda bi, i: (jnp.maximum((bi * seq + i * tt) // POOL_HALO - 1, 0), 0)),
            pl.BlockSpec(w_grp.shape, lambda bi, i: (0, 0, 0)),
            pl.BlockSpec((1, d), const2),
            pl.BlockSpec((1, d), const2),
            pl.BlockSpec((1, d), const2),
        ] + r_in,
        out_specs=[pl.BlockSpec((tt, d), lambda bi, i: (bi * nt + i, 0))] + r_out,
        out_shape=[jax.ShapeDtypeStruct((t, d), F32)] + r_shapes,
        scratch_shapes=[pltpu.VMEM((3, SUBLANES + POOL_HALO + tt, d), F32)],
        compiler_params=_params("parallel", "parallel"),
        name="pool_mixer",
    )(x2, x2, w_grp.astype(BF16), row(scale), row(g), row(b), *r_ops)


def _qkv_kernel(x_ref, w_ref, b_ref, c_ref, s_ref, q_ref, k_ref, v_ref, *, nq, nk):
    y = jnp.dot(x_ref[...].astype(BF16), w_ref[...], preferred_element_type=F32) + b_ref[...]
    c = c_ref[...]
    s = s_ref[...]
    lane = lax.broadcasted_iota(jnp.int32, c.shape, 1)
    first_half = (lane % HEAD_DIM) < (HEAD_DIM // 2)

    def rope(v):
        partner = jnp.where(first_half, pltpu.roll(v, LANES - HEAD_DIM // 2, 1),
                            pltpu.roll(v, HEAD_DIM // 2, 1))
        return v * c + partner * s

    for j in range(nq // LANES):
        q_ref[:, j * LANES:(j + 1) * LANES] = (
            rope(y[:, j * LANES:(j + 1) * LANES]) * (HEAD_DIM ** -0.5)).astype(BF16)
    for j in range(nk // LANES):
        k_ref[:, j * LANES:(j + 1) * LANES] = rope(
            y[:, nq + j * LANES:nq + (j + 1) * LANES]).astype(BF16)
    v_ref[...] = y[:, nq + nk:].astype(BF16)


def _attn_kernel(sink_ref, q_ref, k_ref, kh_ref, v_ref, vh_ref, x_ref, wo_ref, bo_ref, g_ref, b_ref,
                 wt_ref, rb_ref, o_ref, pos_ref, gate_ref, n8_ref, plan_ref, *, alpha, tq, tg):
    i = pl.program_id(1)
    nqb = tq // WINDOW
    r = lax.broadcasted_iota(jnp.int32, (tq, 2 * WINDOW), 0)
    c = lax.broadcasted_iota(jnp.int32, (tq, 2 * WINDOW), 1)
    dist = (r & (WINDOW - 1)) + WINDOW - c
    valid = (dist >= 0) & (dist < WINDOW) & ((c >= WINDOW) | (r >= WINDOW) | (i > 0))
    lane = lax.broadcasted_iota(jnp.int32, (tq, LANES), 1)
    low = lane < HEAD_DIM
    dn = (((1,), (1,)), ((), ()))
    outs = []
    for kv in range(N_KV_HEADS):
        sl = slice(kv * LANES, (kv + 1) * LANES)
        kk = jnp.concatenate([kh_ref[:, sl], k_ref[:, sl]], axis=0)
        vv = jnp.concatenate([vh_ref[:, sl], v_ref[:, sl]], axis=0)
        for p in range(GQA_GROUP // 2):
            c0 = (kv * GQA_GROUP + 2 * p) * HEAD_DIM
            qp = q_ref[:, c0:c0 + LANES]
            res = []
            for half in range(2):
                keep = low if half == 0 else jnp.logical_not(low)
                qh = jnp.where(keep, qp, jnp.zeros_like(qp))
                sc_all = lax.dot_general(qh, kk, dn, preferred_element_type=F32)
                sink = sink_ref[kv * GQA_GROUP + 2 * p + half]
                sc = jnp.concatenate([sc_all[qb * WINDOW:(qb + 1) * WINDOW, qb * WINDOW:(qb + 2) * WINDOW]
                                      for qb in range(nqb)], axis=0)
                sc = jnp.where(valid, sc, NEG_INF)
                m = jnp.maximum(jnp.max(sc, axis=1, keepdims=True), sink)
                pm = jnp.exp(sc - m)
                rden = 1.0 / (jnp.sum(pm, axis=1, keepdims=True) + jnp.exp(sink - m))
                pm = pm.astype(BF16)
                res.append(jnp.concatenate(
                    [jnp.dot(pm[qb * WINDOW:(qb + 1) * WINDOW, :], vv[qb * WINDOW:(qb + 2) * WINDOW, :],
                             preferred_element_type=F32) for qb in range(nqb)], axis=0) * rden)
            outs.append(jnp.where(low, res[0], res[1]))
    o_all = jnp.concatenate(outs, axis=1).astype(BF16)
    mix = jnp.dot(o_all, wo_ref[...], preferred_element_type=F32) + bo_ref[...]
    o_ref[...] = _layer_norm(alpha * x_ref[...] + mix, g_ref[...], b_ref[...])
    _route_tile(o_ref, (wt_ref, rb_ref), (pos_ref, gate_ref, n8_ref, plan_ref), tg)


def _dup_heads(w, n_heads):
    lead = w.shape[:-1]
    w = w.reshape(lead + (n_heads, 1, HEAD_DIM))
    w = jnp.broadcast_to(w, lead + (n_heads, 2, HEAD_DIM))
    return w.reshape(lead + (n_heads * 2 * HEAD_DIM,))


def _swa_mixer(x2, seq, wqkv, bqkv, sinks, wo, bo, g, b, alpha, router_w, router_bias, tg):
    t, d = x2.shape
    nb = t // seq
    nq = N_Q_HEADS * HEAD_DIM
    nkv = N_KV_HEADS * HEAD_DIM
    nk2 = 2 * nkv
    w_ext = jnp.concatenate([wqkv[:, :nq], _dup_heads(wqkv[:, nq:nq + nkv], N_KV_HEADS),
                             _dup_heads(wqkv[:, nq + nkv:], N_KV_HEADS)], axis=1).astype(BF16)
    b_ext = jnp.concatenate([bqkv[:nq], _dup_heads(bqkv[nq:nq + nkv], N_KV_HEADS),
                             _dup_heads(bqkv[nq + nkv:], N_KV_HEADS)]).reshape(1, -1)
    n_ext = nq + 2 * nk2

    pos = jnp.arange(seq, dtype=F32)
    inv_freq = ROPE_THETA ** (-jnp.arange(0, HEAD_DIM, 2, dtype=F32) / HEAD_DIM)
    ang = pos[:, None] * inv_freq[None, :]
    cos, sin = jnp.cos(ang), jnp.sin(ang)
    cos_t = jnp.concatenate([cos, cos, cos, cos], axis=1)
    sin_t = jnp.concatenate([-sin, sin, -sin, sin], axis=1)

    tt = min(QKV_TT, seq)
    nt = seq // tt
    const2 = lambda i: (0, 0)
    q, k2, v2 = pl.pallas_call(
        functools.partial(_qkv_kernel, nq=nq, nk=nk2),
        grid=(t // tt,),
        in_specs=[
            pl.BlockSpec((tt, d), lambda i: (i, 0)),
            pl.BlockSpec((d, n_ext), const2),
            pl.BlockSpec((1, n_ext), const2),
            pl.BlockSpec((tt, LANES), lambda i: (i % nt, 0)),
            pl.BlockSpec((tt, LANES), lambda i: (i % nt, 0)),
        ],
        out_specs=[pl.BlockSpec((tt, nq), lambda i: (i, 0)),
                   pl.BlockSpec((tt, nk2), lambda i: (i, 0)),
                   pl.BlockSpec((tt, nk2), lambda i: (i, 0))],
        out_shape=[jax.ShapeDtypeStruct((t, nq), BF16),
                   jax.ShapeDtypeStruct((t, nk2), BF16),
                   jax.ShapeDtypeStruct((t, nk2), BF16)],
        compiler_params=_params("parallel"),
        name="swa_qkv",
    )(x2, w_ext, b_ext, cos_t, sin_t)

    tq = min(ATTN_TQ, seq)
    ntq = seq // tq
    main = lambda bi, i: (bi * ntq + i, 0)
    halo = lambda bi, i: (jnp.maximum((bi * seq + i * tq) // WINDOW - 1, 0), 0)
    const = lambda bi, i: (0, 0)
    row = lambda v: v.reshape(1, d)
    r_ops, r_in, r_out, r_shapes = _router_plumbing(router_w, router_bias, t, d, tq, tg,
                                                    lambda bi, i: bi * ntq + i)
    return pl.pallas_call(
        functools.partial(_attn_kernel, alpha=alpha, tq=tq, tg=tg),
        grid=(nb, ntq),
        in_specs=[
            pl.BlockSpec(memory_space=pltpu.SMEM),
            pl.BlockSpec((tq, nq), main),
            pl.BlockSpec((tq, nk2), main),
            pl.BlockSpec((WINDOW, nk2), halo),
            pl.BlockSpec((tq, nk2), main),
            pl.BlockSpec((WINDOW, nk2), halo),
            pl.BlockSpec((tq, d), main),
            pl.BlockSpec((nq, d), const),
            pl.BlockSpec((1, d), const),
            pl.BlockSpec((1, d), const),
            pl.BlockSpec((1, d), const),
        ] + r_in,
        out_specs=[pl.BlockSpec((tq, d), main)] + r_out,
        out_shape=[jax.ShapeDtypeStruct((t, d), F32)] + r_shapes,
        compiler_params=_params("parallel", "parallel"),
        name="swa_attention",
    )(sinks, q, k2, k2, v2, v2, x2, wo.astype(BF16), row(bo), row(g), row(b), *r_ops)


def _conv_kernel(x_ref, win_ref, cw_ref, wout_ref, g_ref, b_ref, wt_ref, rb_ref, o_ref, pos_ref, gate_ref,
                 n8_ref, plan_ref, ubuf_ref, *, alpha, tt, tg):
    i = pl.program_id(1)
    d = x_ref.shape[1]

    @pl.when(i == 0)
    def _():
        ubuf_ref[0:SUBLANES, :] = jnp.zeros((SUBLANES, d), F32)

    x = x_ref[...]
    proj = jnp.dot(x.astype(BF16), win_ref[...], preferred_element_type=F32)
    gate_b = proj[:, :d]
    u = proj[:, d:2 * d] * proj[:, 2 * d:]
    ubuf_ref[SUBLANES:SUBLANES + tt, :] = u
    cw = cw_ref[...]
    conv = (cw[0:1, :] * ubuf_ref[SUBLANES - 2:SUBLANES - 2 + tt, :]
            + cw[1:2, :] * ubuf_ref[SUBLANES - 1:SUBLANES - 1 + tt, :]
            + cw[2:3, :] * u)
    ubuf_ref[0:SUBLANES, :] = u[tt - SUBLANES:tt, :]
    y = jnp.dot((gate_b * conv).astype(BF16), wout_ref[...], preferred_element_type=F32)
    o_ref[...] = _layer_norm(alpha * x + y, g_ref[...], b_ref[...])
    _route_tile(o_ref, (wt_ref, rb_ref), (pos_ref, gate_ref, n8_ref, plan_ref), tg)


def _conv_mixer(x2, seq, w_in, conv_w, w_out, g, b, alpha, router_w, router_bias, tg):
    t, d = x2.shape
    nb = t // seq
    tt = min(CONV_TT, seq)
    nt = seq // tt
    const = lambda bi, i: (0, 0)
    row = lambda v: v.reshape(1, d)
    r_ops, r_in, r_out, r_shapes = _router_plumbing(router_w, router_bias, t, d, tt, tg,
                                                    lambda bi, i: bi * nt + i)
    return pl.pallas_call(
        functools.partial(_conv_kernel, alpha=alpha, tt=tt, tg=tg),
        grid=(nb, nt),
        in_specs=[
            pl.BlockSpec((tt, d), lambda bi, i: (bi * nt + i, 0)),
            pl.BlockSpec((d, 3 * d), const),
            pl.BlockSpec(conv_w.shape, const),
            pl.BlockSpec((d, d), const),
            pl.BlockSpec((1, d), const),
            pl.BlockSpec((1, d), const),
        ] + r_in,
        out_specs=[pl.BlockSpec((tt, d), lambda bi, i: (bi * nt + i, 0))] + r_out,
        out_shape=[jax.ShapeDtypeStruct((t, d), F32)] + r_shapes,
        scratch_shapes=[pltpu.VMEM((SUBLANES + tt, d), F32)],
        compiler_params=_params("arbitrary", "arbitrary"),
        name="conv_mixer",
    )(x2, w_in.astype(BF16), conv_w, w_out.astype(BF16), row(g), row(b), *r_ops)


def _first_argmax(v, rows, n):
    m = jnp.max(v, axis=0, keepdims=True)
    first = jnp.min(jnp.where(v == m, rows, n), axis=0, keepdims=True)
    return m, first


def _route_group(x_ref, wt_ref, bias_ref, pos_ref, gate_ref, n8_ref, plan_ref, *, tt):
    x = x_ref[...]
    xh = x.astype(BF16)
    xl = (x - xh.astype(F32)).astype(BF16)
    wt = wt_ref[...]
    wh = wt.astype(BF16)
    wl = (wt - wh.astype(F32)).astype(BF16)
    dn = (((1,), (1,)), ((), ()))
    logits = (lax.dot_general(wh, xh, dn, preferred_element_type=F32)
              + (lax.dot_general(wh, xl, dn, preferred_element_type=F32)
                 + lax.dot_general(wl, xh, dn, preferred_element_type=F32)))
    scores = jax.nn.sigmoid(logits)
    sel = scores + bias_ref[...]

    rows8 = lax.broadcasted_iota(jnp.int32, (GROUP_SIZE, tt), 0)
    gsc = []
    for gi in range(N_EXPERT_GROUPS):
        blk = sel[gi * GROUP_SIZE:(gi + 1) * GROUP_SIZE, :]
        m1, f1 = _first_argmax(blk, rows8, GROUP_SIZE)
        m2 = jnp.max(jnp.where(rows8 == f1, NEG_INF, blk), axis=0, keepdims=True)
        gsc.append(m1 + m2)
    cur = jnp.concatenate(gsc, axis=0)
    rows_g = lax.broadcasted_iota(jnp.int32, (N_EXPERT_GROUPS, tt), 0)
    gmask = jnp.zeros((N_EXPERT_GROUPS, tt), F32)
    for _ in range(TOPK_GROUPS):
        _, f = _first_argmax(cur, rows_g, N_EXPERT_GROUPS)
        hit = rows_g == f
        gmask = jnp.where(hit, 1.0, gmask)
        cur = jnp.where(hit, NEG_INF, cur)
    emask = jnp.concatenate(
        [jnp.broadcast_to(gmask[gi:gi + 1, :], (GROUP_SIZE, tt)) for gi in range(N_EXPERT_GROUPS)],
        axis=0)

    rows_e = lax.broadcasted_iota(jnp.int32, (N_EXPERTS, tt), 0)
    cur = jnp.where(emask > 0.0, sel, NEG_INF)
    idxs, gates = [], []
    member = jnp.zeros((N_EXPERTS, tt), F32)
    for _ in range(TOP_K):
        _, f = _first_argmax(cur, rows_e, N_EXPERTS)
        hit = rows_e == f
        idxs.append(f)
        gates.append(jnp.sum(jnp.where(hit, scores, 0.0), axis=0, keepdims=True))
        member = jnp.where(hit, 1.0, member)
        cur = jnp.where(hit, NEG_INF, cur)
    gate = jnp.concatenate(gates, axis=0)
    gate = gate / jnp.sum(gate, axis=0, keepdims=True) * ROUTED_SCALE

    ta = lax.broadcasted_iota(jnp.int32, (tt, tt), 0)
    tb = lax.broadcasted_iota(jnp.int32, (tt, tt), 1)
    earlier = (ta < tb).astype(BF16)
    before = jnp.dot(member.astype(BF16), earlier, preferred_element_type=F32)
    n = jnp.sum(member, axis=1, keepdims=True)
    n8 = jnp.floor((n + (CHUNK_ALIGN - 1)) * (1.0 / CHUNK_ALIGN)) * CHUNK_ALIGN
    n8_wide = jnp.broadcast_to(n8, (N_EXPERTS, LANES))
    ea = lax.broadcasted_iota(jnp.int32, (N_EXPERTS, N_EXPERTS), 0)
    eb = lax.broadcasted_iota(jnp.int32, (N_EXPERTS, N_EXPERTS), 1)
    lower = (eb < ea).astype(BF16)
    chunk_start = jnp.dot(lower, n8_wide.astype(BF16), preferred_element_type=F32)
    where_to = before + chunk_start[:, 0:1]
    for k in range(TOP_K):
        pk = jnp.sum(jnp.where(rows_e == idxs[k], where_to, 0.0), axis=0, keepdims=True)
        pos_ref[k:k + 1, :] = pk.astype(jnp.int32)
    gate_ref[...] = gate
    n8_int = n8_wide.astype(jnp.int32)
    n8_ref[...] = n8_int

    lane = lax.broadcasted_iota(jnp.int32, (N_EXPERTS, LANES), 1)
    flag = jnp.zeros((N_EXPERTS, LANES), F32)
    for j in range(CHUNK_BITS):
        flag = jnp.where((lane == j) & ((n8_int & (CHUNK_ALIGN << j)) != 0), 1.0, flag)
    rank = jnp.dot(lower, flag.astype(BF16), preferred_element_type=F32)
    e_id = lax.broadcasted_iota(jnp.int32, (N_EXPERTS, LANES), 0)
    for j in range(CHUNK_BITS):
        chosen = (rank[:, j:j + 1] == lane.astype(F32)) & (flag[:, j:j + 1] > 0.0)
        plan_ref[j:j + 1, :] = jnp.sum(jnp.where(chosen, e_id, 0), axis=0, keepdims=True)
    plan_ref[CHUNK_BITS:CHUNK_BITS + 1, :] = jnp.sum(flag, axis=0, keepdims=True).astype(jnp.int32)
    plan_ref[CHUNK_BITS + 1:CHUNK_BITS + 2, :] = jnp.sum(n8_int, axis=0, keepdims=True)


def _pack_halves(v, exact=False):
    h = v.shape[1] // 2
    return _pack_pair(v[:, :h], v[:, h:], exact)


def _pack_pair(lo, hi, exact=False):
    if not exact:
        lo, hi = lo.astype(BF16).astype(F32), hi.astype(BF16).astype(F32)
    lo = lax.bitcast_convert_type(lo, U32)
    hi = lax.bitcast_convert_type(hi, U32)
    return (lo >> 16) | (hi & jnp.uint32(HI_MASK))


def _chunk_relative(pos, c):
    rel = pos - c * SORT_CHUNK
    inside = (rel >= 0) & (rel < SORT_CHUNK)
    return jnp.where(inside, rel, -1).astype(F32).astype(BF16)


def _unpack_halves(w):
    lo = lax.bitcast_convert_type(w << 16, F32).astype(BF16)
    hi = lax.bitcast_convert_type(w & jnp.uint32(HI_MASK), F32).astype(BF16)
    return lo, hi


def _start_group_copies(plan_ref, make_copy):
    for j in range(CHUNK_BITS):
        size = CHUNK_ALIGN << j

        def body(k, carry):
            make_copy(pl.multiple_of(plan_ref[j * LANES + k], CHUNK_ALIGN),
                      pl.multiple_of(plan_ref[(PLAN_ROWS + j) * LANES + k], CHUNK_ALIGN), size).start()
            return carry

        lax.fori_loop(0, plan_ref[CHUNK_BITS * LANES + j], body, 0)


def _wait_group_rows(total_rows, ns, make_copy):
    for j in range(ns.bit_length()):
        size = CHUNK_ALIGN << j
        if size > ns:
            break

        @pl.when((total_rows & size) != 0)
        def _():
            make_copy(0, 0, size).wait()


def _dispatch_kernel(tot_ref, zrow_ref, plan_ref, pos_ref, x_ref, xs_ref, sbuf_ref, zbuf_ref,
                     onehot_ref, sem, zsem, *, tg, ns, bm):
    g = pl.program_id(0)
    ng = pl.num_programs(0)
    slot = g % 2

    @pl.when(g == 0)
    def _():
        zbuf_ref[...] = jnp.zeros_like(zbuf_ref)
        row = pl.multiple_of(zrow_ref[0], CHUNK_ALIGN)
        tail = pltpu.make_async_copy(zbuf_ref, xs_ref.at[pl.ds(row, bm), :], zsem)
        tail.start()
        tail.wait()

    def copy_from(sl):
        def make_copy(src, dst, size):
            return pltpu.make_async_copy(sbuf_ref.at[sl, pl.ds(src, size), :],
                                         xs_ref.at[pl.ds(dst, size), :], sem.at[sl])
        return make_copy

    @pl.when(g >= 2)
    def _():
        _wait_group_rows(tot_ref[g - 2], ns, copy_from(slot))

    xb = x_ref[...].astype(BF16)
    pos = pos_ref[...]
    local_rows = lax.broadcasted_iota(jnp.int32, (SORT_CHUNK, tg), 0).astype(F32).astype(BF16)
    one = jnp.ones((SORT_CHUNK, tg), BF16)
    zero = jnp.zeros((SORT_CHUNK, tg), BF16)
    for c in range(ns // SORT_CHUNK):
        rel = _chunk_relative(pos, c)
        hit = local_rows == rel[0:1, :]
        for k in range(1, TOP_K):
            hit = jnp.logical_or(hit, local_rows == rel[k:k + 1, :])
        onehot_ref[c * SORT_CHUNK:(c + 1) * SORT_CHUNK, :] = jnp.where(hit, one, zero)
    onehot = onehot_ref[...]
    half = x_ref.shape[1] // 2
    for n in range(half // SORT_CHUNK):
        lo = jnp.dot(onehot, xb[:, n * SORT_CHUNK:(n + 1) * SORT_CHUNK], preferred_element_type=F32)
        hi = jnp.dot(onehot, xb[:, half + n * SORT_CHUNK:half + (n + 1) * SORT_CHUNK],
                     preferred_element_type=F32)
        sbuf_ref[slot, :, n * SORT_CHUNK:(n + 1) * SORT_CHUNK] = _pack_pair(lo, hi, exact=True)

    _start_group_copies(plan_ref, copy_from(slot))

    @pl.when(g == ng - 1)
    def _():
        @pl.when(g >= 1)
        def _():
            _wait_group_rows(tot_ref[g - 1], ns, copy_from(1 - slot))
        _wait_group_rows(tot_ref[g], ns, copy_from(slot))


def _dispatch(x1, pos, plan, tot, zrow, n_rows, tg, ns, bm):
    t, d = x1.shape
    grid_spec = pltpu.PrefetchScalarGridSpec(
        num_scalar_prefetch=2,
        grid=(t // tg,),
        in_specs=[
            pl.BlockSpec((PLAN_WORDS,), lambda i, *_: (i,), memory_space=pltpu.SMEM),
            pl.BlockSpec((TOP_K, tg), lambda i, *_: (0, i)),
            pl.BlockSpec((tg, d), lambda i, *_: (i, 0)),
        ],
        out_specs=pl.BlockSpec(memory_space=pl.ANY),
        scratch_shapes=[pltpu.VMEM((2, ns, d // 2), U32), pltpu.VMEM((bm, d // 2), U32),
                        pltpu.VMEM((ns, tg), BF16),
                        pltpu.SemaphoreType.DMA((2,)), pltpu.SemaphoreType.DMA],
    )
    return pl.pallas_call(
        functools.partial(_dispatch_kernel, tg=tg, ns=ns, bm=bm),
        grid_spec=grid_spec,
        out_shape=jax.ShapeDtypeStruct((n_rows, d // 2), U32),
        compiler_params=_params("arbitrary"),
        name="moe_dispatch",
    )(tot, zrow, plan, pos, x1)


def _gmm_kernel(nq_ref, brow_ref, bval_ref, bord_ref, bfirst_ref, aexp_ref,
                xs_ref, w1_ref, w3_ref, w2_ref, y_ref,
                xbuf_ref, ybuf_ref, w1s_ref, w3s_ref, w2s_ref, w13_ref, w2b_ref, xsem, ysem, wsem,
                *, layer, bm):
    nq = nq_ref[0]
    na = nq_ref[1]
    f = w1s_ref.shape[2]
    half = xbuf_ref.shape[2]
    nx = xbuf_ref.shape[0]

    def x_copy(q, slot):
        row = pl.multiple_of(brow_ref[q], CHUNK_ALIGN)
        return pltpu.make_async_copy(xs_ref.at[pl.ds(row, bm), :], xbuf_ref.at[slot], xsem.at[slot])

    def w_copies(j, slot):
        e = aexp_ref[j]
        return (pltpu.make_async_copy(w1_ref.at[layer, e], w1s_ref.at[slot], wsem.at[slot]),
                pltpu.make_async_copy(w3_ref.at[layer, e], w3s_ref.at[slot], wsem.at[slot]),
                pltpu.make_async_copy(w2_ref.at[layer, e], w2s_ref.at[slot], wsem.at[slot]))

    def y_copy(slot, row, off, size):
        return pltpu.make_async_copy(ybuf_ref.at[slot, pl.ds(off, size), :],
                                     y_ref.at[pl.ds(pl.multiple_of(row + off, CHUNK_ALIGN), size), :],
                                     ysem.at[slot])

    def for_each_y_copy(slot, row, valid, fn):
        @pl.when(valid == bm)
        def _():
            fn(y_copy(slot, row, 0, bm))

        @pl.when(valid < bm)
        def _():
            for j in range((bm // CHUNK_ALIGN).bit_length() - 1):
                size = CHUNK_ALIGN << j

                @pl.when((valid & size) != 0)
                def _():
                    fn(y_copy(slot, row, pl.multiple_of(valid & (size - 1), CHUNK_ALIGN), size))

    for c in w_copies(0, 0):
        c.start()
    for q0 in range(nx - 1):
        @pl.when(q0 < nq)
        def _():
            x_copy(q0, q0).start()

    def body(q, carry):
        xslot = q % nx
        yslot = q % 2

        @pl.when(q + (nx - 1) < nq)
        def _():
            x_copy(q + (nx - 1), (q + (nx - 1)) % nx).start()

        @pl.when(bfirst_ref[q] == 1)
        def _():
            j = bord_ref[q]
            ws = j % 2
            for c in w_copies(j, ws):
                c.wait()

            @pl.when(j + 1 < na)
            def _():
                for c in w_copies(j + 1, 1 - ws):
                    c.start()
            w13_ref[:, :f] = w1s_ref[ws].astype(BF16)
            w13_ref[:, f:] = w3s_ref[ws].astype(BF16)
            w2b_ref[...] = w2s_ref[ws].astype(BF16)

        x_copy(q, xslot).wait()

        @pl.when(q >= 2)
        def _():
            for_each_y_copy(yslot, 0, bval_ref[jnp.maximum(q - 2, 0)], lambda c: c.wait())

        x_lo, x_hi = _unpack_halves(xbuf_ref[xslot])
        h13 = (jnp.dot(x_lo, w13_ref[:half, :], preferred_element_type=F32)
               + jnp.dot(x_hi, w13_ref[half:, :], preferred_element_type=F32))
        h = jax.nn.silu(h13[:, :f]) * h13[:, f:]
        ybuf_ref[yslot] = _pack_halves(jnp.dot(h.astype(BF16), w2b_ref[...], preferred_element_type=F32))
        for_each_y_copy(yslot, brow_ref[q], bval_ref[q], lambda c: c.start())
        return carry

    lax.fori_loop(0, nq, body, 0)

    @pl.when(nq >= 2)
    def _():
        for_each_y_copy(nq % 2, 0, bval_ref[jnp.maximum(nq - 2, 0)], lambda c: c.wait())
    for_each_y_copy((nq - 1) % 2, 0, bval_ref[nq - 1], lambda c: c.wait())


def _gmm(xs, tables, w1, w3, w2, layer, bm):
    n_rows, half = xs.shape
    d = 2 * half
    f = w1.shape[-1]
    any_spec = pl.BlockSpec(memory_space=pl.ANY)
    grid_spec = pltpu.PrefetchScalarGridSpec(
        num_scalar_prefetch=len(tables),
        grid=(1,),
        in_specs=[any_spec, any_spec, any_spec, any_spec],
        out_specs=any_spec,
        scratch_shapes=[pltpu.VMEM((GMM_X_SLOTS, bm, half), U32), pltpu.VMEM((2, bm, half), U32),
                        pltpu.VMEM((2, d, f), F32), pltpu.VMEM((2, d, f), F32), pltpu.VMEM((2, f, d), F32),
                        pltpu.VMEM((d, 2 * f), BF16), pltpu.VMEM((f, d), BF16),
                        pltpu.SemaphoreType.DMA((GMM_X_SLOTS,)), pltpu.SemaphoreType.DMA((2,)),
                        pltpu.SemaphoreType.DMA((2,))],
    )
    return pl.pallas_call(
        functools.partial(_gmm_kernel, layer=layer, bm=bm),
        grid_spec=grid_spec,
        out_shape=jax.ShapeDtypeStruct((n_rows, half), U32),
        compiler_params=_params("arbitrary"),
        name="moe_experts",
    )(*tables, xs, w1, w3, w2)


def _combine_kernel(tot_ref, plan_ref, plan_next_ref, yb_ref, pos_ref, gate_ref, x_ref, ws13_ref, ws2_ref,
                    g_ref, b_ref, o_ref, ybuf_ref, wgt_ref, ylo_ref, yhi_ref, sem, *, alpha, tg, ns):
    g = pl.program_id(0)
    ng = pl.num_programs(0)
    slot = g % 2

    def copy_into(sl):
        def make_copy(buf_row, hbm_row, size):
            return pltpu.make_async_copy(yb_ref.at[pl.ds(hbm_row, size), :],
                                         ybuf_ref.at[sl, pl.ds(buf_row, size), :], sem.at[sl])
        return make_copy

    @pl.when(g == 0)
    def _():
        ybuf_ref[...] = jnp.zeros_like(ybuf_ref)
        _start_group_copies(plan_ref, copy_into(0))

    @pl.when(g + 1 < ng)
    def _():
        _start_group_copies(plan_next_ref, copy_into(1 - slot))

    x = x_ref[...]
    f = ws2_ref.shape[0]
    h13 = jnp.dot(x.astype(BF16), ws13_ref[...], preferred_element_type=F32)
    hs = jax.nn.silu(h13[:, :f]) * h13[:, f:]
    shared = jnp.dot(hs.astype(BF16), ws2_ref[...], preferred_element_type=F32)

    _wait_group_rows(tot_ref[g], ns, copy_into(slot))

    pos = pos_ref[...]
    gate_b = gate_ref[...].astype(BF16)
    local_rows = lax.broadcasted_iota(jnp.int32, (SORT_CHUNK, tg), 0).astype(F32).astype(BF16)
    for c in range(ns // SORT_CHUNK):
        rel = _chunk_relative(pos, c)
        wgt = jnp.zeros((SORT_CHUNK, tg), BF16)
        for k in range(TOP_K):
            wgt = jnp.where(local_rows == rel[k:k + 1, :], gate_b[k:k + 1, :], wgt)
        wgt_ref[c * SORT_CHUNK:(c + 1) * SORT_CHUNK, :] = wgt
        y_lo, y_hi = _unpack_halves(ybuf_ref[slot, c * SORT_CHUNK:(c + 1) * SORT_CHUNK, :])
        ylo_ref[c * SORT_CHUNK:(c + 1) * SORT_CHUNK, :] = y_lo
        yhi_ref[c * SORT_CHUNK:(c + 1) * SORT_CHUNK, :] = y_hi
    wgt_all = wgt_ref[...]
    dn = (((0,), (0,)), ((), ()))
    routed = jnp.concatenate([lax.dot_general(wgt_all, ylo_ref[...], dn, preferred_element_type=F32),
                              lax.dot_general(wgt_all, yhi_ref[...], dn, preferred_element_type=F32)],
                             axis=1)
    o_ref[...] = _layer_norm(alpha * x + (routed + shared), g_ref[...], b_ref[...])


def _combine(yb, pos_kt, gate_kt, plan, tot, x1, ws1, ws3, ws2, g, b, alpha, tg, ns):
    t, d = x1.shape
    ng = t // tg
    f = ws1.shape[1]
    ws13 = jnp.concatenate([ws1, ws3], axis=1).astype(BF16)
    const = lambda i, *_: (0, 0)
    tile = lambda i, *_: (i, 0)
    row = lambda v: v.reshape(1, d)
    grid_spec = pltpu.PrefetchScalarGridSpec(
        num_scalar_prefetch=1,
        grid=(ng,),
        in_specs=[
            pl.BlockSpec((PLAN_WORDS,), lambda i, *_: (i,), memory_space=pltpu.SMEM),
            pl.BlockSpec((PLAN_WORDS,), lambda i, *_: (jnp.minimum(i + 1, ng - 1),),
                         memory_space=pltpu.SMEM),
            pl.BlockSpec(memory_space=pl.ANY),
            pl.BlockSpec((TOP_K, tg), lambda i, *_: (0, i)),
            pl.BlockSpec((TOP_K, tg), lambda i, *_: (0, i)),
            pl.BlockSpec((tg, d), tile),
            pl.BlockSpec((d, 2 * f), const),
            pl.BlockSpec((f, d), const),
            pl.BlockSpec((1, d), const),
            pl.BlockSpec((1, d), const),
        ],
        out_specs=pl.BlockSpec((tg, d), tile),
        scratch_shapes=[pltpu.VMEM((2, ns, d // 2), U32), pltpu.VMEM((ns, tg), BF16),
                        pltpu.VMEM((ns, d // 2), BF16), pltpu.VMEM((ns, d // 2), BF16),
                        pltpu.SemaphoreType.DMA((2,))],
    )
    return pl.pallas_call(
        functools.partial(_combine_kernel, alpha=alpha, tg=tg, ns=ns),
        grid_spec=grid_spec,
        out_shape=jax.ShapeDtypeStruct((t, d), F32),
        compiler_params=_params("arbitrary"),
        name="moe_combine",
    )(tot, plan, plan, yb, pos_kt, gate_kt, x1, ws13, ws2.astype(BF16), row(g), row(b))


def _moe_layer(x1, routing, tg, layer, w1, w3, w2, ws1, ws3, ws2, g, b, alpha):
    t, d = x1.shape
    ng = t // tg
    bm = EXPERT_BM
    ns = -(-(tg * TOP_K + N_EXPERTS * (CHUNK_ALIGN - 1)) // SORT_CHUNK) * SORT_CHUNK
    pos, gate, n8_wide, plan_wide = routing
    plan3 = plan_wide.reshape(ng, PLAN_ROWS, LANES)
    tot = plan3[:, CHUNK_BITS + 1, 0]

    n8 = n8_wide[:, 0].reshape(ng, N_EXPERTS)
    goff = jnp.cumsum(n8, axis=1) - n8
    seg = jnp.sum(n8, axis=0)
    pend = jnp.cumsum(seg)
    pstart = pend - seg
    gdest = pstart[None, :] + jnp.cumsum(n8, axis=0) - n8
    max_rows = ng * (tg * TOP_K + N_EXPERTS * (CHUNK_ALIGN - 1))
    n_rows = max_rows + bm

    experts = jnp.arange(N_EXPERTS, dtype=jnp.int32)
    nb = (seg + bm - 1) // bm
    cb = jnp.cumsum(nb)
    blk = jnp.arange(max_rows // bm + N_EXPERTS, dtype=jnp.int32)
    blk_expert = jnp.minimum(jnp.sum((cb[None, :] <= blk[:, None]).astype(jnp.int32), axis=1), N_EXPERTS - 1)
    mine = blk_expert[:, None] == experts[None, :]
    of_expert = lambda v: jnp.sum(jnp.where(mine, v[None, :], 0), axis=1)
    local = blk - of_expert(cb - nb)
    active = seg > 0
    ordinal = jnp.cumsum(active.astype(jnp.int32)) - 1
    active_experts = jnp.sum(jnp.where((ordinal[None, :] == experts[:, None]) & active[None, :],
                                       experts[None, :], 0), axis=1)
    tables = [jnp.stack([cb[-1], jnp.sum(active.astype(jnp.int32))]),
              of_expert(pstart) + local * bm,
              jnp.clip(of_expert(seg) - local * bm, 0, bm),
              of_expert(ordinal),
              (local == 0).astype(jnp.int32),
              active_experts]
    tables = [a.astype(jnp.int32) for a in tables]
    zrow = pend[-1:].astype(jnp.int32)

    lists = plan3[:, :CHUNK_BITS, :N_EXPERTS]
    sizes = CHUNK_ALIGN << jnp.arange(CHUNK_BITS, dtype=jnp.int32)
    done = n8[:, None, :] & (sizes[None, :, None] - 1)
    pick = lists[..., None] == jnp.arange(N_EXPERTS, dtype=jnp.int32)
    src = jnp.sum(jnp.where(pick, (goff[:, None, :] + done)[:, :, None, :], 0), axis=-1)
    dst = jnp.sum(jnp.where(pick, (gdest[:, None, :] + done)[:, :, None, :], 0), axis=-1)
    widen = lambda a: jnp.pad(a, ((0, 0), (0, 0), (0, LANES - N_EXPERTS)))
    plan = jnp.concatenate([widen(src), plan3[:, CHUNK_BITS:], widen(dst),
                            jnp.zeros((ng, PLAN_ROWS - CHUNK_BITS, LANES), jnp.int32)],
                           axis=1).astype(jnp.int32).reshape(-1)

    xs = _dispatch(x1, pos, plan, tot, zrow, n_rows, tg, ns, bm)
    yb = _gmm(xs, tables, w1, w3, w2, layer, bm)
    return _combine(yb, pos, gate, plan, tot, x1, ws1, ws3, ws2, g, b, alpha, tg, ns)


def kernel(x, ln_gain, ln_bias, pool_w, pool_scale, attn_wqkv, attn_bqkv, attn_sinks, attn_wo, attn_bo,
           conv_w_in, conv_w, conv_w_out, router_w, router_bias, expert_w1, expert_w3, expert_w2,
           shared_w1, shared_w3, shared_w2):
    bsz, seq, d = x.shape
    depth = ln_gain.shape[0]
    alpha = float((2 * depth) ** 0.25)
    h = x.reshape(bsz * seq, d)
    tg = min(GROUP_TG, seq)
    for i in range(depth):
        kind = i % N_MIXERS
        j = i // N_MIXERS
        g1, b1 = ln_gain[i, 0], ln_bias[i, 0]
        route = (router_w[i], router_bias[i], tg)
        if kind == 0:
            h, *routing = _pool_mixer(h, seq, pool_w[j], pool_scale[j], g1, b1, alpha, *route)
        elif kind == 1:
            h, *routing = _swa_mixer(h, seq, attn_wqkv[j], attn_bqkv[j], attn_sinks[j], attn_wo[j],
                                     attn_bo[j], g1, b1, alpha, *route)
        else:
            h, *routing = _conv_mixer(h, seq, conv_w_in[j], conv_w[j], conv_w_out[j], g1, b1, alpha, *route)
        h = _moe_layer(h, routing, tg, i, expert_w1, expert_w3, expert_w2,
                       shared_w1[i], shared_w3[i], shared_w2[i], ln_gain[i, 1], ln_bias[i, 1], alpha)
    return h.reshape(bsz, seq, d)
```

```python
import functools

import jax
import jax.numpy as jnp
from jax import lax
from jax.experimental import pallas as pl
from jax.experimental.pallas import tpu as pltpu

POOL_WINDOWS = (2, 4, 8, 16)
N_MIXERS = 3
HEAD_DIM = 64
N_Q_HEADS = 16
N_KV_HEADS = 4
GQA_GROUP = N_Q_HEADS // N_KV_HEADS
WINDOW = 128
ROPE_THETA = 10000.0
N_EXPERTS = 64
TOP_K = 8
N_EXPERT_GROUPS = 8
GROUP_SIZE = N_EXPERTS // N_EXPERT_GROUPS
TOPK_GROUPS = 4
ROUTED_SCALE = 2.5
LN_EPS = 1e-5

LANES = 128
SUBLANES = 8
POOL_HALO = 16
VMEM_LIMIT = 48 * 1024 * 1024

POOL_TT = 512
QKV_TT = 512
ATTN_TQ = 512
CONV_TT = 512
GROUP_TG = 256
CHUNK_ALIGN = SUBLANES
SORT_CHUNK = 256
EXPERT_BM = 512
GMM_X_SLOTS = 3
CHUNK_BITS = 6
PLAN_ROWS = SUBLANES
PLAN_WORDS = 2 * PLAN_ROWS * LANES

U32 = jnp.uint32
HI_MASK = 0xFFFF0000

F32 = jnp.float32
BF16 = jnp.bfloat16
NEG_INF = float("-inf")


def _layer_norm(z, g, b):
    mu = jnp.mean(z, axis=-1, keepdims=True)
    zc = z - mu
    var = jnp.mean(zc * zc, axis=-1, keepdims=True)
    return zc * lax.rsqrt(var + LN_EPS) * g + b


def _params(*sem):
    return pltpu.CompilerParams(dimension_semantics=sem, vmem_limit_bytes=VMEM_LIMIT)


def _router_plumbing(router_w, router_bias, t, d, tt, tg, tile_index):
    groups = tt // tg
    const = lambda *ids: (0, 0)
    by_cols = lambda *ids: (0, tile_index(*ids))
    by_rows = lambda *ids: (tile_index(*ids), 0)
    operands = [router_w.T, router_bias.reshape(N_EXPERTS, 1)]
    in_specs = [pl.BlockSpec((N_EXPERTS, d), const), pl.BlockSpec((N_EXPERTS, 1), const)]
    out_specs = [pl.BlockSpec((TOP_K, tt), by_cols), pl.BlockSpec((TOP_K, tt), by_cols),
                 pl.BlockSpec((groups * N_EXPERTS, LANES), by_rows),
                 pl.BlockSpec((groups * PLAN_ROWS, LANES), by_rows)]
    out_shapes = [jax.ShapeDtypeStruct((TOP_K, t), jnp.int32), jax.ShapeDtypeStruct((TOP_K, t), F32),
                  jax.ShapeDtypeStruct((t // tg * N_EXPERTS, LANES), jnp.int32),
                  jax.ShapeDtypeStruct((t // tg * PLAN_ROWS, LANES), jnp.int32)]
    return operands, in_specs, out_specs, out_shapes


def _pool_kernel(x_ref, halo_ref, w_ref, scale_ref, g_ref, b_ref, wt_ref, rb_ref, o_ref, pos_ref, gate_ref,
                 n8_ref, plan_ref, buf_ref, *, alpha, tt, tg):
    i = pl.program_id(1)
    x = x_ref[...]
    d = x.shape[1]
    ch = d // len(POOL_WINDOWS)
    top = SUBLANES
    first = top + POOL_HALO
    n = first + tt
    for k in range(3):
        buf_ref[k, 0:top, :] = jnp.zeros((top, d), F32)
    buf_ref[0, top:first, :] = jnp.where(i > 0, halo_ref[...], 0.0)
    buf_ref[0, first:n, :] = x
    pos = i * tt + lax.broadcasted_iota(jnp.int32, (tt, 1), 0)
    ys = []
    for g, w in enumerate(POOL_WINDOWS):
        c0 = g * ch
        xg = x[:, c0:c0 + ch]
        src, span = 0, 1
        while span < w:
            dst = 1 if src != 1 else 2
            buf_ref[dst, top:n, c0:c0 + ch] = (buf_ref[src, top:n, c0:c0 + ch]
                                               + buf_ref[src, top - span:n - span, c0:c0 + ch])
            src, span = dst, 2 * span
        s = buf_ref[src, first:n, c0:c0 + ch]
        inv_cnt = 1.0 / jnp.minimum(pos + 1, w).astype(F32)
        pooled = s * inv_cnt - xg
        ys.append(jnp.dot(pooled.astype(BF16), w_ref[g], preferred_element_type=F32))
    y = jnp.concatenate(ys, axis=1) * scale_ref[...]
    o_ref[...] = _layer_norm(alpha * x + y, g_ref[...], b_ref[...])
    _route_tile(o_ref, (wt_ref, rb_ref), (pos_ref, gate_ref, n8_ref, plan_ref), tg)


def _pool_mixer(x2, seq, w_grp, scale, g, b, alpha, router_w, router_bias, tg):
    t, d = x2.shape
    tt = min(POOL_TT, seq)
    nt = seq // tt
    nb = t // seq
    row = lambda v: v.reshape(1, d)
    const2 = lambda bi, i: (0, 0)
    tile = lambda bi, i: bi * nt + i
    r_ops, r_in, r_out, r_shapes = _router_plumbing(router_w, router_bias, t, d, tt, tg, tile)
    return pl.pallas_call(
        functools.partial(_pool_kernel, alpha=alpha, tt=tt, tg=tg),
        grid=(nb, nt),
        in_specs=[
            pl.BlockSpec((tt, d), lambda bi, i: (bi * nt + i, 0)),
            pl.BlockSpec((POOL_HALO, d),
                         lambda bi, i: (jnp.maximum((bi * seq + i * tt) // POOL_HALO - 1, 0), 0)),
            pl.BlockSpec(w_grp.shape, lambda bi, i: (0, 0, 0)),
            pl.BlockSpec((1, d), const2),
            pl.BlockSpec((1, d), const2),
            pl.BlockSpec((1, d), const2),
        ] + r_in,
        out_specs=[pl.BlockSpec((tt, d), lambda bi, i: (bi * nt + i, 0))] + r_out,
        out_shape=[jax.ShapeDtypeStruct((t, d), F32)] + r_shapes,
        scratch_shapes=[pltpu.VMEM((3, SUBLANES + POOL_HALO + tt, d), F32)],
        compiler_params=_params("parallel", "parallel"),
        name="pool_mixer",
    )(x2, x2, w_grp.astype(BF16), row(scale), row(g), row(b), *r_ops)


def _qkv_kernel(x_ref, w_ref, b_ref, c_ref, s_ref, q_ref, k_ref, v_ref, *, nq, nk):
    y = jnp.dot(x_ref[...].astype(BF16), w_ref[...], preferred_element_type=F32) + b_ref[...]
    c = c_ref[...]
    s = s_ref[...]
    lane = lax.broadcasted_iota(jnp.int32, c.shape, 1)
    first_half = (lane % HEAD_DIM) < (HEAD_DIM // 2)

    def rope(v):
        partner = jnp.where(first_half, pltpu.roll(v, LANES - HEAD_DIM // 2, 1),
                            pltpu.roll(v, HEAD_DIM // 2, 1))
        return v * c + partner * s

    for j in range(nq // LANES):
        q_ref[:, j * LANES:(j + 1) * LANES] = (
            rope(y[:, j * LANES:(j + 1) * LANES]) * (HEAD_DIM ** -0.5)).astype(BF16)
    for j in range(nk // LANES):
        k_ref[:, j * LANES:(j + 1) * LANES] = rope(
            y[:, nq + j * LANES:nq + (j + 1) * LANES]).astype(BF16)
    v_ref[...] = y[:, nq + nk:].astype(BF16)


def _attn_kernel(sink_ref, q_ref, k_ref, kh_ref, v_ref, vh_ref, x_ref, wo_ref, bo_ref, g_ref, b_ref,
                 wt_ref, rb_ref, o_ref, pos_ref, gate_ref, n8_ref, plan_ref, *, alpha, tq, tg):
    i = pl.program_id(1)
    nqb = tq // WINDOW
    r = lax.broadcasted_iota(jnp.int32, (tq, 2 * WINDOW), 0)
    c = lax.broadcasted_iota(jnp.int32, (tq, 2 * WINDOW), 1)
    dist = (r & (WINDOW - 1)) + WINDOW - c
    valid = (dist >= 0) & (dist < WINDOW) & ((c >= WINDOW) | (r >= WINDOW) | (i > 0))
    lane = lax.broadcasted_iota(jnp.int32, (tq, LANES), 1)
    low = lane < HEAD_DIM
    dn = (((1,), (1,)), ((), ()))
    outs = []
    for kv in range(N_KV_HEADS):
        sl = slice(kv * LANES, (kv + 1) * LANES)
        kk = jnp.concatenate([kh_ref[:, sl], k_ref[:, sl]], axis=0)
        vv = jnp.concatenate([vh_ref[:, sl], v_ref[:, sl]], axis=0)
        for p in range(GQA_GROUP // 2):
            c0 = (kv * GQA_GROUP + 2 * p) * HEAD_DIM
            qp = q_ref[:, c0:c0 + LANES]
            res = []
            for half in range(2):
                keep = low if half == 0 else jnp.logical_not(low)
                qh = jnp.where(keep, qp, jnp.zeros_like(qp))
                sc_all = lax.dot_general(qh, kk, dn, preferred_element_type=F32)
                sink = sink_ref[kv * GQA_GROUP + 2 * p + half]
                sc = jnp.concatenate([sc_all[qb * WINDOW:(qb + 1) * WINDOW, qb * WINDOW:(qb + 2) * WINDOW]
                                      for qb in range(nqb)], axis=0)
                sc = jnp.where(valid, sc, NEG_INF)
                m = jnp.maximum(jnp.max(sc, axis=1, keepdims=True), sink)
                pm = jnp.exp(sc - m)
                rden = 1.0 / (jnp.sum(pm, axis=1, keepdims=True) + jnp.exp(sink - m))
                pm = pm.astype(BF16)
                res.append(jnp.concatenate(
                    [jnp.dot(pm[qb * WINDOW:(qb + 1) * WINDOW, :], vv[qb * WINDOW:(qb + 2) * WINDOW, :],
                             preferred_element_type=F32) for qb in range(nqb)], axis=0) * rden)
            outs.append(jnp.where(low, res[0], res[1]))
    o_all = jnp.concatenate(outs, axis=1).astype(BF16)
    mix = jnp.dot(o_all, wo_ref[...], preferred_element_type=F32) + bo_ref[...]
    o_ref[...] = _layer_norm(alpha * x_ref[...] + mix, g_ref[...], b_ref[...])
    _route_tile(o_ref, (wt_ref, rb_ref), (pos_ref, gate_ref, n8_ref, plan_ref), tg)


def _dup_heads(w, n_heads):
    lead = w.shape[:-1]
    w = w.reshape(lead + (n_heads, 1, HEAD_DIM))
    w = jnp.broadcast_to(w, lead + (n_heads, 2, HEAD_DIM))
    return w.reshape(lead + (n_heads * 2 * HEAD_DIM,))


def _swa_mixer(x2, seq, wqkv, bqkv, sinks, wo, bo, g, b, alpha, router_w, router_bias, tg):
    t, d = x2.shape
    nb = t // seq
    nq = N_Q_HEADS * HEAD_DIM
    nkv = N_KV_HEADS * HEAD_DIM
    nk2 = 2 * nkv
    w_ext = jnp.concatenate([wqkv[:, :nq], _dup_heads(wqkv[:, nq:nq + nkv], N_KV_HEADS),
                             _dup_heads(wqkv[:, nq + nkv:], N_KV_HEADS)], axis=1).astype(BF16)
    b_ext = jnp.concatenate([bqkv[:nq], _dup_heads(bqkv[nq:nq + nkv], N_KV_HEADS),
                             _dup_heads(bqkv[nq + nkv:], N_KV_HEADS)]).reshape(1, -1)
    n_ext = nq + 2 * nk2

    pos = jnp.arange(seq, dtype=F32)
    inv_freq = ROPE_THETA ** (-jnp.arange(0, HEAD_DIM, 2, dtype=F32) / HEAD_DIM)
    ang = pos[:, None] * inv_freq[None, :]
    cos, sin = jnp.cos(ang), jnp.sin(ang)
    cos_t = jnp.concatenate([cos, cos, cos, cos], axis=1)
    sin_t = jnp.concatenate([-sin, sin, -sin, sin], axis=1)

    tt = min(QKV_TT, seq)
    nt = seq // tt
    const2 = lambda i: (0, 0)
    q, k2, v2 = pl.pallas_call(
        functools.partial(_qkv_kernel, nq=nq, nk=nk2),
        grid=(t // tt,),
        in_specs=[
            pl.BlockSpec((tt, d), lambda i: (i, 0)),
            pl.BlockSpec((d, n_ext), const2),
            pl.BlockSpec((1, n_ext), const2),
            pl.BlockSpec((tt, LANES), lambda i: (i % nt, 0)),
            pl.BlockSpec((tt, LANES), lambda i: (i % nt, 0)),
        ],
        out_specs=[pl.BlockSpec((tt, nq), lambda i: (i, 0)),
                   pl.BlockSpec((tt, nk2), lambda i: (i, 0)),
                   pl.BlockSpec((tt, nk2), lambda i: (i, 0))],
        out_shape=[jax.ShapeDtypeStruct((t, nq), BF16),
                   jax.ShapeDtypeStruct((t, nk2), BF16),
                   jax.ShapeDtypeStruct((t, nk2), BF16)],
        compiler_params=_params("parallel"),
        name="swa_qkv",
    )(x2, w_ext, b_ext, cos_t, sin_t)

    tq = min(ATTN_TQ, seq)
    ntq = seq // tq
    main = lambda bi, i: (bi * ntq + i, 0)
    halo = lambda bi, i: (jnp.maximum((bi * seq + i * tq) // WINDOW - 1, 0), 0)
    const = lambda bi, i: (0, 0)
    row = lambda v: v.reshape(1, d)
    r_ops, r_in, r_out, r_shapes = _router_plumbing(router_w, router_bias, t, d, tq, tg,
                                                    lambda bi, i: bi * ntq + i)
    return pl.pallas_call(
        functools.partial(_attn_kernel, alpha=alpha, tq=tq, tg=tg),
        grid=(nb, ntq),
        in_specs=[
            pl.BlockSpec(memory_space=pltpu.SMEM),
            pl.BlockSpec((tq, nq), main),
            pl.BlockSpec((tq, nk2), main),
            pl.BlockSpec((WINDOW, nk2), halo),
            pl.BlockSpec((tq, nk2), main),
            pl.BlockSpec((WINDOW, nk2), halo),
            pl.BlockSpec((tq, d), main),
            pl.BlockSpec((nq, d), const),
            pl.BlockSpec((1, d), const),
            pl.BlockSpec((1, d), const),
            pl.BlockSpec((1, d), const),
        ] + r_in,
        out_specs=[pl.BlockSpec((tq, d), main)] + r_out,
        out_shape=[jax.ShapeDtypeStruct((t, d), F32)] + r_shapes,
        compiler_params=_params("parallel", "parallel"),
        name="swa_attention",
    )(sinks, q, k2, k2, v2, v2, x2, wo.astype(BF16), row(bo), row(g), row(b), *r_ops)


def _conv_kernel(x_ref, win_ref, cw_ref, wout_ref, g_ref, b_ref, wt_ref, rb_ref, o_ref, pos_ref, gate_ref,
                 n8_ref, plan_ref, ubuf_ref, *, alpha, tt, tg):
    i = pl.program_id(1)
    d = x_ref.shape[1]

    @pl.when(i == 0)
    def _():
        ubuf_ref[0:SUBLANES, :] = jnp.zeros((SUBLANES, d), F32)

    x = x_ref[...]
    proj = jnp.dot(x.astype(BF16), win_ref[...], preferred_element_type=F32)
    gate_b = proj[:, :d]
    u = proj[:, d:2 * d] * proj[:, 2 * d:]
    ubuf_ref[SUBLANES:SUBLANES + tt, :] = u
    cw = cw_ref[...]
    conv = (cw[0:1, :] * ubuf_ref[SUBLANES - 2:SUBLANES - 2 + tt, :]
            + cw[1:2, :] * ubuf_ref[SUBLANES - 1:SUBLANES - 1 + tt, :]
            + cw[2:3, :] * u)
    ubuf_ref[0:SUBLANES, :] = u[tt - SUBLANES:tt, :]
    y = jnp.dot((gate_b * conv).astype(BF16), wout_ref[...], preferred_element_type=F32)
    o_ref[...] = _layer_norm(alpha * x + y, g_ref[...], b_ref[...])
    _route_tile(o_ref, (wt_ref, rb_ref), (pos_ref, gate_ref, n8_ref, plan_ref), tg)


def _conv_mixer(x2, seq, w_in, conv_w, w_out, g, b, alpha, router_w, router_bias, tg):
    t, d = x2.shape
    nb = t // seq
    tt = min(CONV_TT, seq)
    nt = seq // tt
    const = lambda bi, i: (0, 0)
    row = lambda v: v.reshape(1, d)
    r_ops, r_in, r_out, r_shapes = _router_plumbing(router_w, router_bias, t, d, tt, tg,
                                                    lambda bi, i: bi * nt + i)
    return pl.pallas_call(
        functools.partial(_conv_kernel, alpha=alpha, tt=tt, tg=tg),
        grid=(nb, nt),
        in_specs=[
            pl.BlockSpec((tt, d), lambda bi, i: (bi * nt + i, 0)),
            pl.BlockSpec((d, 3 * d), const),
            pl.BlockSpec(conv_w.shape, const),
            pl.BlockSpec((d, d), const),
            pl.BlockSpec((1, d), const),
            pl.BlockSpec((1, d), const),
        ] + r_in,
        out_specs=[pl.BlockSpec((tt, d), lambda bi, i: (bi * nt + i, 0))] + r_out,
        out_shape=[jax.ShapeDtypeStruct((t, d), F32)] + r_shapes,
        scratch_shapes=[pltpu.VMEM((SUBLANES + tt, d), F32)],
        compiler_params=_params("arbitrary", "arbitrary"),
        name="conv_mixer",
    )(x2, w_in.astype(BF16), conv_w, w_out.astype(BF16), row(g), row(b), *r_ops)


def _first_argmax(v, rows, n):
    m = jnp.max(v, axis=0, keepdims=True)
    first = jnp.min(jnp.where(v == m, rows, n), axis=0, keepdims=True)
    return m, first


def _route_tile(o_ref, router_refs, route_out_refs, tg):
    wt_ref, bias_ref = router_refs
    pos_ref, gate_ref, n8_ref, plan_ref = route_out_refs
    tt = o_ref.shape[0]
    x = o_ref[...]
    xh = x.astype(BF16)
    xl = (x - xh.astype(F32)).astype(BF16)
    wt = wt_ref[...]
    wh = wt.astype(BF16)
    wl = (wt - wh.astype(F32)).astype(BF16)
    dn = (((1,), (1,)), ((), ()))
    logits = (lax.dot_general(wh, xh, dn, preferred_element_type=F32)
              + (lax.dot_general(wh, xl, dn, preferred_element_type=F32)
                 + lax.dot_general(wl, xh, dn, preferred_element_type=F32)))
    scores = jax.nn.sigmoid(logits)
    sel = scores + bias_ref[...]

    rows8 = lax.broadcasted_iota(jnp.int32, (GROUP_SIZE, tt), 0)
    gsc = []
    for gi in range(N_EXPERT_GROUPS):
        blk = sel[gi * GROUP_SIZE:(gi + 1) * GROUP_SIZE, :]
        m1, f1 = _first_argmax(blk, rows8, GROUP_SIZE)
        m2 = jnp.max(jnp.where(rows8 == f1, NEG_INF, blk), axis=0, keepdims=True)
        gsc.append(m1 + m2)
    cur = jnp.concatenate(gsc, axis=0)
    rows_g = lax.broadcasted_iota(jnp.int32, (N_EXPERT_GROUPS, tt), 0)
    gmask = jnp.zeros((N_EXPERT_GROUPS, tt), F32)
    for _ in range(TOPK_GROUPS):
        _, f = _first_argmax(cur, rows_g, N_EXPERT_GROUPS)
        hit = rows_g == f
        gmask = jnp.where(hit, 1.0, gmask)
        cur = jnp.where(hit, NEG_INF, cur)
    emask = jnp.concatenate(
        [jnp.broadcast_to(gmask[gi:gi + 1, :], (GROUP_SIZE, tt)) for gi in range(N_EXPERT_GROUPS)],
        axis=0)

    rows_e = lax.broadcasted_iota(jnp.int32, (N_EXPERTS, tt), 0)
    cur = jnp.where(emask > 0.0, sel, NEG_INF)
    idxs, gates = [], []
    member = jnp.zeros((N_EXPERTS, tt), F32)
    for _ in range(TOP_K):
        _, f = _first_argmax(cur, rows_e, N_EXPERTS)
        hit = rows_e == f
        idxs.append(f)
        gates.append(jnp.sum(jnp.where(hit, scores, 0.0), axis=0, keepdims=True))
        member = jnp.where(hit, 1.0, member)
        cur = jnp.where(hit, NEG_INF, cur)
    gate = jnp.concatenate(gates, axis=0)
    gate_ref[...] = gate / jnp.sum(gate, axis=0, keepdims=True) * ROUTED_SCALE

    for s in range(tt // tg):
        cols = slice(s * tg, (s + 1) * tg)
        _plan_group(member[:, cols], [f[:, cols] for f in idxs], pos_ref.at[:, pl.ds(s * tg, tg)],
                    n8_ref.at[pl.ds(s * N_EXPERTS, N_EXPERTS), :],
                    plan_ref.at[pl.ds(s * PLAN_ROWS, PLAN_ROWS), :], tg)


def _plan_group(member, idxs, pos_ref, n8_ref, plan_ref, tt):
    rows_e = lax.broadcasted_iota(jnp.int32, (N_EXPERTS, tt), 0)
    ta = lax.broadcasted_iota(jnp.int32, (tt, tt), 0)
    tb = lax.broadcasted_iota(jnp.int32, (tt, tt), 1)
    earlier = (ta < tb).astype(BF16)
    before = jnp.dot(member.astype(BF16), earlier, preferred_element_type=F32)
    n = jnp.sum(member, axis=1, keepdims=True)
    n8 = jnp.floor((n + (CHUNK_ALIGN - 1)) * (1.0 / CHUNK_ALIGN)) * CHUNK_ALIGN
    n8_wide = jnp.broadcast_to(n8, (N_EXPERTS, LANES))
    ea = lax.broadcasted_iota(jnp.int32, (N_EXPERTS, N_EXPERTS), 0)
    eb = lax.broadcasted_iota(jnp.int32, (N_EXPERTS, N_EXPERTS), 1)
    lower = (eb < ea).astype(BF16)
    chunk_start = jnp.dot(lower, n8_wide.astype(BF16), preferred_element_type=F32)
    where_to = before + chunk_start[:, 0:1]
    for k in range(TOP_K):
        pk = jnp.sum(jnp.where(rows_e == idxs[k], where_to, 0.0), axis=0, keepdims=True)
        pos_ref[k:k + 1, :] = pk.astype(jnp.int32)
    n8_int = n8_wide.astype(jnp.int32)
    n8_ref[...] = n8_int

    lane = lax.broadcasted_iota(jnp.int32, (N_EXPERTS, LANES), 1)
    flag = jnp.zeros((N_EXPERTS, LANES), F32)
    for j in range(CHUNK_BITS):
        flag = jnp.where((lane == j) & ((n8_int & (CHUNK_ALIGN << j)) != 0), 1.0, flag)
    rank = jnp.dot(lower, flag.astype(BF16), preferred_element_type=F32)
    e_id = lax.broadcasted_iota(jnp.int32, (N_EXPERTS, LANES), 0)
    for j in range(CHUNK_BITS):
        chosen = (rank[:, j:j + 1] == lane.astype(F32)) & (flag[:, j:j + 1] > 0.0)
        plan_ref[j:j + 1, :] = jnp.sum(jnp.where(chosen, e_id, 0), axis=0, keepdims=True)
    plan_ref[CHUNK_BITS:CHUNK_BITS + 1, :] = jnp.sum(flag, axis=0, keepdims=True).astype(jnp.int32)
    plan_ref[CHUNK_BITS + 1:CHUNK_BITS + 2, :] = jnp.sum(n8_int, axis=0, keepdims=True)


def _pack_halves(v, exact=False):
    h = v.shape[1] // 2
    return _pack_pair(v[:, :h], v[:, h:], exact)


def _pack_pair(lo, hi, exact=False):
    if not exact:
        lo, hi = lo.astype(BF16).astype(F32), hi.astype(BF16).astype(F32)
    lo = lax.bitcast_convert_type(lo, U32)
    hi = lax.bitcast_convert_type(hi, U32)
    return (lo >> 16) | (hi & jnp.uint32(HI_MASK))


def _chunk_relative(pos, c):
    rel = pos - c * SORT_CHUNK
    inside = (rel >= 0) & (rel < SORT_CHUNK)
    return jnp.where(inside, rel, -1).astype(F32).astype(BF16)


def _unpack_halves(w):
    lo = lax.bitcast_convert_type(w << 16, F32).astype(BF16)
    hi = lax.bitcast_convert_type(w & jnp.uint32(HI_MASK), F32).astype(BF16)
    return lo, hi


def _start_group_copies(plan_ref, make_copy):
    for j in range(CHUNK_BITS):
        size = CHUNK_ALIGN << j

        def body(k, carry):
            make_copy(pl.multiple_of(plan_ref[j * LANES + k], CHUNK_ALIGN),
                      pl.multiple_of(plan_ref[(PLAN_ROWS + j) * LANES + k], CHUNK_ALIGN), size).start()
            return carry

        lax.fori_loop(0, plan_ref[CHUNK_BITS * LANES + j], body, 0)


def _wait_group_rows(total_rows, ns, make_copy):
    for j in range(ns.bit_length()):
        size = CHUNK_ALIGN << j
        if size > ns:
            break

        @pl.when((total_rows & size) != 0)
        def _():
            make_copy(0, 0, size).wait()


def _dispatch_kernel(tot_ref, zrow_ref, plan_ref, pos_ref, x_ref, xs_ref, sbuf_ref, zbuf_ref,
                     onehot_ref, sem, zsem, *, tg, ns, bm):
    g = pl.program_id(0)
    ng = pl.num_programs(0)
    slot = g % 2

    @pl.when(g == 0)
    def _():
        zbuf_ref[...] = jnp.zeros_like(zbuf_ref)
        row = pl.multiple_of(zrow_ref[0], CHUNK_ALIGN)
        tail = pltpu.make_async_copy(zbuf_ref, xs_ref.at[pl.ds(row, bm), :], zsem)
        tail.start()
        tail.wait()

    def copy_from(sl):
        def make_copy(src, dst, size):
            return pltpu.make_async_copy(sbuf_ref.at[sl, pl.ds(src, size), :],
                                         xs_ref.at[pl.ds(dst, size), :], sem.at[sl])
        return make_copy

    @pl.when(g >= 2)
    def _():
        _wait_group_rows(tot_ref[g - 2], ns, copy_from(slot))

    xb = x_ref[...].astype(BF16)
    pos = pos_ref[...]
    local_rows = lax.broadcasted_iota(jnp.int32, (SORT_CHUNK, tg), 0).astype(F32).astype(BF16)
    one = jnp.ones((SORT_CHUNK, tg), BF16)
    zero = jnp.zeros((SORT_CHUNK, tg), BF16)
    for c in range(ns // SORT_CHUNK):
        rel = _chunk_relative(pos, c)
        hit = local_rows == rel[0:1, :]
        for k in range(1, TOP_K):
            hit = jnp.logical_or(hit, local_rows == rel[k:k + 1, :])
        onehot_ref[c * SORT_CHUNK:(c + 1) * SORT_CHUNK, :] = jnp.where(hit, one, zero)
    onehot = onehot_ref[...]
    half = x_ref.shape[1] // 2
    for n in range(half // SORT_CHUNK):
        lo = jnp.dot(onehot, xb[:, n * SORT_CHUNK:(n + 1) * SORT_CHUNK], preferred_element_type=F32)
        hi = jnp.dot(onehot, xb[:, half + n * SORT_CHUNK:half + (n + 1) * SORT_CHUNK],
                     preferred_element_type=F32)
        sbuf_ref[slot, :, n * SORT_CHUNK:(n + 1) * SORT_CHUNK] = _pack_pair(lo, hi, exact=True)

    _start_group_copies(plan_ref, copy_from(slot))

    @pl.when(g == ng - 1)
    def _():
        @pl.when(g >= 1)
        def _():
            _wait_group_rows(tot_ref[g - 1], ns, copy_from(1 - slot))
        _wait_group_rows(tot_ref[g], ns, copy_from(slot))


def _dispatch(x1, pos, plan, tot, zrow, n_rows, tg, ns, bm):
    t, d = x1.shape
    grid_spec = pltpu.PrefetchScalarGridSpec(
        num_scalar_prefetch=2,
        grid=(t // tg,),
        in_specs=[
            pl.BlockSpec((PLAN_WORDS,), lambda i, *_: (i,), memory_space=pltpu.SMEM),
            pl.BlockSpec((TOP_K, tg), lambda i, *_: (0, i)),
            pl.BlockSpec((tg, d), lambda i, *_: (i, 0)),
        ],
        out_specs=pl.BlockSpec(memory_space=pl.ANY),
        scratch_shapes=[pltpu.VMEM((2, ns, d // 2), U32), pltpu.VMEM((bm, d // 2), U32),
                        pltpu.VMEM((ns, tg), BF16),
                        pltpu.SemaphoreType.DMA((2,)), pltpu.SemaphoreType.DMA],
    )
    return pl.pallas_call(
        functools.partial(_dispatch_kernel, tg=tg, ns=ns, bm=bm),
        grid_spec=grid_spec,
        out_shape=jax.ShapeDtypeStruct((n_rows, d // 2), U32),
        compiler_params=_params("arbitrary"),
        name="moe_dispatch",
    )(tot, zrow, plan, pos, x1)


def _gmm_kernel(nq_ref, brow_ref, bval_ref, bord_ref, bfirst_ref, aexp_ref,
                xs_ref, w1_ref, w3_ref, w2_ref, y_ref,
                xbuf_ref, ybuf_ref, w1s_ref, w3s_ref, w2s_ref, w13_ref, w2b_ref, xsem, ysem, wsem,
                *, layer, bm):
    nq = nq_ref[0]
    na = nq_ref[1]
    f = w1s_ref.shape[2]
    half = xbuf_ref.shape[2]
    nx = xbuf_ref.shape[0]

    def x_copy(q, slot):
        row = pl.multiple_of(brow_ref[q], CHUNK_ALIGN)
        return pltpu.make_async_copy(xs_ref.at[pl.ds(row, bm), :], xbuf_ref.at[slot], xsem.at[slot])

    def w_copies(j, slot):
        e = aexp_ref[j]
        return (pltpu.make_async_copy(w1_ref.at[layer, e], w1s_ref.at[slot], wsem.at[slot]),
                pltpu.make_async_copy(w3_ref.at[layer, e], w3s_ref.at[slot], wsem.at[slot]),
                pltpu.make_async_copy(w2_ref.at[layer, e], w2s_ref.at[slot], wsem.at[slot]))

    def y_copy(slot, row, off, size):
        return pltpu.make_async_copy(ybuf_ref.at[slot, pl.ds(off, size), :],
                                     y_ref.at[pl.ds(pl.multiple_of(row + off, CHUNK_ALIGN), size), :],
                                     ysem.at[slot])

    def for_each_y_copy(slot, row, valid, fn):
        @pl.when(valid == bm)
        def _():
            fn(y_copy(slot, row, 0, bm))

        @pl.when(valid < bm)
        def _():
            for j in range((bm // CHUNK_ALIGN).bit_length() - 1):
                size = CHUNK_ALIGN << j

                @pl.when((valid & size) != 0)
                def _():
                    fn(y_copy(slot, row, pl.multiple_of(valid & (size - 1), CHUNK_ALIGN), size))

    for c in w_copies(0, 0):
        c.start()
    for q0 in range(nx - 1):
        @pl.when(q0 < nq)
        def _():
            x_copy(q0, q0).start()

    def body(q, carry):
        xslot = q % nx
        yslot = q % 2

        @pl.when(q + (nx - 1) < nq)
        def _():
            x_copy(q + (nx - 1), (q + (nx - 1)) % nx).start()

        @pl.when(bfirst_ref[q] == 1)
        def _():
            j = bord_ref[q]
            ws = j % 2
            for c in w_copies(j, ws):
                c.wait()

            @pl.when(j + 1 < na)
            def _():
                for c in w_copies(j + 1, 1 - ws):
                    c.start()
            w13_ref[:, :f] = w1s_ref[ws].astype(BF16)
            w13_ref[:, f:] = w3s_ref[ws].astype(BF16)
            w2b_ref[...] = w2s_ref[ws].astype(BF16)

        x_copy(q, xslot).wait()

        @pl.when(q >= 2)
        def _():
            for_each_y_copy(yslot, 0, bval_ref[jnp.maximum(q - 2, 0)], lambda c: c.wait())

        x_lo, x_hi = _unpack_halves(xbuf_ref[xslot])
        h13 = (jnp.dot(x_lo, w13_ref[:half, :], preferred_element_type=F32)
               + jnp.dot(x_hi, w13_ref[half:, :], preferred_element_type=F32))
        h = jax.nn.silu(h13[:, :f]) * h13[:, f:]
        ybuf_ref[yslot] = _pack_halves(jnp.dot(h.astype(BF16), w2b_ref[...], preferred_element_type=F32))
        for_each_y_copy(yslot, brow_ref[q], bval_ref[q], lambda c: c.start())
        return carry

    lax.fori_loop(0, nq, body, 0)

    @pl.when(nq >= 2)
    def _():
        for_each_y_copy(nq % 2, 0, bval_ref[jnp.maximum(nq - 2, 0)], lambda c: c.wait())
    for_each_y_copy((nq - 1) % 2, 0, bval_ref[nq - 1], lambda c: c.wait())


def _gmm(xs, tables, w1, w3, w2, layer, bm):
    n_rows, half = xs.shape
    d = 2 * half
    f = w1.shape[-1]
    any_spec = pl.BlockSpec(memory_space=pl.ANY)
    grid_spec = pltpu.PrefetchScalarGridSpec(
        num_scalar_prefetch=len(tables),
        grid=(1,),
        in_specs=[any_spec, any_spec, any_spec, any_spec],
        out_specs=any_spec,
        scratch_shapes=[pltpu.VMEM((GMM_X_SLOTS, bm, half), U32), pltpu.VMEM((2, bm, half), U32),
                        pltpu.VMEM((2, d, f), F32), pltpu.VMEM((2, d, f), F32), pltpu.VMEM((2, f, d), F32),
                        pltpu.VMEM((d, 2 * f), BF16), pltpu.VMEM((f, d), BF16),
                        pltpu.SemaphoreType.DMA((GMM_X_SLOTS,)), pltpu.SemaphoreType.DMA((2,)),
                        pltpu.SemaphoreType.DMA((2,))],
    )
    return pl.pallas_call(
        functools.partial(_gmm_kernel, layer=layer, bm=bm),
        grid_spec=grid_spec,
        out_shape=jax.ShapeDtypeStruct((n_rows, half), U32),
        compiler_params=_params("arbitrary"),
        name="moe_experts",
    )(*tables, xs, w1, w3, w2)


def _combine_kernel(tot_ref, plan_ref, plan_next_ref, yb_ref, pos_ref, gate_ref, x_ref, ws13_ref, ws2_ref,
                    g_ref, b_ref, o_ref, ybuf_ref, wgt_ref, ylo_ref, yhi_ref, sem, *, alpha, tg, ns):
    g = pl.program_id(0)
    ng = pl.num_programs(0)
    slot = g % 2

    def copy_into(sl):
        def make_copy(buf_row, hbm_row, size):
            return pltpu.make_async_copy(yb_ref.at[pl.ds(hbm_row, size), :],
                                         ybuf_ref.at[sl, pl.ds(buf_row, size), :], sem.at[sl])
        return make_copy

    @pl.when(g == 0)
    def _():
        ybuf_ref[...] = jnp.zeros_like(ybuf_ref)
        _start_group_copies(plan_ref, copy_into(0))

    @pl.when(g + 1 < ng)
    def _():
        _start_group_copies(plan_next_ref, copy_into(1 - slot))

    x = x_ref[...]
    f = ws2_ref.shape[0]
    h13 = jnp.dot(x.astype(BF16), ws13_ref[...], preferred_element_type=F32)
    hs = jax.nn.silu(h13[:, :f]) * h13[:, f:]
    shared = jnp.dot(hs.astype(BF16), ws2_ref[...], preferred_element_type=F32)

    _wait_group_rows(tot_ref[g], ns, copy_into(slot))

    pos = pos_ref[...]
    gate_b = gate_ref[...].astype(BF16)
    local_rows = lax.broadcasted_iota(jnp.int32, (SORT_CHUNK, tg), 0).astype(F32).astype(BF16)
    for c in range(ns // SORT_CHUNK):
        rel = _chunk_relative(pos, c)
        wgt = jnp.zeros((SORT_CHUNK, tg), BF16)
        for k in range(TOP_K):
            wgt = jnp.where(local_rows == rel[k:k + 1, :], gate_b[k:k + 1, :], wgt)
        wgt_ref[c * SORT_CHUNK:(c + 1) * SORT_CHUNK, :] = wgt
        y_lo, y_hi = _unpack_halves(ybuf_ref[slot, c * SORT_CHUNK:(c + 1) * SORT_CHUNK, :])
        ylo_ref[c * SORT_CHUNK:(c + 1) * SORT_CHUNK, :] = y_lo
        yhi_ref[c * SORT_CHUNK:(c + 1) * SORT_CHUNK, :] = y_hi
    wgt_all = wgt_ref[...]
    dn = (((0,), (0,)), ((), ()))
    routed = jnp.concatenate([lax.dot_general(wgt_all, ylo_ref[...], dn, preferred_element_type=F32),
                              lax.dot_general(wgt_all, yhi_ref[...], dn, preferred_element_type=F32)],
                             axis=1)
    o_ref[...] = _layer_norm(alpha * x + (routed + shared), g_ref[...], b_ref[...])


def _combine(yb, pos_kt, gate_kt, plan, tot, x1, ws1, ws3, ws2, g, b, alpha, tg, ns):
    t, d = x1.shape
    ng = t // tg
    f = ws1.shape[1]
    ws13 = jnp.concatenate([ws1, ws3], axis=1).astype(BF16)
    const = lambda i, *_: (0, 0)
    tile = lambda i, *_: (i, 0)
    row = lambda v: v.reshape(1, d)
    grid_spec = pltpu.PrefetchScalarGridSpec(
        num_scalar_prefetch=1,
        grid=(ng,),
        in_specs=[
            pl.BlockSpec((PLAN_WORDS,), lambda i, *_: (i,), memory_space=pltpu.SMEM),
            pl.BlockSpec((PLAN_WORDS,), lambda i, *_: (jnp.minimum(i + 1, ng - 1),),
                         memory_space=pltpu.SMEM),
            pl.BlockSpec(memory_space=pl.ANY),
            pl.BlockSpec((TOP_K, tg), lambda i, *_: (0, i)),
            pl.BlockSpec((TOP_K, tg), lambda i, *_: (0, i)),
            pl.BlockSpec((tg, d), tile),
            pl.BlockSpec((d, 2 * f), const),
            pl.BlockSpec((f, d), const),
            pl.BlockSpec((1, d), const),
            pl.BlockSpec((1, d), const),
        ],
        out_specs=pl.BlockSpec((tg, d), tile),
        scratch_shapes=[pltpu.VMEM((2, ns, d // 2), U32), pltpu.VMEM((ns, tg), BF16),
                        pltpu.VMEM((ns, d // 2), BF16), pltpu.VMEM((ns, d // 2), BF16),
                        pltpu.SemaphoreType.DMA((2,))],
    )
    return pl.pallas_call(
        functools.partial(_combine_kernel, alpha=alpha, tg=tg, ns=ns),
        grid_spec=grid_spec,
        out_shape=jax.ShapeDtypeStruct((t, d), F32),
        compiler_params=_params("arbitrary"),
        name="moe_combine",
    )(tot, plan, plan, yb, pos_kt, gate_kt, x1, ws13, ws2.astype(BF16), row(g), row(b))


def _moe_layer(x1, routing, tg, layer, w1, w3, w2, ws1, ws3, ws2, g, b, alpha):
    t, d = x1.shape
    ng = t // tg
    bm = EXPERT_BM
    ns = -(-(tg * TOP_K + N_EXPERTS * (CHUNK_ALIGN - 1)) // SORT_CHUNK) * SORT_CHUNK
    pos, gate, n8_wide, plan_wide = routing
    plan3 = plan_wide.reshape(ng, PLAN_ROWS, LANES)
    tot = plan3[:, CHUNK_BITS + 1, 0]

    n8 = n8_wide[:, 0].reshape(ng, N_EXPERTS)
    goff = jnp.cumsum(n8, axis=1) - n8
    seg = jnp.sum(n8, axis=0)
    pend = jnp.cumsum(seg)
    pstart = pend - seg
    gdest = pstart[None, :] + jnp.cumsum(n8, axis=0) - n8
    max_rows = ng * (tg * TOP_K + N_EXPERTS * (CHUNK_ALIGN - 1))
    n_rows = max_rows + bm

    experts = jnp.arange(N_EXPERTS, dtype=jnp.int32)
    nb = (seg + bm - 1) // bm
    cb = jnp.cumsum(nb)
    blk = jnp.arange(max_rows // bm + N_EXPERTS, dtype=jnp.int32)
    blk_expert = jnp.minimum(jnp.sum((cb[None, :] <= blk[:, None]).astype(jnp.int32), axis=1), N_EXPERTS - 1)
    mine = blk_expert[:, None] == experts[None, :]
    of_expert = lambda v: jnp.sum(jnp.where(mine, v[None, :], 0), axis=1)
    local = blk - of_expert(cb - nb)
    active = seg > 0
    ordinal = jnp.cumsum(active.astype(jnp.int32)) - 1
    active_experts = jnp.sum(jnp.where((ordinal[None, :] == experts[:, None]) & active[None, :],
                                       experts[None, :], 0), axis=1)
    tables = [jnp.stack([cb[-1], jnp.sum(active.astype(jnp.int32))]),
              of_expert(pstart) + local * bm,
              jnp.clip(of_expert(seg) - local * bm, 0, bm),
              of_expert(ordinal),
              (local == 0).astype(jnp.int32),
              active_experts]
    tables = [a.astype(jnp.int32) for a in tables]
    zrow = pend[-1:].astype(jnp.int32)

    lists = plan3[:, :CHUNK_BITS, :N_EXPERTS]
    sizes = CHUNK_ALIGN << jnp.arange(CHUNK_BITS, dtype=jnp.int32)
    done = n8[:, None, :] & (sizes[None, :, None] - 1)
    pick = lists[..., None] == jnp.arange(N_EXPERTS, dtype=jnp.int32)
    src = jnp.sum(jnp.where(pick, (goff[:, None, :] + done)[:, :, None, :], 0), axis=-1)
    dst = jnp.sum(jnp.where(pick, (gdest[:, None, :] + done)[:, :, None, :], 0), axis=-1)
    widen = lambda a: jnp.pad(a, ((0, 0), (0, 0), (0, LANES - N_EXPERTS)))
    plan = jnp.concatenate([widen(src), plan3[:, CHUNK_BITS:], widen(dst),
                            jnp.zeros((ng, PLAN_ROWS - CHUNK_BITS, LANES), jnp.int32)],
                           axis=1).astype(jnp.int32).reshape(-1)

    xs = _dispatch(x1, pos, plan, tot, zrow, n_rows, tg, ns, bm)
    yb = _gmm(xs, tables, w1, w3, w2, layer, bm)
    return _combine(yb, pos, gate, plan, tot, x1, ws1, ws3, ws2, g, b, alpha, tg, ns)


def kernel(x, ln_gain, ln_bias, pool_w, pool_scale, attn_wqkv, attn_bqkv, attn_sinks, attn_wo, attn_bo,
           conv_w_in, conv_w, conv_w_out, router_w, router_bias, expert_w1, expert_w3, expert_w2,
           shared_w1, shared_w3, shared_w2):
    bsz, seq, d = x.shape
    depth = ln_gain.shape[0]
    alpha = float((2 * depth) ** 0.25)
    h = x.reshape(bsz * seq, d)
    tg = min(GROUP_TG, seq)
    for i in range(depth):
        kind = i % N_MIXERS
        j = i // N_MIXERS
        g1, b1 = ln_gain[i, 0], ln_bias[i, 0]
        route = (router_w[i], router_bias[i], tg)
        if kind == 0:
            h, *routing = _pool_mixer(h, seq, pool_w[j], pool_scale[j], g1, b1, alpha, *route)
        elif kind == 1:
            h, *routing = _swa_mixer(h, seq, attn_wqkv[j], attn_bqkv[j], attn_sinks[j], attn_wo[j],
                                     attn_bo[j], g1, b1, alpha, *route)
        else:
            h, *routing = _conv_mixer(h, seq, conv_w_in[j], conv_w[j], conv_w_out[j], g1, b1, alpha, *route)
        h = _moe_layer(h, routing, tg, i, expert_w1, expert_w3, expert_w2,
                       shared_w1[i], shared_w3[i], shared_w2[i], ln_gain[i, 1], ln_bias[i, 1], alpha)
    return h.reshape(bsz, seq, d)
```

```python
import functools

import jax
import jax.numpy as jnp
from jax import lax
from jax.experimental import pallas as pl
from jax.experimental.pallas import tpu as pltpu

POOL_WINDOWS = (2, 4, 8, 16)
N_MIXERS = 3
HEAD_DIM = 64
N_Q_HEADS = 16
N_KV_HEADS = 4
GQA_GROUP = N_Q_HEADS // N_KV_HEADS
WINDOW = 128
ROPE_THETA = 10000.0
N_EXPERTS = 64
TOP_K = 8
N_EXPERT_GROUPS = 8
GROUP_SIZE = N_EXPERTS // N_EXPERT_GROUPS
TOPK_GROUPS = 4
ROUTED_SCALE = 2.5
LN_EPS = 1e-5

LANES = 128
SUBLANES = 8
POOL_HALO = 16
VMEM_LIMIT = 48 * 1024 * 1024

POOL_TT = 512
QKV_TT = 512
ATTN_TQ = 512
CONV_TT = 512
GROUP_TG = 256
CHUNK_ALIGN = SUBLANES
SORT_CHUNK = 256
EXPERT_BM = 1024
GMM_X_SLOTS = 3
CHUNK_BITS = 6
PLAN_ROWS = SUBLANES
PLAN_WORDS = 2 * PLAN_ROWS * LANES

U32 = jnp.uint32
HI_MASK = 0xFFFF0000

F32 = jnp.float32
BF16 = jnp.bfloat16
NEG_INF = float("-inf")


def _layer_norm(z, g, b):
    mu = jnp.mean(z, axis=-1, keepdims=True)
    zc = z - mu
    var = jnp.mean(zc * zc, axis=-1, keepdims=True)
    return zc * lax.rsqrt(var + LN_EPS) * g + b


def _params(*sem):
    return pltpu.CompilerParams(dimension_semantics=sem, vmem_limit_bytes=VMEM_LIMIT)


def _router_plumbing(router_w, router_bias, t, d, tt, tg, tile_index):
    groups = tt // tg
    const = lambda *ids: (0, 0)
    by_cols = lambda *ids: (0, tile_index(*ids))
    by_rows = lambda *ids: (tile_index(*ids), 0)
    operands = [router_w.T, router_bias.reshape(N_EXPERTS, 1)]
    in_specs = [pl.BlockSpec((N_EXPERTS, d), const), pl.BlockSpec((N_EXPERTS, 1), const)]
    out_specs = [pl.BlockSpec((TOP_K, tt), by_cols), pl.BlockSpec((TOP_K, tt), by_cols),
                 pl.BlockSpec((groups * N_EXPERTS, LANES), by_rows),
                 pl.BlockSpec((groups * PLAN_ROWS, LANES), by_rows)]
    out_shapes = [jax.ShapeDtypeStruct((TOP_K, t), jnp.int32), jax.ShapeDtypeStruct((TOP_K, t), F32),
                  jax.ShapeDtypeStruct((t // tg * N_EXPERTS, LANES), jnp.int32),
                  jax.ShapeDtypeStruct((t // tg * PLAN_ROWS, LANES), jnp.int32)]
    return operands, in_specs, out_specs, out_shapes


def _pool_kernel(x_ref, halo_ref, w_ref, scale_ref, g_ref, b_ref, wt_ref, rb_ref, o_ref, pos_ref, gate_ref,
                 n8_ref, plan_ref, buf_ref, *, alpha, tt, tg):
    i = pl.program_id(1)
    x = x_ref[...]
    d = x.shape[1]
    ch = d // len(POOL_WINDOWS)
    top = SUBLANES
    first = top + POOL_HALO
    n = first + tt
    for k in range(3):
        buf_ref[k, 0:top, :] = jnp.zeros((top, d), F32)
    buf_ref[0, top:first, :] = jnp.where(i > 0, halo_ref[...], 0.0)
    buf_ref[0, first:n, :] = x
    pos = i * tt + lax.broadcasted_iota(jnp.int32, (tt, 1), 0)
    ys = []
    for g, w in enumerate(POOL_WINDOWS):
        c0 = g * ch
        xg = x[:, c0:c0 + ch]
        src, span = 0, 1
        while span < w:
            dst = 1 if src != 1 else 2
            buf_ref[dst, top:n, c0:c0 + ch] = (buf_ref[src, top:n, c0:c0 + ch]
                                               + buf_ref[src, top - span:n - span, c0:c0 + ch])
            src, span = dst, 2 * span
        s = buf_ref[src, first:n, c0:c0 + ch]
        inv_cnt = 1.0 / jnp.minimum(pos + 1, w).astype(F32)
        pooled = s * inv_cnt - xg
        ys.append(jnp.dot(pooled.astype(BF16), w_ref[g], preferred_element_type=F32))
    y = jnp.concatenate(ys, axis=1) * scale_ref[...]
    o_ref[...] = _layer_norm(alpha * x + y, g_ref[...], b_ref[...])
    _route_tile(o_ref, (wt_ref, rb_ref), (pos_ref, gate_ref, n8_ref, plan_ref), tg)


def _pool_mixer(x2, seq, w_grp, scale, g, b, alpha, router_w, router_bias, tg):
    t, d = x2.shape
    tt = min(POOL_TT, seq)
    nt = seq // tt
    nb = t // seq
    row = lambda v: v.reshape(1, d)
    const2 = lambda bi, i: (0, 0)
    tile = lambda bi, i: bi * nt + i
    r_ops, r_in, r_out, r_shapes = _router_plumbing(router_w, router_bias, t, d, tt, tg, tile)
    return pl.pallas_call(
        functools.partial(_pool_kernel, alpha=alpha, tt=tt, tg=tg),
        grid=(nb, nt),
        in_specs=[
            pl.BlockSpec((tt, d), lambda bi, i: (bi * nt + i, 0)),
            pl.BlockSpec((POOL_HALO, d),
                         lambda bi, i: (jnp.maximum((bi * seq + i * tt) // POOL_HALO - 1, 0), 0)),
            pl.BlockSpec(w_grp.shape, lambda bi, i: (0, 0, 0)),
            pl.BlockSpec((1, d), const2),
            pl.BlockSpec((1, d), const2),
            pl.BlockSpec((1, d), const2),
        ] + r_in,
        out_specs=[pl.BlockSpec((tt, d), lambda bi, i: (bi * nt + i, 0))] + r_out,
        out_shape=[jax.ShapeDtypeStruct((t, d), F32)] + r_shapes,
        scratch_shapes=[pltpu.VMEM((3, SUBLANES + POOL_HALO + tt, d), F32)],
        compiler_params=_params("parallel", "parallel"),
        name="pool_mixer",
    )(x2, x2, w_grp.astype(BF16), row(scale), row(g), row(b), *r_ops)


def _qkv_kernel(x_ref, w_ref, b_ref, c_ref, s_ref, q_ref, k_ref, v_ref, *, nq, nk):
    y = jnp.dot(x_ref[...].astype(BF16), w_ref[...], preferred_element_type=F32) + b_ref[...]
    c = c_ref[...]
    s = s_ref[...]
    lane = lax.broadcasted_iota(jnp.int32, c.shape, 1)
    first_half = (lane % HEAD_DIM) < (HEAD_DIM // 2)

    def rope(v):
        partner = jnp.where(first_half, pltpu.roll(v, LANES - HEAD_DIM // 2, 1),
                            pltpu.roll(v, HEAD_DIM // 2, 1))
        return v * c + partner * s

    for j in range(nq // LANES):
        q_ref[:, j * LANES:(j + 1) * LANES] = (
            rope(y[:, j * LANES:(j + 1) * LANES]) * (HEAD_DIM ** -0.5)).astype(BF16)
    for j in range(nk // LANES):
        k_ref[:, j * LANES:(j + 1) * LANES] = rope(
            y[:, nq + j * LANES:nq + (j + 1) * LANES]).astype(BF16)
    v_ref[...] = y[:, nq + nk:].astype(BF16)


def _attn_kernel(sink_ref, q_ref, k_ref, kh_ref, v_ref, vh_ref, x_ref, wo_ref, bo_ref, g_ref, b_ref,
                 wt_ref, rb_ref, o_ref, pos_ref, gate_ref, n8_ref, plan_ref, *, alpha, tq, tg):
    i = pl.program_id(1)
    nqb = tq // WINDOW
    r = lax.broadcasted_iota(jnp.int32, (tq, 2 * WINDOW), 0)
    c = lax.broadcasted_iota(jnp.int32, (tq, 2 * WINDOW), 1)
    dist = (r & (WINDOW - 1)) + WINDOW - c
    valid = (dist >= 0) & (dist < WINDOW) & ((c >= WINDOW) | (r >= WINDOW) | (i > 0))
    lane = lax.broadcasted_iota(jnp.int32, (tq, LANES), 1)
    low = lane < HEAD_DIM
    dn = (((1,), (1,)), ((), ()))
    outs = []
    for kv in range(N_KV_HEADS):
        sl = slice(kv * LANES, (kv + 1) * LANES)
        kk = jnp.concatenate([kh_ref[:, sl], k_ref[:, sl]], axis=0)
        vv = jnp.concatenate([vh_ref[:, sl], v_ref[:, sl]], axis=0)
        for p in range(GQA_GROUP // 2):
            c0 = (kv * GQA_GROUP + 2 * p) * HEAD_DIM
            qp = q_ref[:, c0:c0 + LANES]
            res = []
            for half in range(2):
                keep = low if half == 0 else jnp.logical_not(low)
                qh = jnp.where(keep, qp, jnp.zeros_like(qp))
                sc_all = lax.dot_general(qh, kk, dn, preferred_element_type=F32)
                sink = sink_ref[kv * GQA_GROUP + 2 * p + half]
                sc = jnp.concatenate([sc_all[qb * WINDOW:(qb + 1) * WINDOW, qb * WINDOW:(qb + 2) * WINDOW]
                                      for qb in range(nqb)], axis=0)
                sc = jnp.where(valid, sc, NEG_INF)
                m = jnp.maximum(jnp.max(sc, axis=1, keepdims=True), sink)
                pm = jnp.exp(sc - m)
                rden = 1.0 / (jnp.sum(pm, axis=1, keepdims=True) + jnp.exp(sink - m))
                pm = pm.astype(BF16)
                res.append(jnp.concatenate(
                    [jnp.dot(pm[qb * WINDOW:(qb + 1) * WINDOW, :], vv[qb * WINDOW:(qb + 2) * WINDOW, :],
                             preferred_element_type=F32) for qb in range(nqb)], axis=0) * rden)
            outs.append(jnp.where(low, res[0], res[1]))
    o_all = jnp.concatenate(outs, axis=1).astype(BF16)
    mix = jnp.dot(o_all, wo_ref[...], preferred_element_type=F32) + bo_ref[...]
    o_ref[...] = _layer_norm(alpha * x_ref[...] + mix, g_ref[...], b_ref[...])
    _route_tile(o_ref, (wt_ref, rb_ref), (pos_ref, gate_ref, n8_ref, plan_ref), tg)


def _dup_heads(w, n_heads):
    lead = w.shape[:-1]
    w = w.reshape(lead + (n_heads, 1, HEAD_DIM))
    w = jnp.broadcast_to(w, lead + (n_heads, 2, HEAD_DIM))
    return w.reshape(lead + (n_heads * 2 * HEAD_DIM,))


def _swa_mixer(x2, seq, wqkv, bqkv, sinks, wo, bo, g, b, alpha, router_w, router_bias, tg):
    t, d = x2.shape
    nb = t // seq
    nq = N_Q_HEADS * HEAD_DIM
    nkv = N_KV_HEADS * HEAD_DIM
    nk2 = 2 * nkv
    w_ext = jnp.concatenate([wqkv[:, :nq], _dup_heads(wqkv[:, nq:nq + nkv], N_KV_HEADS),
                             _dup_heads(wqkv[:, nq + nkv:], N_KV_HEADS)], axis=1).astype(BF16)
    b_ext = jnp.concatenate([bqkv[:nq], _dup_heads(bqkv[nq:nq + nkv], N_KV_HEADS),
                             _dup_heads(bqkv[nq + nkv:], N_KV_HEADS)]).reshape(1, -1)
    n_ext = nq + 2 * nk2

    pos = jnp.arange(seq, dtype=F32)
    inv_freq = ROPE_THETA ** (-jnp.arange(0, HEAD_DIM, 2, dtype=F32) / HEAD_DIM)
    ang = pos[:, None] * inv_freq[None, :]
    cos, sin = jnp.cos(ang), jnp.sin(ang)
    cos_t = jnp.concatenate([cos, cos, cos, cos], axis=1)
    sin_t = jnp.concatenate([-sin, sin, -sin, sin], axis=1)

    tt = min(QKV_TT, seq)
    nt = seq // tt
    const2 = lambda i: (0, 0)
    q, k2, v2 = pl.pallas_call(
        functools.partial(_qkv_kernel, nq=nq, nk=nk2),
        grid=(t // tt,),
        in_specs=[
            pl.BlockSpec((tt, d), lambda i: (i, 0)),
            pl.BlockSpec((d, n_ext), const2),
            pl.BlockSpec((1, n_ext), const2),
            pl.BlockSpec((tt, LANES), lambda i: (i % nt, 0)),
            pl.BlockSpec((tt, LANES), lambda i: (i % nt, 0)),
        ],
        out_specs=[pl.BlockSpec((tt, nq), lambda i: (i, 0)),
                   pl.BlockSpec((tt, nk2), lambda i: (i, 0)),
                   pl.BlockSpec((tt, nk2), lambda i: (i, 0))],
        out_shape=[jax.ShapeDtypeStruct((t, nq), BF16),
                   jax.ShapeDtypeStruct((t, nk2), BF16),
                   jax.ShapeDtypeStruct((t, nk2), BF16)],
        compiler_params=_params("parallel"),
        name="swa_qkv",
    )(x2, w_ext, b_ext, cos_t, sin_t)

    tq = min(ATTN_TQ, seq)
    ntq = seq // tq
    main = lambda bi, i: (bi * ntq + i, 0)
    halo = lambda bi, i: (jnp.maximum((bi * seq + i * tq) // WINDOW - 1, 0), 0)
    const = lambda bi, i: (0, 0)
    row = lambda v: v.reshape(1, d)
    r_ops, r_in, r_out, r_shapes = _router_plumbing(router_w, router_bias, t, d, tq, tg,
                                                    lambda bi, i: bi * ntq + i)
    return pl.pallas_call(
        functools.partial(_attn_kernel, alpha=alpha, tq=tq, tg=tg),
        grid=(nb, ntq),
        in_specs=[
            pl.BlockSpec(memory_space=pltpu.SMEM),
            pl.BlockSpec((tq, nq), main),
            pl.BlockSpec((tq, nk2), main),
            pl.BlockSpec((WINDOW, nk2), halo),
            pl.BlockSpec((tq, nk2), main),
            pl.BlockSpec((WINDOW, nk2), halo),
            pl.BlockSpec((tq, d), main),
            pl.BlockSpec((nq, d), const),
            pl.BlockSpec((1, d), const),
            pl.BlockSpec((1, d), const),
            pl.BlockSpec((1, d), const),
        ] + r_in,
        out_specs=[pl.BlockSpec((tq, d), main)] + r_out,
        out_shape=[jax.ShapeDtypeStruct((t, d), F32)] + r_shapes,
        compiler_params=_params("parallel", "parallel"),
        name="swa_attention",
    )(sinks, q, k2, k2, v2, v2, x2, wo.astype(BF16), row(bo), row(g), row(b), *r_ops)


def _conv_kernel(x_ref, win_ref, cw_ref, wout_ref, g_ref, b_ref, wt_ref, rb_ref, o_ref, pos_ref, gate_ref,
                 n8_ref, plan_ref, ubuf_ref, *, alpha, tt, tg):
    i = pl.program_id(1)
    d = x_ref.shape[1]

    @pl.when(i == 0)
    def _():
        ubuf_ref[0:SUBLANES, :] = jnp.zeros((SUBLANES, d), F32)

    x = x_ref[...]
    proj = jnp.dot(x.astype(BF16), win_ref[...], preferred_element_type=F32)
    gate_b = proj[:, :d]
    u = proj[:, d:2 * d] * proj[:, 2 * d:]
    ubuf_ref[SUBLANES:SUBLANES + tt, :] = u
    cw = cw_ref[...]
    conv = (cw[0:1, :] * ubuf_ref[SUBLANES - 2:SUBLANES - 2 + tt, :]
            + cw[1:2, :] * ubuf_ref[SUBLANES - 1:SUBLANES - 1 + tt, :]
            + cw[2:3, :] * u)
    ubuf_ref[0:SUBLANES, :] = u[tt - SUBLANES:tt, :]
    y = jnp.dot((gate_b * conv).astype(BF16), wout_ref[...], preferred_element_type=F32)
    o_ref[...] = _layer_norm(alpha * x + y, g_ref[...], b_ref[...])
    _route_tile(o_ref, (wt_ref, rb_ref), (pos_ref, gate_ref, n8_ref, plan_ref), tg)


def _conv_mixer(x2, seq, w_in, conv_w, w_out, g, b, alpha, router_w, router_bias, tg):
    t, d = x2.shape
    nb = t // seq
    tt = min(CONV_TT, seq)
    nt = seq // tt
    const = lambda bi, i: (0, 0)
    row = lambda v: v.reshape(1, d)
    r_ops, r_in, r_out, r_shapes = _router_plumbing(router_w, router_bias, t, d, tt, tg,
                                                    lambda bi, i: bi * nt + i)
    return pl.pallas_call(
        functools.partial(_conv_kernel, alpha=alpha, tt=tt, tg=tg),
        grid=(nb, nt),
        in_specs=[
            pl.BlockSpec((tt, d), lambda bi, i: (bi * nt + i, 0)),
            pl.BlockSpec((d, 3 * d), const),
            pl.BlockSpec(conv_w.shape, const),
            pl.BlockSpec((d, d), const),
            pl.BlockSpec((1, d), const),
            pl.BlockSpec((1, d), const),
        ] + r_in,
        out_specs=[pl.BlockSpec((tt, d), lambda bi, i: (bi * nt + i, 0))] + r_out,
        out_shape=[jax.ShapeDtypeStruct((t, d), F32)] + r_shapes,
        scratch_shapes=[pltpu.VMEM((SUBLANES + tt, d), F32)],
        compiler_params=_params("arbitrary", "arbitrary"),
        name="conv_mixer",
    )(x2, w_in.astype(BF16), conv_w, w_out.astype(BF16), row(g), row(b), *r_ops)


def _first_argmax(v, rows, n):
    m = jnp.max(v, axis=0, keepdims=True)
    first = jnp.min(jnp.where(v == m, rows, n), axis=0, keepdims=True)
    return m, first


def _route_tile(o_ref, router_refs, route_out_refs, tg):
    wt_ref, bias_ref = router_refs
    pos_ref, gate_ref, n8_ref, plan_ref = route_out_refs
    tt = o_ref.shape[0]
    x = o_ref[...]
    xh = x.astype(BF16)
    xl = (x - xh.astype(F32)).astype(BF16)
    wt = wt_ref[...]
    wh = wt.astype(BF16)
    wl = (wt - wh.astype(F32)).astype(BF16)
    dn = (((1,), (1,)), ((), ()))
    logits = (lax.dot_general(wh, xh, dn, preferred_element_type=F32)
              + (lax.dot_general(wh, xl, dn, preferred_element_type=F32)
                 + lax.dot_general(wl, xh, dn, preferred_element_type=F32)))
    scores = jax.nn.sigmoid(logits)
    sel = scores + bias_ref[...]

    rows8 = lax.broadcasted_iota(jnp.int32, (GROUP_SIZE, tt), 0)
    gsc = []
    for gi in range(N_EXPERT_GROUPS):
        blk = sel[gi * GROUP_SIZE:(gi + 1) * GROUP_SIZE, :]
        m1, f1 = _first_argmax(blk, rows8, GROUP_SIZE)
        m2 = jnp.max(jnp.where(rows8 == f1, NEG_INF, blk), axis=0, keepdims=True)
        gsc.append(m1 + m2)
    cur = jnp.concatenate(gsc, axis=0)
    rows_g = lax.broadcasted_iota(jnp.int32, (N_EXPERT_GROUPS, tt), 0)
    gmask = jnp.zeros((N_EXPERT_GROUPS, tt), F32)
    for _ in range(TOPK_GROUPS):
        _, f = _first_argmax(cur, rows_g, N_EXPERT_GROUPS)
        hit = rows_g == f
        gmask = jnp.where(hit, 1.0, gmask)
        cur = jnp.where(hit, NEG_INF, cur)
    emask = jnp.concatenate(
        [jnp.broadcast_to(gmask[gi:gi + 1, :], (GROUP_SIZE, tt)) for gi in range(N_EXPERT_GROUPS)],
        axis=0)

    rows_e = lax.broadcasted_iota(jnp.int32, (N_EXPERTS, tt), 0)
    cur = jnp.where(emask > 0.0, sel, NEG_INF)
    idxs, gates = [], []
    member = jnp.zeros((N_EXPERTS, tt), F32)
    for _ in range(TOP_K):
        _, f = _first_argmax(cur, rows_e, N_EXPERTS)
        hit = rows_e == f
        idxs.append(f)
        gates.append(jnp.sum(jnp.where(hit, scores, 0.0), axis=0, keepdims=True))
        member = jnp.where(hit, 1.0, member)
        cur = jnp.where(hit, NEG_INF, cur)
    gate = jnp.concatenate(gates, axis=0)
    gate_ref[...] = gate / jnp.sum(gate, axis=0, keepdims=True) * ROUTED_SCALE

    for s in range(tt // tg):
        cols = slice(s * tg, (s + 1) * tg)
        _plan_group(member[:, cols], [f[:, cols] for f in idxs], pos_ref.at[:, pl.ds(s * tg, tg)],
                    n8_ref.at[pl.ds(s * N_EXPERTS, N_EXPERTS), :],
                    plan_ref.at[pl.ds(s * PLAN_ROWS, PLAN_ROWS), :], tg)


def _plan_group(member, idxs, pos_ref, n8_ref, plan_ref, tt):
    rows_e = lax.broadcasted_iota(jnp.int32, (N_EXPERTS, tt), 0)
    ta = lax.broadcasted_iota(jnp.int32, (tt, tt), 0)
    tb = lax.broadcasted_iota(jnp.int32, (tt, tt), 1)
    earlier = (ta < tb).astype(BF16)
    before = jnp.dot(member.astype(BF16), earlier, preferred_element_type=F32)
    n = jnp.sum(member, axis=1, keepdims=True)
    n8 = jnp.floor((n + (CHUNK_ALIGN - 1)) * (1.0 / CHUNK_ALIGN)) * CHUNK_ALIGN
    n8_wide = jnp.broadcast_to(n8, (N_EXPERTS, LANES))
    ea = lax.broadcasted_iota(jnp.int32, (N_EXPERTS, N_EXPERTS), 0)
    eb = lax.broadcasted_iota(jnp.int32, (N_EXPERTS, N_EXPERTS), 1)
    lower = (eb < ea).astype(BF16)
    chunk_start = jnp.dot(lower, n8_wide.astype(BF16), preferred_element_type=F32)
    where_to = before + chunk_start[:, 0:1]
    for k in range(TOP_K):
        pk = jnp.sum(jnp.where(rows_e == idxs[k], where_to, 0.0), axis=0, keepdims=True)
        pos_ref[k:k + 1, :] = pk.astype(jnp.int32)
    n8_int = n8_wide.astype(jnp.int32)
    n8_ref[...] = n8_int

    lane = lax.broadcasted_iota(jnp.int32, (N_EXPERTS, LANES), 1)
    flag = jnp.zeros((N_EXPERTS, LANES), F32)
    for j in range(CHUNK_BITS):
        flag = jnp.where((lane == j) & ((n8_int & (CHUNK_ALIGN << j)) != 0), 1.0, flag)
    rank = jnp.dot(lower, flag.astype(BF16), preferred_element_type=F32)
    e_id = lax.broadcasted_iota(jnp.int32, (N_EXPERTS, LANES), 0)
    for j in range(CHUNK_BITS):
        chosen = (rank[:, j:j + 1] == lane.astype(F32)) & (flag[:, j:j + 1] > 0.0)
        plan_ref[j:j + 1, :] = jnp.sum(jnp.where(chosen, e_id, 0), axis=0, keepdims=True)
    plan_ref[CHUNK_BITS:CHUNK_BITS + 1, :] = jnp.sum(flag, axis=0, keepdims=True).astype(jnp.int32)
    plan_ref[CHUNK_BITS + 1:CHUNK_BITS + 2, :] = jnp.sum(n8_int, axis=0, keepdims=True)


def _pack_halves(v, exact=False):
    h = v.shape[1] // 2
    return _pack_pair(v[:, :h], v[:, h:], exact)


def _pack_pair(lo, hi, exact=False):
    if not exact:
        lo, hi = lo.astype(BF16).astype(F32), hi.astype(BF16).astype(F32)
    lo = lax.bitcast_convert_type(lo, U32)
    hi = lax.bitcast_convert_type(hi, U32)
    return (lo >> 16) | (hi & jnp.uint32(HI_MASK))


def _chunk_relative(pos, c):
    rel = pos - c * SORT_CHUNK
    inside = (rel >= 0) & (rel < SORT_CHUNK)
    return jnp.where(inside, rel, -1).astype(F32).astype(BF16)


def _unpack_halves(w):
    lo = lax.bitcast_convert_type(w << 16, F32).astype(BF16)
    hi = lax.bitcast_convert_type(w & jnp.uint32(HI_MASK), F32).astype(BF16)
    return lo, hi


def _start_group_copies(plan_ref, make_copy):
    for j in range(CHUNK_BITS):
        size = CHUNK_ALIGN << j

        def body(k, carry):
            make_copy(pl.multiple_of(plan_ref[j * LANES + k], CHUNK_ALIGN),
                      pl.multiple_of(plan_ref[(PLAN_ROWS + j) * LANES + k], CHUNK_ALIGN), size).start()
            return carry

        lax.fori_loop(0, plan_ref[CHUNK_BITS * LANES + j], body, 0)


def _wait_group_rows(total_rows, ns, make_copy):
    for j in range(ns.bit_length()):
        size = CHUNK_ALIGN << j
        if size > ns:
            break

        @pl.when((total_rows & size) != 0)
        def _():
            make_copy(0, 0, size).wait()


def _dispatch_kernel(tot_ref, zrow_ref, plan_ref, pos_ref, x_ref, xs_ref, sbuf_ref, zbuf_ref,
                     onehot_ref, sem, zsem, *, tg, ns, bm):
    g = pl.program_id(0)
    ng = pl.num_programs(0)
    slot = g % 2

    @pl.when(g == 0)
    def _():
        zbuf_ref[...] = jnp.zeros_like(zbuf_ref)
        row = pl.multiple_of(zrow_ref[0], CHUNK_ALIGN)
        tail = pltpu.make_async_copy(zbuf_ref, xs_ref.at[pl.ds(row, bm), :], zsem)
        tail.start()
        tail.wait()

    def copy_from(sl):
        def make_copy(src, dst, size):
            return pltpu.make_async_copy(sbuf_ref.at[sl, pl.ds(src, size), :],
                                         xs_ref.at[pl.ds(dst, size), :], sem.at[sl])
        return make_copy

    @pl.when(g >= 2)
    def _():
        _wait_group_rows(tot_ref[g - 2], ns, copy_from(slot))

    xb = x_ref[...].astype(BF16)
    pos = pos_ref[...]
    local_rows = lax.broadcasted_iota(jnp.int32, (SORT_CHUNK, tg), 0).astype(F32).astype(BF16)
    one = jnp.ones((SORT_CHUNK, tg), BF16)
    zero = jnp.zeros((SORT_CHUNK, tg), BF16)
    for c in range(ns // SORT_CHUNK):
        rel = _chunk_relative(pos, c)
        hit = local_rows == rel[0:1, :]
        for k in range(1, TOP_K):
            hit = jnp.logical_or(hit, local_rows == rel[k:k + 1, :])
        onehot_ref[c * SORT_CHUNK:(c + 1) * SORT_CHUNK, :] = jnp.where(hit, one, zero)
    onehot = onehot_ref[...]
    half = x_ref.shape[1] // 2
    for n in range(half // SORT_CHUNK):
        lo = jnp.dot(onehot, xb[:, n * SORT_CHUNK:(n + 1) * SORT_CHUNK], preferred_element_type=F32)
        hi = jnp.dot(onehot, xb[:, half + n * SORT_CHUNK:half + (n + 1) * SORT_CHUNK],
                     preferred_element_type=F32)
        sbuf_ref[slot, :, n * SORT_CHUNK:(n + 1) * SORT_CHUNK] = _pack_pair(lo, hi, exact=True)

    _start_group_copies(plan_ref, copy_from(slot))

    @pl.when(g == ng - 1)
    def _():
        @pl.when(g >= 1)
        def _():
            _wait_group_rows(tot_ref[g - 1], ns, copy_from(1 - slot))
        _wait_group_rows(tot_ref[g], ns, copy_from(slot))


def _dispatch(x1, pos, plan, tot, zrow, n_rows, tg, ns, bm):
    t, d = x1.shape
    grid_spec = pltpu.PrefetchScalarGridSpec(
        num_scalar_prefetch=2,
        grid=(t // tg,),
        in_specs=[
            pl.BlockSpec((PLAN_WORDS,), lambda i, *_: (i,), memory_space=pltpu.SMEM),
            pl.BlockSpec((TOP_K, tg), lambda i, *_: (0, i)),
            pl.BlockSpec((tg, d), lambda i, *_: (i, 0)),
        ],
        out_specs=pl.BlockSpec(memory_space=pl.ANY),
        scratch_shapes=[pltpu.VMEM((2, ns, d // 2), U32), pltpu.VMEM((bm, d // 2), U32),
                        pltpu.VMEM((ns, tg), BF16),
                        pltpu.SemaphoreType.DMA((2,)), pltpu.SemaphoreType.DMA],
    )
    return pl.pallas_call(
        functools.partial(_dispatch_kernel, tg=tg, ns=ns, bm=bm),
        grid_spec=grid_spec,
        out_shape=jax.ShapeDtypeStruct((n_rows, d // 2), U32),
        compiler_params=_params("arbitrary"),
        name="moe_dispatch",
    )(tot, zrow, plan, pos, x1)


def _gmm_kernel(nq_ref, brow_ref, bval_ref, bord_ref, bfirst_ref, aexp_ref,
                xs_ref, w1_ref, w3_ref, w2_ref, y_ref,
                xbuf_ref, ybuf_ref, w1s_ref, w3s_ref, w2s_ref, w13_ref, w2b_ref, xsem, ysem, wsem,
                *, layer, bm):
    nq = nq_ref[0]
    na = nq_ref[1]
    f = w1s_ref.shape[2]
    half = xbuf_ref.shape[2]
    nx = xbuf_ref.shape[0]

    def x_copy(q, slot):
        row = pl.multiple_of(brow_ref[q], CHUNK_ALIGN)
        return pltpu.make_async_copy(xs_ref.at[pl.ds(row, bm), :], xbuf_ref.at[slot], xsem.at[slot])

    def w_copies(j, slot):
        e = aexp_ref[j]
        return (pltpu.make_async_copy(w1_ref.at[layer, e], w1s_ref.at[slot], wsem.at[slot]),
                pltpu.make_async_copy(w3_ref.at[layer, e], w3s_ref.at[slot], wsem.at[slot]),
                pltpu.make_async_copy(w2_ref.at[layer, e], w2s_ref.at[slot], wsem.at[slot]))

    def y_copy(slot, row, off, size):
        return pltpu.make_async_copy(ybuf_ref.at[slot, pl.ds(off, size), :],
                                     y_ref.at[pl.ds(pl.multiple_of(row + off, CHUNK_ALIGN), size), :],
                                     ysem.at[slot])

    def for_each_y_copy(slot, row, valid, fn):
        @pl.when(valid == bm)
        def _():
            fn(y_copy(slot, row, 0, bm))

        @pl.when(valid < bm)
        def _():
            for j in range((bm // CHUNK_ALIGN).bit_length() - 1):
                size = CHUNK_ALIGN << j

                @pl.when((valid & size) != 0)
                def _():
                    fn(y_copy(slot, row, pl.multiple_of(valid & (size - 1), CHUNK_ALIGN), size))

    for c in w_copies(0, 0):
        c.start()
    for q0 in range(nx - 1):
        @pl.when(q0 < nq)
        def _():
            x_copy(q0, q0).start()

    def body(q, carry):
        xslot = q % nx
        yslot = q % 2

        @pl.when(q + (nx - 1) < nq)
        def _():
            x_copy(q + (nx - 1), (q + (nx - 1)) % nx).start()

        @pl.when(bfirst_ref[q] == 1)
        def _():
            j = bord_ref[q]
            ws = j % 2
            for c in w_copies(j, ws):
                c.wait()

            @pl.when(j + 1 < na)
            def _():
                for c in w_copies(j + 1, 1 - ws):
                    c.start()
            w13_ref[:, :f] = w1s_ref[ws].astype(BF16)
            w13_ref[:, f:] = w3s_ref[ws].astype(BF16)
            w2b_ref[...] = w2s_ref[ws].astype(BF16)

        x_copy(q, xslot).wait()

        @pl.when(q >= 2)
        def _():
            for_each_y_copy(yslot, 0, bval_ref[jnp.maximum(q - 2, 0)], lambda c: c.wait())

        def swiglu(rows):
            x_lo, x_hi = _unpack_halves(xbuf_ref[xslot, 0:rows, :])
            h13 = (jnp.dot(x_lo, w13_ref[:half, :], preferred_element_type=F32)
                   + jnp.dot(x_hi, w13_ref[half:, :], preferred_element_type=F32))
            h = jax.nn.silu(h13[:, :f]) * h13[:, f:]
            ybuf_ref[yslot, 0:rows, :] = _pack_halves(
                jnp.dot(h.astype(BF16), w2b_ref[...], preferred_element_type=F32))

        @pl.when(bval_ref[q] > bm // 2)
        def _():
            swiglu(bm)

        @pl.when(bval_ref[q] <= bm // 2)
        def _():
            swiglu(bm // 2)
        for_each_y_copy(yslot, brow_ref[q], bval_ref[q], lambda c: c.start())
        return carry

    lax.fori_loop(0, nq, body, 0)

    @pl.when(nq >= 2)
    def _():
        for_each_y_copy(nq % 2, 0, bval_ref[jnp.maximum(nq - 2, 0)], lambda c: c.wait())
    for_each_y_copy((nq - 1) % 2, 0, bval_ref[nq - 1], lambda c: c.wait())


def _gmm(xs, tables, w1, w3, w2, layer, bm):
    n_rows, half = xs.shape
    d = 2 * half
    f = w1.shape[-1]
    any_spec = pl.BlockSpec(memory_space=pl.ANY)
    grid_spec = pltpu.PrefetchScalarGridSpec(
        num_scalar_prefetch=len(tables),
        grid=(1,),
        in_specs=[any_spec, any_spec, any_spec, any_spec],
        out_specs=any_spec,
        scratch_shapes=[pltpu.VMEM((GMM_X_SLOTS, bm, half), U32), pltpu.VMEM((2, bm, half), U32),
                        pltpu.VMEM((2, d, f), F32), pltpu.VMEM((2, d, f), F32), pltpu.VMEM((2, f, d), F32),
                        pltpu.VMEM((d, 2 * f), BF16), pltpu.VMEM((f, d), BF16),
                        pltpu.SemaphoreType.DMA((GMM_X_SLOTS,)), pltpu.SemaphoreType.DMA((2,)),
                        pltpu.SemaphoreType.DMA((2,))],
    )
    return pl.pallas_call(
        functools.partial(_gmm_kernel, layer=layer, bm=bm),
        grid_spec=grid_spec,
        out_shape=jax.ShapeDtypeStruct((n_rows, half), U32),
        compiler_params=_params("arbitrary"),
        name="moe_experts",
    )(*tables, xs, w1, w3, w2)


def _combine_kernel(tot_ref, plan_ref, plan_next_ref, yb_ref, pos_ref, gate_ref, x_ref, ws13_ref, ws2_ref,
                    g_ref, b_ref, o_ref, ybuf_ref, wgt_ref, ylo_ref, yhi_ref, sem, *, alpha, tg, ns):
    g = pl.program_id(0)
    ng = pl.num_programs(0)
    slot = g % 2

    def copy_into(sl):
        def make_copy(buf_row, hbm_row, size):
            return pltpu.make_async_copy(yb_ref.at[pl.ds(hbm_row, size), :],
                                         ybuf_ref.at[sl, pl.ds(buf_row, size), :], sem.at[sl])
        return make_copy

    @pl.when(g == 0)
    def _():
        ybuf_ref[...] = jnp.zeros_like(ybuf_ref)
        _start_group_copies(plan_ref, copy_into(0))

    @pl.when(g + 1 < ng)
    def _():
        _start_group_copies(plan_next_ref, copy_into(1 - slot))

    x = x_ref[...]
    f = ws2_ref.shape[0]
    h13 = jnp.dot(x.astype(BF16), ws13_ref[...], preferred_element_type=F32)
    hs = jax.nn.silu(h13[:, :f]) * h13[:, f:]
    shared = jnp.dot(hs.astype(BF16), ws2_ref[...], preferred_element_type=F32)

    _wait_group_rows(tot_ref[g], ns, copy_into(slot))

    pos = pos_ref[...]
    gate_b = gate_ref[...].astype(BF16)
    local_rows = lax.broadcasted_iota(jnp.int32, (SORT_CHUNK, tg), 0).astype(F32).astype(BF16)
    for c in range(ns // SORT_CHUNK):
        rel = _chunk_relative(pos, c)
        wgt = jnp.zeros((SORT_CHUNK, tg), BF16)
        for k in range(TOP_K):
            wgt = jnp.where(local_rows == rel[k:k + 1, :], gate_b[k:k + 1, :], wgt)
        wgt_ref[c * SORT_CHUNK:(c + 1) * SORT_CHUNK, :] = wgt
        y_lo, y_hi = _unpack_halves(ybuf_ref[slot, c * SORT_CHUNK:(c + 1) * SORT_CHUNK, :])
        ylo_ref[c * SORT_CHUNK:(c + 1) * SORT_CHUNK, :] = y_lo
        yhi_ref[c * SORT_CHUNK:(c + 1) * SORT_CHUNK, :] = y_hi
    wgt_all = wgt_ref[...]
    dn = (((0,), (0,)), ((), ()))
    routed = jnp.concatenate([lax.dot_general(wgt_all, ylo_ref[...], dn, preferred_element_type=F32),
                              lax.dot_general(wgt_all, yhi_ref[...], dn, preferred_element_type=F32)],
                             axis=1)
    o_ref[...] = _layer_norm(alpha * x + (routed + shared), g_ref[...], b_ref[...])


def _combine(yb, pos_kt, gate_kt, plan, tot, x1, ws1, ws3, ws2, g, b, alpha, tg, ns):
    t, d = x1.shape
    ng = t // tg
    f = ws1.shape[1]
    ws13 = jnp.concatenate([ws1, ws3], axis=1).astype(BF16)
    const = lambda i, *_: (0, 0)
    tile = lambda i, *_: (i, 0)
    row = lambda v: v.reshape(1, d)
    grid_spec = pltpu.PrefetchScalarGridSpec(
        num_scalar_prefetch=1,
        grid=(ng,),
        in_specs=[
            pl.BlockSpec((PLAN_WORDS,), lambda i, *_: (i,), memory_space=pltpu.SMEM),
            pl.BlockSpec((PLAN_WORDS,), lambda i, *_: (jnp.minimum(i + 1, ng - 1),),
                         memory_space=pltpu.SMEM),
            pl.BlockSpec(memory_space=pl.ANY),
            pl.BlockSpec((TOP_K, tg), lambda i, *_: (0, i)),
            pl.BlockSpec((TOP_K, tg), lambda i, *_: (0, i)),
            pl.BlockSpec((tg, d), tile),
            pl.BlockSpec((d, 2 * f), const),
            pl.BlockSpec((f, d), const),
            pl.BlockSpec((1, d), const),
            pl.BlockSpec((1, d), const),
        ],
        out_specs=pl.BlockSpec((tg, d), tile),
        scratch_shapes=[pltpu.VMEM((2, ns, d // 2), U32), pltpu.VMEM((ns, tg), BF16),
                        pltpu.VMEM((ns, d // 2), BF16), pltpu.VMEM((ns, d // 2), BF16),
                        pltpu.SemaphoreType.DMA((2,))],
    )
    return pl.pallas_call(
        functools.partial(_combine_kernel, alpha=alpha, tg=tg, ns=ns),
        grid_spec=grid_spec,
        out_shape=jax.ShapeDtypeStruct((t, d), F32),
        compiler_params=_params("arbitrary"),
        name="moe_combine",
    )(tot, plan, plan, yb, pos_kt, gate_kt, x1, ws13, ws2.astype(BF16), row(g), row(b))


def _moe_layer(x1, routing, tg, layer, w1, w3, w2, ws1, ws3, ws2, g, b, alpha):
    t, d = x1.shape
    ng = t // tg
    bm = EXPERT_BM
    ns = -(-(tg * TOP_K + N_EXPERTS * (CHUNK_ALIGN - 1)) // SORT_CHUNK) * SORT_CHUNK
    pos, gate, n8_wide, plan_wide = routing
    plan3 = plan_wide.reshape(ng, PLAN_ROWS, LANES)
    tot = plan3[:, CHUNK_BITS + 1, 0]

    n8 = n8_wide[:, 0].reshape(ng, N_EXPERTS)
    goff = jnp.cumsum(n8, axis=1) - n8
    seg = jnp.sum(n8, axis=0)
    pend = jnp.cumsum(seg)
    pstart = pend - seg
    gdest = pstart[None, :] + jnp.cumsum(n8, axis=0) - n8
    max_rows = ng * (tg * TOP_K + N_EXPERTS * (CHUNK_ALIGN - 1))
    n_rows = max_rows + bm

    experts = jnp.arange(N_EXPERTS, dtype=jnp.int32)
    nb = (seg + bm - 1) // bm
    cb = jnp.cumsum(nb)
    blk = jnp.arange(max_rows // bm + N_EXPERTS, dtype=jnp.int32)
    blk_expert = jnp.minimum(jnp.sum((cb[None, :] <= blk[:, None]).astype(jnp.int32), axis=1), N_EXPERTS - 1)
    mine = blk_expert[:, None] == experts[None, :]
    of_expert = lambda v: jnp.sum(jnp.where(mine, v[None, :], 0), axis=1)
    local = blk - of_expert(cb - nb)
    active = seg > 0
    ordinal = jnp.cumsum(active.astype(jnp.int32)) - 1
    active_experts = jnp.sum(jnp.where((ordinal[None, :] == experts[:, None]) & active[None, :],
                                       experts[None, :], 0), axis=1)
    tables = [jnp.stack([cb[-1], jnp.sum(active.astype(jnp.int32))]),
              of_expert(pstart) + local * bm,
              jnp.clip(of_expert(seg) - local * bm, 0, bm),
              of_expert(ordinal),
              (local == 0).astype(jnp.int32),
              active_experts]
    tables = [a.astype(jnp.int32) for a in tables]
    zrow = pend[-1:].astype(jnp.int32)

    lists = plan3[:, :CHUNK_BITS, :N_EXPERTS]
    sizes = CHUNK_ALIGN << jnp.arange(CHUNK_BITS, dtype=jnp.int32)
    done = n8[:, None, :] & (sizes[None, :, None] - 1)
    pick = lists[..., None] == jnp.arange(N_EXPERTS, dtype=jnp.int32)
    src = jnp.sum(jnp.where(pick, (goff[:, None, :] + done)[:, :, None, :], 0), axis=-1)
    dst = jnp.sum(jnp.where(pick, (gdest[:, None, :] + done)[:, :, None, :], 0), axis=-1)
    widen = lambda a: jnp.pad(a, ((0, 0), (0, 0), (0, LANES - N_EXPERTS)))
    plan = jnp.concatenate([widen(src), plan3[:, CHUNK_BITS:], widen(dst),
                            jnp.zeros((ng, PLAN_ROWS - CHUNK_BITS, LANES), jnp.int32)],
                           axis=1).astype(jnp.int32).reshape(-1)

    xs = _dispatch(x1, pos, plan, tot, zrow, n_rows, tg, ns, bm)
    yb = _gmm(xs, tables, w1, w3, w2, layer, bm)
    return _combine(yb, pos, gate, plan, tot, x1, ws1, ws3, ws2, g, b, alpha, tg, ns)


def kernel(x, ln_gain, ln_bias, pool_w, pool_scale, attn_wqkv, attn_bqkv, attn_sinks, attn_wo, attn_bo,
           conv_w_in, conv_w, conv_w_out, router_w, router_bias, expert_w1, expert_w3, expert_w2,
           shared_w1, shared_w3, shared_w2):
    bsz, seq, d = x.shape
    depth = ln_gain.shape[0]
    alpha = float((2 * depth) ** 0.25)
    h = x.reshape(bsz * seq, d)
    tg = min(GROUP_TG, seq)
    for i in range(depth):
        kind = i % N_MIXERS
        j = i // N_MIXERS
        g1, b1 = ln_gain[i, 0], ln_bias[i, 0]
        route = (router_w[i], router_bias[i], tg)
        if kind == 0:
            h, *routing = _pool_mixer(h, seq, pool_w[j], pool_scale[j], g1, b1, alpha, *route)
        elif kind == 1:
            h, *routing = _swa_mixer(h, seq, attn_wqkv[j], attn_bqkv[j], attn_sinks[j], attn_wo[j],
                                     attn_bo[j], g1, b1, alpha, *route)
        else:
            h, *routing = _conv_mixer(h, seq, conv_w_in[j], conv_w[j], conv_w_out[j], g1, b1, alpha, *route)
        h = _moe_layer(h, routing, tg, i, expert_w1, expert_w3, expert_w2,
                       shared_w1[i], shared_w3[i], shared_w2[i], ln_gain[i, 1], ln_bias[i, 1], alpha)
    return h.reshape(bsz, seq, d)
```

```python
import functools

import jax
import jax.numpy as jnp
from jax import lax
from jax.experimental import pallas as pl
from jax.experimental.pallas import tpu as pltpu

POOL_WINDOWS = (2, 4, 8, 16)
N_MIXERS = 3
HEAD_DIM = 64
N_Q_HEADS = 16
N_KV_HEADS = 4
GQA_GROUP = N_Q_HEADS // N_KV_HEADS
WINDOW = 128
ROPE_THETA = 10000.0
N_EXPERTS = 64
TOP_K = 8
N_EXPERT_GROUPS = 8
GROUP_SIZE = N_EXPERTS // N_EXPERT_GROUPS
TOPK_GROUPS = 4
ROUTED_SCALE = 2.5
LN_EPS = 1e-5

LANES = 128
SUBLANES = 8
POOL_HALO = 16
VMEM_LIMIT = 48 * 1024 * 1024

POOL_TT = 512
QKV_TT = 512
ATTN_TQ = 512
CONV_TT = 512
GROUP_TG = 256
CHUNK_ALIGN = SUBLANES
SORT_CHUNK = 256
EXPERT_BM = 1024
GMM_X_SLOTS = 3
CHUNK_BITS = 6
PLAN_ROWS = SUBLANES
PLAN_WORDS = 2 * PLAN_ROWS * LANES

U32 = jnp.uint32
HI_MASK = 0xFFFF0000

F32 = jnp.float32
BF16 = jnp.bfloat16
NEG_INF = float("-inf")


def _layer_norm(z, g, b):
    mu = jnp.mean(z, axis=-1, keepdims=True)
    zc = z - mu
    var = jnp.mean(zc * zc, axis=-1, keepdims=True)
    return zc * lax.rsqrt(var + LN_EPS) * g + b


def _params(*sem):
    return pltpu.CompilerParams(dimension_semantics=sem, vmem_limit_bytes=VMEM_LIMIT)


def _router_plumbing(router_w, router_bias, t, d, tt, tg, tile_index):
    groups = tt // tg
    const = lambda *ids: (0, 0)
    by_cols = lambda *ids: (0, tile_index(*ids))
    by_rows = lambda *ids: (tile_index(*ids), 0)
    operands = [router_w.T, router_bias.reshape(N_EXPERTS, 1)]
    in_specs = [pl.BlockSpec((N_EXPERTS, d), const), pl.BlockSpec((N_EXPERTS, 1), const)]
    out_specs = [pl.BlockSpec((TOP_K, tt), by_cols), pl.BlockSpec((TOP_K, tt), by_cols),
                 pl.BlockSpec((groups * N_EXPERTS, LANES), by_rows),
                 pl.BlockSpec((groups * PLAN_ROWS, LANES), by_rows)]
    out_shapes = [jax.ShapeDtypeStruct((TOP_K, t), jnp.int32), jax.ShapeDtypeStruct((TOP_K, t), F32),
                  jax.ShapeDtypeStruct((t // tg * N_EXPERTS, LANES), jnp.int32),
                  jax.ShapeDtypeStruct((t // tg * PLAN_ROWS, LANES), jnp.int32)]
    return operands, in_specs, out_specs, out_shapes


def _pool_kernel(x_ref, halo_ref, w_ref, scale_ref, g_ref, b_ref, wt_ref, rb_ref, o_ref, pos_ref, gate_ref,
                 n8_ref, plan_ref, buf_ref, *, alpha, tt, tg):
    i = pl.program_id(1)
    x = x_ref[...]
    d = x.shape[1]
    ch = d // len(POOL_WINDOWS)
    top = SUBLANES
    first = top + POOL_HALO
    n = first + tt
    for k in range(3):
        buf_ref[k, 0:top, :] = jnp.zeros((top, d), F32)
    buf_ref[0, top:first, :] = jnp.where(i > 0, halo_ref[...], 0.0)
    buf_ref[0, first:n, :] = x
    pos = i * tt + lax.broadcasted_iota(jnp.int32, (tt, 1), 0)
    ys = []
    for g, w in enumerate(POOL_WINDOWS):
        c0 = g * ch
        xg = x[:, c0:c0 + ch]
        src, span = 0, 1
        while span < w:
            dst = 1 if src != 1 else 2
            buf_ref[dst, top:n, c0:c0 + ch] = (buf_ref[src, top:n, c0:c0 + ch]
                                               + buf_ref[src, top - span:n - span, c0:c0 + ch])
            src, span = dst, 2 * span
        s = buf_ref[src, first:n, c0:c0 + ch]
        inv_cnt = 1.0 / jnp.minimum(pos + 1, w).astype(F32)
        pooled = s * inv_cnt - xg
        ys.append(jnp.dot(pooled.astype(BF16), w_ref[g], preferred_element_type=F32))
    y = jnp.concatenate(ys, axis=1) * scale_ref[...]
    o_ref[...] = _layer_norm(alpha * x + y, g_ref[...], b_ref[...])
    _route_tile(o_ref, (wt_ref, rb_ref), (pos_ref, gate_ref, n8_ref, plan_ref), tg)


def _pool_mixer(x2, seq, w_grp, scale, g, b, alpha, router_w, router_bias, tg):
    t, d = x2.shape
    tt = min(POOL_TT, seq)
    nt = seq // tt
    nb = t // seq
    row = lambda v: v.reshape(1, d)
    const2 = lambda bi, i: (0, 0)
    tile = lambda bi, i: bi * nt + i
    r_ops, r_in, r_out, r_shapes = _router_plumbing(router_w, router_bias, t, d, tt, tg, tile)
    return pl.pallas_call(
        functools.partial(_pool_kernel, alpha=alpha, tt=tt, tg=tg),
        grid=(nb, nt),
        in_specs=[
            pl.BlockSpec((tt, d), lambda bi, i: (bi * nt + i, 0)),
            pl.BlockSpec((POOL_HALO, d),
                         lambda bi, i: (jnp.maximum((bi * seq + i * tt) // POOL_HALO - 1, 0), 0)),
            pl.BlockSpec(w_grp.shape, lambda bi, i: (0, 0, 0)),
            pl.BlockSpec((1, d), const2),
            pl.BlockSpec((1, d), const2),
            pl.BlockSpec((1, d), const2),
        ] + r_in,
        out_specs=[pl.BlockSpec((tt, d), lambda bi, i: (bi * nt + i, 0))] + r_out,
        out_shape=[jax.ShapeDtypeStruct((t, d), F32)] + r_shapes,
        scratch_shapes=[pltpu.VMEM((3, SUBLANES + POOL_HALO + tt, d), F32)],
        compiler_params=_params("parallel", "parallel"),
        name="pool_mixer",
    )(x2, x2, w_grp.astype(BF16), row(scale), row(g), row(b), *r_ops)


def _qkv_kernel(x_ref, w_ref, b_ref, c_ref, s_ref, q_ref, k_ref, v_ref, *, nq, nk):
    y = jnp.dot(x_ref[...].astype(BF16), w_ref[...], preferred_element_type=F32) + b_ref[...]
    c = c_ref[...]
    s = s_ref[...]
    lane = lax.broadcasted_iota(jnp.int32, c.shape, 1)
    first_half = (lane % HEAD_DIM) < (HEAD_DIM // 2)

    def rope(v):
        partner = jnp.where(first_half, pltpu.roll(v, LANES - HEAD_DIM // 2, 1),
                            pltpu.roll(v, HEAD_DIM // 2, 1))
        return v * c + partner * s

    for j in range(nq // LANES):
        q_ref[:, j * LANES:(j + 1) * LANES] = (
            rope(y[:, j * LANES:(j + 1) * LANES]) * (HEAD_DIM ** -0.5)).astype(BF16)
    for j in range(nk // LANES):
        k_ref[:, j * LANES:(j + 1) * LANES] = rope(
            y[:, nq + j * LANES:nq + (j + 1) * LANES]).astype(BF16)
    v_ref[...] = y[:, nq + nk:].astype(BF16)


def _attn_kernel(sink_ref, q_ref, k_ref, kh_ref, v_ref, vh_ref, x_ref, wo_ref, bo_ref, g_ref, b_ref,
                 wt_ref, rb_ref, o_ref, pos_ref, gate_ref, n8_ref, plan_ref, *, alpha, tq, tg):
    i = pl.program_id(1)
    nqb = tq // WINDOW
    stack = GQA_GROUP * tq
    row_head = lax.broadcasted_iota(jnp.int32, (stack, 1), 0) // tq
    r = lax.broadcasted_iota(jnp.int32, (stack, 2 * WINDOW), 0) % tq
    c = lax.broadcasted_iota(jnp.int32, (stack, 2 * WINDOW), 1)
    dist = (r & (WINDOW - 1)) + WINDOW - c
    valid = (dist >= 0) & (dist < WINDOW) & ((c >= WINDOW) | (r >= WINDOW) | (i > 0))
    lane = lax.broadcasted_iota(jnp.int32, (tq, LANES), 1)
    low = lane < HEAD_DIM
    dn = (((1,), (1,)), ((), ()))
    outs = []
    for kv in range(N_KV_HEADS):
        sl = slice(kv * LANES, (kv + 1) * LANES)
        kk = jnp.concatenate([kh_ref[:, sl], k_ref[:, sl]], axis=0)
        vv = jnp.concatenate([vh_ref[:, sl], v_ref[:, sl]], axis=0)
        blocks = []
        for h in range(GQA_GROUP):
            c0 = (kv * GQA_GROUP + h - h % 2) * HEAD_DIM
            qp = q_ref[:, c0:c0 + LANES]
            keep = low if h % 2 == 0 else jnp.logical_not(low)
            qh = jnp.where(keep, qp, jnp.zeros_like(qp))
            sc_all = lax.dot_general(qh, kk, dn, preferred_element_type=F32)
            blocks += [sc_all[qb * WINDOW:(qb + 1) * WINDOW, qb * WINDOW:(qb + 2) * WINDOW]
                       for qb in range(nqb)]
        sc = jnp.where(valid, jnp.concatenate(blocks, axis=0), NEG_INF)
        sink = jnp.zeros((stack, 1), F32)
        for h in range(GQA_GROUP):
            sink = jnp.where(row_head == h, sink_ref[kv * GQA_GROUP + h], sink)
        m = jnp.maximum(jnp.max(sc, axis=1, keepdims=True), sink)
        pm = jnp.exp(sc - m)
        rden = 1.0 / (jnp.sum(pm, axis=1, keepdims=True) + jnp.exp(sink - m))
        pm = pm.astype(BF16)
        res = jnp.concatenate(
            [jnp.dot(pm[b * WINDOW:(b + 1) * WINDOW, :],
                     vv[(b % nqb) * WINDOW:(b % nqb + 2) * WINDOW, :], preferred_element_type=F32)
             for b in range(GQA_GROUP * nqb)], axis=0) * rden
        for p in range(GQA_GROUP // 2):
            outs.append(jnp.where(low, res[2 * p * tq:(2 * p + 1) * tq, :],
                                  res[(2 * p + 1) * tq:(2 * p + 2) * tq, :]))
    o_all = jnp.concatenate(outs, axis=1).astype(BF16)
    mix = jnp.dot(o_all, wo_ref[...], preferred_element_type=F32) + bo_ref[...]
    o_ref[...] = _layer_norm(alpha * x_ref[...] + mix, g_ref[...], b_ref[...])
    _route_tile(o_ref, (wt_ref, rb_ref), (pos_ref, gate_ref, n8_ref, plan_ref), tg)


def _dup_heads(w, n_heads):
    lead = w.shape[:-1]
    w = w.reshape(lead + (n_heads, 1, HEAD_DIM))
    w = jnp.broadcast_to(w, lead + (n_heads, 2, HEAD_DIM))
    return w.reshape(lead + (n_heads * 2 * HEAD_DIM,))


def _swa_mixer(x2, seq, wqkv, bqkv, sinks, wo, bo, g, b, alpha, router_w, router_bias, tg):
    t, d = x2.shape
    nb = t // seq
    nq = N_Q_HEADS * HEAD_DIM
    nkv = N_KV_HEADS * HEAD_DIM
    nk2 = 2 * nkv
    w_ext = jnp.concatenate([wqkv[:, :nq], _dup_heads(wqkv[:, nq:nq + nkv], N_KV_HEADS),
                             _dup_heads(wqkv[:, nq + nkv:], N_KV_HEADS)], axis=1).astype(BF16)
    b_ext = jnp.concatenate([bqkv[:nq], _dup_heads(bqkv[nq:nq + nkv], N_KV_HEADS),
                             _dup_heads(bqkv[nq + nkv:], N_KV_HEADS)]).reshape(1, -1)
    n_ext = nq + 2 * nk2

    pos = jnp.arange(seq, dtype=F32)
    inv_freq = ROPE_THETA ** (-jnp.arange(0, HEAD_DIM, 2, dtype=F32) / HEAD_DIM)
    ang = pos[:, None] * inv_freq[None, :]
    cos, sin = jnp.cos(ang), jnp.sin(ang)
    cos_t = jnp.concatenate([cos, cos, cos, cos], axis=1)
    sin_t = jnp.concatenate([-sin, sin, -sin, sin], axis=1)

    tt = min(QKV_TT, seq)
    nt = seq // tt
    const2 = lambda i: (0, 0)
    q, k2, v2 = pl.pallas_call(
        functools.partial(_qkv_kernel, nq=nq, nk=nk2),
        grid=(t // tt,),
        in_specs=[
            pl.BlockSpec((tt, d), lambda i: (i, 0)),
            pl.BlockSpec((d, n_ext), const2),
            pl.BlockSpec((1, n_ext), const2),
            pl.BlockSpec((tt, LANES), lambda i: (i % nt, 0)),
            pl.BlockSpec((tt, LANES), lambda i: (i % nt, 0)),
        ],
        out_specs=[pl.BlockSpec((tt, nq), lambda i: (i, 0)),
                   pl.BlockSpec((tt, nk2), lambda i: (i, 0)),
                   pl.BlockSpec((tt, nk2), lambda i: (i, 0))],
        out_shape=[jax.ShapeDtypeStruct((t, nq), BF16),
                   jax.ShapeDtypeStruct((t, nk2), BF16),
                   jax.ShapeDtypeStruct((t, nk2), BF16)],
        compiler_params=_params("parallel"),
        name="swa_qkv",
    )(x2, w_ext, b_ext, cos_t, sin_t)

    tq = min(ATTN_TQ, seq)
    ntq = seq // tq
    main = lambda bi, i: (bi * ntq + i, 0)
    halo = lambda bi, i: (jnp.maximum((bi * seq + i * tq) // WINDOW - 1, 0), 0)
    const = lambda bi, i: (0, 0)
    row = lambda v: v.reshape(1, d)
    r_ops, r_in, r_out, r_shapes = _router_plumbing(router_w, router_bias, t, d, tq, tg,
                                                    lambda bi, i: bi * ntq + i)
    return pl.pallas_call(
        functools.partial(_attn_kernel, alpha=alpha, tq=tq, tg=tg),
        grid=(nb, ntq),
        in_specs=[
            pl.BlockSpec(memory_space=pltpu.SMEM),
            pl.BlockSpec((tq, nq), main),
            pl.BlockSpec((tq, nk2), main),
            pl.BlockSpec((WINDOW, nk2), halo),
            pl.BlockSpec((tq, nk2), main),
            pl.BlockSpec((WINDOW, nk2), halo),
            pl.BlockSpec((tq, d), main),
            pl.BlockSpec((nq, d), const),
            pl.BlockSpec((1, d), const),
            pl.BlockSpec((1, d), const),
            pl.BlockSpec((1, d), const),
        ] + r_in,
        out_specs=[pl.BlockSpec((tq, d), main)] + r_out,
        out_shape=[jax.ShapeDtypeStruct((t, d), F32)] + r_shapes,
        compiler_params=_params("parallel", "parallel"),
        name="swa_attention",
    )(sinks, q, k2, k2, v2, v2, x2, wo.astype(BF16), row(bo), row(g), row(b), *r_ops)


def _conv_kernel(x_ref, win_ref, cw_ref, wout_ref, g_ref, b_ref, wt_ref, rb_ref, o_ref, pos_ref, gate_ref,
                 n8_ref, plan_ref, ubuf_ref, *, alpha, tt, tg):
    i = pl.program_id(1)
    d = x_ref.shape[1]

    @pl.when(i == 0)
    def _():
        ubuf_ref[0:SUBLANES, :] = jnp.zeros((SUBLANES, d), F32)

    x = x_ref[...]
    proj = jnp.dot(x.astype(BF16), win_ref[...], preferred_element_type=F32)
    gate_b = proj[:, :d]
    u = proj[:, d:2 * d] * proj[:, 2 * d:]
    ubuf_ref[SUBLANES:SUBLANES + tt, :] = u
    cw = cw_ref[...]
    conv = (cw[0:1, :] * ubuf_ref[SUBLANES - 2:SUBLANES - 2 + tt, :]
            + cw[1:2, :] * ubuf_ref[SUBLANES - 1:SUBLANES - 1 + tt, :]
            + cw[2:3, :] * u)
    ubuf_ref[0:SUBLANES, :] = u[tt - SUBLANES:tt, :]
    y = jnp.dot((gate_b * conv).astype(BF16), wout_ref[...], preferred_element_type=F32)
    o_ref[...] = _layer_norm(alpha * x + y, g_ref[...], b_ref[...])
    _route_tile(o_ref, (wt_ref, rb_ref), (pos_ref, gate_ref, n8_ref, plan_ref), tg)


def _conv_mixer(x2, seq, w_in, conv_w, w_out, g, b, alpha, router_w, router_bias, tg):
    t, d = x2.shape
    nb = t // seq
    tt = min(CONV_TT, seq)
    nt = seq // tt
    const = lambda bi, i: (0, 0)
    row = lambda v: v.reshape(1, d)
    r_ops, r_in, r_out, r_shapes = _router_plumbing(router_w, router_bias, t, d, tt, tg,
                                                    lambda bi, i: bi * nt + i)
    return pl.pallas_call(
        functools.partial(_conv_kernel, alpha=alpha, tt=tt, tg=tg),
        grid=(nb, nt),
        in_specs=[
            pl.BlockSpec((tt, d), lambda bi, i: (bi * nt + i, 0)),
            pl.BlockSpec((d, 3 * d), const),
            pl.BlockSpec(conv_w.shape, const),
            pl.BlockSpec((d, d), const),
            pl.BlockSpec((1, d), const),
            pl.BlockSpec((1, d), const),
        ] + r_in,
        out_specs=[pl.BlockSpec((tt, d), lambda bi, i: (bi * nt + i, 0))] + r_out,
        out_shape=[jax.ShapeDtypeStruct((t, d), F32)] + r_shapes,
        scratch_shapes=[pltpu.VMEM((SUBLANES + tt, d), F32)],
        compiler_params=_params("arbitrary", "arbitrary"),
        name="conv_mixer",
    )(x2, w_in.astype(BF16), conv_w, w_out.astype(BF16), row(g), row(b), *r_ops)


def _first_argmax(v, rows, n):
    m = jnp.max(v, axis=0, keepdims=True)
    first = jnp.min(jnp.where(v == m, rows, n), axis=0, keepdims=True)
    return m, first


def _route_tile(o_ref, router_refs, route_out_refs, tg):
    wt_ref, bias_ref = router_refs
    pos_ref, gate_ref, n8_ref, plan_ref = route_out_refs
    tt = o_ref.shape[0]
    x = o_ref[...]
    xh = x.astype(BF16)
    xl = (x - xh.astype(F32)).astype(BF16)
    wt = wt_ref[...]
    wh = wt.astype(BF16)
    wl = (wt - wh.astype(F32)).astype(BF16)
    dn = (((1,), (1,)), ((), ()))
    logits = (lax.dot_general(wh, xh, dn, preferred_element_type=F32)
              + (lax.dot_general(wh, xl, dn, preferred_element_type=F32)
                 + lax.dot_general(wl, xh, dn, preferred_element_type=F32)))
    scores = jax.nn.sigmoid(logits)
    sel = scores + bias_ref[...]

    rows8 = lax.broadcasted_iota(jnp.int32, (GROUP_SIZE, tt), 0)
    gsc = []
    for gi in range(N_EXPERT_GROUPS):
        blk = sel[gi * GROUP_SIZE:(gi + 1) * GROUP_SIZE, :]
        m1, f1 = _first_argmax(blk, rows8, GROUP_SIZE)
        m2 = jnp.max(jnp.where(rows8 == f1, NEG_INF, blk), axis=0, keepdims=True)
        gsc.append(m1 + m2)
    cur = jnp.concatenate(gsc, axis=0)
    rows_g = lax.broadcasted_iota(jnp.int32, (N_EXPERT_GROUPS, tt), 0)
    gmask = jnp.zeros((N_EXPERT_GROUPS, tt), F32)
    for _ in range(TOPK_GROUPS):
        _, f = _first_argmax(cur, rows_g, N_EXPERT_GROUPS)
        hit = rows_g == f
        gmask = jnp.where(hit, 1.0, gmask)
        cur = jnp.where(hit, NEG_INF, cur)
    emask = jnp.concatenate(
        [jnp.broadcast_to(gmask[gi:gi + 1, :], (GROUP_SIZE, tt)) for gi in range(N_EXPERT_GROUPS)],
        axis=0)

    rows_e = lax.broadcasted_iota(jnp.int32, (N_EXPERTS, tt), 0)
    cur = jnp.where(emask > 0.0, sel, NEG_INF)
    idxs, gates = [], []
    member = jnp.zeros((N_EXPERTS, tt), F32)
    for _ in range(TOP_K):
        _, f = _first_argmax(cur, rows_e, N_EXPERTS)
        hit = rows_e == f
        idxs.append(f)
        gates.append(jnp.sum(jnp.where(hit, scores, 0.0), axis=0, keepdims=True))
        member = jnp.where(hit, 1.0, member)
        cur = jnp.where(hit, NEG_INF, cur)
    gate = jnp.concatenate(gates, axis=0)
    gate_ref[...] = gate / jnp.sum(gate, axis=0, keepdims=True) * ROUTED_SCALE

    for s in range(tt // tg):
        cols = slice(s * tg, (s + 1) * tg)
        _plan_group(member[:, cols], [f[:, cols] for f in idxs], pos_ref.at[:, pl.ds(s * tg, tg)],
                    n8_ref.at[pl.ds(s * N_EXPERTS, N_EXPERTS), :],
                    plan_ref.at[pl.ds(s * PLAN_ROWS, PLAN_ROWS), :], tg)


def _plan_group(member, idxs, pos_ref, n8_ref, plan_ref, tt):
    rows_e = lax.broadcasted_iota(jnp.int32, (N_EXPERTS, tt), 0)
    ta = lax.broadcasted_iota(jnp.int32, (tt, tt), 0)
    tb = lax.broadcasted_iota(jnp.int32, (tt, tt), 1)
    earlier = (ta < tb).astype(BF16)
    before = jnp.dot(member.astype(BF16), earlier, preferred_element_type=F32)
    n = jnp.sum(member, axis=1, keepdims=True)
    n8 = jnp.floor((n + (CHUNK_ALIGN - 1)) * (1.0 / CHUNK_ALIGN)) * CHUNK_ALIGN
    n8_wide = jnp.broadcast_to(n8, (N_EXPERTS, LANES))
    ea = lax.broadcasted_iota(jnp.int32, (N_EXPERTS, N_EXPERTS), 0)
    eb = lax.broadcasted_iota(jnp.int32, (N_EXPERTS, N_EXPERTS), 1)
    lower = (eb < ea).astype(BF16)
    chunk_start = jnp.dot(lower, n8_wide.astype(BF16), preferred_element_type=F32)
    where_to = before + chunk_start[:, 0:1]
    for k in range(TOP_K):
        pk = jnp.sum(jnp.where(rows_e == idxs[k], where_to, 0.0), axis=0, keepdims=True)
        pos_ref[k:k + 1, :] = pk.astype(jnp.int32)
    n8_int = n8_wide.astype(jnp.int32)
    n8_ref[...] = n8_int

    lane = lax.broadcasted_iota(jnp.int32, (N_EXPERTS, LANES), 1)
    flag = jnp.zeros((N_EXPERTS, LANES), F32)
    for j in range(CHUNK_BITS):
        flag = jnp.where((lane == j) & ((n8_int & (CHUNK_ALIGN << j)) != 0), 1.0, flag)
    rank = jnp.dot(lower, flag.astype(BF16), preferred_element_type=F32)
    e_id = lax.broadcasted_iota(jnp.int32, (N_EXPERTS, LANES), 0)
    for j in range(CHUNK_BITS):
        chosen = (rank[:, j:j + 1] == lane.astype(F32)) & (flag[:, j:j + 1] > 0.0)
        plan_ref[j:j + 1, :] = jnp.sum(jnp.where(chosen, e_id, 0), axis=0, keepdims=True)
    plan_ref[CHUNK_BITS:CHUNK_BITS + 1, :] = jnp.sum(flag, axis=0, keepdims=True).astype(jnp.int32)
    plan_ref[CHUNK_BITS + 1:CHUNK_BITS + 2, :] = jnp.sum(n8_int, axis=0, keepdims=True)


def _pack_halves(v, exact=False):
    h = v.shape[1] // 2
    return _pack_pair(v[:, :h], v[:, h:], exact)


def _pack_pair(lo, hi, exact=False):
    if not exact:
        lo, hi = lo.astype(BF16).astype(F32), hi.astype(BF16).astype(F32)
    lo = lax.bitcast_convert_type(lo, U32)
    hi = lax.bitcast_convert_type(hi, U32)
    return (lo >> 16) | (hi & jnp.uint32(HI_MASK))


def _chunk_relative(pos, c):
    rel = pos - c * SORT_CHUNK
    inside = (rel >= 0) & (rel < SORT_CHUNK)
    return jnp.where(inside, rel, -1).astype(F32).astype(BF16)


def _unpack_halves(w):
    lo = lax.bitcast_convert_type(w << 16, F32).astype(BF16)
    hi = lax.bitcast_convert_type(w & jnp.uint32(HI_MASK), F32).astype(BF16)
    return lo, hi


def _start_group_copies(plan_ref, make_copy):
    for j in range(CHUNK_BITS):
        size = CHUNK_ALIGN << j

        def body(k, carry):
            make_copy(pl.multiple_of(plan_ref[j * LANES + k], CHUNK_ALIGN),
                      pl.multiple_of(plan_ref[(PLAN_ROWS + j) * LANES + k], CHUNK_ALIGN), size).start()
            return carry

        lax.fori_loop(0, plan_ref[CHUNK_BITS * LANES + j], body, 0)


def _wait_group_rows(total_rows, ns, make_copy):
    for j in range(ns.bit_length()):
        size = CHUNK_ALIGN << j
        if size > ns:
            break

        @pl.when((total_rows & size) != 0)
        def _():
            make_copy(0, 0, size).wait()


def _dispatch_kernel(tot_ref, zrow_ref, plan_ref, pos_ref, x_ref, xs_ref, sbuf_ref, zbuf_ref,
                     onehot_ref, sem, zsem, *, tg, ns, bm):
    g = pl.program_id(0)
    ng = pl.num_programs(0)
    slot = g % 2

    @pl.when(g == 0)
    def _():
        zbuf_ref[...] = jnp.zeros_like(zbuf_ref)
        row = pl.multiple_of(zrow_ref[0], CHUNK_ALIGN)
        tail = pltpu.make_async_copy(zbuf_ref, xs_ref.at[pl.ds(row, bm), :], zsem)
        tail.start()
        tail.wait()

    def copy_from(sl):
        def make_copy(src, dst, size):
            return pltpu.make_async_copy(sbuf_ref.at[sl, pl.ds(src, size), :],
                                         xs_ref.at[pl.ds(dst, size), :], sem.at[sl])
        return make_copy

    @pl.when(g >= 2)
    def _():
        _wait_group_rows(tot_ref[g - 2], ns, copy_from(slot))

    xb = x_ref[...].astype(BF16)
    pos = pos_ref[...]
    local_rows = lax.broadcasted_iota(jnp.int32, (SORT_CHUNK, tg), 0).astype(F32).astype(BF16)
    one = jnp.ones((SORT_CHUNK, tg), BF16)
    zero = jnp.zeros((SORT_CHUNK, tg), BF16)
    for c in range(ns // SORT_CHUNK):
        rel = _chunk_relative(pos, c)
        hit = local_rows == rel[0:1, :]
        for k in range(1, TOP_K):
            hit = jnp.logical_or(hit, local_rows == rel[k:k + 1, :])
        onehot_ref[c * SORT_CHUNK:(c + 1) * SORT_CHUNK, :] = jnp.where(hit, one, zero)
    onehot = onehot_ref[...]
    half = x_ref.shape[1] // 2
    for n in range(half // SORT_CHUNK):
        lo = jnp.dot(onehot, xb[:, n * SORT_CHUNK:(n + 1) * SORT_CHUNK], preferred_element_type=F32)
        hi = jnp.dot(onehot, xb[:, half + n * SORT_CHUNK:half + (n + 1) * SORT_CHUNK],
                     preferred_element_type=F32)
        sbuf_ref[slot, :, n * SORT_CHUNK:(n + 1) * SORT_CHUNK] = _pack_pair(lo, hi, exact=True)

    _start_group_copies(plan_ref, copy_from(slot))

    @pl.when(g == ng - 1)
    def _():
        @pl.when(g >= 1)
        def _():
            _wait_group_rows(tot_ref[g - 1], ns, copy_from(1 - slot))
        _wait_group_rows(tot_ref[g], ns, copy_from(slot))


def _dispatch(x1, pos, plan, tot, zrow, n_rows, tg, ns, bm):
    t, d = x1.shape
    grid_spec = pltpu.PrefetchScalarGridSpec(
        num_scalar_prefetch=2,
        grid=(t // tg,),
        in_specs=[
            pl.BlockSpec((PLAN_WORDS,), lambda i, *_: (i,), memory_space=pltpu.SMEM),
            pl.BlockSpec((TOP_K, tg), lambda i, *_: (0, i)),
            pl.BlockSpec((tg, d), lambda i, *_: (i, 0)),
        ],
        out_specs=pl.BlockSpec(memory_space=pl.ANY),
        scratch_shapes=[pltpu.VMEM((2, ns, d // 2), U32), pltpu.VMEM((bm, d // 2), U32),
                        pltpu.VMEM((ns, tg), BF16),
                        pltpu.SemaphoreType.DMA((2,)), pltpu.SemaphoreType.DMA],
    )
    return pl.pallas_call(
        functools.partial(_dispatch_kernel, tg=tg, ns=ns, bm=bm),
        grid_spec=grid_spec,
        out_shape=jax.ShapeDtypeStruct((n_rows, d // 2), U32),
        compiler_params=_params("arbitrary"),
        name="moe_dispatch",
    )(tot, zrow, plan, pos, x1)


def _gmm_kernel(nq_ref, brow_ref, bval_ref, bord_ref, bfirst_ref, aexp_ref,
                xs_ref, w1_ref, w3_ref, w2_ref, y_ref,
                xbuf_ref, ybuf_ref, w1s_ref, w3s_ref, w2s_ref, w13_ref, w2b_ref, xsem, ysem, wsem,
                *, layer, bm):
    nq = nq_ref[0]
    na = nq_ref[1]
    f = w1s_ref.shape[2]
    half = xbuf_ref.shape[2]
    nx = xbuf_ref.shape[0]

    def x_copy(q, slot):
        row = pl.multiple_of(brow_ref[q], CHUNK_ALIGN)
        return pltpu.make_async_copy(xs_ref.at[pl.ds(row, bm), :], xbuf_ref.at[slot], xsem.at[slot])

    def w_copies(j, slot):
        e = aexp_ref[j]
        return (pltpu.make_async_copy(w1_ref.at[layer, e], w1s_ref.at[slot], wsem.at[slot]),
                pltpu.make_async_copy(w3_ref.at[layer, e], w3s_ref.at[slot], wsem.at[slot]),
                pltpu.make_async_copy(w2_ref.at[layer, e], w2s_ref.at[slot], wsem.at[slot]))

    def y_copy(slot, row, off, size):
        return pltpu.make_async_copy(ybuf_ref.at[slot, pl.ds(off, size), :],
                                     y_ref.at[pl.ds(pl.multiple_of(row + off, CHUNK_ALIGN), size), :],
                                     ysem.at[slot])

    def for_each_y_copy(slot, row, valid, fn):
        @pl.when(valid == bm)
        def _():
            fn(y_copy(slot, row, 0, bm))

        @pl.when(valid < bm)
        def _():
            for j in range((bm // CHUNK_ALIGN).bit_length() - 1):
                size = CHUNK_ALIGN << j

                @pl.when((valid & size) != 0)
                def _():
                    fn(y_copy(slot, row, pl.multiple_of(valid & (size - 1), CHUNK_ALIGN), size))

    for c in w_copies(0, 0):
        c.start()
    for q0 in range(nx - 1):
        @pl.when(q0 < nq)
        def _():
            x_copy(q0, q0).start()

    def body(q, carry):
        xslot = q % nx
        yslot = q % 2

        @pl.when(q + (nx - 1) < nq)
        def _():
            x_copy(q + (nx - 1), (q + (nx - 1)) % nx).start()

        @pl.when(bfirst_ref[q] == 1)
        def _():
            j = bord_ref[q]
            ws = j % 2
            for c in w_copies(j, ws):
                c.wait()

            @pl.when(j + 1 < na)
            def _():
                for c in w_copies(j + 1, 1 - ws):
                    c.start()
            w13_ref[:, :f] = w1s_ref[ws].astype(BF16)
            w13_ref[:, f:] = w3s_ref[ws].astype(BF16)
            w2b_ref[...] = w2s_ref[ws].astype(BF16)

        x_copy(q, xslot).wait()

        @pl.when(q >= 2)
        def _():
            for_each_y_copy(yslot, 0, bval_ref[jnp.maximum(q - 2, 0)], lambda c: c.wait())

        def swiglu(rows):
            x_lo, x_hi = _unpack_halves(xbuf_ref[xslot, 0:rows, :])
            h13 = (jnp.dot(x_lo, w13_ref[:half, :], preferred_element_type=F32)
                   + jnp.dot(x_hi, w13_ref[half:, :], preferred_element_type=F32))
            h = jax.nn.silu(h13[:, :f]) * h13[:, f:]
            ybuf_ref[yslot, 0:rows, :] = _pack_halves(
                jnp.dot(h.astype(BF16), w2b_ref[...], preferred_element_type=F32))

        @pl.when(bval_ref[q] > bm // 2)
        def _():
            swiglu(bm)

        @pl.when(bval_ref[q] <= bm // 2)
        def _():
            swiglu(bm // 2)
        for_each_y_copy(yslot, brow_ref[q], bval_ref[q], lambda c: c.start())
        return carry

    lax.fori_loop(0, nq, body, 0)

    @pl.when(nq >= 2)
    def _():
        for_each_y_copy(nq % 2, 0, bval_ref[jnp.maximum(nq - 2, 0)], lambda c: c.wait())
    for_each_y_copy((nq - 1) % 2, 0, bval_ref[nq - 1], lambda c: c.wait())


def _gmm(xs, tables, w1, w3, w2, layer, bm):
    n_rows, half = xs.shape
    d = 2 * half
    f = w1.shape[-1]
    any_spec = pl.BlockSpec(memory_space=pl.ANY)
    grid_spec = pltpu.PrefetchScalarGridSpec(
        num_scalar_prefetch=len(tables),
        grid=(1,),
        in_specs=[any_spec, any_spec, any_spec, any_spec],
        out_specs=any_spec,
        scratch_shapes=[pltpu.VMEM((GMM_X_SLOTS, bm, half), U32), pltpu.VMEM((2, bm, half), U32),
                        pltpu.VMEM((2, d, f), F32), pltpu.VMEM((2, d, f), F32), pltpu.VMEM((2, f, d), F32),
                        pltpu.VMEM((d, 2 * f), BF16), pltpu.VMEM((f, d), BF16),
                        pltpu.SemaphoreType.DMA((GMM_X_SLOTS,)), pltpu.SemaphoreType.DMA((2,)),
                        pltpu.SemaphoreType.DMA((2,))],
    )
    return pl.pallas_call(
        functools.partial(_gmm_kernel, layer=layer, bm=bm),
        grid_spec=grid_spec,
        out_shape=jax.ShapeDtypeStruct((n_rows, half), U32),
        compiler_params=_params("arbitrary"),
        name="moe_experts",
    )(*tables, xs, w1, w3, w2)


def _combine_kernel(tot_ref, plan_ref, plan_next_ref, yb_ref, pos_ref, gate_ref, x_ref, ws13_ref, ws2_ref,
                    g_ref, b_ref, o_ref, ybuf_ref, wgt_ref, ylo_ref, yhi_ref, sem, *, alpha, tg, ns):
    g = pl.program_id(0)
    ng = pl.num_programs(0)
    slot = g % 2

    def copy_into(sl):
        def make_copy(buf_row, hbm_row, size):
            return pltpu.make_async_copy(yb_ref.at[pl.ds(hbm_row, size), :],
                                         ybuf_ref.at[sl, pl.ds(buf_row, size), :], sem.at[sl])
        return make_copy

    @pl.when(g == 0)
    def _():
        ybuf_ref[...] = jnp.zeros_like(ybuf_ref)
        _start_group_copies(plan_ref, copy_into(0))

    @pl.when(g + 1 < ng)
    def _():
        _start_group_copies(plan_next_ref, copy_into(1 - slot))

    x = x_ref[...]
    f = ws2_ref.shape[0]
    h13 = jnp.dot(x.astype(BF16), ws13_ref[...], preferred_element_type=F32)
    hs = jax.nn.silu(h13[:, :f]) * h13[:, f:]
    shared = jnp.dot(hs.astype(BF16), ws2_ref[...], preferred_element_type=F32)

    _wait_group_rows(tot_ref[g], ns, copy_into(slot))

    pos = pos_ref[...]
    gate_b = gate_ref[...].astype(BF16)
    local_rows = lax.broadcasted_iota(jnp.int32, (SORT_CHUNK, tg), 0).astype(F32).astype(BF16)
    for c in range(ns // SORT_CHUNK):
        rel = _chunk_relative(pos, c)
        wgt = jnp.zeros((SORT_CHUNK, tg), BF16)
        for k in range(TOP_K):
            wgt = jnp.where(local_rows == rel[k:k + 1, :], gate_b[k:k + 1, :], wgt)
        wgt_ref[c * SORT_CHUNK:(c + 1) * SORT_CHUNK, :] = wgt
        y_lo, y_hi = _unpack_halves(ybuf_ref[slot, c * SORT_CHUNK:(c + 1) * SORT_CHUNK, :])
        ylo_ref[c * SORT_CHUNK:(c + 1) * SORT_CHUNK, :] = y_lo
        yhi_ref[c * SORT_CHUNK:(c + 1) * SORT_CHUNK, :] = y_hi
    wgt_all = wgt_ref[...]
    dn = (((0,), (0,)), ((), ()))
    routed = jnp.concatenate([lax.dot_general(wgt_all, ylo_ref[...], dn, preferred_element_type=F32),
                              lax.dot_general(wgt_all, yhi_ref[...], dn, preferred_element_type=F32)],
                             axis=1)
    o_ref[...] = _layer_norm(alpha * x + (routed + shared), g_ref[...], b_ref[...])


def _combine(yb, pos_kt, gate_kt, plan, tot, x1, ws1, ws3, ws2, g, b, alpha, tg, ns):
    t, d = x1.shape
    ng = t // tg
    f = ws1.shape[1]
    ws13 = jnp.concatenate([ws1, ws3], axis=1).astype(BF16)
    const = lambda i, *_: (0, 0)
    tile = lambda i, *_: (i, 0)
    row = lambda v: v.reshape(1, d)
    grid_spec = pltpu.PrefetchScalarGridSpec(
        num_scalar_prefetch=1,
        grid=(ng,),
        in_specs=[
            pl.BlockSpec((PLAN_WORDS,), lambda i, *_: (i,), memory_space=pltpu.SMEM),
            pl.BlockSpec((PLAN_WORDS,), lambda i, *_: (jnp.minimum(i + 1, ng - 1),),
                         memory_space=pltpu.SMEM),
            pl.BlockSpec(memory_space=pl.ANY),
            pl.BlockSpec((TOP_K, tg), lambda i, *_: (0, i)),
            pl.BlockSpec((TOP_K, tg), lambda i, *_: (0, i)),
            pl.BlockSpec((tg, d), tile),
            pl.BlockSpec((d, 2 * f), const),
            pl.BlockSpec((f, d), const),
            pl.BlockSpec((1, d), const),
            pl.BlockSpec((1, d), const),
        ],
        out_specs=pl.BlockSpec((tg, d), tile),
        scratch_shapes=[pltpu.VMEM((2, ns, d // 2), U32), pltpu.VMEM((ns, tg), BF16),
                        pltpu.VMEM((ns, d // 2), BF16), pltpu.VMEM((ns, d // 2), BF16),
                        pltpu.SemaphoreType.DMA((2,))],
    )
    return pl.pallas_call(
        functools.partial(_combine_kernel, alpha=alpha, tg=tg, ns=ns),
        grid_spec=grid_spec,
        out_shape=jax.ShapeDtypeStruct((t, d), F32),
        compiler_params=_params("arbitrary"),
        name="moe_combine",
    )(tot, plan, plan, yb, pos_kt, gate_kt, x1, ws13, ws2.astype(BF16), row(g), row(b))


def _moe_layer(x1, routing, tg, layer, w1, w3, w2, ws1, ws3, ws2, g, b, alpha):
    t, d = x1.shape
    ng = t // tg
    bm = EXPERT_BM
    ns = -(-(tg * TOP_K + N_EXPERTS * (CHUNK_ALIGN - 1)) // SORT_CHUNK) * SORT_CHUNK
    pos, gate, n8_wide, plan_wide = routing
    plan3 = plan_wide.reshape(ng, PLAN_ROWS, LANES)
    tot = plan3[:, CHUNK_BITS + 1, 0]

    n8 = n8_wide[:, 0].reshape(ng, N_EXPERTS)
    goff = jnp.cumsum(n8, axis=1) - n8
    seg = jnp.sum(n8, axis=0)
    pend = jnp.cumsum(seg)
    pstart = pend - seg
    gdest = pstart[None, :] + jnp.cumsum(n8, axis=0) - n8
    max_rows = ng * (tg * TOP_K + N_EXPERTS * (CHUNK_ALIGN - 1))
    n_rows = max_rows + bm

    experts = jnp.arange(N_EXPERTS, dtype=jnp.int32)
    nb = (seg + bm - 1) // bm
    cb = jnp.cumsum(nb)
    blk = jnp.arange(max_rows // bm + N_EXPERTS, dtype=jnp.int32)
    blk_expert = jnp.minimum(jnp.sum((cb[None, :] <= blk[:, None]).astype(jnp.int32), axis=1), N_EXPERTS - 1)
    mine = blk_expert[:, None] == experts[None, :]
    of_expert = lambda v: jnp.sum(jnp.where(mine, v[None, :], 0), axis=1)
    local = blk - of_expert(cb - nb)
    active = seg > 0
    ordinal = jnp.cumsum(active.astype(jnp.int32)) - 1
    active_experts = jnp.sum(jnp.where((ordinal[None, :] == experts[:, None]) & active[None, :],
                                       experts[None, :], 0), axis=1)
    tables = [jnp.stack([cb[-1], jnp.sum(active.astype(jnp.int32))]),
              of_expert(pstart) + local * bm,
              jnp.clip(of_expert(seg) - local * bm, 0, bm),
              of_expert(ordinal),
              (local == 0).astype(jnp.int32),
              active_experts]
    tables = [a.astype(jnp.int32) for a in tables]
    zrow = pend[-1:].astype(jnp.int32)

    lists = plan3[:, :CHUNK_BITS, :N_EXPERTS]
    sizes = CHUNK_ALIGN << jnp.arange(CHUNK_BITS, dtype=jnp.int32)
    done = n8[:, None, :] & (sizes[None, :, None] - 1)
    pick = lists[..., None] == jnp.arange(N_EXPERTS, dtype=jnp.int32)
    src = jnp.sum(jnp.where(pick, (goff[:, None, :] + done)[:, :, None, :], 0), axis=-1)
    dst = jnp.sum(jnp.where(pick, (gdest[:, None, :] + done)[:, :, None, :], 0), axis=-1)
    widen = lambda a: jnp.pad(a, ((0, 0), (0, 0), (0, LANES - N_EXPERTS)))
    plan = jnp.concatenate([widen(src), plan3[:, CHUNK_BITS:], widen(dst),
                            jnp.zeros((ng, PLAN_ROWS - CHUNK_BITS, LANES), jnp.int32)],
                           axis=1).astype(jnp.int32).reshape(-1)

    xs = _dispatch(x1, pos, plan, tot, zrow, n_rows, tg, ns, bm)
    yb = _gmm(xs, tables, w1, w3, w2, layer, bm)
    return _combine(yb, pos, gate, plan, tot, x1, ws1, ws3, ws2, g, b, alpha, tg, ns)


def kernel(x, ln_gain, ln_bias, pool_w, pool_scale, attn_wqkv, attn_bqkv, attn_sinks, attn_wo, attn_bo,
           conv_w_in, conv_w, conv_w_out, router_w, router_bias, expert_w1, expert_w3, expert_w2,
           shared_w1, shared_w3, shared_w2):
    bsz, seq, d = x.shape
    depth = ln_gain.shape[0]
    alpha = float((2 * depth) ** 0.25)
    h = x.reshape(bsz * seq, d)
    tg = min(GROUP_TG, seq)
    for i in range(depth):
        kind = i % N_MIXERS
        j = i // N_MIXERS
        g1, b1 = ln_gain[i, 0], ln_bias[i, 0]
        route = (router_w[i], router_bias[i], tg)
        if kind == 0:
            h, *routing = _pool_mixer(h, seq, pool_w[j], pool_scale[j], g1, b1, alpha, *route)
        elif kind == 1:
            h, *routing = _swa_mixer(h, seq, attn_wqkv[j], attn_bqkv[j], attn_sinks[j], attn_wo[j],
                                     attn_bo[j], g1, b1, alpha, *route)
        else:
            h, *routing = _conv_mixer(h, seq, conv_w_in[j], conv_w[j], conv_w_out[j], g1, b1, alpha, *route)
        h = _moe_layer(h, routing, tg, i, expert_w1, expert_w3, expert_w2,
                       shared_w1[i], shared_w3[i], shared_w2[i], ln_gain[i, 1], ln_bias[i, 1], alpha)
    return h.reshape(bsz, seq, d)
```

```python
import functools

import jax
import jax.numpy as jnp
from jax import lax
from jax.experimental import pallas as pl
from jax.experimental.pallas import tpu as pltpu

POOL_WINDOWS = (2, 4, 8, 16)
N_MIXERS = 3
HEAD_DIM = 64
N_Q_HEADS = 16
N_KV_HEADS = 4
GQA_GROUP = N_Q_HEADS // N_KV_HEADS
WINDOW = 128
ROPE_THETA = 10000.0
N_EXPERTS = 64
TOP_K = 8
N_EXPERT_GROUPS = 8
GROUP_SIZE = N_EXPERTS // N_EXPERT_GROUPS
TOPK_GROUPS = 4
ROUTED_SCALE = 2.5
LN_EPS = 1e-5

LANES = 128
SUBLANES = 8
POOL_HALO = 16
VMEM_LIMIT = 48 * 1024 * 1024

POOL_TT = 512
QKV_TT = 512
ATTN_TQ = 512
CONV_TT = 512
GROUP_TG = 256
CHUNK_ALIGN = SUBLANES
SORT_CHUNK = 256
EXPERT_BM = 1024
GMM_X_SLOTS = 3
FULL_COPY = 64
FULL_COPIES = GROUP_TG // FULL_COPY
COPY_SIZES = tuple(range(CHUNK_ALIGN, FULL_COPY, CHUNK_ALIGN)) + (FULL_COPY,) * FULL_COPIES
N_COPY_CLASSES = len(COPY_SIZES)
PLAN_HALF = 2 * SUBLANES
PLAN_COUNTS = N_COPY_CLASSES
PLAN_TOTAL = N_COPY_CLASSES + 1
PLAN_ROWS = 2 * PLAN_HALF
PLAN_WORDS = PLAN_ROWS * LANES

U32 = jnp.uint32
HI_MASK = 0xFFFF0000

F32 = jnp.float32
BF16 = jnp.bfloat16
NEG_INF = float("-inf")


def _layer_norm(z, g, b):
    mu = jnp.mean(z, axis=-1, keepdims=True)
    zc = z - mu
    var = jnp.mean(zc * zc, axis=-1, keepdims=True)
    return zc * lax.rsqrt(var + LN_EPS) * g + b


def _params(*sem):
    return pltpu.CompilerParams(dimension_semantics=sem, vmem_limit_bytes=VMEM_LIMIT)


def _router_plumbing(router_w, router_bias, t, d, tt, tg, tile_index):
    groups = tt // tg
    const = lambda *ids: (0, 0)
    by_cols = lambda *ids: (0, tile_index(*ids))
    by_rows = lambda *ids: (tile_index(*ids), 0)
    operands = [router_w.T, router_bias.reshape(N_EXPERTS, 1)]
    in_specs = [pl.BlockSpec((N_EXPERTS, d), const), pl.BlockSpec((N_EXPERTS, 1), const)]
    out_specs = [pl.BlockSpec((TOP_K, tt), by_cols), pl.BlockSpec((TOP_K, tt), by_cols),
                 pl.BlockSpec((groups * N_EXPERTS, LANES), by_rows),
                 pl.BlockSpec((groups * PLAN_ROWS, LANES), by_rows)]
    out_shapes = [jax.ShapeDtypeStruct((TOP_K, t), jnp.int32), jax.ShapeDtypeStruct((TOP_K, t), F32),
                  jax.ShapeDtypeStruct((t // tg * N_EXPERTS, LANES), jnp.int32),
                  jax.ShapeDtypeStruct((t // tg * PLAN_ROWS, LANES), jnp.int32)]
    return operands, in_specs, out_specs, out_shapes


def _pool_kernel(x_ref, halo_ref, w_ref, scale_ref, g_ref, b_ref, wt_ref, rb_ref, o_ref, pos_ref, gate_ref,
                 n8_ref, plan_ref, buf_ref, *, alpha, tt, tg):
    i = pl.program_id(1)
    x = x_ref[...]
    d = x.shape[1]
    ch = d // len(POOL_WINDOWS)
    top = SUBLANES
    first = top + POOL_HALO
    n = first + tt
    for k in range(3):
        buf_ref[k, 0:top, :] = jnp.zeros((top, d), F32)
    buf_ref[0, top:first, :] = jnp.where(i > 0, halo_ref[...], 0.0)
    buf_ref[0, first:n, :] = x
    pos = i * tt + lax.broadcasted_iota(jnp.int32, (tt, 1), 0)
    ys = []
    for g, w in enumerate(POOL_WINDOWS):
        c0 = g * ch
        xg = x[:, c0:c0 + ch]
        src, span = 0, 1
        while span < w:
            dst = 1 if src != 1 else 2
            buf_ref[dst, top:n, c0:c0 + ch] = (buf_ref[src, top:n, c0:c0 + ch]
                                               + buf_ref[src, top - span:n - span, c0:c0 + ch])
            src, span = dst, 2 * span
        s = buf_ref[src, first:n, c0:c0 + ch]
        inv_cnt = 1.0 / jnp.minimum(pos + 1, w).astype(F32)
        pooled = s * inv_cnt - xg
        ys.append(jnp.dot(pooled.astype(BF16), w_ref[g], preferred_element_type=F32))
    y = jnp.concatenate(ys, axis=1) * scale_ref[...]
    o_ref[...] = _layer_norm(alpha * x + y, g_ref[...], b_ref[...])
    _route_tile(o_ref, (wt_ref, rb_ref), (pos_ref, gate_ref, n8_ref, plan_ref), tg)


def _pool_mixer(x2, seq, w_grp, scale, g, b, alpha, router_w, router_bias, tg):
    t, d = x2.shape
    tt = min(POOL_TT, seq)
    nt = seq // tt
    nb = t // seq
    row = lambda v: v.reshape(1, d)
    const2 = lambda bi, i: (0, 0)
    tile = lambda bi, i: bi * nt + i
    r_ops, r_in, r_out, r_shapes = _router_plumbing(router_w, router_bias, t, d, tt, tg, tile)
    return pl.pallas_call(
        functools.partial(_pool_kernel, alpha=alpha, tt=tt, tg=tg),
        grid=(nb, nt),
        in_specs=[
            pl.BlockSpec((tt, d), lambda bi, i: (bi * nt + i, 0)),
            pl.BlockSpec((POOL_HALO, d),
                         lambda bi, i: (jnp.maximum((bi * seq + i * tt) // POOL_HALO - 1, 0), 0)),
            pl.BlockSpec(w_grp.shape, lambda bi, i: (0, 0, 0)),
            pl.BlockSpec((1, d), const2),
            pl.BlockSpec((1, d), const2),
            pl.BlockSpec((1, d), const2),
        ] + r_in,
        out_specs=[pl.BlockSpec((tt, d), lambda bi, i: (bi * nt + i, 0))] + r_out,
        out_shape=[jax.ShapeDtypeStruct((t, d), F32)] + r_shapes,
        scratch_shapes=[pltpu.VMEM((3, SUBLANES + POOL_HALO + tt, d), F32)],
        compiler_params=_params("parallel", "parallel"),
        name="pool_mixer",
    )(x2, x2, w_grp.astype(BF16), row(scale), row(g), row(b), *r_ops)


def _qkv_kernel(x_ref, w_ref, b_ref, c_ref, s_ref, q_ref, k_ref, v_ref, *, nq, nk):
    y = jnp.dot(x_ref[...].astype(BF16), w_ref[...], preferred_element_type=F32) + b_ref[...]
    c = c_ref[...]
    s = s_ref[...]
    lane = lax.broadcasted_iota(jnp.int32, c.shape, 1)
    first_half = (lane % HEAD_DIM) < (HEAD_DIM // 2)

    def rope(v):
        partner = jnp.where(first_half, pltpu.roll(v, LANES - HEAD_DIM // 2, 1),
                            pltpu.roll(v, HEAD_DIM // 2, 1))
        return v * c + partner * s

    for j in range(nq // LANES):
        q_ref[:, j * LANES:(j + 1) * LANES] = (
            rope(y[:, j * LANES:(j + 1) * LANES]) * (HEAD_DIM ** -0.5)).astype(BF16)
    for j in range(nk // LANES):
        k_ref[:, j * LANES:(j + 1) * LANES] = rope(
            y[:, nq + j * LANES:nq + (j + 1) * LANES]).astype(BF16)
    v_ref[...] = y[:, nq + nk:].astype(BF16)


def _attn_kernel(sink_ref, q_ref, k_ref, kh_ref, v_ref, vh_ref, x_ref, wo_ref, bo_ref, g_ref, b_ref,
                 wt_ref, rb_ref, o_ref, pos_ref, gate_ref, n8_ref, plan_ref, *, alpha, tq, tg):
    i = pl.program_id(1)
    nqb = tq // WINDOW
    stack = GQA_GROUP * tq
    row_head = lax.broadcasted_iota(jnp.int32, (stack, 1), 0) // tq
    r = lax.broadcasted_iota(jnp.int32, (stack, 2 * WINDOW), 0) % tq
    c = lax.broadcasted_iota(jnp.int32, (stack, 2 * WINDOW), 1)
    dist = (r & (WINDOW - 1)) + WINDOW - c
    valid = (dist >= 0) & (dist < WINDOW) & ((c >= WINDOW) | (r >= WINDOW) | (i > 0))
    lane = lax.broadcasted_iota(jnp.int32, (tq, LANES), 1)
    low = lane < HEAD_DIM
    dn = (((1,), (1,)), ((), ()))
    outs = []
    for kv in range(N_KV_HEADS):
        sl = slice(kv * LANES, (kv + 1) * LANES)
        kk = jnp.concatenate([kh_ref[:, sl], k_ref[:, sl]], axis=0)
        vv = jnp.concatenate([vh_ref[:, sl], v_ref[:, sl]], axis=0)
        blocks = []
        for h in range(GQA_GROUP):
            c0 = (kv * GQA_GROUP + h - h % 2) * HEAD_DIM
            qp = q_ref[:, c0:c0 + LANES]
            keep = low if h % 2 == 0 else jnp.logical_not(low)
            qh = jnp.where(keep, qp, jnp.zeros_like(qp))
            sc_all = lax.dot_general(qh, kk, dn, preferred_element_type=F32)
            blocks += [sc_all[qb * WINDOW:(qb + 1) * WINDOW, qb * WINDOW:(qb + 2) * WINDOW]
                       for qb in range(nqb)]
        sc = jnp.where(valid, jnp.concatenate(blocks, axis=0), NEG_INF)
        sink = jnp.zeros((stack, 1), F32)
        for h in range(GQA_GROUP):
            sink = jnp.where(row_head == h, sink_ref[kv * GQA_GROUP + h], sink)
        m = jnp.maximum(jnp.max(sc, axis=1, keepdims=True), sink)
        pm = jnp.exp(sc - m)
        rden = 1.0 / (jnp.sum(pm, axis=1, keepdims=True) + jnp.exp(sink - m))
        pm = pm.astype(BF16)
        res = jnp.concatenate(
            [jnp.dot(pm[b * WINDOW:(b + 1) * WINDOW, :],
                     vv[(b % nqb) * WINDOW:(b % nqb + 2) * WINDOW, :], preferred_element_type=F32)
             for b in range(GQA_GROUP * nqb)], axis=0) * rden
        for p in range(GQA_GROUP // 2):
            outs.append(jnp.where(low, res[2 * p * tq:(2 * p + 1) * tq, :],
                                  res[(2 * p + 1) * tq:(2 * p + 2) * tq, :]))
    o_all = jnp.concatenate(outs, axis=1).astype(BF16)
    mix = jnp.dot(o_all, wo_ref[...], preferred_element_type=F32) + bo_ref[...]
    o_ref[...] = _layer_norm(alpha * x_ref[...] + mix, g_ref[...], b_ref[...])
    _route_tile(o_ref, (wt_ref, rb_ref), (pos_ref, gate_ref, n8_ref, plan_ref), tg)


def _dup_heads(w, n_heads):
    lead = w.shape[:-1]
    w = w.reshape(lead + (n_heads, 1, HEAD_DIM))
    w = jnp.broadcast_to(w, lead + (n_heads, 2, HEAD_DIM))
    return w.reshape(lead + (n_heads * 2 * HEAD_DIM,))


def _swa_mixer(x2, seq, wqkv, bqkv, sinks, wo, bo, g, b, alpha, router_w, router_bias, tg):
    t, d = x2.shape
    nb = t // seq
    nq = N_Q_HEADS * HEAD_DIM
    nkv = N_KV_HEADS * HEAD_DIM
    nk2 = 2 * nkv
    w_ext = jnp.concatenate([wqkv[:, :nq], _dup_heads(wqkv[:, nq:nq + nkv], N_KV_HEADS),
                             _dup_heads(wqkv[:, nq + nkv:], N_KV_HEADS)], axis=1).astype(BF16)
    b_ext = jnp.concatenate([bqkv[:nq], _dup_heads(bqkv[nq:nq + nkv], N_KV_HEADS),
                             _dup_heads(bqkv[nq + nkv:], N_KV_HEADS)]).reshape(1, -1)
    n_ext = nq + 2 * nk2

    pos = jnp.arange(seq, dtype=F32)
    inv_freq = ROPE_THETA ** (-jnp.arange(0, HEAD_DIM, 2, dtype=F32) / HEAD_DIM)
    ang = pos[:, None] * inv_freq[None, :]
    cos, sin = jnp.cos(ang), jnp.sin(ang)
    cos_t = jnp.concatenate([cos, cos, cos, cos], axis=1)
    sin_t = jnp.concatenate([-sin, sin, -sin, sin], axis=1)

    tt = min(QKV_TT, seq)
    nt = seq // tt
    const2 = lambda i: (0, 0)
    q, k2, v2 = pl.pallas_call(
        functools.partial(_qkv_kernel, nq=nq, nk=nk2),
        grid=(t // tt,),
        in_specs=[
            pl.BlockSpec((tt, d), lambda i: (i, 0)),
            pl.BlockSpec((d, n_ext), const2),
            pl.BlockSpec((1, n_ext), const2),
            pl.BlockSpec((tt, LANES), lambda i: (i % nt, 0)),
            pl.BlockSpec((tt, LANES), lambda i: (i % nt, 0)),
        ],
        out_specs=[pl.BlockSpec((tt, nq), lambda i: (i, 0)),
                   pl.BlockSpec((tt, nk2), lambda i: (i, 0)),
                   pl.BlockSpec((tt, nk2), lambda i: (i, 0))],
        out_shape=[jax.ShapeDtypeStruct((t, nq), BF16),
                   jax.ShapeDtypeStruct((t, nk2), BF16),
                   jax.ShapeDtypeStruct((t, nk2), BF16)],
        compiler_params=_params("parallel"),
        name="swa_qkv",
    )(x2, w_ext, b_ext, cos_t, sin_t)

    tq = min(ATTN_TQ, seq)
    ntq = seq // tq
    main = lambda bi, i: (bi * ntq + i, 0)
    halo = lambda bi, i: (jnp.maximum((bi * seq + i * tq) // WINDOW - 1, 0), 0)
    const = lambda bi, i: (0, 0)
    row = lambda v: v.reshape(1, d)
    r_ops, r_in, r_out, r_shapes = _router_plumbing(router_w, router_bias, t, d, tq, tg,
                                                    lambda bi, i: bi * ntq + i)
    return pl.pallas_call(
        functools.partial(_attn_kernel, alpha=alpha, tq=tq, tg=tg),
        grid=(nb, ntq),
        in_specs=[
            pl.BlockSpec(memory_space=pltpu.SMEM),
            pl.BlockSpec((tq, nq), main),
            pl.BlockSpec((tq, nk2), main),
            pl.BlockSpec((WINDOW, nk2), halo),
            pl.BlockSpec((tq, nk2), main),
            pl.BlockSpec((WINDOW, nk2), halo),
            pl.BlockSpec((tq, d), main),
            pl.BlockSpec((nq, d), const),
            pl.BlockSpec((1, d), const),
            pl.BlockSpec((1, d), const),
            pl.BlockSpec((1, d), const),
        ] + r_in,
        out_specs=[pl.BlockSpec((tq, d), main)] + r_out,
        out_shape=[jax.ShapeDtypeStruct((t, d), F32)] + r_shapes,
        compiler_params=_params("parallel", "parallel"),
        name="swa_attention",
    )(sinks, q, k2, k2, v2, v2, x2, wo.astype(BF16), row(bo), row(g), row(b), *r_ops)


def _conv_kernel(x_ref, win_ref, cw_ref, wout_ref, g_ref, b_ref, wt_ref, rb_ref, o_ref, pos_ref, gate_ref,
                 n8_ref, plan_ref, ubuf_ref, *, alpha, tt, tg):
    i = pl.program_id(1)
    d = x_ref.shape[1]

    @pl.when(i == 0)
    def _():
        ubuf_ref[0:SUBLANES, :] = jnp.zeros((SUBLANES, d), F32)

    x = x_ref[...]
    proj = jnp.dot(x.astype(BF16), win_ref[...], preferred_element_type=F32)
    gate_b = proj[:, :d]
    u = proj[:, d:2 * d] * proj[:, 2 * d:]
    ubuf_ref[SUBLANES:SUBLANES + tt, :] = u
    cw = cw_ref[...]
    conv = (cw[0:1, :] * ubuf_ref[SUBLANES - 2:SUBLANES - 2 + tt, :]
            + cw[1:2, :] * ubuf_ref[SUBLANES - 1:SUBLANES - 1 + tt, :]
            + cw[2:3, :] * u)
    ubuf_ref[0:SUBLANES, :] = u[tt - SUBLANES:tt, :]
    y = jnp.dot((gate_b * conv).astype(BF16), wout_ref[...], preferred_element_type=F32)
    o_ref[...] = _layer_norm(alpha * x + y, g_ref[...], b_ref[...])
    _route_tile(o_ref, (wt_ref, rb_ref), (pos_ref, gate_ref, n8_ref, plan_ref), tg)


def _conv_mixer(x2, seq, w_in, conv_w, w_out, g, b, alpha, router_w, router_bias, tg):
    t, d = x2.shape
    nb = t // seq
    tt = min(CONV_TT, seq)
    nt = seq // tt
    const = lambda bi, i: (0, 0)
    row = lambda v: v.reshape(1, d)
    r_ops, r_in, r_out, r_shapes = _router_plumbing(router_w, router_bias, t, d, tt, tg,
                                                    lambda bi, i: bi * nt + i)
    return pl.pallas_call(
        functools.partial(_conv_kernel, alpha=alpha, tt=tt, tg=tg),
        grid=(nb, nt),
        in_specs=[
            pl.BlockSpec((tt, d), lambda bi, i: (bi * nt + i, 0)),
            pl.BlockSpec((d, 3 * d), const),
            pl.BlockSpec(conv_w.shape, const),
            pl.BlockSpec((d, d), const),
            pl.BlockSpec((1, d), const),
            pl.BlockSpec((1, d), const),
        ] + r_in,
        out_specs=[pl.BlockSpec((tt, d), lambda bi, i: (bi * nt + i, 0))] + r_out,
        out_shape=[jax.ShapeDtypeStruct((t, d), F32)] + r_shapes,
        scratch_shapes=[pltpu.VMEM((SUBLANES + tt, d), F32)],
        compiler_params=_params("arbitrary", "arbitrary"),
        name="conv_mixer",
    )(x2, w_in.astype(BF16), conv_w, w_out.astype(BF16), row(g), row(b), *r_ops)


def _first_argmax(v, rows, n):
    m = jnp.max(v, axis=0, keepdims=True)
    first = jnp.min(jnp.where(v == m, rows, n), axis=0, keepdims=True)
    return m, first


def _route_tile(o_ref, router_refs, route_out_refs, tg):
    wt_ref, bias_ref = router_refs
    pos_ref, gate_ref, n8_ref, plan_ref = route_out_refs
    tt = o_ref.shape[0]
    x = o_ref[...]
    xh = x.astype(BF16)
    xl = (x - xh.astype(F32)).astype(BF16)
    wt = wt_ref[...]
    wh = wt.astype(BF16)
    wl = (wt - wh.astype(F32)).astype(BF16)
    dn = (((1,), (1,)), ((), ()))
    logits = (lax.dot_general(wh, xh, dn, preferred_element_type=F32)
              + (lax.dot_general(wh, xl, dn, preferred_element_type=F32)
                 + lax.dot_general(wl, xh, dn, preferred_element_type=F32)))
    scores = jax.nn.sigmoid(logits)
    sel = scores + bias_ref[...]

    rows8 = lax.broadcasted_iota(jnp.int32, (GROUP_SIZE, tt), 0)
    gsc = []
    for gi in range(N_EXPERT_GROUPS):
        blk = sel[gi * GROUP_SIZE:(gi + 1) * GROUP_SIZE, :]
        m1, f1 = _first_argmax(blk, rows8, GROUP_SIZE)
        m2 = jnp.max(jnp.where(rows8 == f1, NEG_INF, blk), axis=0, keepdims=True)
        gsc.append(m1 + m2)
    cur = jnp.concatenate(gsc, axis=0)
    rows_g = lax.broadcasted_iota(jnp.int32, (N_EXPERT_GROUPS, tt), 0)
    gmask = jnp.zeros((N_EXPERT_GROUPS, tt), F32)
    for _ in range(TOPK_GROUPS):
        _, f = _first_argmax(cur, rows_g, N_EXPERT_GROUPS)
        hit = rows_g == f
        gmask = jnp.where(hit, 1.0, gmask)
        cur = jnp.where(hit, NEG_INF, cur)
    emask = jnp.concatenate(
        [jnp.broadcast_to(gmask[gi:gi + 1, :], (GROUP_SIZE, tt)) for gi in range(N_EXPERT_GROUPS)],
        axis=0)

    rows_e = lax.broadcasted_iota(jnp.int32, (N_EXPERTS, tt), 0)
    cur = jnp.where(emask > 0.0, sel, NEG_INF)
    idxs, gates = [], []
    member = jnp.zeros((N_EXPERTS, tt), F32)
    for _ in range(TOP_K):
        _, f = _first_argmax(cur, rows_e, N_EXPERTS)
        hit = rows_e == f
        idxs.append(f)
        gates.append(jnp.sum(jnp.where(hit, scores, 0.0), axis=0, keepdims=True))
        member = jnp.where(hit, 1.0, member)
        cur = jnp.where(hit, NEG_INF, cur)
    gate = jnp.concatenate(gates, axis=0)
    gate_ref[...] = gate / jnp.sum(gate, axis=0, keepdims=True) * ROUTED_SCALE

    for s in range(tt // tg):
        cols = slice(s * tg, (s + 1) * tg)
        _plan_group(member[:, cols], [f[:, cols] for f in idxs], pos_ref.at[:, pl.ds(s * tg, tg)],
                    n8_ref.at[pl.ds(s * N_EXPERTS, N_EXPERTS), :],
                    plan_ref.at[pl.ds(s * PLAN_ROWS, PLAN_ROWS), :], tg)


def _plan_group(member, idxs, pos_ref, n8_ref, plan_ref, tt):
    rows_e = lax.broadcasted_iota(jnp.int32, (N_EXPERTS, tt), 0)
    ta = lax.broadcasted_iota(jnp.int32, (tt, tt), 0)
    tb = lax.broadcasted_iota(jnp.int32, (tt, tt), 1)
    earlier = (ta < tb).astype(BF16)
    before = jnp.dot(member.astype(BF16), earlier, preferred_element_type=F32)
    n = jnp.sum(member, axis=1, keepdims=True)
    n8 = jnp.floor((n + (CHUNK_ALIGN - 1)) * (1.0 / CHUNK_ALIGN)) * CHUNK_ALIGN
    n8_wide = jnp.broadcast_to(n8, (N_EXPERTS, LANES))
    ea = lax.broadcasted_iota(jnp.int32, (N_EXPERTS, N_EXPERTS), 0)
    eb = lax.broadcasted_iota(jnp.int32, (N_EXPERTS, N_EXPERTS), 1)
    lower = (eb < ea).astype(BF16)
    chunk_start = jnp.dot(lower, n8_wide.astype(BF16), preferred_element_type=F32)
    where_to = before + chunk_start[:, 0:1]
    for k in range(TOP_K):
        pk = jnp.sum(jnp.where(rows_e == idxs[k], where_to, 0.0), axis=0, keepdims=True)
        pos_ref[k:k + 1, :] = pk.astype(jnp.int32)
    n8_int = n8_wide.astype(jnp.int32)
    n8_ref[...] = n8_int

    lane = lax.broadcasted_iota(jnp.int32, (N_EXPERTS, LANES), 1)
    full, rest = _copy_split(n8_int)
    flag = jnp.zeros((N_EXPERTS, LANES), F32)
    before_copy = jnp.zeros((N_EXPERTS, LANES), jnp.int32)
    for j, size in enumerate(COPY_SIZES):
        if size < FULL_COPY:
            needs, skip = rest == size, full * FULL_COPY
        else:
            nth = j - (N_COPY_CLASSES - FULL_COPIES)
            needs, skip = full > nth, nth * FULL_COPY
        flag = jnp.where((lane == j) & needs, 1.0, flag)
        before_copy = jnp.where(lane == j, skip, before_copy)
    rank = jnp.dot(lower, flag.astype(BF16), preferred_element_type=F32)
    e_id = lax.broadcasted_iota(jnp.int32, (N_EXPERTS, LANES), 0)
    start_int = chunk_start.astype(jnp.int32)
    plan_ref[...] = jnp.zeros(plan_ref.shape, jnp.int32)
    for j in range(N_COPY_CLASSES):
        chosen = (rank[:, j:j + 1] == lane.astype(F32)) & (flag[:, j:j + 1] > 0.0)
        buf_row = start_int + before_copy[:, j:j + 1]
        plan_ref[j:j + 1, :] = jnp.sum(jnp.where(chosen, buf_row, 0), axis=0, keepdims=True)
        plan_ref[PLAN_HALF + j:PLAN_HALF + j + 1, :] = jnp.sum(jnp.where(chosen, e_id, 0), axis=0,
                                                              keepdims=True)
    plan_ref[PLAN_COUNTS:PLAN_COUNTS + 1, :] = jnp.sum(flag, axis=0, keepdims=True).astype(jnp.int32)
    plan_ref[PLAN_TOTAL:PLAN_TOTAL + 1, :] = jnp.sum(n8_int, axis=0, keepdims=True)


def _copy_split(n8):
    return lax.shift_right_logical(n8, FULL_COPY.bit_length() - 1), n8 & (FULL_COPY - 1)


def _pack_halves(v, exact=False):
    h = v.shape[1] // 2
    return _pack_pair(v[:, :h], v[:, h:], exact)


def _pack_pair(lo, hi, exact=False):
    if not exact:
        lo, hi = lo.astype(BF16).astype(F32), hi.astype(BF16).astype(F32)
    lo = lax.bitcast_convert_type(lo, U32)
    hi = lax.bitcast_convert_type(hi, U32)
    return (lo >> 16) | (hi & jnp.uint32(HI_MASK))


def _chunk_relative(pos, c):
    rel = pos - c * SORT_CHUNK
    inside = (rel >= 0) & (rel < SORT_CHUNK)
    return jnp.where(inside, rel, -1).astype(F32).astype(BF16)


def _unpack_halves(w):
    lo = lax.bitcast_convert_type(w << 16, F32).astype(BF16)
    hi = lax.bitcast_convert_type(w & jnp.uint32(HI_MASK), F32).astype(BF16)
    return lo, hi


def _start_group_copies(plan_ref, make_copy):
    for j, size in enumerate(COPY_SIZES):
        def body(k, carry):
            make_copy(pl.multiple_of(plan_ref[j * LANES + k], CHUNK_ALIGN),
                      pl.multiple_of(plan_ref[(PLAN_HALF + j) * LANES + k], CHUNK_ALIGN), size).start()
            return carry

        lax.fori_loop(0, plan_ref[PLAN_COUNTS * LANES + j], body, 0)


def _wait_group_rows(total_rows, ns, make_copy):
    for j in range(ns.bit_length()):
        size = CHUNK_ALIGN << j
        if size > ns:
            break

        @pl.when((total_rows & size) != 0)
        def _():
            make_copy(0, 0, size).wait()


def _dispatch_kernel(tot_ref, zrow_ref, plan_ref, pos_ref, x_ref, xs_ref, sbuf_ref, zbuf_ref,
                     onehot_ref, sem, zsem, *, tg, ns, bm):
    g = pl.program_id(0)
    ng = pl.num_programs(0)
    slot = g % 2

    @pl.when(g == 0)
    def _():
        zbuf_ref[...] = jnp.zeros_like(zbuf_ref)
        row = pl.multiple_of(zrow_ref[0], CHUNK_ALIGN)
        tail = pltpu.make_async_copy(zbuf_ref, xs_ref.at[pl.ds(row, bm), :], zsem)
        tail.start()
        tail.wait()

    def copy_from(sl):
        def make_copy(src, dst, size):
            return pltpu.make_async_copy(sbuf_ref.at[sl, pl.ds(src, size), :],
                                         xs_ref.at[pl.ds(dst, size), :], sem.at[sl])
        return make_copy

    @pl.when(g >= 2)
    def _():
        _wait_group_rows(tot_ref[g - 2], ns, copy_from(slot))

    xb = x_ref[...].astype(BF16)
    pos = pos_ref[...]
    local_rows = lax.broadcasted_iota(jnp.int32, (SORT_CHUNK, tg), 0).astype(F32).astype(BF16)
    one = jnp.ones((SORT_CHUNK, tg), BF16)
    zero = jnp.zeros((SORT_CHUNK, tg), BF16)
    for c in range(ns // SORT_CHUNK):
        rel = _chunk_relative(pos, c)
        hit = local_rows == rel[0:1, :]
        for k in range(1, TOP_K):
            hit = jnp.logical_or(hit, local_rows == rel[k:k + 1, :])
        onehot_ref[c * SORT_CHUNK:(c + 1) * SORT_CHUNK, :] = jnp.where(hit, one, zero)
    onehot = onehot_ref[...]
    half = x_ref.shape[1] // 2
    for n in range(half // SORT_CHUNK):
        lo = jnp.dot(onehot, xb[:, n * SORT_CHUNK:(n + 1) * SORT_CHUNK], preferred_element_type=F32)
        hi = jnp.dot(onehot, xb[:, half + n * SORT_CHUNK:half + (n + 1) * SORT_CHUNK],
                     preferred_element_type=F32)
        sbuf_ref[slot, :, n * SORT_CHUNK:(n + 1) * SORT_CHUNK] = _pack_pair(lo, hi, exact=True)

    _start_group_copies(plan_ref, copy_from(slot))

    @pl.when(g == ng - 1)
    def _():
        @pl.when(g >= 1)
        def _():
            _wait_group_rows(tot_ref[g - 1], ns, copy_from(1 - slot))
        _wait_group_rows(tot_ref[g], ns, copy_from(slot))


def _dispatch(x1, pos, plan, tot, zrow, n_rows, tg, ns, bm):
    t, d = x1.shape
    grid_spec = pltpu.PrefetchScalarGridSpec(
        num_scalar_prefetch=2,
        grid=(t // tg,),
        in_specs=[
            pl.BlockSpec((PLAN_WORDS,), lambda i, *_: (i,), memory_space=pltpu.SMEM),
            pl.BlockSpec((TOP_K, tg), lambda i, *_: (0, i)),
            pl.BlockSpec((tg, d), lambda i, *_: (i, 0)),
        ],
        out_specs=pl.BlockSpec(memory_space=pl.ANY),
        scratch_shapes=[pltpu.VMEM((2, ns, d // 2), U32), pltpu.VMEM((bm, d // 2), U32),
                        pltpu.VMEM((ns, tg), BF16),
                        pltpu.SemaphoreType.DMA((2,)), pltpu.SemaphoreType.DMA],
    )
    return pl.pallas_call(
        functools.partial(_dispatch_kernel, tg=tg, ns=ns, bm=bm),
        grid_spec=grid_spec,
        out_shape=jax.ShapeDtypeStruct((n_rows, d // 2), U32),
        compiler_params=_params("arbitrary"),
        name="moe_dispatch",
    )(tot, zrow, plan, pos, x1)


def _gmm_kernel(nq_ref, brow_ref, bval_ref, bord_ref, bfirst_ref, aexp_ref,
                xs_ref, w1_ref, w3_ref, w2_ref, y_ref,
                xbuf_ref, ybuf_ref, w1s_ref, w3s_ref, w2s_ref, w13_ref, w2b_ref, xsem, ysem, wsem,
                *, layer, bm):
    nq = nq_ref[0]
    na = nq_ref[1]
    f = w1s_ref.shape[2]
    half = xbuf_ref.shape[2]
    nx = xbuf_ref.shape[0]

    def x_copy(q, slot):
        row = pl.multiple_of(brow_ref[q], CHUNK_ALIGN)
        return pltpu.make_async_copy(xs_ref.at[pl.ds(row, bm), :], xbuf_ref.at[slot], xsem.at[slot])

    def w_copies(j, slot):
        e = aexp_ref[j]
        return (pltpu.make_async_copy(w1_ref.at[layer, e], w1s_ref.at[slot], wsem.at[slot]),
                pltpu.make_async_copy(w3_ref.at[layer, e], w3s_ref.at[slot], wsem.at[slot]),
                pltpu.make_async_copy(w2_ref.at[layer, e], w2s_ref.at[slot], wsem.at[slot]))

    def y_copy(slot, row, off, size):
        return pltpu.make_async_copy(ybuf_ref.at[slot, pl.ds(off, size), :],
                                     y_ref.at[pl.ds(pl.multiple_of(row + off, CHUNK_ALIGN), size), :],
                                     ysem.at[slot])

    def for_each_y_copy(slot, row, valid, fn):
        @pl.when(valid == bm)
        def _():
            fn(y_copy(slot, row, 0, bm))

        @pl.when(valid < bm)
        def _():
            for j in range((bm // CHUNK_ALIGN).bit_length() - 1):
                size = CHUNK_ALIGN << j

                @pl.when((valid & size) != 0)
                def _():
                    fn(y_copy(slot, row, pl.multiple_of(valid & (size - 1), CHUNK_ALIGN), size))

    for c in w_copies(0, 0):
        c.start()
    for q0 in range(nx - 1):
        @pl.when(q0 < nq)
        def _():
            x_copy(q0, q0).start()

    def body(q, carry):
        xslot = q % nx
        yslot = q % 2

        @pl.when(q + (nx - 1) < nq)
        def _():
            x_copy(q + (nx - 1), (q + (nx - 1)) % nx).start()

        @pl.when(bfirst_ref[q] == 1)
        def _():
            j = bord_ref[q]
            ws = j % 2
            for c in w_copies(j, ws):
                c.wait()

            @pl.when(j + 1 < na)
            def _():
                for c in w_copies(j + 1, 1 - ws):
                    c.start()
            w13_ref[:, :f] = w1s_ref[ws].astype(BF16)
            w13_ref[:, f:] = w3s_ref[ws].astype(BF16)
            w2b_ref[...] = w2s_ref[ws].astype(BF16)

        x_copy(q, xslot).wait()

        @pl.when(q >= 2)
        def _():
            for_each_y_copy(yslot, 0, bval_ref[jnp.maximum(q - 2, 0)], lambda c: c.wait())

        def swiglu(rows):
            x_lo, x_hi = _unpack_halves(xbuf_ref[xslot, 0:rows, :])
            h13 = (jnp.dot(x_lo, w13_ref[:half, :], preferred_element_type=F32)
                   + jnp.dot(x_hi, w13_ref[half:, :], preferred_element_type=F32))
            h = jax.nn.silu(h13[:, :f]) * h13[:, f:]
            ybuf_ref[yslot, 0:rows, :] = _pack_halves(
                jnp.dot(h.astype(BF16), w2b_ref[...], preferred_element_type=F32))

        @pl.when(bval_ref[q] > bm // 2)
        def _():
            swiglu(bm)

        @pl.when(bval_ref[q] <= bm // 2)
        def _():
            swiglu(bm // 2)
        for_each_y_copy(yslot, brow_ref[q], bval_ref[q], lambda c: c.start())
        return carry

    lax.fori_loop(0, nq, body, 0)

    @pl.when(nq >= 2)
    def _():
        for_each_y_copy(nq % 2, 0, bval_ref[jnp.maximum(nq - 2, 0)], lambda c: c.wait())
    for_each_y_copy((nq - 1) % 2, 0, bval_ref[nq - 1], lambda c: c.wait())


def _gmm(xs, tables, w1, w3, w2, layer, bm):
    n_rows, half = xs.shape
    d = 2 * half
    f = w1.shape[-1]
    any_spec = pl.BlockSpec(memory_space=pl.ANY)
    grid_spec = pltpu.PrefetchScalarGridSpec(
        num_scalar_prefetch=len(tables),
        grid=(1,),
        in_specs=[any_spec, any_spec, any_spec, any_spec],
        out_specs=any_spec,
        scratch_shapes=[pltpu.VMEM((GMM_X_SLOTS, bm, half), U32), pltpu.VMEM((2, bm, half), U32),
                        pltpu.VMEM((2, d, f), F32), pltpu.VMEM((2, d, f), F32), pltpu.VMEM((2, f, d), F32),
                        pltpu.VMEM((d, 2 * f), BF16), pltpu.VMEM((f, d), BF16),
                        pltpu.SemaphoreType.DMA((GMM_X_SLOTS,)), pltpu.SemaphoreType.DMA((2,)),
                        pltpu.SemaphoreType.DMA((2,))],
    )
    return pl.pallas_call(
        functools.partial(_gmm_kernel, layer=layer, bm=bm),
        grid_spec=grid_spec,
        out_shape=jax.ShapeDtypeStruct((n_rows, half), U32),
        compiler_params=_params("arbitrary"),
        name="moe_experts",
    )(*tables, xs, w1, w3, w2)


def _combine_kernel(tot_ref, plan_ref, plan_next_ref, yb_ref, pos_ref, gate_ref, x_ref, ws13_ref, ws2_ref,
                    g_ref, b_ref, o_ref, ybuf_ref, wgt_ref, ylo_ref, yhi_ref, sem, *, alpha, tg, ns):
    g = pl.program_id(0)
    ng = pl.num_programs(0)
    slot = g % 2

    def copy_into(sl):
        def make_copy(buf_row, hbm_row, size):
            return pltpu.make_async_copy(yb_ref.at[pl.ds(hbm_row, size), :],
                                         ybuf_ref.at[sl, pl.ds(buf_row, size), :], sem.at[sl])
        return make_copy

    @pl.when(g == 0)
    def _():
        ybuf_ref[...] = jnp.zeros_like(ybuf_ref)
        _start_group_copies(plan_ref, copy_into(0))

    @pl.when(g + 1 < ng)
    def _():
        _start_group_copies(plan_next_ref, copy_into(1 - slot))

    x = x_ref[...]
    f = ws2_ref.shape[0]
    h13 = jnp.dot(x.astype(BF16), ws13_ref[...], preferred_element_type=F32)
    hs = jax.nn.silu(h13[:, :f]) * h13[:, f:]
    shared = jnp.dot(hs.astype(BF16), ws2_ref[...], preferred_element_type=F32)

    _wait_group_rows(tot_ref[g], ns, copy_into(slot))

    pos = pos_ref[...]
    gate_b = gate_ref[...].astype(BF16)
    local_rows = lax.broadcasted_iota(jnp.int32, (SORT_CHUNK, tg), 0).astype(F32).astype(BF16)
    for c in range(ns // SORT_CHUNK):
        rel = _chunk_relative(pos, c)
        wgt = jnp.zeros((SORT_CHUNK, tg), BF16)
        for k in range(TOP_K):
            wgt = jnp.where(local_rows == rel[k:k + 1, :], gate_b[k:k + 1, :], wgt)
        wgt_ref[c * SORT_CHUNK:(c + 1) * SORT_CHUNK, :] = wgt
        y_lo, y_hi = _unpack_halves(ybuf_ref[slot, c * SORT_CHUNK:(c + 1) * SORT_CHUNK, :])
        ylo_ref[c * SORT_CHUNK:(c + 1) * SORT_CHUNK, :] = y_lo
        yhi_ref[c * SORT_CHUNK:(c + 1) * SORT_CHUNK, :] = y_hi
    wgt_all = wgt_ref[...]
    dn = (((0,), (0,)), ((), ()))
    routed = jnp.concatenate([lax.dot_general(wgt_all, ylo_ref[...], dn, preferred_element_type=F32),
                              lax.dot_general(wgt_all, yhi_ref[...], dn, preferred_element_type=F32)],
                             axis=1)
    o_ref[...] = _layer_norm(alpha * x + (routed + shared), g_ref[...], b_ref[...])


def _combine(yb, pos_kt, gate_kt, plan, tot, x1, ws1, ws3, ws2, g, b, alpha, tg, ns):
    t, d = x1.shape
    ng = t // tg
    f = ws1.shape[1]
    ws13 = jnp.concatenate([ws1, ws3], axis=1).astype(BF16)
    const = lambda i, *_: (0, 0)
    tile = lambda i, *_: (i, 0)
    row = lambda v: v.reshape(1, d)
    grid_spec = pltpu.PrefetchScalarGridSpec(
        num_scalar_prefetch=1,
        grid=(ng,),
        in_specs=[
            pl.BlockSpec((PLAN_WORDS,), lambda i, *_: (i,), memory_space=pltpu.SMEM),
            pl.BlockSpec((PLAN_WORDS,), lambda i, *_: (jnp.minimum(i + 1, ng - 1),),
                         memory_space=pltpu.SMEM),
            pl.BlockSpec(memory_space=pl.ANY),
            pl.BlockSpec((TOP_K, tg), lambda i, *_: (0, i)),
            pl.BlockSpec((TOP_K, tg), lambda i, *_: (0, i)),
            pl.BlockSpec((tg, d), tile),
            pl.BlockSpec((d, 2 * f), const),
            pl.BlockSpec((f, d), const),
            pl.BlockSpec((1, d), const),
            pl.BlockSpec((1, d), const),
        ],
        out_specs=pl.BlockSpec((tg, d), tile),
        scratch_shapes=[pltpu.VMEM((2, ns, d // 2), U32), pltpu.VMEM((ns, tg), BF16),
                        pltpu.VMEM((ns, d // 2), BF16), pltpu.VMEM((ns, d // 2), BF16),
                        pltpu.SemaphoreType.DMA((2,))],
    )
    return pl.pallas_call(
        functools.partial(_combine_kernel, alpha=alpha, tg=tg, ns=ns),
        grid_spec=grid_spec,
        out_shape=jax.ShapeDtypeStruct((t, d), F32),
        compiler_params=_params("arbitrary"),
        name="moe_combine",
    )(tot, plan, plan, yb, pos_kt, gate_kt, x1, ws13, ws2.astype(BF16), row(g), row(b))


def _moe_layer(x1, routing, tg, layer, w1, w3, w2, ws1, ws3, ws2, g, b, alpha):
    t, d = x1.shape
    ng = t // tg
    bm = EXPERT_BM
    ns = -(-(tg * TOP_K + N_EXPERTS * (CHUNK_ALIGN - 1)) // SORT_CHUNK) * SORT_CHUNK
    pos, gate, n8_wide, plan_wide = routing
    plan3 = plan_wide.reshape(ng, PLAN_ROWS, LANES)
    tot = plan3[:, PLAN_TOTAL, 0]

    n8 = n8_wide[:, 0].reshape(ng, N_EXPERTS)
    seg = jnp.sum(n8, axis=0)
    pend = jnp.cumsum(seg)
    pstart = pend - seg
    gdest = pstart[None, :] + jnp.cumsum(n8, axis=0) - n8
    max_rows = ng * (tg * TOP_K + N_EXPERTS * (CHUNK_ALIGN - 1))
    n_rows = max_rows + bm

    experts = jnp.arange(N_EXPERTS, dtype=jnp.int32)
    nb = (seg + bm - 1) // bm
    cb = jnp.cumsum(nb)
    blk = jnp.arange(max_rows // bm + N_EXPERTS, dtype=jnp.int32)
    blk_expert = jnp.minimum(jnp.sum((cb[None, :] <= blk[:, None]).astype(jnp.int32), axis=1), N_EXPERTS - 1)
    mine = blk_expert[:, None] == experts[None, :]
    of_expert = lambda v: jnp.sum(jnp.where(mine, v[None, :], 0), axis=1)
    local = blk - of_expert(cb - nb)
    active = seg > 0
    ordinal = jnp.cumsum(active.astype(jnp.int32)) - 1
    active_experts = jnp.sum(jnp.where((ordinal[None, :] == experts[:, None]) & active[None, :],
                                       experts[None, :], 0), axis=1)
    tables = [jnp.stack([cb[-1], jnp.sum(active.astype(jnp.int32))]),
              of_expert(pstart) + local * bm,
              jnp.clip(of_expert(seg) - local * bm, 0, bm),
              of_expert(ordinal),
              (local == 0).astype(jnp.int32),
              active_experts]
    tables = [a.astype(jnp.int32) for a in tables]
    zrow = pend[-1:].astype(jnp.int32)

    lists = plan3[:, PLAN_HALF:PLAN_HALF + N_COPY_CLASSES, :N_EXPERTS]
    full, _ = _copy_split(n8)
    nth = jnp.arange(FULL_COPIES, dtype=jnp.int32) * FULL_COPY
    skip = jnp.concatenate([jnp.broadcast_to((full * FULL_COPY)[:, None, :],
                                             (ng, N_COPY_CLASSES - FULL_COPIES, N_EXPERTS)),
                            jnp.broadcast_to(nth[None, :, None], (ng, FULL_COPIES, N_EXPERTS))], axis=1)
    pick = lists[..., None] == experts
    dst = jnp.sum(jnp.where(pick, (gdest[:, None, :] + skip)[:, :, None, :], 0), axis=-1)
    dst = jnp.pad(dst, ((0, 0), (0, PLAN_HALF - N_COPY_CLASSES), (0, LANES - N_EXPERTS)))
    plan = jnp.concatenate([plan3[:, :PLAN_HALF], dst], axis=1).astype(jnp.int32).reshape(-1)

    xs = _dispatch(x1, pos, plan, tot, zrow, n_rows, tg, ns, bm)
    yb = _gmm(xs, tables, w1, w3, w2, layer, bm)
    return _combine(yb, pos, gate, plan, tot, x1, ws1, ws3, ws2, g, b, alpha, tg, ns)


def kernel(x, ln_gain, ln_bias, pool_w, pool_scale, attn_wqkv, attn_bqkv, attn_sinks, attn_wo, attn_bo,
           conv_w_in, conv_w, conv_w_out, router_w, router_bias, expert_w1, expert_w3, expert_w2,
           shared_w1, shared_w3, shared_w2):
    bsz, seq, d = x.shape
    depth = ln_gain.shape[0]
    alpha = float((2 * depth) ** 0.25)
    h = x.reshape(bsz * seq, d)
    tg = min(GROUP_TG, seq)
    for i in range(depth):
        kind = i % N_MIXERS
        j = i // N_MIXERS
        g1, b1 = ln_gain[i, 0], ln_bias[i, 0]
        route = (router_w[i], router_bias[i], tg)
        if kind == 0:
            h, *routing = _pool_mixer(h, seq, pool_w[j], pool_scale[j], g1, b1, alpha, *route)
        elif kind == 1:
            h, *routing = _swa_mixer(h, seq, attn_wqkv[j], attn_bqkv[j], attn_sinks[j], attn_wo[j],
                                     attn_bo[j], g1, b1, alpha, *route)
        else:
            h, *routing = _conv_mixer(h, seq, conv_w_in[j], conv_w[j], conv_w_out[j], g1, b1, alpha, *route)
        h = _moe_layer(h, routing, tg, i, expert_w1, expert_w3, expert_w2,
                       shared_w1[i], shared_w3[i], shared_w2[i], ln_gain[i, 1], ln_bias[i, 1], alpha)
    return h.reshape(bsz, seq, d)
```

```python
import functools

import jax
import jax.numpy as jnp
from jax import lax
from jax.experimental import pallas as pl
from jax.experimental.pallas import tpu as pltpu

POOL_WINDOWS = (2, 4, 8, 16)
N_MIXERS = 3
HEAD_DIM = 64
N_Q_HEADS = 16
N_KV_HEADS = 4
GQA_GROUP = N_Q_HEADS // N_KV_HEADS
WINDOW = 128
ROPE_THETA = 10000.0
N_EXPERTS = 64
TOP_K = 8
N_EXPERT_GROUPS = 8
GROUP_SIZE = N_EXPERTS // N_EXPERT_GROUPS
TOPK_GROUPS = 4
ROUTED_SCALE = 2.5
LN_EPS = 1e-5

LANES = 128
SUBLANES = 8
POOL_HALO = 16
VMEM_LIMIT = 48 * 1024 * 1024

POOL_TT = 1024
QKV_TT = 512
ATTN_TQ = 512
CONV_TT = 512
GROUP_TG = 256
CHUNK_ALIGN = SUBLANES
SORT_CHUNK = 256
EXPERT_BM = 1024
GMM_X_SLOTS = 3
FULL_COPY = 64
FULL_COPIES = GROUP_TG // FULL_COPY
COPY_SIZES = tuple(range(CHUNK_ALIGN, FULL_COPY, CHUNK_ALIGN)) + (FULL_COPY,) * FULL_COPIES
N_COPY_CLASSES = len(COPY_SIZES)
PLAN_HALF = 2 * SUBLANES
PLAN_COUNTS = N_COPY_CLASSES
PLAN_TOTAL = N_COPY_CLASSES + 1
PLAN_ROWS = 2 * PLAN_HALF
PLAN_WORDS = PLAN_ROWS * LANES

U32 = jnp.uint32
HI_MASK = 0xFFFF0000

F32 = jnp.float32
BF16 = jnp.bfloat16
NEG_INF = float("-inf")


def _layer_norm(z, g, b):
    mu = jnp.mean(z, axis=-1, keepdims=True)
    zc = z - mu
    var = jnp.mean(zc * zc, axis=-1, keepdims=True)
    return zc * lax.rsqrt(var + LN_EPS) * g + b


def _params(*sem):
    return pltpu.CompilerParams(dimension_semantics=sem, vmem_limit_bytes=VMEM_LIMIT)


def _router_plumbing(router_w, router_bias, t, d, tt, tg, tile_index):
    groups = tt // tg
    const = lambda *ids: (0, 0)
    by_cols = lambda *ids: (0, tile_index(*ids))
    by_rows = lambda *ids: (tile_index(*ids), 0)
    operands = [router_w.T, router_bias.reshape(N_EXPERTS, 1)]
    in_specs = [pl.BlockSpec((N_EXPERTS, d), const), pl.BlockSpec((N_EXPERTS, 1), const)]
    out_specs = [pl.BlockSpec((TOP_K, tt), by_cols), pl.BlockSpec((TOP_K, tt), by_cols),
                 pl.BlockSpec((groups * N_EXPERTS, LANES), by_rows),
                 pl.BlockSpec((groups * PLAN_ROWS, LANES), by_rows)]
    out_shapes = [jax.ShapeDtypeStruct((TOP_K, t), jnp.int32), jax.ShapeDtypeStruct((TOP_K, t), F32),
                  jax.ShapeDtypeStruct((t // tg * N_EXPERTS, LANES), jnp.int32),
                  jax.ShapeDtypeStruct((t // tg * PLAN_ROWS, LANES), jnp.int32)]
    return operands, in_specs, out_specs, out_shapes


def _pool_kernel(x_ref, halo_ref, w_ref, scale_ref, g_ref, b_ref, wt_ref, rb_ref, o_ref, pos_ref, gate_ref,
                 n8_ref, plan_ref, buf_ref, *, alpha, tt, tg):
    i = pl.program_id(1)
    x = x_ref[...]
    d = x.shape[1]
    ch = d // len(POOL_WINDOWS)
    top = SUBLANES
    first = top + POOL_HALO
    n = first + tt
    for k in range(3):
        buf_ref[k, 0:top, :] = jnp.zeros((top, d), F32)
    buf_ref[0, top:first, :] = jnp.where(i > 0, halo_ref[...], 0.0)
    buf_ref[0, first:n, :] = x
    pos = i * tt + lax.broadcasted_iota(jnp.int32, (tt, 1), 0)
    ys = []
    for g, w in enumerate(POOL_WINDOWS):
        c0 = g * ch
        xg = x[:, c0:c0 + ch]
        src, span = 0, 1
        while span < w:
            dst = 1 if src != 1 else 2
            buf_ref[dst, top:n, c0:c0 + ch] = (buf_ref[src, top:n, c0:c0 + ch]
                                               + buf_ref[src, top - span:n - span, c0:c0 + ch])
            src, span = dst, 2 * span
        s = buf_ref[src, first:n, c0:c0 + ch]
        inv_cnt = 1.0 / jnp.minimum(pos + 1, w).astype(F32)
        pooled = s * inv_cnt - xg
        ys.append(jnp.dot(pooled.astype(BF16), w_ref[g], preferred_element_type=F32))
    y = jnp.concatenate(ys, axis=1) * scale_ref[...]
    o_ref[...] = _layer_norm(alpha * x + y, g_ref[...], b_ref[...])
    _route_tile(o_ref, (wt_ref, rb_ref), (pos_ref, gate_ref, n8_ref, plan_ref), tg)


def _pool_mixer(x2, seq, w_grp, scale, g, b, alpha, router_w, router_bias, tg):
    t, d = x2.shape
    tt = min(POOL_TT, seq)
    nt = seq // tt
    nb = t // seq
    row = lambda v: v.reshape(1, d)
    const2 = lambda bi, i: (0, 0)
    tile = lambda bi, i: bi * nt + i
    r_ops, r_in, r_out, r_shapes = _router_plumbing(router_w, router_bias, t, d, tt, tg, tile)
    return pl.pallas_call(
        functools.partial(_pool_kernel, alpha=alpha, tt=tt, tg=tg),
        grid=(nb, nt),
        in_specs=[
            pl.BlockSpec((tt, d), lambda bi, i: (bi * nt + i, 0)),
            pl.BlockSpec((POOL_HALO, d),
                         lambda bi, i: (jnp.maximum((bi * seq + i * tt) // POOL_HALO - 1, 0), 0)),
            pl.BlockSpec(w_grp.shape, lambda bi, i: (0, 0, 0)),
            pl.BlockSpec((1, d), const2),
            pl.BlockSpec((1, d), const2),
            pl.BlockSpec((1, d), const2),
        ] + r_in,
        out_specs=[pl.BlockSpec((tt, d), lambda bi, i: (bi * nt + i, 0))] + r_out,
        out_shape=[jax.ShapeDtypeStruct((t, d), F32)] + r_shapes,
        scratch_shapes=[pltpu.VMEM((3, SUBLANES + POOL_HALO + tt, d), F32)],
        compiler_params=_params("parallel", "parallel"),
        name="pool_mixer",
    )(x2, x2, w_grp.astype(BF16), row(scale), row(g), row(b), *r_ops)


def _qkv_kernel(x_ref, w_ref, b_ref, c_ref, s_ref, q_ref, k_ref, v_ref, *, nq, nk):
    y = jnp.dot(x_ref[...].astype(BF16), w_ref[...], preferred_element_type=F32) + b_ref[...]
    c = c_ref[...]
    s = s_ref[...]
    lane = lax.broadcasted_iota(jnp.int32, c.shape, 1)
    first_half = (lane % HEAD_DIM) < (HEAD_DIM // 2)

    def rope(v):
        partner = jnp.where(first_half, pltpu.roll(v, LANES - HEAD_DIM // 2, 1),
                            pltpu.roll(v, HEAD_DIM // 2, 1))
        return v * c + partner * s

    for j in range(nq // LANES):
        q_ref[:, j * LANES:(j + 1) * LANES] = (
            rope(y[:, j * LANES:(j + 1) * LANES]) * (HEAD_DIM ** -0.5)).astype(BF16)
    for j in range(nk // LANES):
        k_ref[:, j * LANES:(j + 1) * LANES] = rope(
            y[:, nq + j * LANES:nq + (j + 1) * LANES]).astype(BF16)
    v_ref[...] = y[:, nq + nk:].astype(BF16)


def _attn_kernel(sink_ref, q_ref, k_ref, kh_ref, v_ref, vh_ref, x_ref, wo_ref, bo_ref, g_ref, b_ref,
                 wt_ref, rb_ref, o_ref, pos_ref, gate_ref, n8_ref, plan_ref, *, alpha, tq, tg):
    i = pl.program_id(1)
    nqb = tq // WINDOW
    stack = GQA_GROUP * tq
    row_head = lax.broadcasted_iota(jnp.int32, (stack, 1), 0) // tq
    r = lax.broadcasted_iota(jnp.int32, (stack, 2 * WINDOW), 0) % tq
    c = lax.broadcasted_iota(jnp.int32, (stack, 2 * WINDOW), 1)
    dist = (r & (WINDOW - 1)) + WINDOW - c
    valid = (dist >= 0) & (dist < WINDOW) & ((c >= WINDOW) | (r >= WINDOW) | (i > 0))
    lane = lax.broadcasted_iota(jnp.int32, (tq, LANES), 1)
    low = lane < HEAD_DIM
    dn = (((1,), (1,)), ((), ()))
    outs = []
    for kv in range(N_KV_HEADS):
        sl = slice(kv * LANES, (kv + 1) * LANES)
        kk = jnp.concatenate([kh_ref[:, sl], k_ref[:, sl]], axis=0)
        vv = jnp.concatenate([vh_ref[:, sl], v_ref[:, sl]], axis=0)
        blocks = []
        for h in range(GQA_GROUP):
            c0 = (kv * GQA_GROUP + h - h % 2) * HEAD_DIM
            qp = q_ref[:, c0:c0 + LANES]
            keep = low if h % 2 == 0 else jnp.logical_not(low)
            qh = jnp.where(keep, qp, jnp.zeros_like(qp))
            sc_all = lax.dot_general(qh, kk, dn, preferred_element_type=F32)
            blocks += [sc_all[qb * WINDOW:(qb + 1) * WINDOW, qb * WINDOW:(qb + 2) * WINDOW]
                       for qb in range(nqb)]
        sc = jnp.where(valid, jnp.concatenate(blocks, axis=0), NEG_INF)
        sink = jnp.zeros((stack, 1), F32)
        for h in range(GQA_GROUP):
            sink = jnp.where(row_head == h, sink_ref[kv * GQA_GROUP + h], sink)
        m = jnp.maximum(jnp.max(sc, axis=1, keepdims=True), sink)
        pm = jnp.exp(sc - m)
        rden = 1.0 / (jnp.sum(pm, axis=1, keepdims=True) + jnp.exp(sink - m))
        pm = pm.astype(BF16)
        res = jnp.concatenate(
            [jnp.dot(pm[b * WINDOW:(b + 1) * WINDOW, :],
                     vv[(b % nqb) * WINDOW:(b % nqb + 2) * WINDOW, :], preferred_element_type=F32)
             for b in range(GQA_GROUP * nqb)], axis=0) * rden
        for p in range(GQA_GROUP // 2):
            outs.append(jnp.where(low, res[2 * p * tq:(2 * p + 1) * tq, :],
                                  res[(2 * p + 1) * tq:(2 * p + 2) * tq, :]))
    o_all = jnp.concatenate(outs, axis=1).astype(BF16)
    mix = jnp.dot(o_all, wo_ref[...], preferred_element_type=F32) + bo_ref[...]
    o_ref[...] = _layer_norm(alpha * x_ref[...] + mix, g_ref[...], b_ref[...])
    _route_tile(o_ref, (wt_ref, rb_ref), (pos_ref, gate_ref, n8_ref, plan_ref), tg)


def _dup_heads(w, n_heads):
    lead = w.shape[:-1]
    w = w.reshape(lead + (n_heads, 1, HEAD_DIM))
    w = jnp.broadcast_to(w, lead + (n_heads, 2, HEAD_DIM))
    return w.reshape(lead + (n_heads * 2 * HEAD_DIM,))


def _swa_mixer(x2, seq, wqkv, bqkv, sinks, wo, bo, g, b, alpha, router_w, router_bias, tg):
    t, d = x2.shape
    nb = t // seq
    nq = N_Q_HEADS * HEAD_DIM
    nkv = N_KV_HEADS * HEAD_DIM
    nk2 = 2 * nkv
    w_ext = jnp.concatenate([wqkv[:, :nq], _dup_heads(wqkv[:, nq:nq + nkv], N_KV_HEADS),
                             _dup_heads(wqkv[:, nq + nkv:], N_KV_HEADS)], axis=1).astype(BF16)
    b_ext = jnp.concatenate([bqkv[:nq], _dup_heads(bqkv[nq:nq + nkv], N_KV_HEADS),
                             _dup_heads(bqkv[nq + nkv:], N_KV_HEADS)]).reshape(1, -1)
    n_ext = nq + 2 * nk2

    pos = jnp.arange(seq, dtype=F32)
    inv_freq = ROPE_THETA ** (-jnp.arange(0, HEAD_DIM, 2, dtype=F32) / HEAD_DIM)
    ang = pos[:, None] * inv_freq[None, :]
    cos, sin = jnp.cos(ang), jnp.sin(ang)
    cos_t = jnp.concatenate([cos, cos, cos, cos], axis=1)
    sin_t = jnp.concatenate([-sin, sin, -sin, sin], axis=1)

    tt = min(QKV_TT, seq)
    nt = seq // tt
    const2 = lambda i: (0, 0)
    q, k2, v2 = pl.pallas_call(
        functools.partial(_qkv_kernel, nq=nq, nk=nk2),
        grid=(t // tt,),
        in_specs=[
            pl.BlockSpec((tt, d), lambda i: (i, 0)),
            pl.BlockSpec((d, n_ext), const2),
            pl.BlockSpec((1, n_ext), const2),
            pl.BlockSpec((tt, LANES), lambda i: (i % nt, 0)),
            pl.BlockSpec((tt, LANES), lambda i: (i % nt, 0)),
        ],
        out_specs=[pl.BlockSpec((tt, nq), lambda i: (i, 0)),
                   pl.BlockSpec((tt, nk2), lambda i: (i, 0)),
                   pl.BlockSpec((tt, nk2), lambda i: (i, 0))],
        out_shape=[jax.ShapeDtypeStruct((t, nq), BF16),
                   jax.ShapeDtypeStruct((t, nk2), BF16),
                   jax.ShapeDtypeStruct((t, nk2), BF16)],
        compiler_params=_params("parallel"),
        name="swa_qkv",
    )(x2, w_ext, b_ext, cos_t, sin_t)

    tq = min(ATTN_TQ, seq)
    ntq = seq // tq
    main = lambda bi, i: (bi * ntq + i, 0)
    halo = lambda bi, i: (jnp.maximum((bi * seq + i * tq) // WINDOW - 1, 0), 0)
    const = lambda bi, i: (0, 0)
    row = lambda v: v.reshape(1, d)
    r_ops, r_in, r_out, r_shapes = _router_plumbing(router_w, router_bias, t, d, tq, tg,
                                                    lambda bi, i: bi * ntq + i)
    return pl.pallas_call(
        functools.partial(_attn_kernel, alpha=alpha, tq=tq, tg=tg),
        grid=(nb, ntq),
        in_specs=[
            pl.BlockSpec(memory_space=pltpu.SMEM),
            pl.BlockSpec((tq, nq), main),
            pl.BlockSpec((tq, nk2), main),
            pl.BlockSpec((WINDOW, nk2), halo),
            pl.BlockSpec((tq, nk2), main),
            pl.BlockSpec((WINDOW, nk2), halo),
            pl.BlockSpec((tq, d), main),
            pl.BlockSpec((nq, d), const),
            pl.BlockSpec((1, d), const),
            pl.BlockSpec((1, d), const),
            pl.BlockSpec((1, d), const),
        ] + r_in,
        out_specs=[pl.BlockSpec((tq, d), main)] + r_out,
        out_shape=[jax.ShapeDtypeStruct((t, d), F32)] + r_shapes,
        compiler_params=_params("parallel", "parallel"),
        name="swa_attention",
    )(sinks, q, k2, k2, v2, v2, x2, wo.astype(BF16), row(bo), row(g), row(b), *r_ops)


def _conv_kernel(x_ref, win_ref, cw_ref, wout_ref, g_ref, b_ref, wt_ref, rb_ref, o_ref, pos_ref, gate_ref,
                 n8_ref, plan_ref, ubuf_ref, *, alpha, tt, tg):
    i = pl.program_id(1)
    d = x_ref.shape[1]

    @pl.when(i == 0)
    def _():
        ubuf_ref[0:SUBLANES, :] = jnp.zeros((SUBLANES, d), F32)

    x = x_ref[...]
    proj = jnp.dot(x.astype(BF16), win_ref[...], preferred_element_type=F32)
    gate_b = proj[:, :d]
    u = proj[:, d:2 * d] * proj[:, 2 * d:]
    ubuf_ref[SUBLANES:SUBLANES + tt, :] = u
    cw = cw_ref[...]
    conv = (cw[0:1, :] * ubuf_ref[SUBLANES - 2:SUBLANES - 2 + tt, :]
            + cw[1:2, :] * ubuf_ref[SUBLANES - 1:SUBLANES - 1 + tt, :]
            + cw[2:3, :] * u)
    ubuf_ref[0:SUBLANES, :] = u[tt - SUBLANES:tt, :]
    y = jnp.dot((gate_b * conv).astype(BF16), wout_ref[...], preferred_element_type=F32)
    o_ref[...] = _layer_norm(alpha * x + y, g_ref[...], b_ref[...])
    _route_tile(o_ref, (wt_ref, rb_ref), (pos_ref, gate_ref, n8_ref, plan_ref), tg)


def _conv_mixer(x2, seq, w_in, conv_w, w_out, g, b, alpha, router_w, router_bias, tg):
    t, d = x2.shape
    nb = t // seq
    tt = min(CONV_TT, seq)
    nt = seq // tt
    const = lambda bi, i: (0, 0)
    row = lambda v: v.reshape(1, d)
    r_ops, r_in, r_out, r_shapes = _router_plumbing(router_w, router_bias, t, d, tt, tg,
                                                    lambda bi, i: bi * nt + i)
    return pl.pallas_call(
        functools.partial(_conv_kernel, alpha=alpha, tt=tt, tg=tg),
        grid=(nb, nt),
        in_specs=[
            pl.BlockSpec((tt, d), lambda bi, i: (bi * nt + i, 0)),
            pl.BlockSpec((d, 3 * d), const),
            pl.BlockSpec(conv_w.shape, const),
            pl.BlockSpec((d, d), const),
            pl.BlockSpec((1, d), const),
            pl.BlockSpec((1, d), const),
        ] + r_in,
        out_specs=[pl.BlockSpec((tt, d), lambda bi, i: (bi * nt + i, 0))] + r_out,
        out_shape=[jax.ShapeDtypeStruct((t, d), F32)] + r_shapes,
        scratch_shapes=[pltpu.VMEM((SUBLANES + tt, d), F32)],
        compiler_params=_params("arbitrary", "arbitrary"),
        name="conv_mixer",
    )(x2, w_in.astype(BF16), conv_w, w_out.astype(BF16), row(g), row(b), *r_ops)


def _first_argmax(v, rows, n):
    m = jnp.max(v, axis=0, keepdims=True)
    first = jnp.min(jnp.where(v == m, rows, n), axis=0, keepdims=True)
    return m, first


def _route_tile(o_ref, router_refs, route_out_refs, tg):
    wt_ref, bias_ref = router_refs
    pos_ref, gate_ref, n8_ref, plan_ref = route_out_refs
    tt = o_ref.shape[0]
    x = o_ref[...]
    xh = x.astype(BF16)
    xl = (x - xh.astype(F32)).astype(BF16)
    wt = wt_ref[...]
    wh = wt.astype(BF16)
    wl = (wt - wh.astype(F32)).astype(BF16)
    dn = (((1,), (1,)), ((), ()))
    logits = (lax.dot_general(wh, xh, dn, preferred_element_type=F32)
              + (lax.dot_general(wh, xl, dn, preferred_element_type=F32)
                 + lax.dot_general(wl, xh, dn, preferred_element_type=F32)))
    scores = jax.nn.sigmoid(logits)
    sel = scores + bias_ref[...]

    rows8 = lax.broadcasted_iota(jnp.int32, (GROUP_SIZE, tt), 0)
    gsc = []
    for gi in range(N_EXPERT_GROUPS):
        blk = sel[gi * GROUP_SIZE:(gi + 1) * GROUP_SIZE, :]
        m1, f1 = _first_argmax(blk, rows8, GROUP_SIZE)
        m2 = jnp.max(jnp.where(rows8 == f1, NEG_INF, blk), axis=0, keepdims=True)
        gsc.append(m1 + m2)
    cur = jnp.concatenate(gsc, axis=0)
    rows_g = lax.broadcasted_iota(jnp.int32, (N_EXPERT_GROUPS, tt), 0)
    gmask = jnp.zeros((N_EXPERT_GROUPS, tt), F32)
    for _ in range(TOPK_GROUPS):
        _, f = _first_argmax(cur, rows_g, N_EXPERT_GROUPS)
        hit = rows_g == f
        gmask = jnp.where(hit, 1.0, gmask)
        cur = jnp.where(hit, NEG_INF, cur)
    emask = jnp.concatenate(
        [jnp.broadcast_to(gmask[gi:gi + 1, :], (GROUP_SIZE, tt)) for gi in range(N_EXPERT_GROUPS)],
        axis=0)

    rows_e = lax.broadcasted_iota(jnp.int32, (N_EXPERTS, tt), 0)
    cur = jnp.where(emask > 0.0, sel, NEG_INF)
    idxs, gates = [], []
    member = jnp.zeros((N_EXPERTS, tt), F32)
    for _ in range(TOP_K):
        _, f = _first_argmax(cur, rows_e, N_EXPERTS)
        hit = rows_e == f
        idxs.append(f)
        gates.append(jnp.sum(jnp.where(hit, scores, 0.0), axis=0, keepdims=True))
        member = jnp.where(hit, 1.0, member)
        cur = jnp.where(hit, NEG_INF, cur)
    gate = jnp.concatenate(gates, axis=0)
    gate_ref[...] = gate / jnp.sum(gate, axis=0, keepdims=True) * ROUTED_SCALE

    for s in range(tt // tg):
        cols = slice(s * tg, (s + 1) * tg)
        _plan_group(member[:, cols], [f[:, cols] for f in idxs], pos_ref.at[:, pl.ds(s * tg, tg)],
                    n8_ref.at[pl.ds(s * N_EXPERTS, N_EXPERTS), :],
                    plan_ref.at[pl.ds(s * PLAN_ROWS, PLAN_ROWS), :], tg)


def _plan_group(member, idxs, pos_ref, n8_ref, plan_ref, tt):
    rows_e = lax.broadcasted_iota(jnp.int32, (N_EXPERTS, tt), 0)
    ta = lax.broadcasted_iota(jnp.int32, (tt, tt), 0)
    tb = lax.broadcasted_iota(jnp.int32, (tt, tt), 1)
    earlier = (ta < tb).astype(BF16)
    before = jnp.dot(member.astype(BF16), earlier, preferred_element_type=F32)
    n = jnp.sum(member, axis=1, keepdims=True)
    n8 = jnp.floor((n + (CHUNK_ALIGN - 1)) * (1.0 / CHUNK_ALIGN)) * CHUNK_ALIGN
    n8_wide = jnp.broadcast_to(n8, (N_EXPERTS, LANES))
    ea = lax.broadcasted_iota(jnp.int32, (N_EXPERTS, N_EXPERTS), 0)
    eb = lax.broadcasted_iota(jnp.int32, (N_EXPERTS, N_EXPERTS), 1)
    lower = (eb < ea).astype(BF16)
    chunk_start = jnp.dot(lower, n8_wide.astype(BF16), preferred_element_type=F32)
    where_to = before + chunk_start[:, 0:1]
    for k in range(TOP_K):
        pk = jnp.sum(jnp.where(rows_e == idxs[k], where_to, 0.0), axis=0, keepdims=True)
        pos_ref[k:k + 1, :] = pk.astype(jnp.int32)
    n8_int = n8_wide.astype(jnp.int32)
    n8_ref[...] = n8_int

    lane = lax.broadcasted_iota(jnp.int32, (N_EXPERTS, LANES), 1)
    full, rest = _copy_split(n8_int)
    flag = jnp.zeros((N_EXPERTS, LANES), F32)
    before_copy = jnp.zeros((N_EXPERTS, LANES), jnp.int32)
    for j, size in enumerate(COPY_SIZES):
        if size < FULL_COPY:
            needs, skip = rest == size, full * FULL_COPY
        else:
            nth = j - (N_COPY_CLASSES - FULL_COPIES)
            needs, skip = full > nth, nth * FULL_COPY
        flag = jnp.where((lane == j) & needs, 1.0, flag)
        before_copy = jnp.where(lane == j, skip, before_copy)
    rank = jnp.dot(lower, flag.astype(BF16), preferred_element_type=F32)
    e_id = lax.broadcasted_iota(jnp.int32, (N_EXPERTS, LANES), 0)
    start_int = chunk_start.astype(jnp.int32)
    plan_ref[...] = jnp.zeros(plan_ref.shape, jnp.int32)
    for j in range(N_COPY_CLASSES):
        chosen = (rank[:, j:j + 1] == lane.astype(F32)) & (flag[:, j:j + 1] > 0.0)
        buf_row = start_int + before_copy[:, j:j + 1]
        plan_ref[j:j + 1, :] = jnp.sum(jnp.where(chosen, buf_row, 0), axis=0, keepdims=True)
        plan_ref[PLAN_HALF + j:PLAN_HALF + j + 1, :] = jnp.sum(jnp.where(chosen, e_id, 0), axis=0,
                                                              keepdims=True)
    plan_ref[PLAN_COUNTS:PLAN_COUNTS + 1, :] = jnp.sum(flag, axis=0, keepdims=True).astype(jnp.int32)
    plan_ref[PLAN_TOTAL:PLAN_TOTAL + 1, :] = jnp.sum(n8_int, axis=0, keepdims=True)


def _copy_split(n8):
    return lax.shift_right_logical(n8, FULL_COPY.bit_length() - 1), n8 & (FULL_COPY - 1)


def _pack_halves(v, exact=False):
    h = v.shape[1] // 2
    return _pack_pair(v[:, :h], v[:, h:], exact)


def _pack_pair(lo, hi, exact=False):
    if not exact:
        lo, hi = lo.astype(BF16).astype(F32), hi.astype(BF16).astype(F32)
    lo = lax.bitcast_convert_type(lo, U32)
    hi = lax.bitcast_convert_type(hi, U32)
    return (lo >> 16) | (hi & jnp.uint32(HI_MASK))


def _chunk_relative(pos, c):
    rel = pos - c * SORT_CHUNK
    inside = (rel >= 0) & (rel < SORT_CHUNK)
    return jnp.where(inside, rel, -1).astype(F32).astype(BF16)


def _unpack_halves(w):
    lo = lax.bitcast_convert_type(w << 16, F32).astype(BF16)
    hi = lax.bitcast_convert_type(w & jnp.uint32(HI_MASK), F32).astype(BF16)
    return lo, hi


def _start_group_copies(plan_ref, make_copy):
    for j, size in enumerate(COPY_SIZES):
        def body(k, carry):
            make_copy(pl.multiple_of(plan_ref[j * LANES + k], CHUNK_ALIGN),
                      pl.multiple_of(plan_ref[(PLAN_HALF + j) * LANES + k], CHUNK_ALIGN), size).start()
            return carry

        lax.fori_loop(0, plan_ref[PLAN_COUNTS * LANES + j], body, 0)


def _wait_group_rows(total_rows, ns, make_copy):
    for j in range(ns.bit_length()):
        size = CHUNK_ALIGN << j
        if size > ns:
            break

        @pl.when((total_rows & size) != 0)
        def _():
            make_copy(0, 0, size).wait()


def _dispatch_kernel(tot_ref, zrow_ref, plan_ref, pos_ref, x_ref, xs_ref, sbuf_ref, zbuf_ref,
                     onehot_ref, sem, zsem, *, tg, ns, bm):
    g = pl.program_id(0)
    ng = pl.num_programs(0)
    slot = g % 2

    @pl.when(g == 0)
    def _():
        zbuf_ref[...] = jnp.zeros_like(zbuf_ref)
        row = pl.multiple_of(zrow_ref[0], CHUNK_ALIGN)
        tail = pltpu.make_async_copy(zbuf_ref, xs_ref.at[pl.ds(row, bm), :], zsem)
        tail.start()
        tail.wait()

    def copy_from(sl):
        def make_copy(src, dst, size):
            return pltpu.make_async_copy(sbuf_ref.at[sl, pl.ds(src, size), :],
                                         xs_ref.at[pl.ds(dst, size), :], sem.at[sl])
        return make_copy

    @pl.when(g >= 2)
    def _():
        _wait_group_rows(tot_ref[g - 2], ns, copy_from(slot))

    xb = x_ref[...].astype(BF16)
    pos = pos_ref[...]
    local_rows = lax.broadcasted_iota(jnp.int32, (SORT_CHUNK, tg), 0).astype(F32).astype(BF16)
    one = jnp.ones((SORT_CHUNK, tg), BF16)
    zero = jnp.zeros((SORT_CHUNK, tg), BF16)
    for c in range(ns // SORT_CHUNK):
        rel = _chunk_relative(pos, c)
        hit = local_rows == rel[0:1, :]
        for k in range(1, TOP_K):
            hit = jnp.logical_or(hit, local_rows == rel[k:k + 1, :])
        onehot_ref[c * SORT_CHUNK:(c + 1) * SORT_CHUNK, :] = jnp.where(hit, one, zero)
    onehot = onehot_ref[...]
    half = x_ref.shape[1] // 2
    for n in range(half // SORT_CHUNK):
        lo = jnp.dot(onehot, xb[:, n * SORT_CHUNK:(n + 1) * SORT_CHUNK], preferred_element_type=F32)
        hi = jnp.dot(onehot, xb[:, half + n * SORT_CHUNK:half + (n + 1) * SORT_CHUNK],
                     preferred_element_type=F32)
        sbuf_ref[slot, :, n * SORT_CHUNK:(n + 1) * SORT_CHUNK] = _pack_pair(lo, hi, exact=True)

    _start_group_copies(plan_ref, copy_from(slot))

    @pl.when(g == ng - 1)
    def _():
        @pl.when(g >= 1)
        def _():
            _wait_group_rows(tot_ref[g - 1], ns, copy_from(1 - slot))
        _wait_group_rows(tot_ref[g], ns, copy_from(slot))


def _dispatch(x1, pos, plan, tot, zrow, n_rows, tg, ns, bm):
    t, d = x1.shape
    grid_spec = pltpu.PrefetchScalarGridSpec(
        num_scalar_prefetch=2,
        grid=(t // tg,),
        in_specs=[
            pl.BlockSpec((PLAN_WORDS,), lambda i, *_: (i,), memory_space=pltpu.SMEM),
            pl.BlockSpec((TOP_K, tg), lambda i, *_: (0, i)),
            pl.BlockSpec((tg, d), lambda i, *_: (i, 0)),
        ],
        out_specs=pl.BlockSpec(memory_space=pl.ANY),
        scratch_shapes=[pltpu.VMEM((2, ns, d // 2), U32), pltpu.VMEM((bm, d // 2), U32),
                        pltpu.VMEM((ns, tg), BF16),
                        pltpu.SemaphoreType.DMA((2,)), pltpu.SemaphoreType.DMA],
    )
    return pl.pallas_call(
        functools.partial(_dispatch_kernel, tg=tg, ns=ns, bm=bm),
        grid_spec=grid_spec,
        out_shape=jax.ShapeDtypeStruct((n_rows, d // 2), U32),
        compiler_params=_params("arbitrary"),
        name="moe_dispatch",
    )(tot, zrow, plan, pos, x1)


def _gmm_kernel(nq_ref, brow_ref, bval_ref, bord_ref, bfirst_ref, aexp_ref,
                xs_ref, w1_ref, w3_ref, w2_ref, y_ref,
                xbuf_ref, ybuf_ref, w1s_ref, w3s_ref, w2s_ref, w13_ref, w2b_ref, xsem, ysem, wsem,
                *, layer, bm):
    nq = nq_ref[0]
    na = nq_ref[1]
    f = w1s_ref.shape[2]
    half = xbuf_ref.shape[2]
    nx = xbuf_ref.shape[0]

    def x_copy(q, slot):
        row = pl.multiple_of(brow_ref[q], CHUNK_ALIGN)
        return pltpu.make_async_copy(xs_ref.at[pl.ds(row, bm), :], xbuf_ref.at[slot], xsem.at[slot])

    def w_copies(j, slot):
        e = aexp_ref[j]
        return (pltpu.make_async_copy(w1_ref.at[layer, e], w1s_ref.at[slot], wsem.at[slot]),
                pltpu.make_async_copy(w3_ref.at[layer, e], w3s_ref.at[slot], wsem.at[slot]),
                pltpu.make_async_copy(w2_ref.at[layer, e], w2s_ref.at[slot], wsem.at[slot]))

    def y_copy(slot, row, off, size):
        return pltpu.make_async_copy(ybuf_ref.at[slot, pl.ds(off, size), :],
                                     y_ref.at[pl.ds(pl.multiple_of(row + off, CHUNK_ALIGN), size), :],
                                     ysem.at[slot])

    def for_each_y_copy(slot, row, valid, fn):
        @pl.when(valid == bm)
        def _():
            fn(y_copy(slot, row, 0, bm))

        @pl.when(valid < bm)
        def _():
            for j in range((bm // CHUNK_ALIGN).bit_length() - 1):
                size = CHUNK_ALIGN << j

                @pl.when((valid & size) != 0)
                def _():
                    fn(y_copy(slot, row, pl.multiple_of(valid & (size - 1), CHUNK_ALIGN), size))

    for c in w_copies(0, 0):
        c.start()
    for q0 in range(nx - 1):
        @pl.when(q0 < nq)
        def _():
            x_copy(q0, q0).start()

    def body(q, carry):
        xslot = q % nx
        yslot = q % 2

        @pl.when(q + (nx - 1) < nq)
        def _():
            x_copy(q + (nx - 1), (q + (nx - 1)) % nx).start()

        @pl.when(bfirst_ref[q] == 1)
        def _():
            j = bord_ref[q]
            ws = j % 2
            for c in w_copies(j, ws):
                c.wait()

            @pl.when(j + 1 < na)
            def _():
                for c in w_copies(j + 1, 1 - ws):
                    c.start()
            w13_ref[:, :f] = w1s_ref[ws].astype(BF16)
            w13_ref[:, f:] = w3s_ref[ws].astype(BF16)
            w2b_ref[...] = w2s_ref[ws].astype(BF16)

        x_copy(q, xslot).wait()

        @pl.when(q >= 2)
        def _():
            for_each_y_copy(yslot, 0, bval_ref[jnp.maximum(q - 2, 0)], lambda c: c.wait())

        def swiglu(rows):
            x_lo, x_hi = _unpack_halves(xbuf_ref[xslot, 0:rows, :])
            h13 = (jnp.dot(x_lo, w13_ref[:half, :], preferred_element_type=F32)
                   + jnp.dot(x_hi, w13_ref[half:, :], preferred_element_type=F32))
            h = jax.nn.silu(h13[:, :f]) * h13[:, f:]
            ybuf_ref[yslot, 0:rows, :] = _pack_halves(
                jnp.dot(h.astype(BF16), w2b_ref[...], preferred_element_type=F32))

        @pl.when(bval_ref[q] > bm // 2)
        def _():
            swiglu(bm)

        @pl.when(bval_ref[q] <= bm // 2)
        def _():
            swiglu(bm // 2)
        for_each_y_copy(yslot, brow_ref[q], bval_ref[q], lambda c: c.start())
        return carry

    lax.fori_loop(0, nq, body, 0)

    @pl.when(nq >= 2)
    def _():
        for_each_y_copy(nq % 2, 0, bval_ref[jnp.maximum(nq - 2, 0)], lambda c: c.wait())
    for_each_y_copy((nq - 1) % 2, 0, bval_ref[nq - 1], lambda c: c.wait())


def _gmm(xs, tables, w1, w3, w2, layer, bm):
    n_rows, half = xs.shape
    d = 2 * half
    f = w1.shape[-1]
    any_spec = pl.BlockSpec(memory_space=pl.ANY)
    grid_spec = pltpu.PrefetchScalarGridSpec(
        num_scalar_prefetch=len(tables),
        grid=(1,),
        in_specs=[any_spec, any_spec, any_spec, any_spec],
        out_specs=any_spec,
        scratch_shapes=[pltpu.VMEM((GMM_X_SLOTS, bm, half), U32), pltpu.VMEM((2, bm, half), U32),
                        pltpu.VMEM((2, d, f), F32), pltpu.VMEM((2, d, f), F32), pltpu.VMEM((2, f, d), F32),
                        pltpu.VMEM((d, 2 * f), BF16), pltpu.VMEM((f, d), BF16),
                        pltpu.SemaphoreType.DMA((GMM_X_SLOTS,)), pltpu.SemaphoreType.DMA((2,)),
                        pltpu.SemaphoreType.DMA((2,))],
    )
    return pl.pallas_call(
        functools.partial(_gmm_kernel, layer=layer, bm=bm),
        grid_spec=grid_spec,
        out_shape=jax.ShapeDtypeStruct((n_rows, half), U32),
        compiler_params=_params("arbitrary"),
        name="moe_experts",
    )(*tables, xs, w1, w3, w2)


def _combine_kernel(tot_ref, plan_ref, plan_next_ref, yb_ref, pos_ref, gate_ref, x_ref, ws13_ref, ws2_ref,
                    g_ref, b_ref, o_ref, ybuf_ref, wgt_ref, ylo_ref, yhi_ref, sem, *, alpha, tg, ns):
    g = pl.program_id(0)
    ng = pl.num_programs(0)
    slot = g % 2

    def copy_into(sl):
        def make_copy(buf_row, hbm_row, size):
            return pltpu.make_async_copy(yb_ref.at[pl.ds(hbm_row, size), :],
                                         ybuf_ref.at[sl, pl.ds(buf_row, size), :], sem.at[sl])
        return make_copy

    @pl.when(g == 0)
    def _():
        ybuf_ref[...] = jnp.zeros_like(ybuf_ref)
        _start_group_copies(plan_ref, copy_into(0))

    @pl.when(g + 1 < ng)
    def _():
        _start_group_copies(plan_next_ref, copy_into(1 - slot))

    x = x_ref[...]
    f = ws2_ref.shape[0]
    h13 = jnp.dot(x.astype(BF16), ws13_ref[...], preferred_element_type=F32)
    hs = jax.nn.silu(h13[:, :f]) * h13[:, f:]
    shared = jnp.dot(hs.astype(BF16), ws2_ref[...], preferred_element_type=F32)

    _wait_group_rows(tot_ref[g], ns, copy_into(slot))

    pos = pos_ref[...]
    gate_b = gate_ref[...].astype(BF16)
    local_rows = lax.broadcasted_iota(jnp.int32, (SORT_CHUNK, tg), 0).astype(F32).astype(BF16)
    for c in range(ns // SORT_CHUNK):
        rel = _chunk_relative(pos, c)
        wgt = jnp.zeros((SORT_CHUNK, tg), BF16)
        for k in range(TOP_K):
            wgt = jnp.where(local_rows == rel[k:k + 1, :], gate_b[k:k + 1, :], wgt)
        wgt_ref[c * SORT_CHUNK:(c + 1) * SORT_CHUNK, :] = wgt
        y_lo, y_hi = _unpack_halves(ybuf_ref[slot, c * SORT_CHUNK:(c + 1) * SORT_CHUNK, :])
        ylo_ref[c * SORT_CHUNK:(c + 1) * SORT_CHUNK, :] = y_lo
        yhi_ref[c * SORT_CHUNK:(c + 1) * SORT_CHUNK, :] = y_hi
    wgt_all = wgt_ref[...]
    dn = (((0,), (0,)), ((), ()))
    routed = jnp.concatenate([lax.dot_general(wgt_all, ylo_ref[...], dn, preferred_element_type=F32),
                              lax.dot_general(wgt_all, yhi_ref[...], dn, preferred_element_type=F32)],
                             axis=1)
    o_ref[...] = _layer_norm(alpha * x + (routed + shared), g_ref[...], b_ref[...])


def _combine(yb, pos_kt, gate_kt, plan, tot, x1, ws1, ws3, ws2, g, b, alpha, tg, ns):
    t, d = x1.shape
    ng = t // tg
    f = ws1.shape[1]
    ws13 = jnp.concatenate([ws1, ws3], axis=1).astype(BF16)
    const = lambda i, *_: (0, 0)
    tile = lambda i, *_: (i, 0)
    row = lambda v: v.reshape(1, d)
    grid_spec = pltpu.PrefetchScalarGridSpec(
        num_scalar_prefetch=1,
        grid=(ng,),
        in_specs=[
            pl.BlockSpec((PLAN_WORDS,), lambda i, *_: (i,), memory_space=pltpu.SMEM),
            pl.BlockSpec((PLAN_WORDS,), lambda i, *_: (jnp.minimum(i + 1, ng - 1),),
                         memory_space=pltpu.SMEM),
            pl.BlockSpec(memory_space=pl.ANY),
            pl.BlockSpec((TOP_K, tg), lambda i, *_: (0, i)),
            pl.BlockSpec((TOP_K, tg), lambda i, *_: (0, i)),
            pl.BlockSpec((tg, d), tile),
            pl.BlockSpec((d, 2 * f), const),
            pl.BlockSpec((f, d), const),
            pl.BlockSpec((1, d), const),
            pl.BlockSpec((1, d), const),
        ],
        out_specs=pl.BlockSpec((tg, d), tile),
        scratch_shapes=[pltpu.VMEM((2, ns, d // 2), U32), pltpu.VMEM((ns, tg), BF16),
                        pltpu.VMEM((ns, d // 2), BF16), pltpu.VMEM((ns, d // 2), BF16),
                        pltpu.SemaphoreType.DMA((2,))],
    )
    return pl.pallas_call(
        functools.partial(_combine_kernel, alpha=alpha, tg=tg, ns=ns),
        grid_spec=grid_spec,
        out_shape=jax.ShapeDtypeStruct((t, d), F32),
        compiler_params=_params("arbitrary"),
        name="moe_combine",
    )(tot, plan, plan, yb, pos_kt, gate_kt, x1, ws13, ws2.astype(BF16), row(g), row(b))


def _moe_layer(x1, routing, tg, layer, w1, w3, w2, ws1, ws3, ws2, g, b, alpha):
    ng = x1.shape[0] // tg
    bm = EXPERT_BM
    ns = -(-(tg * TOP_K + N_EXPERTS * (CHUNK_ALIGN - 1)) // SORT_CHUNK) * SORT_CHUNK
    pos, gate, n8_wide, plan_wide = routing
    plan3 = plan_wide.reshape(ng, PLAN_ROWS, LANES)
    tot = plan3[:, PLAN_TOTAL, 0]

    n8 = n8_wide[:, 0].reshape(ng, N_EXPERTS)
    seg = jnp.sum(n8, axis=0)
    pend = jnp.cumsum(seg)
    pstart = pend - seg
    gdest = pstart[None, :] + jnp.cumsum(n8, axis=0) - n8
    max_rows = ng * (tg * TOP_K + N_EXPERTS * (CHUNK_ALIGN - 1))
    n_rows = max_rows + bm

    experts = jnp.arange(N_EXPERTS, dtype=jnp.int32)
    nb = (seg + bm - 1) // bm
    cb = jnp.cumsum(nb)
    blk = jnp.arange(max_rows // bm + N_EXPERTS, dtype=jnp.int32)
    blk_expert = jnp.minimum(jnp.sum((cb[None, :] <= blk[:, None]).astype(jnp.int32), axis=1), N_EXPERTS - 1)
    mine = blk_expert[:, None] == experts[None, :]
    of_expert = lambda v: jnp.sum(jnp.where(mine, v[None, :], 0), axis=1)
    local = blk - of_expert(cb - nb)
    active = seg > 0
    ordinal = jnp.cumsum(active.astype(jnp.int32)) - 1
    active_experts = jnp.sum(jnp.where((ordinal[None, :] == experts[:, None]) & active[None, :],
                                       experts[None, :], 0), axis=1)
    tables = [jnp.stack([cb[-1], jnp.sum(active.astype(jnp.int32))]),
              of_expert(pstart) + local * bm,
              jnp.clip(of_expert(seg) - local * bm, 0, bm),
              of_expert(ordinal),
              (local == 0).astype(jnp.int32),
              active_experts]
    tables = [a.astype(jnp.int32) for a in tables]
    zrow = pend[-1:].astype(jnp.int32)

    lists = plan3[:, PLAN_HALF:PLAN_HALF + N_COPY_CLASSES, :N_EXPERTS]
    full, _ = _copy_split(n8)
    nth = jnp.arange(FULL_COPIES, dtype=jnp.int32) * FULL_COPY
    skip = jnp.concatenate([jnp.broadcast_to((full * FULL_COPY)[:, None, :],
                                             (ng, N_COPY_CLASSES - FULL_COPIES, N_EXPERTS)),
                            jnp.broadcast_to(nth[None, :, None], (ng, FULL_COPIES, N_EXPERTS))], axis=1)
    pick = lists[..., None] == experts
    dst = jnp.sum(jnp.where(pick, (gdest[:, None, :] + skip)[:, :, None, :], 0), axis=-1)
    dst = jnp.pad(dst, ((0, 0), (0, PLAN_HALF - N_COPY_CLASSES), (0, LANES - N_EXPERTS)))
    plan = jnp.concatenate([plan3[:, :PLAN_HALF], dst], axis=1).astype(jnp.int32).reshape(-1)

    xs = _dispatch(x1, pos, plan, tot, zrow, n_rows, tg, ns, bm)
    yb = _gmm(xs, tables, w1, w3, w2, layer, bm)
    return _combine(yb, pos, gate, plan, tot, x1, ws1, ws3, ws2, g, b, alpha, tg, ns)


def kernel(x, ln_gain, ln_bias, pool_w, pool_scale, attn_wqkv, attn_bqkv, attn_sinks, attn_wo, attn_bo,
           conv_w_in, conv_w, conv_w_out, router_w, router_bias, expert_w1, expert_w3, expert_w2,
           shared_w1, shared_w3, shared_w2):
    bsz, seq, d = x.shape
    depth = ln_gain.shape[0]
    alpha = float((2 * depth) ** 0.25)
    h = x.reshape(bsz * seq, d)
    tg = min(GROUP_TG, seq)
    for i in range(depth):
        kind = i % N_MIXERS
        j = i // N_MIXERS
        g1, b1 = ln_gain[i, 0], ln_bias[i, 0]
        route = (router_w[i], router_bias[i], tg)
        if kind == 0:
            h, *routing = _pool_mixer(h, seq, pool_w[j], pool_scale[j], g1, b1, alpha, *route)
        elif kind == 1:
            h, *routing = _swa_mixer(h, seq, attn_wqkv[j], attn_bqkv[j], attn_sinks[j], attn_wo[j],
                                     attn_bo[j], g1, b1, alpha, *route)
        else:
            h, *routing = _conv_mixer(h, seq, conv_w_in[j], conv_w[j], conv_w_out[j], g1, b1, alpha, *route)
        h = _moe_layer(h, routing, tg, i, expert_w1, expert_w3, expert_w2,
                       shared_w1[i], shared_w3[i], shared_w2[i], ln_gain[i, 1], ln_bias[i, 1], alpha)
    return h.reshape(bsz, seq, d)
```

```python
import functools

import jax
import jax.numpy as jnp
from jax import lax
from jax.experimental import pallas as pl
from jax.experimental.pallas import tpu as pltpu

POOL_WINDOWS = (2, 4, 8, 16)
N_MIXERS = 3
HEAD_DIM = 64
N_Q_HEADS = 16
N_KV_HEADS = 4
GQA_GROUP = N_Q_HEADS // N_KV_HEADS
WINDOW = 128
ROPE_THETA = 10000.0
N_EXPERTS = 64
TOP_K = 8
N_EXPERT_GROUPS = 8
GROUP_SIZE = N_EXPERTS // N_EXPERT_GROUPS
TOPK_GROUPS = 4
ROUTED_SCALE = 2.5
LN_EPS = 1e-5

LANES = 128
SUBLANES = 8
POOL_HALO = 16
VMEM_LIMIT = 48 * 1024 * 1024

POOL_TT = 1024
QKV_TT = 512
ATTN_TQ = 512
CONV_TT = 512
GROUP_TG = 256
CHUNK_ALIGN = SUBLANES
SORT_CHUNK = 256
EXPERT_BM = 1024
GMM_X_SLOTS = 3
FULL_COPY = 64
FULL_COPIES = GROUP_TG // FULL_COPY
COPY_SIZES = tuple(range(CHUNK_ALIGN, FULL_COPY, CHUNK_ALIGN)) + (FULL_COPY,) * FULL_COPIES
N_COPY_CLASSES = len(COPY_SIZES)
PLAN_HALF = 2 * SUBLANES
PLAN_COUNTS = N_COPY_CLASSES
PLAN_TOTAL = N_COPY_CLASSES + 1
PLAN_ROWS = 2 * PLAN_HALF
PLAN_WORDS = PLAN_ROWS * LANES

U32 = jnp.uint32
HI_MASK = 0xFFFF0000

F32 = jnp.float32
BF16 = jnp.bfloat16
NEG_INF = float("-inf")


def _layer_norm(z, g, b):
    mu = jnp.mean(z, axis=-1, keepdims=True)
    zc = z - mu
    var = jnp.mean(zc * zc, axis=-1, keepdims=True)
    return zc * lax.rsqrt(var + LN_EPS) * g + b


def _params(*sem):
    return pltpu.CompilerParams(dimension_semantics=sem, vmem_limit_bytes=VMEM_LIMIT)


def _router_plumbing(router_w, router_bias, t, d, tt, tg, tile_index):
    groups = tt // tg
    const = lambda *ids: (0, 0)
    by_cols = lambda *ids: (0, tile_index(*ids))
    by_rows = lambda *ids: (tile_index(*ids), 0)
    operands = [router_w.T, router_bias.reshape(N_EXPERTS, 1)]
    in_specs = [pl.BlockSpec((N_EXPERTS, d), const), pl.BlockSpec((N_EXPERTS, 1), const)]
    out_specs = [pl.BlockSpec((TOP_K, tt), by_cols), pl.BlockSpec((TOP_K, tt), by_cols),
                 pl.BlockSpec((groups * N_EXPERTS, LANES), by_rows),
                 pl.BlockSpec((groups * PLAN_ROWS, LANES), by_rows)]
    out_shapes = [jax.ShapeDtypeStruct((TOP_K, t), jnp.int32), jax.ShapeDtypeStruct((TOP_K, t), F32),
                  jax.ShapeDtypeStruct((t // tg * N_EXPERTS, LANES), jnp.int32),
                  jax.ShapeDtypeStruct((t // tg * PLAN_ROWS, LANES), jnp.int32)]
    return operands, in_specs, out_specs, out_shapes


def _pool_kernel(x_ref, halo_ref, w_ref, scale_ref, g_ref, b_ref, wt_ref, rb_ref, o_ref, pos_ref, gate_ref,
                 n8_ref, plan_ref, buf_ref, *, alpha, tt, tg):
    i = pl.program_id(1)
    x = x_ref[...]
    d = x.shape[1]
    ch = d // len(POOL_WINDOWS)
    top = SUBLANES
    first = top + POOL_HALO
    n = first + tt
    for k in range(3):
        buf_ref[k, 0:top, :] = jnp.zeros((top, d), F32)
    buf_ref[0, top:first, :] = jnp.where(i > 0, halo_ref[...], 0.0)
    buf_ref[0, first:n, :] = x
    pos = i * tt + lax.broadcasted_iota(jnp.int32, (tt, 1), 0)
    ys = []
    for g, w in enumerate(POOL_WINDOWS):
        c0 = g * ch
        xg = x[:, c0:c0 + ch]
        src, span = 0, 1
        while span < w:
            dst = 1 if src != 1 else 2
            buf_ref[dst, top:n, c0:c0 + ch] = (buf_ref[src, top:n, c0:c0 + ch]
                                               + buf_ref[src, top - span:n - span, c0:c0 + ch])
            src, span = dst, 2 * span
        s = buf_ref[src, first:n, c0:c0 + ch]
        inv_cnt = 1.0 / jnp.minimum(pos + 1, w).astype(F32)
        pooled = s * inv_cnt - xg
        ys.append(jnp.dot(pooled.astype(BF16), w_ref[g], preferred_element_type=F32))
    y = jnp.concatenate(ys, axis=1) * scale_ref[...]
    o_ref[...] = _layer_norm(alpha * x + y, g_ref[...], b_ref[...])
    _route_tile(o_ref, (wt_ref, rb_ref), (pos_ref, gate_ref, n8_ref, plan_ref), tg)


def _pool_mixer(x2, seq, w_grp, scale, g, b, alpha, router_w, router_bias, tg):
    t, d = x2.shape
    tt = min(POOL_TT, seq)
    nt = seq // tt
    nb = t // seq
    row = lambda v: v.reshape(1, d)
    const2 = lambda bi, i: (0, 0)
    tile = lambda bi, i: bi * nt + i
    r_ops, r_in, r_out, r_shapes = _router_plumbing(router_w, router_bias, t, d, tt, tg, tile)
    return pl.pallas_call(
        functools.partial(_pool_kernel, alpha=alpha, tt=tt, tg=tg),
        grid=(nb, nt),
        in_specs=[
            pl.BlockSpec((tt, d), lambda bi, i: (bi * nt + i, 0)),
            pl.BlockSpec((POOL_HALO, d),
                         lambda bi, i: (jnp.maximum((bi * seq + i * tt) // POOL_HALO - 1, 0), 0)),
            pl.BlockSpec(w_grp.shape, lambda bi, i: (0, 0, 0)),
            pl.BlockSpec((1, d), const2),
            pl.BlockSpec((1, d), const2),
            pl.BlockSpec((1, d), const2),
        ] + r_in,
        out_specs=[pl.BlockSpec((tt, d), lambda bi, i: (bi * nt + i, 0))] + r_out,
        out_shape=[jax.ShapeDtypeStruct((t, d), F32)] + r_shapes,
        scratch_shapes=[pltpu.VMEM((3, SUBLANES + POOL_HALO + tt, d), F32)],
        compiler_params=_params("parallel", "parallel"),
        name="pool_mixer",
    )(x2, x2, w_grp.astype(BF16), row(scale), row(g), row(b), *r_ops)


def _qkv_kernel(x_ref, w_ref, b_ref, c_ref, s_ref, q_ref, k_ref, v_ref, *, nq, nk):
    y = jnp.dot(x_ref[...].astype(BF16), w_ref[...], preferred_element_type=F32) + b_ref[...]
    c = c_ref[...]
    s = s_ref[...]
    lane = lax.broadcasted_iota(jnp.int32, c.shape, 1)
    first_half = (lane % HEAD_DIM) < (HEAD_DIM // 2)

    def rope(v):
        partner = jnp.where(first_half, pltpu.roll(v, LANES - HEAD_DIM // 2, 1),
                            pltpu.roll(v, HEAD_DIM // 2, 1))
        return v * c + partner * s

    for j in range(nq // LANES):
        q_ref[:, j * LANES:(j + 1) * LANES] = (
            rope(y[:, j * LANES:(j + 1) * LANES]) * (HEAD_DIM ** -0.5)).astype(BF16)
    for j in range(nk // LANES):
        k_ref[:, j * LANES:(j + 1) * LANES] = rope(
            y[:, nq + j * LANES:nq + (j + 1) * LANES]).astype(BF16)
    v_ref[...] = y[:, nq + nk:].astype(BF16)


def _attn_kernel(sink_ref, q_ref, k_ref, kh_ref, v_ref, vh_ref, x_ref, wo_ref, bo_ref, g_ref, b_ref,
                 wt_ref, rb_ref, o_ref, pos_ref, gate_ref, n8_ref, plan_ref, *, alpha, tq, tg):
    i = pl.program_id(1)
    nqb = tq // WINDOW
    stack = GQA_GROUP * tq
    row_head = lax.broadcasted_iota(jnp.int32, (stack, 1), 0) // tq
    r = lax.broadcasted_iota(jnp.int32, (stack, 2 * WINDOW), 0) % tq
    c = lax.broadcasted_iota(jnp.int32, (stack, 2 * WINDOW), 1)
    dist = (r & (WINDOW - 1)) + WINDOW - c
    valid = (dist >= 0) & (dist < WINDOW) & ((c >= WINDOW) | (r >= WINDOW) | (i > 0))
    lane = lax.broadcasted_iota(jnp.int32, (tq, LANES), 1)
    low = lane < HEAD_DIM
    dn = (((1,), (1,)), ((), ()))
    outs = []
    for kv in range(N_KV_HEADS):
        sl = slice(kv * LANES, (kv + 1) * LANES)
        kk = jnp.concatenate([kh_ref[:, sl], k_ref[:, sl]], axis=0)
        vv = jnp.concatenate([vh_ref[:, sl], v_ref[:, sl]], axis=0)
        blocks = []
        for h in range(GQA_GROUP):
            c0 = (kv * GQA_GROUP + h - h % 2) * HEAD_DIM
            qp = q_ref[:, c0:c0 + LANES]
            keep = low if h % 2 == 0 else jnp.logical_not(low)
            qh = jnp.where(keep, qp, jnp.zeros_like(qp))
            sc_all = lax.dot_general(qh, kk, dn, preferred_element_type=F32)
            blocks += [sc_all[qb * WINDOW:(qb + 1) * WINDOW, qb * WINDOW:(qb + 2) * WINDOW]
                       for qb in range(nqb)]
        sc = jnp.where(valid, jnp.concatenate(blocks, axis=0), NEG_INF)
        sink = jnp.zeros((stack, 1), F32)
        for h in range(GQA_GROUP):
            sink = jnp.where(row_head == h, sink_ref[kv * GQA_GROUP + h], sink)
        m = jnp.maximum(jnp.max(sc, axis=1, keepdims=True), sink)
        pm = jnp.exp(sc - m)
        rden = 1.0 / (jnp.sum(pm, axis=1, keepdims=True) + jnp.exp(sink - m))
        pm = pm.astype(BF16)
        res = jnp.concatenate(
            [jnp.dot(pm[b * WINDOW:(b + 1) * WINDOW, :],
                     vv[(b % nqb) * WINDOW:(b % nqb + 2) * WINDOW, :], preferred_element_type=F32)
             for b in range(GQA_GROUP * nqb)], axis=0) * rden
        for p in range(GQA_GROUP // 2):
            outs.append(jnp.where(low, res[2 * p * tq:(2 * p + 1) * tq, :],
                                  res[(2 * p + 1) * tq:(2 * p + 2) * tq, :]))
    o_all = jnp.concatenate(outs, axis=1).astype(BF16)
    mix = jnp.dot(o_all, wo_ref[...], preferred_element_type=F32) + bo_ref[...]
    o_ref[...] = _layer_norm(alpha * x_ref[...] + mix, g_ref[...], b_ref[...])
    _route_tile(o_ref, (wt_ref, rb_ref), (pos_ref, gate_ref, n8_ref, plan_ref), tg)


def _dup_heads(w, n_heads):
    lead = w.shape[:-1]
    w = w.reshape(lead + (n_heads, 1, HEAD_DIM))
    w = jnp.broadcast_to(w, lead + (n_heads, 2, HEAD_DIM))
    return w.reshape(lead + (n_heads * 2 * HEAD_DIM,))


def _swa_mixer(x2, seq, wqkv, bqkv, sinks, wo, bo, g, b, alpha, router_w, router_bias, tg):
    t, d = x2.shape
    nb = t // seq
    nq = N_Q_HEADS * HEAD_DIM
    nkv = N_KV_HEADS * HEAD_DIM
    nk2 = 2 * nkv
    w_ext = jnp.concatenate([wqkv[:, :nq], _dup_heads(wqkv[:, nq:nq + nkv], N_KV_HEADS),
                             _dup_heads(wqkv[:, nq + nkv:], N_KV_HEADS)], axis=1).astype(BF16)
    b_ext = jnp.concatenate([bqkv[:nq], _dup_heads(bqkv[nq:nq + nkv], N_KV_HEADS),
                             _dup_heads(bqkv[nq + nkv:], N_KV_HEADS)]).reshape(1, -1)
    n_ext = nq + 2 * nk2

    pos = jnp.arange(seq, dtype=F32)
    inv_freq = ROPE_THETA ** (-jnp.arange(0, HEAD_DIM, 2, dtype=F32) / HEAD_DIM)
    ang = pos[:, None] * inv_freq[None, :]
    cos, sin = jnp.cos(ang), jnp.sin(ang)
    cos_t = jnp.concatenate([cos, cos, cos, cos], axis=1)
    sin_t = jnp.concatenate([-sin, sin, -sin, sin], axis=1)

    tt = min(QKV_TT, seq)
    nt = seq // tt
    const2 = lambda i: (0, 0)
    q, k2, v2 = pl.pallas_call(
        functools.partial(_qkv_kernel, nq=nq, nk=nk2),
        grid=(t // tt,),
        in_specs=[
            pl.BlockSpec((tt, d), lambda i: (i, 0)),
            pl.BlockSpec((d, n_ext), const2),
            pl.BlockSpec((1, n_ext), const2),
            pl.BlockSpec((tt, LANES), lambda i: (i % nt, 0)),
            pl.BlockSpec((tt, LANES), lambda i: (i % nt, 0)),
        ],
        out_specs=[pl.BlockSpec((tt, nq), lambda i: (i, 0)),
                   pl.BlockSpec((tt, nk2), lambda i: (i, 0)),
                   pl.BlockSpec((tt, nk2), lambda i: (i, 0))],
        out_shape=[jax.ShapeDtypeStruct((t, nq), BF16),
                   jax.ShapeDtypeStruct((t, nk2), BF16),
                   jax.ShapeDtypeStruct((t, nk2), BF16)],
        compiler_params=_params("parallel"),
        name="swa_qkv",
    )(x2, w_ext, b_ext, cos_t, sin_t)

    tq = min(ATTN_TQ, seq)
    ntq = seq // tq
    main = lambda bi, i: (bi * ntq + i, 0)
    halo = lambda bi, i: (jnp.maximum((bi * seq + i * tq) // WINDOW - 1, 0), 0)
    const = lambda bi, i: (0, 0)
    row = lambda v: v.reshape(1, d)
    r_ops, r_in, r_out, r_shapes = _router_plumbing(router_w, router_bias, t, d, tq, tg,
                                                    lambda bi, i: bi * ntq + i)
    return pl.pallas_call(
        functools.partial(_attn_kernel, alpha=alpha, tq=tq, tg=tg),
        grid=(nb, ntq),
        in_specs=[
            pl.BlockSpec(memory_space=pltpu.SMEM),
            pl.BlockSpec((tq, nq), main),
            pl.BlockSpec((tq, nk2), main),
            pl.BlockSpec((WINDOW, nk2), halo),
            pl.BlockSpec((tq, nk2), main),
            pl.BlockSpec((WINDOW, nk2), halo),
            pl.BlockSpec((tq, d), main),
            pl.BlockSpec((nq, d), const),
            pl.BlockSpec((1, d), const),
            pl.BlockSpec((1, d), const),
            pl.BlockSpec((1, d), const),
        ] + r_in,
        out_specs=[pl.BlockSpec((tq, d), main)] + r_out,
        out_shape=[jax.ShapeDtypeStruct((t, d), F32)] + r_shapes,
        compiler_params=_params("parallel", "parallel"),
        name="swa_attention",
    )(sinks, q, k2, k2, v2, v2, x2, wo.astype(BF16), row(bo), row(g), row(b), *r_ops)


def _conv_kernel(x_ref, win_ref, cw_ref, wout_ref, g_ref, b_ref, wt_ref, rb_ref, o_ref, pos_ref, gate_ref,
                 n8_ref, plan_ref, ubuf_ref, *, alpha, tt, tg):
    i = pl.program_id(1)
    d = x_ref.shape[1]

    @pl.when(i == 0)
    def _():
        ubuf_ref[0:SUBLANES, :] = jnp.zeros((SUBLANES, d), F32)

    x = x_ref[...]
    proj = jnp.dot(x.astype(BF16), win_ref[...], preferred_element_type=F32)
    gate_b = proj[:, :d]
    u = proj[:, d:2 * d] * proj[:, 2 * d:]
    ubuf_ref[SUBLANES:SUBLANES + tt, :] = u
    cw = cw_ref[...]
    conv = (cw[0:1, :] * ubuf_ref[SUBLANES - 2:SUBLANES - 2 + tt, :]
            + cw[1:2, :] * ubuf_ref[SUBLANES - 1:SUBLANES - 1 + tt, :]
            + cw[2:3, :] * u)
    ubuf_ref[0:SUBLANES, :] = u[tt - SUBLANES:tt, :]
    y = jnp.dot((gate_b * conv).astype(BF16), wout_ref[...], preferred_element_type=F32)
    o_ref[...] = _layer_norm(alpha * x + y, g_ref[...], b_ref[...])
    _route_tile(o_ref, (wt_ref, rb_ref), (pos_ref, gate_ref, n8_ref, plan_ref), tg)


def _conv_mixer(x2, seq, w_in, conv_w, w_out, g, b, alpha, router_w, router_bias, tg):
    t, d = x2.shape
    nb = t // seq
    tt = min(CONV_TT, seq)
    nt = seq // tt
    const = lambda bi, i: (0, 0)
    row = lambda v: v.reshape(1, d)
    r_ops, r_in, r_out, r_shapes = _router_plumbing(router_w, router_bias, t, d, tt, tg,
                                                    lambda bi, i: bi * nt + i)
    return pl.pallas_call(
        functools.partial(_conv_kernel, alpha=alpha, tt=tt, tg=tg),
        grid=(nb, nt),
        in_specs=[
            pl.BlockSpec((tt, d), lambda bi, i: (bi * nt + i, 0)),
            pl.BlockSpec((d, 3 * d), const),
            pl.BlockSpec(conv_w.shape, const),
            pl.BlockSpec((d, d), const),
            pl.BlockSpec((1, d), const),
            pl.BlockSpec((1, d), const),
        ] + r_in,
        out_specs=[pl.BlockSpec((tt, d), lambda bi, i: (bi * nt + i, 0))] + r_out,
        out_shape=[jax.ShapeDtypeStruct((t, d), F32)] + r_shapes,
        scratch_shapes=[pltpu.VMEM((SUBLANES + tt, d), F32)],
        compiler_params=_params("arbitrary", "arbitrary"),
        name="conv_mixer",
    )(x2, w_in.astype(BF16), conv_w, w_out.astype(BF16), row(g), row(b), *r_ops)


def _first_argmax(v, rows, n):
    m = jnp.max(v, axis=0, keepdims=True)
    first = jnp.min(jnp.where(v == m, rows, n), axis=0, keepdims=True)
    return m, first


def _route_tile(o_ref, router_refs, route_out_refs, tg):
    wt_ref, bias_ref = router_refs
    pos_ref, gate_ref, n8_ref, plan_ref = route_out_refs
    tt = o_ref.shape[0]
    x = o_ref[...]
    xh = x.astype(BF16)
    xl = (x - xh.astype(F32)).astype(BF16)
    wt = wt_ref[...]
    wh = wt.astype(BF16)
    wl = (wt - wh.astype(F32)).astype(BF16)
    dn = (((1,), (1,)), ((), ()))
    logits = (lax.dot_general(wh, xh, dn, preferred_element_type=F32)
              + (lax.dot_general(wh, xl, dn, preferred_element_type=F32)
                 + lax.dot_general(wl, xh, dn, preferred_element_type=F32)))
    scores = jax.nn.sigmoid(logits)
    sel = scores + bias_ref[...]

    rows8 = lax.broadcasted_iota(jnp.int32, (GROUP_SIZE, tt), 0)
    gsc = []
    for gi in range(N_EXPERT_GROUPS):
        blk = sel[gi * GROUP_SIZE:(gi + 1) * GROUP_SIZE, :]
        m1, f1 = _first_argmax(blk, rows8, GROUP_SIZE)
        m2 = jnp.max(jnp.where(rows8 == f1, NEG_INF, blk), axis=0, keepdims=True)
        gsc.append(m1 + m2)
    cur = jnp.concatenate(gsc, axis=0)
    rows_g = lax.broadcasted_iota(jnp.int32, (N_EXPERT_GROUPS, tt), 0)
    gmask = jnp.zeros((N_EXPERT_GROUPS, tt), F32)
    for _ in range(TOPK_GROUPS):
        _, f = _first_argmax(cur, rows_g, N_EXPERT_GROUPS)
        hit = rows_g == f
        gmask = jnp.where(hit, 1.0, gmask)
        cur = jnp.where(hit, NEG_INF, cur)
    emask = jnp.concatenate(
        [jnp.broadcast_to(gmask[gi:gi + 1, :], (GROUP_SIZE, tt)) for gi in range(N_EXPERT_GROUPS)],
        axis=0)

    rows_e = lax.broadcasted_iota(jnp.int32, (N_EXPERTS, tt), 0)
    cur = jnp.where(emask > 0.0, sel, NEG_INF)
    idxs, gates = [], []
    member = jnp.zeros((N_EXPERTS, tt), F32)
    for _ in range(TOP_K):
        _, f = _first_argmax(cur, rows_e, N_EXPERTS)
        hit = rows_e == f
        idxs.append(f)
        gates.append(jnp.sum(jnp.where(hit, scores, 0.0), axis=0, keepdims=True))
        member = jnp.where(hit, 1.0, member)
        cur = jnp.where(hit, NEG_INF, cur)
    gate = jnp.concatenate(gates, axis=0)
    gate_ref[...] = gate / jnp.sum(gate, axis=0, keepdims=True) * ROUTED_SCALE

    for s in range(tt // tg):
        cols = slice(s * tg, (s + 1) * tg)
        _plan_group(member[:, cols], [f[:, cols] for f in idxs], pos_ref.at[:, pl.ds(s * tg, tg)],
                    n8_ref.at[pl.ds(s * N_EXPERTS, N_EXPERTS), :],
                    plan_ref.at[pl.ds(s * PLAN_ROWS, PLAN_ROWS), :], tg)


def _plan_group(member, idxs, pos_ref, n8_ref, plan_ref, tt):
    rows_e = lax.broadcasted_iota(jnp.int32, (N_EXPERTS, tt), 0)
    ta = lax.broadcasted_iota(jnp.int32, (tt, tt), 0)
    tb = lax.broadcasted_iota(jnp.int32, (tt, tt), 1)
    earlier = (ta < tb).astype(BF16)
    before = jnp.dot(member.astype(BF16), earlier, preferred_element_type=F32)
    n = jnp.sum(member, axis=1, keepdims=True)
    n8 = jnp.floor((n + (CHUNK_ALIGN - 1)) * (1.0 / CHUNK_ALIGN)) * CHUNK_ALIGN
    n8_wide = jnp.broadcast_to(n8, (N_EXPERTS, LANES))
    ea = lax.broadcasted_iota(jnp.int32, (N_EXPERTS, N_EXPERTS), 0)
    eb = lax.broadcasted_iota(jnp.int32, (N_EXPERTS, N_EXPERTS), 1)
    lower = (eb < ea).astype(BF16)
    chunk_start = jnp.dot(lower, n8_wide.astype(BF16), preferred_element_type=F32)
    where_to = before + chunk_start[:, 0:1]
    for k in range(TOP_K):
        pk = jnp.sum(jnp.where(rows_e == idxs[k], where_to, 0.0), axis=0, keepdims=True)
        pos_ref[k:k + 1, :] = pk.astype(jnp.int32)
    n8_int = n8_wide.astype(jnp.int32)
    n8_ref[...] = n8_int

    lane = lax.broadcasted_iota(jnp.int32, (N_EXPERTS, LANES), 1)
    full, rest = _copy_split(n8_int)
    flag = jnp.zeros((N_EXPERTS, LANES), F32)
    before_copy = jnp.zeros((N_EXPERTS, LANES), jnp.int32)
    for j, size in enumerate(COPY_SIZES):
        if size < FULL_COPY:
            needs, skip = rest == size, full * FULL_COPY
        else:
            nth = j - (N_COPY_CLASSES - FULL_COPIES)
            needs, skip = full > nth, nth * FULL_COPY
        flag = jnp.where((lane == j) & needs, 1.0, flag)
        before_copy = jnp.where(lane == j, skip, before_copy)
    rank = jnp.dot(lower, flag.astype(BF16), preferred_element_type=F32)
    e_id = lax.broadcasted_iota(jnp.int32, (N_EXPERTS, LANES), 0)
    start_int = chunk_start.astype(jnp.int32)
    plan_ref[...] = jnp.zeros(plan_ref.shape, jnp.int32)
    for j in range(N_COPY_CLASSES):
        chosen = (rank[:, j:j + 1] == lane.astype(F32)) & (flag[:, j:j + 1] > 0.0)
        buf_row = start_int + before_copy[:, j:j + 1]
        plan_ref[j:j + 1, :] = jnp.sum(jnp.where(chosen, buf_row, 0), axis=0, keepdims=True)
        plan_ref[PLAN_HALF + j:PLAN_HALF + j + 1, :] = jnp.sum(jnp.where(chosen, e_id, 0), axis=0,
                                                              keepdims=True)
    plan_ref[PLAN_COUNTS:PLAN_COUNTS + 1, :] = jnp.sum(flag, axis=0, keepdims=True).astype(jnp.int32)
    plan_ref[PLAN_TOTAL:PLAN_TOTAL + 1, :] = jnp.sum(n8_int, axis=0, keepdims=True)


def _copy_split(n8):
    return lax.shift_right_logical(n8, FULL_COPY.bit_length() - 1), n8 & (FULL_COPY - 1)


def _pack_halves(v, exact=False):
    h = v.shape[1] // 2
    return _pack_pair(v[:, :h], v[:, h:], exact)


def _pack_pair(lo, hi, exact=False):
    if not exact:
        lo, hi = lo.astype(BF16).astype(F32), hi.astype(BF16).astype(F32)
    lo = lax.bitcast_convert_type(lo, U32)
    hi = lax.bitcast_convert_type(hi, U32)
    return (lo >> 16) | (hi & jnp.uint32(HI_MASK))


def _chunk_relative(pos, c):
    rel = pos - c * SORT_CHUNK
    inside = (rel >= 0) & (rel < SORT_CHUNK)
    return jnp.where(inside, rel, -1).astype(F32).astype(BF16)


def _unpack_halves(w):
    lo = lax.bitcast_convert_type(w << 16, F32).astype(BF16)
    hi = lax.bitcast_convert_type(w & jnp.uint32(HI_MASK), F32).astype(BF16)
    return lo, hi


def _start_group_copies(plan_ref, make_copy):
    for j, size in enumerate(COPY_SIZES):
        def body(k, carry):
            make_copy(pl.multiple_of(plan_ref[j * LANES + k], CHUNK_ALIGN),
                      pl.multiple_of(plan_ref[(PLAN_HALF + j) * LANES + k], CHUNK_ALIGN), size).start()
            return carry

        lax.fori_loop(0, plan_ref[PLAN_COUNTS * LANES + j], body, 0)


def _wait_group_rows(total_rows, ns, make_copy):
    for j in range(ns.bit_length()):
        size = CHUNK_ALIGN << j
        if size > ns:
            break

        @pl.when((total_rows & size) != 0)
        def _():
            make_copy(0, 0, size).wait()


def _dispatch_kernel(tot_ref, zrow_ref, plan_ref, pos_ref, x_ref, xs_ref, sbuf_ref, zbuf_ref,
                     onehot_ref, sem, zsem, *, tg, ns, bm):
    g = pl.program_id(0)
    ng = pl.num_programs(0)
    slot = g % 2

    @pl.when(g == 0)
    def _():
        zbuf_ref[...] = jnp.zeros_like(zbuf_ref)
        row = pl.multiple_of(zrow_ref[0], CHUNK_ALIGN)
        tail = pltpu.make_async_copy(zbuf_ref, xs_ref.at[pl.ds(row, bm), :], zsem)
        tail.start()
        tail.wait()

    def copy_from(sl):
        def make_copy(src, dst, size):
            return pltpu.make_async_copy(sbuf_ref.at[sl, pl.ds(src, size), :],
                                         xs_ref.at[pl.ds(dst, size), :], sem.at[sl])
        return make_copy

    @pl.when(g >= 2)
    def _():
        _wait_group_rows(tot_ref[g - 2], ns, copy_from(slot))

    xb = x_ref[...].astype(BF16)
    pos = pos_ref[...]
    local_rows = lax.broadcasted_iota(jnp.int32, (SORT_CHUNK, tg), 0).astype(F32).astype(BF16)
    one = jnp.ones((SORT_CHUNK, tg), BF16)
    zero = jnp.zeros((SORT_CHUNK, tg), BF16)
    def onehot_chunk(c):
        rel = _chunk_relative(pos, c)
        hit = local_rows == rel[0:1, :]
        for k in range(1, TOP_K):
            hit = jnp.logical_or(hit, local_rows == rel[k:k + 1, :])
        return jnp.where(hit, one, zero)

    half = x_ref.shape[1] // 2

    def sort_rows(onehot, first_row):
        rows = onehot.shape[0]
        for n in range(half // SORT_CHUNK):
            lo = jnp.dot(onehot, xb[:, n * SORT_CHUNK:(n + 1) * SORT_CHUNK], preferred_element_type=F32)
            hi = jnp.dot(onehot, xb[:, half + n * SORT_CHUNK:half + (n + 1) * SORT_CHUNK],
                         preferred_element_type=F32)
            sbuf_ref[slot, first_row:first_row + rows, n * SORT_CHUNK:(n + 1) * SORT_CHUNK] = _pack_pair(
                lo, hi, exact=True)

    main = ns - SORT_CHUNK
    for c in range(main // SORT_CHUNK):
        onehot_ref[c * SORT_CHUNK:(c + 1) * SORT_CHUNK, :] = onehot_chunk(c)
    sort_rows(onehot_ref[0:main, :], 0)

    @pl.when(tot_ref[g] > main)
    def _():
        sort_rows(onehot_chunk(main // SORT_CHUNK), main)

    _start_group_copies(plan_ref, copy_from(slot))

    @pl.when(g == ng - 1)
    def _():
        @pl.when(g >= 1)
        def _():
            _wait_group_rows(tot_ref[g - 1], ns, copy_from(1 - slot))
        _wait_group_rows(tot_ref[g], ns, copy_from(slot))


def _dispatch(x1, pos, plan, tot, zrow, n_rows, tg, ns, bm):
    t, d = x1.shape
    grid_spec = pltpu.PrefetchScalarGridSpec(
        num_scalar_prefetch=2,
        grid=(t // tg,),
        in_specs=[
            pl.BlockSpec((PLAN_WORDS,), lambda i, *_: (i,), memory_space=pltpu.SMEM),
            pl.BlockSpec((TOP_K, tg), lambda i, *_: (0, i)),
            pl.BlockSpec((tg, d), lambda i, *_: (i, 0)),
        ],
        out_specs=pl.BlockSpec(memory_space=pl.ANY),
        scratch_shapes=[pltpu.VMEM((2, ns, d // 2), U32), pltpu.VMEM((bm, d // 2), U32),
                        pltpu.VMEM((ns - SORT_CHUNK, tg), BF16),
                        pltpu.SemaphoreType.DMA((2,)), pltpu.SemaphoreType.DMA],
    )
    return pl.pallas_call(
        functools.partial(_dispatch_kernel, tg=tg, ns=ns, bm=bm),
        grid_spec=grid_spec,
        out_shape=jax.ShapeDtypeStruct((n_rows, d // 2), U32),
        compiler_params=_params("arbitrary"),
        name="moe_dispatch",
    )(tot, zrow, plan, pos, x1)


def _gmm_kernel(nq_ref, brow_ref, bval_ref, bord_ref, bfirst_ref, aexp_ref,
                xs_ref, w1_ref, w3_ref, w2_ref, y_ref,
                xbuf_ref, ybuf_ref, w1s_ref, w3s_ref, w2s_ref, w13_ref, w2b_ref, xsem, ysem, wsem,
                *, layer, bm):
    nq = nq_ref[0]
    na = nq_ref[1]
    f = w1s_ref.shape[2]
    half = xbuf_ref.shape[2]
    nx = xbuf_ref.shape[0]

    def x_copy(q, slot):
        row = pl.multiple_of(brow_ref[q], CHUNK_ALIGN)
        return pltpu.make_async_copy(xs_ref.at[pl.ds(row, bm), :], xbuf_ref.at[slot], xsem.at[slot])

    def w_copies(j, slot):
        e = aexp_ref[j]
        return (pltpu.make_async_copy(w1_ref.at[layer, e], w1s_ref.at[slot], wsem.at[slot]),
                pltpu.make_async_copy(w3_ref.at[layer, e], w3s_ref.at[slot], wsem.at[slot]),
                pltpu.make_async_copy(w2_ref.at[layer, e], w2s_ref.at[slot], wsem.at[slot]))

    def y_copy(slot, row, off, size):
        return pltpu.make_async_copy(ybuf_ref.at[slot, pl.ds(off, size), :],
                                     y_ref.at[pl.ds(pl.multiple_of(row + off, CHUNK_ALIGN), size), :],
                                     ysem.at[slot])

    def for_each_y_copy(slot, row, valid, fn):
        @pl.when(valid == bm)
        def _():
            fn(y_copy(slot, row, 0, bm))

        @pl.when(valid < bm)
        def _():
            for j in range((bm // CHUNK_ALIGN).bit_length() - 1):
                size = CHUNK_ALIGN << j

                @pl.when((valid & size) != 0)
                def _():
                    fn(y_copy(slot, row, pl.multiple_of(valid & (size - 1), CHUNK_ALIGN), size))

    for c in w_copies(0, 0):
        c.start()
    for q0 in range(nx - 1):
        @pl.when(q0 < nq)
        def _():
            x_copy(q0, q0).start()

    def body(q, carry):
        xslot = q % nx
        yslot = q % 2

        @pl.when(q + (nx - 1) < nq)
        def _():
            x_copy(q + (nx - 1), (q + (nx - 1)) % nx).start()

        @pl.when(bfirst_ref[q] == 1)
        def _():
            j = bord_ref[q]
            ws = j % 2
            for c in w_copies(j, ws):
                c.wait()

            @pl.when(j + 1 < na)
            def _():
                for c in w_copies(j + 1, 1 - ws):
                    c.start()
            w13_ref[:, :f] = w1s_ref[ws].astype(BF16)
            w13_ref[:, f:] = w3s_ref[ws].astype(BF16)
            w2b_ref[...] = w2s_ref[ws].astype(BF16)

        x_copy(q, xslot).wait()

        @pl.when(q >= 2)
        def _():
            for_each_y_copy(yslot, 0, bval_ref[jnp.maximum(q - 2, 0)], lambda c: c.wait())

        def swiglu(rows):
            x_lo, x_hi = _unpack_halves(xbuf_ref[xslot, 0:rows, :])
            h13 = (jnp.dot(x_lo, w13_ref[:half, :], preferred_element_type=F32)
                   + jnp.dot(x_hi, w13_ref[half:, :], preferred_element_type=F32))
            h = jax.nn.silu(h13[:, :f]) * h13[:, f:]
            ybuf_ref[yslot, 0:rows, :] = _pack_halves(
                jnp.dot(h.astype(BF16), w2b_ref[...], preferred_element_type=F32))

        @pl.when(bval_ref[q] > bm // 2)
        def _():
            swiglu(bm)

        @pl.when(bval_ref[q] <= bm // 2)
        def _():
            swiglu(bm // 2)
        for_each_y_copy(yslot, brow_ref[q], bval_ref[q], lambda c: c.start())
        return carry

    lax.fori_loop(0, nq, body, 0)

    @pl.when(nq >= 2)
    def _():
        for_each_y_copy(nq % 2, 0, bval_ref[jnp.maximum(nq - 2, 0)], lambda c: c.wait())
    for_each_y_copy((nq - 1) % 2, 0, bval_ref[nq - 1], lambda c: c.wait())


def _gmm(xs, tables, w1, w3, w2, layer, bm):
    n_rows, half = xs.shape
    d = 2 * half
    f = w1.shape[-1]
    any_spec = pl.BlockSpec(memory_space=pl.ANY)
    grid_spec = pltpu.PrefetchScalarGridSpec(
        num_scalar_prefetch=len(tables),
        grid=(1,),
        in_specs=[any_spec, any_spec, any_spec, any_spec],
        out_specs=any_spec,
        scratch_shapes=[pltpu.VMEM((GMM_X_SLOTS, bm, half), U32), pltpu.VMEM((2, bm, half), U32),
                        pltpu.VMEM((2, d, f), F32), pltpu.VMEM((2, d, f), F32), pltpu.VMEM((2, f, d), F32),
                        pltpu.VMEM((d, 2 * f), BF16), pltpu.VMEM((f, d), BF16),
                        pltpu.SemaphoreType.DMA((GMM_X_SLOTS,)), pltpu.SemaphoreType.DMA((2,)),
                        pltpu.SemaphoreType.DMA((2,))],
    )
    return pl.pallas_call(
        functools.partial(_gmm_kernel, layer=layer, bm=bm),
        grid_spec=grid_spec,
        out_shape=jax.ShapeDtypeStruct((n_rows, half), U32),
        compiler_params=_params("arbitrary"),
        name="moe_experts",
    )(*tables, xs, w1, w3, w2)


def _combine_kernel(tot_ref, plan_ref, plan_next_ref, yb_ref, pos_ref, gate_ref, x_ref, ws13_ref, ws2_ref,
                    g_ref, b_ref, o_ref, ybuf_ref, wgt_ref, ylo_ref, yhi_ref, sem, *, alpha, tg, ns):
    g = pl.program_id(0)
    ng = pl.num_programs(0)
    slot = g % 2

    def copy_into(sl):
        def make_copy(buf_row, hbm_row, size):
            return pltpu.make_async_copy(yb_ref.at[pl.ds(hbm_row, size), :],
                                         ybuf_ref.at[sl, pl.ds(buf_row, size), :], sem.at[sl])
        return make_copy

    @pl.when(g == 0)
    def _():
        ybuf_ref[...] = jnp.zeros_like(ybuf_ref)
        _start_group_copies(plan_ref, copy_into(0))

    @pl.when(g + 1 < ng)
    def _():
        _start_group_copies(plan_next_ref, copy_into(1 - slot))

    x = x_ref[...]
    f = ws2_ref.shape[0]
    h13 = jnp.dot(x.astype(BF16), ws13_ref[...], preferred_element_type=F32)
    hs = jax.nn.silu(h13[:, :f]) * h13[:, f:]
    shared = jnp.dot(hs.astype(BF16), ws2_ref[...], preferred_element_type=F32)

    _wait_group_rows(tot_ref[g], ns, copy_into(slot))

    pos = pos_ref[...]
    gate_b = gate_ref[...].astype(BF16)
    local_rows = lax.broadcasted_iota(jnp.int32, (SORT_CHUNK, tg), 0).astype(F32).astype(BF16)
    for c in range(ns // SORT_CHUNK):
        rel = _chunk_relative(pos, c)
        wgt = jnp.zeros((SORT_CHUNK, tg), BF16)
        for k in range(TOP_K):
            wgt = jnp.where(local_rows == rel[k:k + 1, :], gate_b[k:k + 1, :], wgt)
        wgt_ref[c * SORT_CHUNK:(c + 1) * SORT_CHUNK, :] = wgt
        y_lo, y_hi = _unpack_halves(ybuf_ref[slot, c * SORT_CHUNK:(c + 1) * SORT_CHUNK, :])
        ylo_ref[c * SORT_CHUNK:(c + 1) * SORT_CHUNK, :] = y_lo
        yhi_ref[c * SORT_CHUNK:(c + 1) * SORT_CHUNK, :] = y_hi
    wgt_all = wgt_ref[...]
    dn = (((0,), (0,)), ((), ()))
    routed = jnp.concatenate([lax.dot_general(wgt_all, ylo_ref[...], dn, preferred_element_type=F32),
                              lax.dot_general(wgt_all, yhi_ref[...], dn, preferred_element_type=F32)],
                             axis=1)
    o_ref[...] = _layer_norm(alpha * x + (routed + shared), g_ref[...], b_ref[...])


def _combine(yb, pos_kt, gate_kt, plan, tot, x1, ws1, ws3, ws2, g, b, alpha, tg, ns):
    t, d = x1.shape
    ng = t // tg
    f = ws1.shape[1]
    ws13 = jnp.concatenate([ws1, ws3], axis=1).astype(BF16)
    const = lambda i, *_: (0, 0)
    tile = lambda i, *_: (i, 0)
    row = lambda v: v.reshape(1, d)
    grid_spec = pltpu.PrefetchScalarGridSpec(
        num_scalar_prefetch=1,
        grid=(ng,),
        in_specs=[
            pl.BlockSpec((PLAN_WORDS,), lambda i, *_: (i,), memory_space=pltpu.SMEM),
            pl.BlockSpec((PLAN_WORDS,), lambda i, *_: (jnp.minimum(i + 1, ng - 1),),
                         memory_space=pltpu.SMEM),
            pl.BlockSpec(memory_space=pl.ANY),
            pl.BlockSpec((TOP_K, tg), lambda i, *_: (0, i)),
            pl.BlockSpec((TOP_K, tg), lambda i, *_: (0, i)),
            pl.BlockSpec((tg, d), tile),
            pl.BlockSpec((d, 2 * f), const),
            pl.BlockSpec((f, d), const),
            pl.BlockSpec((1, d), const),
            pl.BlockSpec((1, d), const),
        ],
        out_specs=pl.BlockSpec((tg, d), tile),
        scratch_shapes=[pltpu.VMEM((2, ns, d // 2), U32), pltpu.VMEM((ns, tg), BF16),
                        pltpu.VMEM((ns, d // 2), BF16), pltpu.VMEM((ns, d // 2), BF16),
                        pltpu.SemaphoreType.DMA((2,))],
    )
    return pl.pallas_call(
        functools.partial(_combine_kernel, alpha=alpha, tg=tg, ns=ns),
        grid_spec=grid_spec,
        out_shape=jax.ShapeDtypeStruct((t, d), F32),
        compiler_params=_params("arbitrary"),
        name="moe_combine",
    )(tot, plan, plan, yb, pos_kt, gate_kt, x1, ws13, ws2.astype(BF16), row(g), row(b))


def _moe_layer(x1, routing, tg, layer, w1, w3, w2, ws1, ws3, ws2, g, b, alpha):
    ng = x1.shape[0] // tg
    bm = EXPERT_BM
    ns = -(-(tg * TOP_K + N_EXPERTS * (CHUNK_ALIGN - 1)) // SORT_CHUNK) * SORT_CHUNK
    pos, gate, n8_wide, plan_wide = routing
    plan3 = plan_wide.reshape(ng, PLAN_ROWS, LANES)
    tot = plan3[:, PLAN_TOTAL, 0]

    n8 = n8_wide[:, 0].reshape(ng, N_EXPERTS)
    seg = jnp.sum(n8, axis=0)
    pend = jnp.cumsum(seg)
    pstart = pend - seg
    gdest = pstart[None, :] + jnp.cumsum(n8, axis=0) - n8
    max_rows = ng * (tg * TOP_K + N_EXPERTS * (CHUNK_ALIGN - 1))
    n_rows = max_rows + bm

    experts = jnp.arange(N_EXPERTS, dtype=jnp.int32)
    nb = (seg + bm - 1) // bm
    cb = jnp.cumsum(nb)
    blk = jnp.arange(max_rows // bm + N_EXPERTS, dtype=jnp.int32)
    blk_expert = jnp.minimum(jnp.sum((cb[None, :] <= blk[:, None]).astype(jnp.int32), axis=1), N_EXPERTS - 1)
    mine = blk_expert[:, None] == experts[None, :]
    of_expert = lambda v: jnp.sum(jnp.where(mine, v[None, :], 0), axis=1)
    local = blk - of_expert(cb - nb)
    active = seg > 0
    ordinal = jnp.cumsum(active.astype(jnp.int32)) - 1
    active_experts = jnp.sum(jnp.where((ordinal[None, :] == experts[:, None]) & active[None, :],
                                       experts[None, :], 0), axis=1)
    tables = [jnp.stack([cb[-1], jnp.sum(active.astype(jnp.int32))]),
              of_expert(pstart) + local * bm,
              jnp.clip(of_expert(seg) - local * bm, 0, bm),
              of_expert(ordinal),
              (local == 0).astype(jnp.int32),
              active_experts]
    tables = [a.astype(jnp.int32) for a in tables]
    zrow = pend[-1:].astype(jnp.int32)

    lists = plan3[:, PLAN_HALF:PLAN_HALF + N_COPY_CLASSES, :N_EXPERTS]
    full, _ = _copy_split(n8)
    nth = jnp.arange(FULL_COPIES, dtype=jnp.int32) * FULL_COPY
    skip = jnp.concatenate([jnp.broadcast_to((full * FULL_COPY)[:, None, :],
                                             (ng, N_COPY_CLASSES - FULL_COPIES, N_EXPERTS)),
                            jnp.broadcast_to(nth[None, :, None], (ng, FULL_COPIES, N_EXPERTS))], axis=1)
    pick = lists[..., None] == experts
    dst = jnp.sum(jnp.where(pick, (gdest[:, None, :] + skip)[:, :, None, :], 0), axis=-1)
    dst = jnp.pad(dst, ((0, 0), (0, PLAN_HALF - N_COPY_CLASSES), (0, LANES - N_EXPERTS)))
    plan = jnp.concatenate([plan3[:, :PLAN_HALF], dst], axis=1).astype(jnp.int32).reshape(-1)

    xs = _dispatch(x1, pos, plan, tot, zrow, n_rows, tg, ns, bm)
    yb = _gmm(xs, tables, w1, w3, w2, layer, bm)
    return _combine(yb, pos, gate, plan, tot, x1, ws1, ws3, ws2, g, b, alpha, tg, ns)


def kernel(x, ln_gain, ln_bias, pool_w, pool_scale, attn_wqkv, attn_bqkv, attn_sinks, attn_wo, attn_bo,
           conv_w_in, conv_w, conv_w_out, router_w, router_bias, expert_w1, expert_w3, expert_w2,
           shared_w1, shared_w3, shared_w2):
    bsz, seq, d = x.shape
    depth = ln_gain.shape[0]
    alpha = float((2 * depth) ** 0.25)
    h = x.reshape(bsz * seq, d)
    tg = min(GROUP_TG, seq)
    for i in range(depth):
        kind = i % N_MIXERS
        j = i // N_MIXERS
        g1, b1 = ln_gain[i, 0], ln_bias[i, 0]
        route = (router_w[i], router_bias[i], tg)
        if kind == 0:
            h, *routing = _pool_mixer(h, seq, pool_w[j], pool_scale[j], g1, b1, alpha, *route)
        elif kind == 1:
            h, *routing = _swa_mixer(h, seq, attn_wqkv[j], attn_bqkv[j], attn_sinks[j], attn_wo[j],
                                     attn_bo[j], g1, b1, alpha, *route)
        else:
            h, *routing = _conv_mixer(h, seq, conv_w_in[j], conv_w[j], conv_w_out[j], g1, b1, alpha, *route)
        h = _moe_layer(h, routing, tg, i, expert_w1, expert_w3, expert_w2,
                       shared_w1[i], shared_w3[i], shared_w2[i], ln_gain[i, 1], ln_bias[i, 1], alpha)
    return h.reshape(bsz, seq, d)
```

```python
import functools

import jax
import jax.numpy as jnp
from jax import lax
from jax.experimental import pallas as pl
from jax.experimental.pallas import tpu as pltpu

POOL_WINDOWS = (2, 4, 8, 16)
N_MIXERS = 3
HEAD_DIM = 64
N_Q_HEADS = 16
N_KV_HEADS = 4
GQA_GROUP = N_Q_HEADS // N_KV_HEADS
WINDOW = 128
ROPE_THETA = 10000.0
N_EXPERTS = 64
TOP_K = 8
N_EXPERT_GROUPS = 8
GROUP_SIZE = N_EXPERTS // N_EXPERT_GROUPS
TOPK_GROUPS = 4
ROUTED_SCALE = 2.5
LN_EPS = 1e-5

LANES = 128
SUBLANES = 8
POOL_HALO = 16
VMEM_LIMIT = 48 * 1024 * 1024

POOL_TT = 1024
QKV_TT = 512
ATTN_TQ = 512
CONV_TT = 512
GROUP_TG = 256
CHUNK_ALIGN = SUBLANES
SORT_CHUNK = 256
EXPERT_BM = 1024
GMM_X_SLOTS = 3
FULL_COPY = 64
FULL_COPIES = GROUP_TG // FULL_COPY
COPY_SIZES = tuple(range(CHUNK_ALIGN, FULL_COPY, CHUNK_ALIGN)) + (FULL_COPY,) * FULL_COPIES
N_COPY_CLASSES = len(COPY_SIZES)
PLAN_HALF = 2 * SUBLANES
PLAN_COUNTS = N_COPY_CLASSES
PLAN_TOTAL = N_COPY_CLASSES + 1
PLAN_ROWS = 2 * PLAN_HALF
PLAN_WORDS = PLAN_ROWS * LANES

U32 = jnp.uint32
HI_MASK = 0xFFFF0000

F32 = jnp.float32
BF16 = jnp.bfloat16
NEG_INF = float("-inf")


def _layer_norm(z, g, b):
    mu = jnp.mean(z, axis=-1, keepdims=True)
    zc = z - mu
    var = jnp.mean(zc * zc, axis=-1, keepdims=True)
    return zc * lax.rsqrt(var + LN_EPS) * g + b


def _params(*sem):
    return pltpu.CompilerParams(dimension_semantics=sem, vmem_limit_bytes=VMEM_LIMIT)


def _router_plumbing(router_w, router_bias, t, d, tt, tg, tile_index):
    groups = tt // tg
    const = lambda *ids: (0, 0)
    by_cols = lambda *ids: (0, tile_index(*ids))
    by_rows = lambda *ids: (tile_index(*ids), 0)
    operands = [router_w.T, router_bias.reshape(N_EXPERTS, 1)]
    in_specs = [pl.BlockSpec((N_EXPERTS, d), const), pl.BlockSpec((N_EXPERTS, 1), const)]
    out_specs = [pl.BlockSpec((TOP_K, tt), by_cols), pl.BlockSpec((TOP_K, tt), by_cols),
                 pl.BlockSpec((groups * N_EXPERTS, LANES), by_rows),
                 pl.BlockSpec((groups * PLAN_ROWS, LANES), by_rows)]
    out_shapes = [jax.ShapeDtypeStruct((TOP_K, t), jnp.int32), jax.ShapeDtypeStruct((TOP_K, t), F32),
                  jax.ShapeDtypeStruct((t // tg * N_EXPERTS, LANES), jnp.int32),
                  jax.ShapeDtypeStruct((t // tg * PLAN_ROWS, LANES), jnp.int32)]
    return operands, in_specs, out_specs, out_shapes


def _pool_kernel(x_ref, halo_ref, w_ref, scale_ref, g_ref, b_ref, wt_ref, rb_ref, o_ref, pos_ref, gate_ref,
                 n8_ref, plan_ref, buf_ref, *, alpha, tt, tg):
    i = pl.program_id(1)
    x = x_ref[...]
    d = x.shape[1]
    ch = d // len(POOL_WINDOWS)
    top = SUBLANES
    first = top + POOL_HALO
    n = first + tt
    for k in range(3):
        buf_ref[k, 0:top, :] = jnp.zeros((top, d), F32)
    buf_ref[0, top:first, :] = jnp.where(i > 0, halo_ref[...], 0.0)
    buf_ref[0, first:n, :] = x
    pos = i * tt + lax.broadcasted_iota(jnp.int32, (tt, 1), 0)
    ys = []
    for g, w in enumerate(POOL_WINDOWS):
        c0 = g * ch
        xg = x[:, c0:c0 + ch]
        src, span = 0, 1
        while span < w:
            dst = 1 if src != 1 else 2
            buf_ref[dst, top:n, c0:c0 + ch] = (buf_ref[src, top:n, c0:c0 + ch]
                                               + buf_ref[src, top - span:n - span, c0:c0 + ch])
            src, span = dst, 2 * span
        s = buf_ref[src, first:n, c0:c0 + ch]
        inv_cnt = 1.0 / jnp.minimum(pos + 1, w).astype(F32)
        pooled = s * inv_cnt - xg
        ys.append(jnp.dot(pooled.astype(BF16), w_ref[g], preferred_element_type=F32))
    y = jnp.concatenate(ys, axis=1) * scale_ref[...]
    o_ref[...] = _layer_norm(alpha * x + y, g_ref[...], b_ref[...])
    _route_tile(o_ref, (wt_ref, rb_ref), (pos_ref, gate_ref, n8_ref, plan_ref), tg)


def _pool_mixer(x2, seq, w_grp, scale, g, b, alpha, router_w, router_bias, tg):
    t, d = x2.shape
    tt = min(POOL_TT, seq)
    nt = seq // tt
    nb = t // seq
    row = lambda v: v.reshape(1, d)
    const2 = lambda bi, i: (0, 0)
    tile = lambda bi, i: bi * nt + i
    r_ops, r_in, r_out, r_shapes = _router_plumbing(router_w, router_bias, t, d, tt, tg, tile)
    return pl.pallas_call(
        functools.partial(_pool_kernel, alpha=alpha, tt=tt, tg=tg),
        grid=(nb, nt),
        in_specs=[
            pl.BlockSpec((tt, d), lambda bi, i: (bi * nt + i, 0)),
            pl.BlockSpec((POOL_HALO, d),
                         lambda bi, i: (jnp.maximum((bi * seq + i * tt) // POOL_HALO - 1, 0), 0)),
            pl.BlockSpec(w_grp.shape, lambda bi, i: (0, 0, 0)),
            pl.BlockSpec((1, d), const2),
            pl.BlockSpec((1, d), const2),
            pl.BlockSpec((1, d), const2),
        ] + r_in,
        out_specs=[pl.BlockSpec((tt, d), lambda bi, i: (bi * nt + i, 0))] + r_out,
        out_shape=[jax.ShapeDtypeStruct((t, d), F32)] + r_shapes,
        scratch_shapes=[pltpu.VMEM((3, SUBLANES + POOL_HALO + tt, d), F32)],
        compiler_params=_params("parallel", "parallel"),
        name="pool_mixer",
    )(x2, x2, w_grp.astype(BF16), row(scale), row(g), row(b), *r_ops)


def _qkv_kernel(x_ref, w_ref, b_ref, c_ref, s_ref, q_ref, k_ref, v_ref, *, nq, nk):
    y = jnp.dot(x_ref[...].astype(BF16), w_ref[...], preferred_element_type=F32) + b_ref[...]
    c = c_ref[...]
    s = s_ref[...]
    lane = lax.broadcasted_iota(jnp.int32, c.shape, 1)
    first_half = (lane % HEAD_DIM) < (HEAD_DIM // 2)

    def rope(v):
        partner = jnp.where(first_half, pltpu.roll(v, LANES - HEAD_DIM // 2, 1),
                            pltpu.roll(v, HEAD_DIM // 2, 1))
        return v * c + partner * s

    for j in range(nq // LANES):
        q_ref[:, j * LANES:(j + 1) * LANES] = (
            rope(y[:, j * LANES:(j + 1) * LANES]) * (HEAD_DIM ** -0.5)).astype(BF16)
    for j in range(nk // LANES):
        k_ref[:, j * LANES:(j + 1) * LANES] = rope(
            y[:, nq + j * LANES:nq + (j + 1) * LANES]).astype(BF16)
    v_ref[...] = y[:, nq + nk:].astype(BF16)


def _attn_kernel(sink_ref, q_ref, k_ref, kh_ref, v_ref, vh_ref, x_ref, wo_ref, bo_ref, g_ref, b_ref,
                 wt_ref, rb_ref, o_ref, pos_ref, gate_ref, n8_ref, plan_ref, *, alpha, tq, tg):
    i = pl.program_id(1)
    nqb = tq // WINDOW
    stack = GQA_GROUP * tq
    row_head = lax.broadcasted_iota(jnp.int32, (stack, 1), 0) // tq
    r = lax.broadcasted_iota(jnp.int32, (stack, 2 * WINDOW), 0) % tq
    c = lax.broadcasted_iota(jnp.int32, (stack, 2 * WINDOW), 1)
    dist = (r & (WINDOW - 1)) + WINDOW - c
    valid = (dist >= 0) & (dist < WINDOW) & ((c >= WINDOW) | (r >= WINDOW) | (i > 0))
    lane = lax.broadcasted_iota(jnp.int32, (tq, LANES), 1)
    low = lane < HEAD_DIM
    dn = (((1,), (1,)), ((), ()))
    outs = []
    for kv in range(N_KV_HEADS):
        sl = slice(kv * LANES, (kv + 1) * LANES)
        kk = jnp.concatenate([kh_ref[:, sl], k_ref[:, sl]], axis=0)
        vv = jnp.concatenate([vh_ref[:, sl], v_ref[:, sl]], axis=0)
        blocks = []
        for h in range(GQA_GROUP):
            c0 = (kv * GQA_GROUP + h - h % 2) * HEAD_DIM
            qp = q_ref[:, c0:c0 + LANES]
            keep = low if h % 2 == 0 else jnp.logical_not(low)
            qh = jnp.where(keep, qp, jnp.zeros_like(qp))
            sc_all = lax.dot_general(qh, kk, dn, preferred_element_type=F32)
            blocks += [sc_all[qb * WINDOW:(qb + 1) * WINDOW, qb * WINDOW:(qb + 2) * WINDOW]
                       for qb in range(nqb)]
        sc = jnp.where(valid, jnp.concatenate(blocks, axis=0), NEG_INF)
        sink = jnp.zeros((stack, 1), F32)
        for h in range(GQA_GROUP):
            sink = jnp.where(row_head == h, sink_ref[kv * GQA_GROUP + h], sink)
        m = jnp.maximum(jnp.max(sc, axis=1, keepdims=True), sink)
        pm = jnp.exp(sc - m)
        rden = 1.0 / (jnp.sum(pm, axis=1, keepdims=True) + jnp.exp(sink - m))
        pm = pm.astype(BF16)
        res = jnp.concatenate(
            [jnp.dot(pm[b * WINDOW:(b + 1) * WINDOW, :],
                     vv[(b % nqb) * WINDOW:(b % nqb + 2) * WINDOW, :], preferred_element_type=F32)
             for b in range(GQA_GROUP * nqb)], axis=0) * rden
        for p in range(GQA_GROUP // 2):
            outs.append(jnp.where(low, res[2 * p * tq:(2 * p + 1) * tq, :],
                                  res[(2 * p + 1) * tq:(2 * p + 2) * tq, :]))
    o_all = jnp.concatenate(outs, axis=1).astype(BF16)
    mix = jnp.dot(o_all, wo_ref[...], preferred_element_type=F32) + bo_ref[...]
    o_ref[...] = _layer_norm(alpha * x_ref[...] + mix, g_ref[...], b_ref[...])
    _route_tile(o_ref, (wt_ref, rb_ref), (pos_ref, gate_ref, n8_ref, plan_ref), tg)


def _dup_heads(w, n_heads):
    lead = w.shape[:-1]
    w = w.reshape(lead + (n_heads, 1, HEAD_DIM))
    w = jnp.broadcast_to(w, lead + (n_heads, 2, HEAD_DIM))
    return w.reshape(lead + (n_heads * 2 * HEAD_DIM,))


def _swa_mixer(x2, seq, wqkv, bqkv, sinks, wo, bo, g, b, alpha, router_w, router_bias, tg):
    t, d = x2.shape
    nb = t // seq
    nq = N_Q_HEADS * HEAD_DIM
    nkv = N_KV_HEADS * HEAD_DIM
    nk2 = 2 * nkv
    w_ext = jnp.concatenate([wqkv[:, :nq], _dup_heads(wqkv[:, nq:nq + nkv], N_KV_HEADS),
                             _dup_heads(wqkv[:, nq + nkv:], N_KV_HEADS)], axis=1).astype(BF16)
    b_ext = jnp.concatenate([bqkv[:nq], _dup_heads(bqkv[nq:nq + nkv], N_KV_HEADS),
                             _dup_heads(bqkv[nq + nkv:], N_KV_HEADS)]).reshape(1, -1)
    n_ext = nq + 2 * nk2

    pos = jnp.arange(seq, dtype=F32)
    inv_freq = ROPE_THETA ** (-jnp.arange(0, HEAD_DIM, 2, dtype=F32) / HEAD_DIM)
    ang = pos[:, None] * inv_freq[None, :]
    cos, sin = jnp.cos(ang), jnp.sin(ang)
    cos_t = jnp.concatenate([cos, cos, cos, cos], axis=1)
    sin_t = jnp.concatenate([-sin, sin, -sin, sin], axis=1)

    tt = min(QKV_TT, seq)
    nt = seq // tt
    const2 = lambda i: (0, 0)
    q, k2, v2 = pl.pallas_call(
        functools.partial(_qkv_kernel, nq=nq, nk=nk2),
        grid=(t // tt,),
        in_specs=[
            pl.BlockSpec((tt, d), lambda i: (i, 0)),
            pl.BlockSpec((d, n_ext), const2),
            pl.BlockSpec((1, n_ext), const2),
            pl.BlockSpec((tt, LANES), lambda i: (i % nt, 0)),
            pl.BlockSpec((tt, LANES), lambda i: (i % nt, 0)),
        ],
        out_specs=[pl.BlockSpec((tt, nq), lambda i: (i, 0)),
                   pl.BlockSpec((tt, nk2), lambda i: (i, 0)),
                   pl.BlockSpec((tt, nk2), lambda i: (i, 0))],
        out_shape=[jax.ShapeDtypeStruct((t, nq), BF16),
                   jax.ShapeDtypeStruct((t, nk2), BF16),
                   jax.ShapeDtypeStruct((t, nk2), BF16)],
        compiler_params=_params("parallel"),
        name="swa_qkv",
    )(x2, w_ext, b_ext, cos_t, sin_t)

    tq = min(ATTN_TQ, seq)
    ntq = seq // tq
    main = lambda bi, i: (bi * ntq + i, 0)
    halo = lambda bi, i: (jnp.maximum((bi * seq + i * tq) // WINDOW - 1, 0), 0)
    const = lambda bi, i: (0, 0)
    row = lambda v: v.reshape(1, d)
    r_ops, r_in, r_out, r_shapes = _router_plumbing(router_w, router_bias, t, d, tq, tg,
                                                    lambda bi, i: bi * ntq + i)
    return pl.pallas_call(
        functools.partial(_attn_kernel, alpha=alpha, tq=tq, tg=tg),
        grid=(nb, ntq),
        in_specs=[
            pl.BlockSpec(memory_space=pltpu.SMEM),
            pl.BlockSpec((tq, nq), main),
            pl.BlockSpec((tq, nk2), main),
            pl.BlockSpec((WINDOW, nk2), halo),
            pl.BlockSpec((tq, nk2), main),
            pl.BlockSpec((WINDOW, nk2), halo),
            pl.BlockSpec((tq, d), main),
            pl.BlockSpec((nq, d), const),
            pl.BlockSpec((1, d), const),
            pl.BlockSpec((1, d), const),
            pl.BlockSpec((1, d), const),
        ] + r_in,
        out_specs=[pl.BlockSpec((tq, d), main)] + r_out,
        out_shape=[jax.ShapeDtypeStruct((t, d), F32)] + r_shapes,
        compiler_params=_params("parallel", "parallel"),
        name="swa_attention",
    )(sinks, q, k2, k2, v2, v2, x2, wo.astype(BF16), row(bo), row(g), row(b), *r_ops)


def _conv_kernel(x_ref, win_ref, cw_ref, wout_ref, g_ref, b_ref, wt_ref, rb_ref, o_ref, pos_ref, gate_ref,
                 n8_ref, plan_ref, ubuf_ref, *, alpha, tt, tg):
    i = pl.program_id(1)
    d = x_ref.shape[1]

    @pl.when(i == 0)
    def _():
        ubuf_ref[0:SUBLANES, :] = jnp.zeros((SUBLANES, d), F32)

    x = x_ref[...]
    proj = jnp.dot(x.astype(BF16), win_ref[...], preferred_element_type=F32)
    gate_b = proj[:, :d]
    u = proj[:, d:2 * d] * proj[:, 2 * d:]
    ubuf_ref[SUBLANES:SUBLANES + tt, :] = u
    cw = cw_ref[...]
    conv = (cw[0:1, :] * ubuf_ref[SUBLANES - 2:SUBLANES - 2 + tt, :]
            + cw[1:2, :] * ubuf_ref[SUBLANES - 1:SUBLANES - 1 + tt, :]
            + cw[2:3, :] * u)
    ubuf_ref[0:SUBLANES, :] = u[tt - SUBLANES:tt, :]
    y = jnp.dot((gate_b * conv).astype(BF16), wout_ref[...], preferred_element_type=F32)
    o_ref[...] = _layer_norm(alpha * x + y, g_ref[...], b_ref[...])
    _route_tile(o_ref, (wt_ref, rb_ref), (pos_ref, gate_ref, n8_ref, plan_ref), tg)


def _conv_mixer(x2, seq, w_in, conv_w, w_out, g, b, alpha, router_w, router_bias, tg):
    t, d = x2.shape
    nb = t // seq
    tt = min(CONV_TT, seq)
    nt = seq // tt
    const = lambda bi, i: (0, 0)
    row = lambda v: v.reshape(1, d)
    r_ops, r_in, r_out, r_shapes = _router_plumbing(router_w, router_bias, t, d, tt, tg,
                                                    lambda bi, i: bi * nt + i)
    return pl.pallas_call(
        functools.partial(_conv_kernel, alpha=alpha, tt=tt, tg=tg),
        grid=(nb, nt),
        in_specs=[
            pl.BlockSpec((tt, d), lambda bi, i: (bi * nt + i, 0)),
            pl.BlockSpec((d, 3 * d), const),
            pl.BlockSpec(conv_w.shape, const),
            pl.BlockSpec((d, d), const),
            pl.BlockSpec((1, d), const),
            pl.BlockSpec((1, d), const),
        ] + r_in,
        out_specs=[pl.BlockSpec((tt, d), lambda bi, i: (bi * nt + i, 0))] + r_out,
        out_shape=[jax.ShapeDtypeStruct((t, d), F32)] + r_shapes,
        scratch_shapes=[pltpu.VMEM((SUBLANES + tt, d), F32)],
        compiler_params=_params("arbitrary", "arbitrary"),
        name="conv_mixer",
    )(x2, w_in.astype(BF16), conv_w, w_out.astype(BF16), row(g), row(b), *r_ops)


def _first_argmax(v, rows, n):
    m = jnp.max(v, axis=0, keepdims=True)
    first = jnp.min(jnp.where(v == m, rows, n), axis=0, keepdims=True)
    return m, first


def _route_tile(o_ref, router_refs, route_out_refs, tg):
    wt_ref, bias_ref = router_refs
    pos_ref, gate_ref, n8_ref, plan_ref = route_out_refs
    tt = o_ref.shape[0]
    x = o_ref[...]
    xh = x.astype(BF16)
    xl = (x - xh.astype(F32)).astype(BF16)
    wt = wt_ref[...]
    wh = wt.astype(BF16)
    wl = (wt - wh.astype(F32)).astype(BF16)
    dn = (((1,), (1,)), ((), ()))
    logits = (lax.dot_general(wh, xh, dn, preferred_element_type=F32)
              + (lax.dot_general(wh, xl, dn, preferred_element_type=F32)
                 + lax.dot_general(wl, xh, dn, preferred_element_type=F32)))
    scores = jax.nn.sigmoid(logits)
    sel = scores + bias_ref[...]

    rows8 = lax.broadcasted_iota(jnp.int32, (GROUP_SIZE, tt), 0)
    gsc = []
    for gi in range(N_EXPERT_GROUPS):
        blk = sel[gi * GROUP_SIZE:(gi + 1) * GROUP_SIZE, :]
        m1, f1 = _first_argmax(blk, rows8, GROUP_SIZE)
        m2 = jnp.max(jnp.where(rows8 == f1, NEG_INF, blk), axis=0, keepdims=True)
        gsc.append(m1 + m2)
    cur = jnp.concatenate(gsc, axis=0)
    rows_g = lax.broadcasted_iota(jnp.int32, (N_EXPERT_GROUPS, tt), 0)
    gmask = jnp.zeros((N_EXPERT_GROUPS, tt), F32)
    for _ in range(TOPK_GROUPS):
        _, f = _first_argmax(cur, rows_g, N_EXPERT_GROUPS)
        hit = rows_g == f
        gmask = jnp.where(hit, 1.0, gmask)
        cur = jnp.where(hit, NEG_INF, cur)
    emask = jnp.concatenate(
        [jnp.broadcast_to(gmask[gi:gi + 1, :], (GROUP_SIZE, tt)) for gi in range(N_EXPERT_GROUPS)],
        axis=0)

    rows_e = lax.broadcasted_iota(jnp.int32, (N_EXPERTS, tt), 0)
    cur = jnp.where(emask > 0.0, sel, NEG_INF)
    idxs, gates = [], []
    member = jnp.zeros((N_EXPERTS, tt), F32)
    for _ in range(TOP_K):
        _, f = _first_argmax(cur, rows_e, N_EXPERTS)
        hit = rows_e == f
        idxs.append(f)
        gates.append(jnp.sum(jnp.where(hit, scores, 0.0), axis=0, keepdims=True))
        member = jnp.where(hit, 1.0, member)
        cur = jnp.where(hit, NEG_INF, cur)
    gate = jnp.concatenate(gates, axis=0)
    gate_ref[...] = gate / jnp.sum(gate, axis=0, keepdims=True) * ROUTED_SCALE

    for s in range(tt // tg):
        cols = slice(s * tg, (s + 1) * tg)
        _plan_group(member[:, cols], [f[:, cols] for f in idxs], pos_ref.at[:, pl.ds(s * tg, tg)],
                    n8_ref.at[pl.ds(s * N_EXPERTS, N_EXPERTS), :],
                    plan_ref.at[pl.ds(s * PLAN_ROWS, PLAN_ROWS), :], tg)


def _plan_group(member, idxs, pos_ref, n8_ref, plan_ref, tt):
    rows_e = lax.broadcasted_iota(jnp.int32, (N_EXPERTS, tt), 0)
    ta = lax.broadcasted_iota(jnp.int32, (tt, tt), 0)
    tb = lax.broadcasted_iota(jnp.int32, (tt, tt), 1)
    earlier = (ta < tb).astype(BF16)
    before = jnp.dot(member.astype(BF16), earlier, preferred_element_type=F32)
    n = jnp.sum(member, axis=1, keepdims=True)
    n8 = jnp.floor((n + (CHUNK_ALIGN - 1)) * (1.0 / CHUNK_ALIGN)) * CHUNK_ALIGN
    n8_wide = jnp.broadcast_to(n8, (N_EXPERTS, LANES))
    ea = lax.broadcasted_iota(jnp.int32, (N_EXPERTS, N_EXPERTS), 0)
    eb = lax.broadcasted_iota(jnp.int32, (N_EXPERTS, N_EXPERTS), 1)
    lower = (eb < ea).astype(BF16)
    chunk_start = jnp.dot(lower, n8_wide.astype(BF16), preferred_element_type=F32)
    where_to = before + chunk_start[:, 0:1]
    for k in range(TOP_K):
        pk = jnp.sum(jnp.where(rows_e == idxs[k], where_to, 0.0), axis=0, keepdims=True)
        pos_ref[k:k + 1, :] = pk.astype(jnp.int32)
    n8_int = n8_wide.astype(jnp.int32)
    n8_ref[...] = n8_int

    lane = lax.broadcasted_iota(jnp.int32, (N_EXPERTS, LANES), 1)
    full, rest = _copy_split(n8_int)
    flag = jnp.zeros((N_EXPERTS, LANES), F32)
    before_copy = jnp.zeros((N_EXPERTS, LANES), jnp.int32)
    for j, size in enumerate(COPY_SIZES):
        if size < FULL_COPY:
            needs, skip = rest == size, full * FULL_COPY
        else:
            nth = j - (N_COPY_CLASSES - FULL_COPIES)
            needs, skip = full > nth, nth * FULL_COPY
        flag = jnp.where((lane == j) & needs, 1.0, flag)
        before_copy = jnp.where(lane == j, skip, before_copy)
    rank = jnp.dot(lower, flag.astype(BF16), preferred_element_type=F32)
    e_id = lax.broadcasted_iota(jnp.int32, (N_EXPERTS, LANES), 0)
    start_int = chunk_start.astype(jnp.int32)
    plan_ref[...] = jnp.zeros(plan_ref.shape, jnp.int32)
    for j in range(N_COPY_CLASSES):
        chosen = (rank[:, j:j + 1] == lane.astype(F32)) & (flag[:, j:j + 1] > 0.0)
        buf_row = start_int + before_copy[:, j:j + 1]
        plan_ref[j:j + 1, :] = jnp.sum(jnp.where(chosen, buf_row, 0), axis=0, keepdims=True)
        plan_ref[PLAN_HALF + j:PLAN_HALF + j + 1, :] = jnp.sum(jnp.where(chosen, e_id, 0), axis=0,
                                                              keepdims=True)
    plan_ref[PLAN_COUNTS:PLAN_COUNTS + 1, :] = jnp.sum(flag, axis=0, keepdims=True).astype(jnp.int32)
    plan_ref[PLAN_TOTAL:PLAN_TOTAL + 1, :] = jnp.sum(n8_int, axis=0, keepdims=True)


def _copy_split(n8):
    return lax.shift_right_logical(n8, FULL_COPY.bit_length() - 1), n8 & (FULL_COPY - 1)


def _pack_halves(v, exact=False):
    h = v.shape[1] // 2
    return _pack_pair(v[:, :h], v[:, h:], exact)


def _pack_pair(lo, hi, exact=False):
    if not exact:
        lo, hi = lo.astype(BF16).astype(F32), hi.astype(BF16).astype(F32)
    lo = lax.bitcast_convert_type(lo, U32)
    hi = lax.bitcast_convert_type(hi, U32)
    return (lo >> 16) | (hi & jnp.uint32(HI_MASK))


def _chunk_relative(pos, c):
    rel = pos - c * SORT_CHUNK
    inside = (rel >= 0) & (rel < SORT_CHUNK)
    return jnp.where(inside, rel, -1).astype(F32).astype(BF16)


def _unpack_halves(w):
    lo = lax.bitcast_convert_type(w << 16, F32).astype(BF16)
    hi = lax.bitcast_convert_type(w & jnp.uint32(HI_MASK), F32).astype(BF16)
    return lo, hi


def _start_group_copies(plan_ref, make_copy):
    for j, size in enumerate(COPY_SIZES):
        def body(k, carry):
            make_copy(pl.multiple_of(plan_ref[j * LANES + k], CHUNK_ALIGN),
                      pl.multiple_of(plan_ref[(PLAN_HALF + j) * LANES + k], CHUNK_ALIGN), size).start()
            return carry

        lax.fori_loop(0, plan_ref[PLAN_COUNTS * LANES + j], body, 0)


def _wait_group_rows(total_rows, ns, make_copy):
    for j in range(ns.bit_length()):
        size = CHUNK_ALIGN << j
        if size > ns:
            break

        @pl.when((total_rows & size) != 0)
        def _():
            make_copy(0, 0, size).wait()


def _dispatch_kernel(tot_ref, zrow_ref, plan_ref, pos_ref, x_ref, xs_ref, sbuf_ref, zbuf_ref,
                     onehot_ref, sem, zsem, *, tg, ns, bm):
    g = pl.program_id(0)
    ng = pl.num_programs(0)
    slot = g % 2

    @pl.when(g == 0)
    def _():
        zbuf_ref[...] = jnp.zeros_like(zbuf_ref)
        row = pl.multiple_of(zrow_ref[0], CHUNK_ALIGN)
        tail = pltpu.make_async_copy(zbuf_ref, xs_ref.at[pl.ds(row, bm), :], zsem)
        tail.start()
        tail.wait()

    def copy_from(sl):
        def make_copy(src, dst, size):
            return pltpu.make_async_copy(sbuf_ref.at[sl, pl.ds(src, size), :],
                                         xs_ref.at[pl.ds(dst, size), :], sem.at[sl])
        return make_copy

    @pl.when(g >= 2)
    def _():
        _wait_group_rows(tot_ref[g - 2], ns, copy_from(slot))

    xb = x_ref[...].astype(BF16)
    pos = pos_ref[...]
    local_rows = lax.broadcasted_iota(jnp.int32, (SORT_CHUNK, tg), 0).astype(F32).astype(BF16)
    one = jnp.ones((SORT_CHUNK, tg), BF16)
    zero = jnp.zeros((SORT_CHUNK, tg), BF16)
    def onehot_chunk(c):
        rel = _chunk_relative(pos, c)
        hit = local_rows == rel[0:1, :]
        for k in range(1, TOP_K):
            hit = jnp.logical_or(hit, local_rows == rel[k:k + 1, :])
        return jnp.where(hit, one, zero)

    half = x_ref.shape[1] // 2

    def sort_rows(onehot, first_row):
        rows = onehot.shape[0]
        for n in range(half // SORT_CHUNK):
            lo = jnp.dot(onehot, xb[:, n * SORT_CHUNK:(n + 1) * SORT_CHUNK], preferred_element_type=F32)
            hi = jnp.dot(onehot, xb[:, half + n * SORT_CHUNK:half + (n + 1) * SORT_CHUNK],
                         preferred_element_type=F32)
            sbuf_ref[slot, first_row:first_row + rows, n * SORT_CHUNK:(n + 1) * SORT_CHUNK] = _pack_pair(
                lo, hi, exact=True)

    main = ns - SORT_CHUNK
    for c in range(main // SORT_CHUNK):
        onehot_ref[c * SORT_CHUNK:(c + 1) * SORT_CHUNK, :] = onehot_chunk(c)
    sort_rows(onehot_ref[0:main, :], 0)

    @pl.when(tot_ref[g] > main)
    def _():
        sort_rows(onehot_chunk(main // SORT_CHUNK), main)

    _start_group_copies(plan_ref, copy_from(slot))

    @pl.when(g == ng - 1)
    def _():
        @pl.when(g >= 1)
        def _():
            _wait_group_rows(tot_ref[g - 1], ns, copy_from(1 - slot))
        _wait_group_rows(tot_ref[g], ns, copy_from(slot))


def _dispatch(x1, pos, plan, tot, zrow, n_rows, tg, ns, bm):
    t, d = x1.shape
    grid_spec = pltpu.PrefetchScalarGridSpec(
        num_scalar_prefetch=2,
        grid=(t // tg,),
        in_specs=[
            pl.BlockSpec((PLAN_WORDS,), lambda i, *_: (i,), memory_space=pltpu.SMEM),
            pl.BlockSpec((TOP_K, tg), lambda i, *_: (0, i)),
            pl.BlockSpec((tg, d), lambda i, *_: (i, 0)),
        ],
        out_specs=pl.BlockSpec(memory_space=pl.ANY),
        scratch_shapes=[pltpu.VMEM((2, ns, d // 2), U32), pltpu.VMEM((bm, d // 2), U32),
                        pltpu.VMEM((ns - SORT_CHUNK, tg), BF16),
                        pltpu.SemaphoreType.DMA((2,)), pltpu.SemaphoreType.DMA],
    )
    return pl.pallas_call(
        functools.partial(_dispatch_kernel, tg=tg, ns=ns, bm=bm),
        grid_spec=grid_spec,
        out_shape=jax.ShapeDtypeStruct((n_rows, d // 2), U32),
        compiler_params=_params("arbitrary"),
        name="moe_dispatch",
    )(tot, zrow, plan, pos, x1)


def _gmm_kernel(nq_ref, brow_ref, bval_ref, bord_ref, bfirst_ref, aexp_ref,
                xs_ref, w1_ref, w3_ref, w2_ref, y_ref,
                xbuf_ref, ybuf_ref, w1s_ref, w3s_ref, w2s_ref, w13_ref, w2b_ref, xsem, ysem, wsem,
                *, layer, bm):
    nq = nq_ref[0]
    na = nq_ref[1]
    f = w1s_ref.shape[2]
    half = xbuf_ref.shape[2]
    nx = xbuf_ref.shape[0]

    def x_copy(q, slot):
        row = pl.multiple_of(brow_ref[q], CHUNK_ALIGN)
        return pltpu.make_async_copy(xs_ref.at[pl.ds(row, bm), :], xbuf_ref.at[slot], xsem.at[slot])

    def w_copies(j, slot):
        e = aexp_ref[j]
        return (pltpu.make_async_copy(w1_ref.at[layer, e], w1s_ref.at[slot], wsem.at[slot]),
                pltpu.make_async_copy(w3_ref.at[layer, e], w3s_ref.at[slot], wsem.at[slot]),
                pltpu.make_async_copy(w2_ref.at[layer, e], w2s_ref.at[slot], wsem.at[slot]))

    def y_copy(slot, row, off, size):
        return pltpu.make_async_copy(ybuf_ref.at[slot, pl.ds(off, size), :],
                                     y_ref.at[pl.ds(pl.multiple_of(row + off, CHUNK_ALIGN), size), :],
                                     ysem.at[slot])

    def for_each_y_copy(slot, row, valid, fn):
        @pl.when(valid == bm)
        def _():
            fn(y_copy(slot, row, 0, bm))

        @pl.when(valid < bm)
        def _():
            for j in range((bm // CHUNK_ALIGN).bit_length() - 1):
                size = CHUNK_ALIGN << j

                @pl.when((valid & size) != 0)
                def _():
                    fn(y_copy(slot, row, pl.multiple_of(valid & (size - 1), CHUNK_ALIGN), size))

    for c in w_copies(0, 0):
        c.start()
    for q0 in range(nx - 1):
        @pl.when(q0 < nq)
        def _():
            x_copy(q0, q0).start()

    def body(q, carry):
        xslot = q % nx
        yslot = q % 2

        @pl.when(q + (nx - 1) < nq)
        def _():
            x_copy(q + (nx - 1), (q + (nx - 1)) % nx).start()

        @pl.when(bfirst_ref[q] == 1)
        def _():
            j = bord_ref[q]
            ws = j % 2
            for c in w_copies(j, ws):
                c.wait()

            @pl.when(j + 1 < na)
            def _():
                for c in w_copies(j + 1, 1 - ws):
                    c.start()
            w13_ref[:, :f] = w1s_ref[ws].astype(BF16)
            w13_ref[:, f:] = w3s_ref[ws].astype(BF16)
            w2b_ref[...] = w2s_ref[ws].astype(BF16)

        x_copy(q, xslot).wait()

        @pl.when(q >= 2)
        def _():
            for_each_y_copy(yslot, 0, bval_ref[jnp.maximum(q - 2, 0)], lambda c: c.wait())

        def swiglu(rows):
            x_lo, x_hi = _unpack_halves(xbuf_ref[xslot, 0:rows, :])
            h13 = (jnp.dot(x_lo, w13_ref[:half, :], preferred_element_type=F32)
                   + jnp.dot(x_hi, w13_ref[half:, :], preferred_element_type=F32))
            h = jax.nn.silu(h13[:, :f]) * h13[:, f:]
            ybuf_ref[yslot, 0:rows, :] = _pack_halves(
                jnp.dot(h.astype(BF16), w2b_ref[...], preferred_element_type=F32))

        @pl.when(bval_ref[q] > bm // 2)
        def _():
            swiglu(bm)

        @pl.when((bval_ref[q] <= bm // 2) & (bval_ref[q] > bm // 4))
        def _():
            swiglu(bm // 2)

        @pl.when(bval_ref[q] <= bm // 4)
        def _():
            swiglu(bm // 4)
        for_each_y_copy(yslot, brow_ref[q], bval_ref[q], lambda c: c.start())
        return carry

    lax.fori_loop(0, nq, body, 0)

    @pl.when(nq >= 2)
    def _():
        for_each_y_copy(nq % 2, 0, bval_ref[jnp.maximum(nq - 2, 0)], lambda c: c.wait())
    for_each_y_copy((nq - 1) % 2, 0, bval_ref[nq - 1], lambda c: c.wait())


def _gmm(xs, tables, w1, w3, w2, layer, bm):
    n_rows, half = xs.shape
    d = 2 * half
    f = w1.shape[-1]
    any_spec = pl.BlockSpec(memory_space=pl.ANY)
    grid_spec = pltpu.PrefetchScalarGridSpec(
        num_scalar_prefetch=len(tables),
        grid=(1,),
        in_specs=[any_spec, any_spec, any_spec, any_spec],
        out_specs=any_spec,
        scratch_shapes=[pltpu.VMEM((GMM_X_SLOTS, bm, half), U32), pltpu.VMEM((2, bm, half), U32),
                        pltpu.VMEM((2, d, f), F32), pltpu.VMEM((2, d, f), F32), pltpu.VMEM((2, f, d), F32),
                        pltpu.VMEM((d, 2 * f), BF16), pltpu.VMEM((f, d), BF16),
                        pltpu.SemaphoreType.DMA((GMM_X_SLOTS,)), pltpu.SemaphoreType.DMA((2,)),
                        pltpu.SemaphoreType.DMA((2,))],
    )
    return pl.pallas_call(
        functools.partial(_gmm_kernel, layer=layer, bm=bm),
        grid_spec=grid_spec,
        out_shape=jax.ShapeDtypeStruct((n_rows, half), U32),
        compiler_params=_params("arbitrary"),
        name="moe_experts",
    )(*tables, xs, w1, w3, w2)


def _combine_kernel(tot_ref, plan_ref, plan_next_ref, yb_ref, pos_ref, gate_ref, x_ref, ws13_ref, ws2_ref,
                    g_ref, b_ref, o_ref, ybuf_ref, wgt_ref, ylo_ref, yhi_ref, sem, *, alpha, tg, ns):
    g = pl.program_id(0)
    ng = pl.num_programs(0)
    slot = g % 2

    def copy_into(sl):
        def make_copy(buf_row, hbm_row, size):
            return pltpu.make_async_copy(yb_ref.at[pl.ds(hbm_row, size), :],
                                         ybuf_ref.at[sl, pl.ds(buf_row, size), :], sem.at[sl])
        return make_copy

    @pl.when(g == 0)
    def _():
        ybuf_ref[...] = jnp.zeros_like(ybuf_ref)
        _start_group_copies(plan_ref, copy_into(0))

    @pl.when(g + 1 < ng)
    def _():
        _start_group_copies(plan_next_ref, copy_into(1 - slot))

    x = x_ref[...]
    f = ws2_ref.shape[0]
    h13 = jnp.dot(x.astype(BF16), ws13_ref[...], preferred_element_type=F32)
    hs = jax.nn.silu(h13[:, :f]) * h13[:, f:]
    shared = jnp.dot(hs.astype(BF16), ws2_ref[...], preferred_element_type=F32)

    _wait_group_rows(tot_ref[g], ns, copy_into(slot))

    pos = pos_ref[...]
    gate_b = gate_ref[...].astype(BF16)
    local_rows = lax.broadcasted_iota(jnp.int32, (SORT_CHUNK, tg), 0).astype(F32).astype(BF16)
    for c in range(ns // SORT_CHUNK):
        rel = _chunk_relative(pos, c)
        wgt = jnp.zeros((SORT_CHUNK, tg), BF16)
        for k in range(TOP_K):
            wgt = jnp.where(local_rows == rel[k:k + 1, :], gate_b[k:k + 1, :], wgt)
        wgt_ref[c * SORT_CHUNK:(c + 1) * SORT_CHUNK, :] = wgt
        y_lo, y_hi = _unpack_halves(ybuf_ref[slot, c * SORT_CHUNK:(c + 1) * SORT_CHUNK, :])
        ylo_ref[c * SORT_CHUNK:(c + 1) * SORT_CHUNK, :] = y_lo
        yhi_ref[c * SORT_CHUNK:(c + 1) * SORT_CHUNK, :] = y_hi
    wgt_all = wgt_ref[...]
    dn = (((0,), (0,)), ((), ()))
    routed = jnp.concatenate([lax.dot_general(wgt_all, ylo_ref[...], dn, preferred_element_type=F32),
                              lax.dot_general(wgt_all, yhi_ref[...], dn, preferred_element_type=F32)],
                             axis=1)
    o_ref[...] = _layer_norm(alpha * x + (routed + shared), g_ref[...], b_ref[...])


def _combine(yb, pos_kt, gate_kt, plan, tot, x1, ws1, ws3, ws2, g, b, alpha, tg, ns):
    t, d = x1.shape
    ng = t // tg
    f = ws1.shape[1]
    ws13 = jnp.concatenate([ws1, ws3], axis=1).astype(BF16)
    const = lambda i, *_: (0, 0)
    tile = lambda i, *_: (i, 0)
    row = lambda v: v.reshape(1, d)
    grid_spec = pltpu.PrefetchScalarGridSpec(
        num_scalar_prefetch=1,
        grid=(ng,),
        in_specs=[
            pl.BlockSpec((PLAN_WORDS,), lambda i, *_: (i,), memory_space=pltpu.SMEM),
            pl.BlockSpec((PLAN_WORDS,), lambda i, *_: (jnp.minimum(i + 1, ng - 1),),
                         memory_space=pltpu.SMEM),
            pl.BlockSpec(memory_space=pl.ANY),
            pl.BlockSpec((TOP_K, tg), lambda i, *_: (0, i)),
            pl.BlockSpec((TOP_K, tg), lambda i, *_: (0, i)),
            pl.BlockSpec((tg, d), tile),
            pl.BlockSpec((d, 2 * f), const),
            pl.BlockSpec((f, d), const),
            pl.BlockSpec((1, d), const),
            pl.BlockSpec((1, d), const),
        ],
        out_specs=pl.BlockSpec((tg, d), tile),
        scratch_shapes=[pltpu.VMEM((2, ns, d // 2), U32), pltpu.VMEM((ns, tg), BF16),
                        pltpu.VMEM((ns, d // 2), BF16), pltpu.VMEM((ns, d // 2), BF16),
                        pltpu.SemaphoreType.DMA((2,))],
    )
    return pl.pallas_call(
        functools.partial(_combine_kernel, alpha=alpha, tg=tg, ns=ns),
        grid_spec=grid_spec,
        out_shape=jax.ShapeDtypeStruct((t, d), F32),
        compiler_params=_params("arbitrary"),
        name="moe_combine",
    )(tot, plan, plan, yb, pos_kt, gate_kt, x1, ws13, ws2.astype(BF16), row(g), row(b))


def _moe_layer(x1, routing, tg, layer, w1, w3, w2, ws1, ws3, ws2, g, b, alpha):
    ng = x1.shape[0] // tg
    bm = EXPERT_BM
    ns = -(-(tg * TOP_K + N_EXPERTS * (CHUNK_ALIGN - 1)) // SORT_CHUNK) * SORT_CHUNK
    pos, gate, n8_wide, plan_wide = routing
    plan3 = plan_wide.reshape(ng, PLAN_ROWS, LANES)
    tot = plan3[:, PLAN_TOTAL, 0]

    n8 = n8_wide[:, 0].reshape(ng, N_EXPERTS)
    seg = jnp.sum(n8, axis=0)
    pend = jnp.cumsum(seg)
    pstart = pend - seg
    gdest = pstart[None, :] + jnp.cumsum(n8, axis=0) - n8
    max_rows = ng * (tg * TOP_K + N_EXPERTS * (CHUNK_ALIGN - 1))
    n_rows = max_rows + bm

    experts = jnp.arange(N_EXPERTS, dtype=jnp.int32)
    nb = (seg + bm - 1) // bm
    cb = jnp.cumsum(nb)
    blk = jnp.arange(max_rows // bm + N_EXPERTS, dtype=jnp.int32)
    blk_expert = jnp.minimum(jnp.sum((cb[None, :] <= blk[:, None]).astype(jnp.int32), axis=1), N_EXPERTS - 1)
    mine = blk_expert[:, None] == experts[None, :]
    of_expert = lambda v: jnp.sum(jnp.where(mine, v[None, :], 0), axis=1)
    local = blk - of_expert(cb - nb)
    active = seg > 0
    ordinal = jnp.cumsum(active.astype(jnp.int32)) - 1
    active_experts = jnp.sum(jnp.where((ordinal[None, :] == experts[:, None]) & active[None, :],
                                       experts[None, :], 0), axis=1)
    tables = [jnp.stack([cb[-1], jnp.sum(active.astype(jnp.int32))]),
              of_expert(pstart) + local * bm,
              jnp.clip(of_expert(seg) - local * bm, 0, bm),
              of_expert(ordinal),
              (local == 0).astype(jnp.int32),
              active_experts]
    tables = [a.astype(jnp.int32) for a in tables]
    zrow = pend[-1:].astype(jnp.int32)

    lists = plan3[:, PLAN_HALF:PLAN_HALF + N_COPY_CLASSES, :N_EXPERTS]
    full, _ = _copy_split(n8)
    nth = jnp.arange(FULL_COPIES, dtype=jnp.int32) * FULL_COPY
    skip = jnp.concatenate([jnp.broadcast_to((full * FULL_COPY)[:, None, :],
                                             (ng, N_COPY_CLASSES - FULL_COPIES, N_EXPERTS)),
                            jnp.broadcast_to(nth[None, :, None], (ng, FULL_COPIES, N_EXPERTS))], axis=1)
    pick = lists[..., None] == experts
    dst = jnp.sum(jnp.where(pick, (gdest[:, None, :] + skip)[:, :, None, :], 0), axis=-1)
    dst = jnp.pad(dst, ((0, 0), (0, PLAN_HALF - N_COPY_CLASSES), (0, LANES - N_EXPERTS)))
    plan = jnp.concatenate([plan3[:, :PLAN_HALF], dst], axis=1).astype(jnp.int32).reshape(-1)

    xs = _dispatch(x1, pos, plan, tot, zrow, n_rows, tg, ns, bm)
    yb = _gmm(xs, tables, w1, w3, w2, layer, bm)
    return _combine(yb, pos, gate, plan, tot, x1, ws1, ws3, ws2, g, b, alpha, tg, ns)


def kernel(x, ln_gain, ln_bias, pool_w, pool_scale, attn_wqkv, attn_bqkv, attn_sinks, attn_wo, attn_bo,
           conv_w_in, conv_w, conv_w_out, router_w, router_bias, expert_w1, expert_w3, expert_w2,
           shared_w1, shared_w3, shared_w2):
    bsz, seq, d = x.shape
    depth = ln_gain.shape[0]
    alpha = float((2 * depth) ** 0.25)
    h = x.reshape(bsz * seq, d)
    tg = min(GROUP_TG, seq)
    for i in range(depth):
        kind = i % N_MIXERS
        j = i // N_MIXERS
        g1, b1 = ln_gain[i, 0], ln_bias[i, 0]
        route = (router_w[i], router_bias[i], tg)
        if kind == 0:
            h, *routing = _pool_mixer(h, seq, pool_w[j], pool_scale[j], g1, b1, alpha, *route)
        elif kind == 1:
            h, *routing = _swa_mixer(h, seq, attn_wqkv[j], attn_bqkv[j], attn_sinks[j], attn_wo[j],
                                     attn_bo[j], g1, b1, alpha, *route)
        else:
            h, *routing = _conv_mixer(h, seq, conv_w_in[j], conv_w[j], conv_w_out[j], g1, b1, alpha, *route)
        h = _moe_layer(h, routing, tg, i, expert_w1, expert_w3, expert_w2,
                       shared_w1[i], shared_w3[i], shared_w2[i], ln_gain[i, 1], ln_bias[i, 1], alpha)
    return h.reshape(bsz, seq, d)
```

```python
import functools

import jax
import jax.numpy as jnp
from jax import lax
from jax.experimental import pallas as pl
from jax.experimental.pallas import tpu as pltpu

POOL_WINDOWS = (2, 4, 8, 16)
N_MIXERS = 3
HEAD_DIM = 64
N_Q_HEADS = 16
N_KV_HEADS = 4
GQA_GROUP = N_Q_HEADS // N_KV_HEADS
WINDOW = 128
ROPE_THETA = 10000.0
N_EXPERTS = 64
TOP_K = 8
N_EXPERT_GROUPS = 8
GROUP_SIZE = N_EXPERTS // N_EXPERT_GROUPS
TOPK_GROUPS = 4
ROUTED_SCALE = 2.5
LN_EPS = 1e-5

LANES = 128
SUBLANES = 8
POOL_HALO = 16
VMEM_LIMIT = 48 * 1024 * 1024

POOL_TT = 1024
QKV_TT = 512
ATTN_TQ = 512
CONV_TT = 512
GROUP_TG = 256
CHUNK_ALIGN = SUBLANES
SORT_CHUNK = 256
EXPERT_BM = 1024
GMM_X_SLOTS = 3
COMBINE_SLOTS = 3
FULL_COPY = 64
FULL_COPIES = GROUP_TG // FULL_COPY
COPY_SIZES = tuple(range(CHUNK_ALIGN, FULL_COPY, CHUNK_ALIGN)) + (FULL_COPY,) * FULL_COPIES
N_COPY_CLASSES = len(COPY_SIZES)
PLAN_HALF = 2 * SUBLANES
PLAN_COUNTS = N_COPY_CLASSES
PLAN_TOTAL = N_COPY_CLASSES + 1
PLAN_ROWS = 2 * PLAN_HALF
PLAN_WORDS = PLAN_ROWS * LANES

U32 = jnp.uint32
HI_MASK = 0xFFFF0000

F32 = jnp.float32
BF16 = jnp.bfloat16
NEG_INF = float("-inf")


def _layer_norm(z, g, b):
    mu = jnp.mean(z, axis=-1, keepdims=True)
    zc = z - mu
    var = jnp.mean(zc * zc, axis=-1, keepdims=True)
    return zc * lax.rsqrt(var + LN_EPS) * g + b


def _params(*sem):
    return pltpu.CompilerParams(dimension_semantics=sem, vmem_limit_bytes=VMEM_LIMIT)


def _router_plumbing(router_w, router_bias, t, d, tt, tg, tile_index):
    groups = tt // tg
    const = lambda *ids: (0, 0)
    by_cols = lambda *ids: (0, tile_index(*ids))
    by_rows = lambda *ids: (tile_index(*ids), 0)
    operands = [router_w.T, router_bias.reshape(N_EXPERTS, 1)]
    in_specs = [pl.BlockSpec((N_EXPERTS, d), const), pl.BlockSpec((N_EXPERTS, 1), const)]
    out_specs = [pl.BlockSpec((TOP_K, tt), by_cols), pl.BlockSpec((TOP_K, tt), by_cols),
                 pl.BlockSpec((groups * N_EXPERTS, LANES), by_rows),
                 pl.BlockSpec((groups * PLAN_ROWS, LANES), by_rows)]
    out_shapes = [jax.ShapeDtypeStruct((TOP_K, t), jnp.int32), jax.ShapeDtypeStruct((TOP_K, t), F32),
                  jax.ShapeDtypeStruct((t // tg * N_EXPERTS, LANES), jnp.int32),
                  jax.ShapeDtypeStruct((t // tg * PLAN_ROWS, LANES), jnp.int32)]
    return operands, in_specs, out_specs, out_shapes


def _pool_kernel(x_ref, halo_ref, w_ref, scale_ref, g_ref, b_ref, wt_ref, rb_ref, o_ref, pos_ref, gate_ref,
                 n8_ref, plan_ref, buf_ref, *, alpha, tt, tg):
    i = pl.program_id(1)
    x = x_ref[...]
    d = x.shape[1]
    ch = d // len(POOL_WINDOWS)
    top = SUBLANES
    first = top + POOL_HALO
    n = first + tt
    for k in range(3):
        buf_ref[k, 0:top, :] = jnp.zeros((top, d), F32)
    buf_ref[0, top:first, :] = jnp.where(i > 0, halo_ref[...], 0.0)
    buf_ref[0, first:n, :] = x
    pos = i * tt + lax.broadcasted_iota(jnp.int32, (tt, 1), 0)
    ys = []
    for g, w in enumerate(POOL_WINDOWS):
        c0 = g * ch
        xg = x[:, c0:c0 + ch]
        src, span = 0, 1
        while span < w:
            dst = 1 if src != 1 else 2
            buf_ref[dst, top:n, c0:c0 + ch] = (buf_ref[src, top:n, c0:c0 + ch]
                                               + buf_ref[src, top - span:n - span, c0:c0 + ch])
            src, span = dst, 2 * span
        s = buf_ref[src, first:n, c0:c0 + ch]
        inv_cnt = 1.0 / jnp.minimum(pos + 1, w).astype(F32)
        pooled = s * inv_cnt - xg
        ys.append(jnp.dot(pooled.astype(BF16), w_ref[g], preferred_element_type=F32))
    y = jnp.concatenate(ys, axis=1) * scale_ref[...]
    o_ref[...] = _layer_norm(alpha * x + y, g_ref[...], b_ref[...])
    _route_tile(o_ref, (wt_ref, rb_ref), (pos_ref, gate_ref, n8_ref, plan_ref), tg)


def _pool_mixer(x2, seq, w_grp, scale, g, b, alpha, router_w, router_bias, tg):
    t, d = x2.shape
    tt = min(POOL_TT, seq)
    nt = seq // tt
    nb = t // seq
    row = lambda v: v.reshape(1, d)
    const2 = lambda bi, i: (0, 0)
    tile = lambda bi, i: bi * nt + i
    r_ops, r_in, r_out, r_shapes = _router_plumbing(router_w, router_bias, t, d, tt, tg, tile)
    return pl.pallas_call(
        functools.partial(_pool_kernel, alpha=alpha, tt=tt, tg=tg),
        grid=(nb, nt),
        in_specs=[
            pl.BlockSpec((tt, d), lambda bi, i: (bi * nt + i, 0)),
            pl.BlockSpec((POOL_HALO, d),
                         lambda bi, i: (jnp.maximum((bi * seq + i * tt) // POOL_HALO - 1, 0), 0)),
            pl.BlockSpec(w_grp.shape, lambda bi, i: (0, 0, 0)),
            pl.BlockSpec((1, d), const2),
            pl.BlockSpec((1, d), const2),
            pl.BlockSpec((1, d), const2),
        ] + r_in,
        out_specs=[pl.BlockSpec((tt, d), lambda bi, i: (bi * nt + i, 0))] + r_out,
        out_shape=[jax.ShapeDtypeStruct((t, d), F32)] + r_shapes,
        scratch_shapes=[pltpu.VMEM((3, SUBLANES + POOL_HALO + tt, d), F32)],
        compiler_params=_params("parallel", "parallel"),
        name="pool_mixer",
    )(x2, x2, w_grp.astype(BF16), row(scale), row(g), row(b), *r_ops)


def _qkv_kernel(x_ref, w_ref, b_ref, c_ref, s_ref, q_ref, k_ref, v_ref, *, nq, nk):
    y = jnp.dot(x_ref[...].astype(BF16), w_ref[...], preferred_element_type=F32) + b_ref[...]
    c = c_ref[...]
    s = s_ref[...]
    lane = lax.broadcasted_iota(jnp.int32, c.shape, 1)
    first_half = (lane % HEAD_DIM) < (HEAD_DIM // 2)

    def rope(v):
        partner = jnp.where(first_half, pltpu.roll(v, LANES - HEAD_DIM // 2, 1),
                            pltpu.roll(v, HEAD_DIM // 2, 1))
        return v * c + partner * s

    for j in range(nq // LANES):
        q_ref[:, j * LANES:(j + 1) * LANES] = (
            rope(y[:, j * LANES:(j + 1) * LANES]) * (HEAD_DIM ** -0.5)).astype(BF16)
    for j in range(nk // LANES):
        k_ref[:, j * LANES:(j + 1) * LANES] = rope(
            y[:, nq + j * LANES:nq + (j + 1) * LANES]).astype(BF16)
    v_ref[...] = y[:, nq + nk:].astype(BF16)


def _attn_kernel(sink_ref, q_ref, k_ref, kh_ref, v_ref, vh_ref, x_ref, wo_ref, bo_ref, g_ref, b_ref,
                 wt_ref, rb_ref, o_ref, pos_ref, gate_ref, n8_ref, plan_ref, *, alpha, tq, tg):
    i = pl.program_id(1)
    nqb = tq // WINDOW
    stack = GQA_GROUP * tq
    row_head = lax.broadcasted_iota(jnp.int32, (stack, 1), 0) // tq
    r = lax.broadcasted_iota(jnp.int32, (stack, 2 * WINDOW), 0) % tq
    c = lax.broadcasted_iota(jnp.int32, (stack, 2 * WINDOW), 1)
    dist = (r & (WINDOW - 1)) + WINDOW - c
    valid = (dist >= 0) & (dist < WINDOW) & ((c >= WINDOW) | (r >= WINDOW) | (i > 0))
    lane = lax.broadcasted_iota(jnp.int32, (tq, LANES), 1)
    low = lane < HEAD_DIM
    dn = (((1,), (1,)), ((), ()))
    outs = []
    for kv in range(N_KV_HEADS):
        sl = slice(kv * LANES, (kv + 1) * LANES)
        kk = jnp.concatenate([kh_ref[:, sl], k_ref[:, sl]], axis=0)
        vv = jnp.concatenate([vh_ref[:, sl], v_ref[:, sl]], axis=0)
        blocks = []
        for h in range(GQA_GROUP):
            c0 = (kv * GQA_GROUP + h - h % 2) * HEAD_DIM
            qp = q_ref[:, c0:c0 + LANES]
            keep = low if h % 2 == 0 else jnp.logical_not(low)
            qh = jnp.where(keep, qp, jnp.zeros_like(qp))
            sc_all = lax.dot_general(qh, kk, dn, preferred_element_type=F32)
            blocks += [sc_all[qb * WINDOW:(qb + 1) * WINDOW, qb * WINDOW:(qb + 2) * WINDOW]
                       for qb in range(nqb)]
        sc = jnp.where(valid, jnp.concatenate(blocks, axis=0), NEG_INF)
        sink = jnp.zeros((stack, 1), F32)
        for h in range(GQA_GROUP):
            sink = jnp.where(row_head == h, sink_ref[kv * GQA_GROUP + h], sink)
        m = jnp.maximum(jnp.max(sc, axis=1, keepdims=True), sink)
        pm = jnp.exp(sc - m)
        rden = 1.0 / (jnp.sum(pm, axis=1, keepdims=True) + jnp.exp(sink - m))
        pm = pm.astype(BF16)
        res = jnp.concatenate(
            [jnp.dot(pm[b * WINDOW:(b + 1) * WINDOW, :],
                     vv[(b % nqb) * WINDOW:(b % nqb + 2) * WINDOW, :], preferred_element_type=F32)
             for b in range(GQA_GROUP * nqb)], axis=0) * rden
        for p in range(GQA_GROUP // 2):
            outs.append(jnp.where(low, res[2 * p * tq:(2 * p + 1) * tq, :],
                                  res[(2 * p + 1) * tq:(2 * p + 2) * tq, :]))
    o_all = jnp.concatenate(outs, axis=1).astype(BF16)
    mix = jnp.dot(o_all, wo_ref[...], preferred_element_type=F32) + bo_ref[...]
    o_ref[...] = _layer_norm(alpha * x_ref[...] + mix, g_ref[...], b_ref[...])
    _route_tile(o_ref, (wt_ref, rb_ref), (pos_ref, gate_ref, n8_ref, plan_ref), tg)


def _dup_heads(w, n_heads):
    lead = w.shape[:-1]
    w = w.reshape(lead + (n_heads, 1, HEAD_DIM))
    w = jnp.broadcast_to(w, lead + (n_heads, 2, HEAD_DIM))
    return w.reshape(lead + (n_heads * 2 * HEAD_DIM,))


def _swa_mixer(x2, seq, wqkv, bqkv, sinks, wo, bo, g, b, alpha, router_w, router_bias, tg):
    t, d = x2.shape
    nb = t // seq
    nq = N_Q_HEADS * HEAD_DIM
    nkv = N_KV_HEADS * HEAD_DIM
    nk2 = 2 * nkv
    w_ext = jnp.concatenate([wqkv[:, :nq], _dup_heads(wqkv[:, nq:nq + nkv], N_KV_HEADS),
                             _dup_heads(wqkv[:, nq + nkv:], N_KV_HEADS)], axis=1).astype(BF16)
    b_ext = jnp.concatenate([bqkv[:nq], _dup_heads(bqkv[nq:nq + nkv], N_KV_HEADS),
                             _dup_heads(bqkv[nq + nkv:], N_KV_HEADS)]).reshape(1, -1)
    n_ext = nq + 2 * nk2

    pos = jnp.arange(seq, dtype=F32)
    inv_freq = ROPE_THETA ** (-jnp.arange(0, HEAD_DIM, 2, dtype=F32) / HEAD_DIM)
    ang = pos[:, None] * inv_freq[None, :]
    cos, sin = jnp.cos(ang), jnp.sin(ang)
    cos_t = jnp.concatenate([cos, cos, cos, cos], axis=1)
    sin_t = jnp.concatenate([-sin, sin, -sin, sin], axis=1)

    tt = min(QKV_TT, seq)
    nt = seq // tt
    const2 = lambda i: (0, 0)
    q, k2, v2 = pl.pallas_call(
        functools.partial(_qkv_kernel, nq=nq, nk=nk2),
        grid=(t // tt,),
        in_specs=[
            pl.BlockSpec((tt, d), lambda i: (i, 0)),
            pl.BlockSpec((d, n_ext), const2),
            pl.BlockSpec((1, n_ext), const2),
            pl.BlockSpec((tt, LANES), lambda i: (i % nt, 0)),
            pl.BlockSpec((tt, LANES), lambda i: (i % nt, 0)),
        ],
        out_specs=[pl.BlockSpec((tt, nq), lambda i: (i, 0)),
                   pl.BlockSpec((tt, nk2), lambda i: (i, 0)),
                   pl.BlockSpec((tt, nk2), lambda i: (i, 0))],
        out_shape=[jax.ShapeDtypeStruct((t, nq), BF16),
                   jax.ShapeDtypeStruct((t, nk2), BF16),
                   jax.ShapeDtypeStruct((t, nk2), BF16)],
        compiler_params=_params("parallel"),
        name="swa_qkv",
    )(x2, w_ext, b_ext, cos_t, sin_t)

    tq = min(ATTN_TQ, seq)
    ntq = seq // tq
    main = lambda bi, i: (bi * ntq + i, 0)
    halo = lambda bi, i: (jnp.maximum((bi * seq + i * tq) // WINDOW - 1, 0), 0)
    const = lambda bi, i: (0, 0)
    row = lambda v: v.reshape(1, d)
    r_ops, r_in, r_out, r_shapes = _router_plumbing(router_w, router_bias, t, d, tq, tg,
                                                    lambda bi, i: bi * ntq + i)
    return pl.pallas_call(
        functools.partial(_attn_kernel, alpha=alpha, tq=tq, tg=tg),
        grid=(nb, ntq),
        in_specs=[
            pl.BlockSpec(memory_space=pltpu.SMEM),
            pl.BlockSpec((tq, nq), main),
            pl.BlockSpec((tq, nk2), main),
            pl.BlockSpec((WINDOW, nk2), halo),
            pl.BlockSpec((tq, nk2), main),
            pl.BlockSpec((WINDOW, nk2), halo),
            pl.BlockSpec((tq, d), main),
            pl.BlockSpec((nq, d), const),
            pl.BlockSpec((1, d), const),
            pl.BlockSpec((1, d), const),
            pl.BlockSpec((1, d), const),
        ] + r_in,
        out_specs=[pl.BlockSpec((tq, d), main)] + r_out,
        out_shape=[jax.ShapeDtypeStruct((t, d), F32)] + r_shapes,
        compiler_params=_params("parallel", "parallel"),
        name="swa_attention",
    )(sinks, q, k2, k2, v2, v2, x2, wo.astype(BF16), row(bo), row(g), row(b), *r_ops)


def _conv_kernel(x_ref, win_ref, cw_ref, wout_ref, g_ref, b_ref, wt_ref, rb_ref, o_ref, pos_ref, gate_ref,
                 n8_ref, plan_ref, ubuf_ref, *, alpha, tt, tg):
    i = pl.program_id(1)
    d = x_ref.shape[1]

    @pl.when(i == 0)
    def _():
        ubuf_ref[0:SUBLANES, :] = jnp.zeros((SUBLANES, d), F32)

    x = x_ref[...]
    proj = jnp.dot(x.astype(BF16), win_ref[...], preferred_element_type=F32)
    gate_b = proj[:, :d]
    u = proj[:, d:2 * d] * proj[:, 2 * d:]
    ubuf_ref[SUBLANES:SUBLANES + tt, :] = u
    cw = cw_ref[...]
    conv = (cw[0:1, :] * ubuf_ref[SUBLANES - 2:SUBLANES - 2 + tt, :]
            + cw[1:2, :] * ubuf_ref[SUBLANES - 1:SUBLANES - 1 + tt, :]
            + cw[2:3, :] * u)
    ubuf_ref[0:SUBLANES, :] = u[tt - SUBLANES:tt, :]
    y = jnp.dot((gate_b * conv).astype(BF16), wout_ref[...], preferred_element_type=F32)
    o_ref[...] = _layer_norm(alpha * x + y, g_ref[...], b_ref[...])
    _route_tile(o_ref, (wt_ref, rb_ref), (pos_ref, gate_ref, n8_ref, plan_ref), tg)


def _conv_mixer(x2, seq, w_in, conv_w, w_out, g, b, alpha, router_w, router_bias, tg):
    t, d = x2.shape
    nb = t // seq
    tt = min(CONV_TT, seq)
    nt = seq // tt
    const = lambda bi, i: (0, 0)
    row = lambda v: v.reshape(1, d)
    r_ops, r_in, r_out, r_shapes = _router_plumbing(router_w, router_bias, t, d, tt, tg,
                                                    lambda bi, i: bi * nt + i)
    return pl.pallas_call(
        functools.partial(_conv_kernel, alpha=alpha, tt=tt, tg=tg),
        grid=(nb, nt),
        in_specs=[
            pl.BlockSpec((tt, d), lambda bi, i: (bi * nt + i, 0)),
            pl.BlockSpec((d, 3 * d), const),
            pl.BlockSpec(conv_w.shape, const),
            pl.BlockSpec((d, d), const),
            pl.BlockSpec((1, d), const),
            pl.BlockSpec((1, d), const),
        ] + r_in,
        out_specs=[pl.BlockSpec((tt, d), lambda bi, i: (bi * nt + i, 0))] + r_out,
        out_shape=[jax.ShapeDtypeStruct((t, d), F32)] + r_shapes,
        scratch_shapes=[pltpu.VMEM((SUBLANES + tt, d), F32)],
        compiler_params=_params("arbitrary", "arbitrary"),
        name="conv_mixer",
    )(x2, w_in.astype(BF16), conv_w, w_out.astype(BF16), row(g), row(b), *r_ops)


def _first_argmax(v, rows, n):
    m = jnp.max(v, axis=0, keepdims=True)
    first = jnp.min(jnp.where(v == m, rows, n), axis=0, keepdims=True)
    return m, first


def _route_tile(o_ref, router_refs, route_out_refs, tg):
    wt_ref, bias_ref = router_refs
    pos_ref, gate_ref, n8_ref, plan_ref = route_out_refs
    tt = o_ref.shape[0]
    x = o_ref[...]
    xh = x.astype(BF16)
    xl = (x - xh.astype(F32)).astype(BF16)
    wt = wt_ref[...]
    wh = wt.astype(BF16)
    wl = (wt - wh.astype(F32)).astype(BF16)
    dn = (((1,), (1,)), ((), ()))
    logits = (lax.dot_general(wh, xh, dn, preferred_element_type=F32)
              + (lax.dot_general(wh, xl, dn, preferred_element_type=F32)
                 + lax.dot_general(wl, xh, dn, preferred_element_type=F32)))
    scores = jax.nn.sigmoid(logits)
    sel = scores + bias_ref[...]

    rows8 = lax.broadcasted_iota(jnp.int32, (GROUP_SIZE, tt), 0)
    gsc = []
    for gi in range(N_EXPERT_GROUPS):
        blk = sel[gi * GROUP_SIZE:(gi + 1) * GROUP_SIZE, :]
        m1, f1 = _first_argmax(blk, rows8, GROUP_SIZE)
        m2 = jnp.max(jnp.where(rows8 == f1, NEG_INF, blk), axis=0, keepdims=True)
        gsc.append(m1 + m2)
    cur = jnp.concatenate(gsc, axis=0)
    rows_g = lax.broadcasted_iota(jnp.int32, (N_EXPERT_GROUPS, tt), 0)
    gmask = jnp.zeros((N_EXPERT_GROUPS, tt), F32)
    for _ in range(TOPK_GROUPS):
        _, f = _first_argmax(cur, rows_g, N_EXPERT_GROUPS)
        hit = rows_g == f
        gmask = jnp.where(hit, 1.0, gmask)
        cur = jnp.where(hit, NEG_INF, cur)
    emask = jnp.concatenate(
        [jnp.broadcast_to(gmask[gi:gi + 1, :], (GROUP_SIZE, tt)) for gi in range(N_EXPERT_GROUPS)],
        axis=0)

    rows_e = lax.broadcasted_iota(jnp.int32, (N_EXPERTS, tt), 0)
    cur = jnp.where(emask > 0.0, sel, NEG_INF)
    idxs, gates = [], []
    member = jnp.zeros((N_EXPERTS, tt), F32)
    for _ in range(TOP_K):
        _, f = _first_argmax(cur, rows_e, N_EXPERTS)
        hit = rows_e == f
        idxs.append(f)
        gates.append(jnp.sum(jnp.where(hit, scores, 0.0), axis=0, keepdims=True))
        member = jnp.where(hit, 1.0, member)
        cur = jnp.where(hit, NEG_INF, cur)
    gate = jnp.concatenate(gates, axis=0)
    gate_ref[...] = gate / jnp.sum(gate, axis=0, keepdims=True) * ROUTED_SCALE

    for s in range(tt // tg):
        cols = slice(s * tg, (s + 1) * tg)
        _plan_group(member[:, cols], [f[:, cols] for f in idxs], pos_ref.at[:, pl.ds(s * tg, tg)],
                    n8_ref.at[pl.ds(s * N_EXPERTS, N_EXPERTS), :],
                    plan_ref.at[pl.ds(s * PLAN_ROWS, PLAN_ROWS), :], tg)


def _plan_group(member, idxs, pos_ref, n8_ref, plan_ref, tt):
    rows_e = lax.broadcasted_iota(jnp.int32, (N_EXPERTS, tt), 0)
    ta = lax.broadcasted_iota(jnp.int32, (tt, tt), 0)
    tb = lax.broadcasted_iota(jnp.int32, (tt, tt), 1)
    earlier = (ta < tb).astype(BF16)
    before = jnp.dot(member.astype(BF16), earlier, preferred_element_type=F32)
    n = jnp.sum(member, axis=1, keepdims=True)
    n8 = jnp.floor((n + (CHUNK_ALIGN - 1)) * (1.0 / CHUNK_ALIGN)) * CHUNK_ALIGN
    n8_wide = jnp.broadcast_to(n8, (N_EXPERTS, LANES))
    ea = lax.broadcasted_iota(jnp.int32, (N_EXPERTS, N_EXPERTS), 0)
    eb = lax.broadcasted_iota(jnp.int32, (N_EXPERTS, N_EXPERTS), 1)
    lower = (eb < ea).astype(BF16)
    chunk_start = jnp.dot(lower, n8_wide.astype(BF16), preferred_element_type=F32)
    where_to = before + chunk_start[:, 0:1]
    for k in range(TOP_K):
        pk = jnp.sum(jnp.where(rows_e == idxs[k], where_to, 0.0), axis=0, keepdims=True)
        pos_ref[k:k + 1, :] = pk.astype(jnp.int32)
    n8_int = n8_wide.astype(jnp.int32)
    n8_ref[...] = n8_int

    lane = lax.broadcasted_iota(jnp.int32, (N_EXPERTS, LANES), 1)
    full, rest = _copy_split(n8_int)
    flag = jnp.zeros((N_EXPERTS, LANES), F32)
    before_copy = jnp.zeros((N_EXPERTS, LANES), jnp.int32)
    for j, size in enumerate(COPY_SIZES):
        if size < FULL_COPY:
            needs, skip = rest == size, full * FULL_COPY
        else:
            nth = j - (N_COPY_CLASSES - FULL_COPIES)
            needs, skip = full > nth, nth * FULL_COPY
        flag = jnp.where((lane == j) & needs, 1.0, flag)
        before_copy = jnp.where(lane == j, skip, before_copy)
    rank = jnp.dot(lower, flag.astype(BF16), preferred_element_type=F32)
    e_id = lax.broadcasted_iota(jnp.int32, (N_EXPERTS, LANES), 0)
    start_int = chunk_start.astype(jnp.int32)
    plan_ref[...] = jnp.zeros(plan_ref.shape, jnp.int32)
    for j in range(N_COPY_CLASSES):
        chosen = (rank[:, j:j + 1] == lane.astype(F32)) & (flag[:, j:j + 1] > 0.0)
        buf_row = start_int + before_copy[:, j:j + 1]
        plan_ref[j:j + 1, :] = jnp.sum(jnp.where(chosen, buf_row, 0), axis=0, keepdims=True)
        plan_ref[PLAN_HALF + j:PLAN_HALF + j + 1, :] = jnp.sum(jnp.where(chosen, e_id, 0), axis=0,
                                                              keepdims=True)
    plan_ref[PLAN_COUNTS:PLAN_COUNTS + 1, :] = jnp.sum(flag, axis=0, keepdims=True).astype(jnp.int32)
    plan_ref[PLAN_TOTAL:PLAN_TOTAL + 1, :] = jnp.sum(n8_int, axis=0, keepdims=True)


def _copy_split(n8):
    return lax.shift_right_logical(n8, FULL_COPY.bit_length() - 1), n8 & (FULL_COPY - 1)


def _pack_halves(v, exact=False):
    h = v.shape[1] // 2
    return _pack_pair(v[:, :h], v[:, h:], exact)


def _pack_pair(lo, hi, exact=False):
    if not exact:
        lo, hi = lo.astype(BF16).astype(F32), hi.astype(BF16).astype(F32)
    lo = lax.bitcast_convert_type(lo, U32)
    hi = lax.bitcast_convert_type(hi, U32)
    return (lo >> 16) | (hi & jnp.uint32(HI_MASK))


def _chunk_relative(pos, c):
    rel = pos - c * SORT_CHUNK
    inside = (rel >= 0) & (rel < SORT_CHUNK)
    return jnp.where(inside, rel, -1).astype(F32).astype(BF16)


def _unpack_halves(w):
    lo = lax.bitcast_convert_type(w << 16, F32).astype(BF16)
    hi = lax.bitcast_convert_type(w & jnp.uint32(HI_MASK), F32).astype(BF16)
    return lo, hi


def _start_group_copies(plan_ref, make_copy):
    for j, size in enumerate(COPY_SIZES):
        def body(k, carry):
            make_copy(pl.multiple_of(plan_ref[j * LANES + k], CHUNK_ALIGN),
                      pl.multiple_of(plan_ref[(PLAN_HALF + j) * LANES + k], CHUNK_ALIGN), size).start()
            return carry

        lax.fori_loop(0, plan_ref[PLAN_COUNTS * LANES + j], body, 0)


def _wait_group_rows(total_rows, ns, make_copy):
    for j in range(ns.bit_length()):
        size = CHUNK_ALIGN << j
        if size > ns:
            break

        @pl.when((total_rows & size) != 0)
        def _():
            make_copy(0, 0, size).wait()


def _dispatch_kernel(tot_ref, zrow_ref, plan_ref, pos_ref, x_ref, xs_ref, sbuf_ref, zbuf_ref,
                     onehot_ref, sem, zsem, *, tg, ns, bm):
    g = pl.program_id(0)
    ng = pl.num_programs(0)
    slot = g % 2

    @pl.when(g == 0)
    def _():
        zbuf_ref[...] = jnp.zeros_like(zbuf_ref)
        row = pl.multiple_of(zrow_ref[0], CHUNK_ALIGN)
        tail = pltpu.make_async_copy(zbuf_ref, xs_ref.at[pl.ds(row, bm), :], zsem)
        tail.start()
        tail.wait()

    def copy_from(sl):
        def make_copy(src, dst, size):
            return pltpu.make_async_copy(sbuf_ref.at[sl, pl.ds(src, size), :],
                                         xs_ref.at[pl.ds(dst, size), :], sem.at[sl])
        return make_copy

    @pl.when(g >= 2)
    def _():
        _wait_group_rows(tot_ref[g - 2], ns, copy_from(slot))

    xb = x_ref[...].astype(BF16)
    pos = pos_ref[...]
    local_rows = lax.broadcasted_iota(jnp.int32, (SORT_CHUNK, tg), 0).astype(F32).astype(BF16)
    one = jnp.ones((SORT_CHUNK, tg), BF16)
    zero = jnp.zeros((SORT_CHUNK, tg), BF16)
    def onehot_chunk(c):
        rel = _chunk_relative(pos, c)
        hit = local_rows == rel[0:1, :]
        for k in range(1, TOP_K):
            hit = jnp.logical_or(hit, local_rows == rel[k:k + 1, :])
        return jnp.where(hit, one, zero)

    half = x_ref.shape[1] // 2

    def sort_rows(onehot, first_row):
        rows = onehot.shape[0]
        for n in range(half // SORT_CHUNK):
            lo = jnp.dot(onehot, xb[:, n * SORT_CHUNK:(n + 1) * SORT_CHUNK], preferred_element_type=F32)
            hi = jnp.dot(onehot, xb[:, half + n * SORT_CHUNK:half + (n + 1) * SORT_CHUNK],
                         preferred_element_type=F32)
            sbuf_ref[slot, first_row:first_row + rows, n * SORT_CHUNK:(n + 1) * SORT_CHUNK] = _pack_pair(
                lo, hi, exact=True)

    main = ns - SORT_CHUNK
    for c in range(main // SORT_CHUNK):
        onehot_ref[c * SORT_CHUNK:(c + 1) * SORT_CHUNK, :] = onehot_chunk(c)
    sort_rows(onehot_ref[0:main, :], 0)

    @pl.when(tot_ref[g] > main)
    def _():
        sort_rows(onehot_chunk(main // SORT_CHUNK), main)

    _start_group_copies(plan_ref, copy_from(slot))

    @pl.when(g == ng - 1)
    def _():
        @pl.when(g >= 1)
        def _():
            _wait_group_rows(tot_ref[g - 1], ns, copy_from(1 - slot))
        _wait_group_rows(tot_ref[g], ns, copy_from(slot))


def _dispatch(x1, pos, plan, tot, zrow, n_rows, tg, ns, bm):
    t, d = x1.shape
    grid_spec = pltpu.PrefetchScalarGridSpec(
        num_scalar_prefetch=2,
        grid=(t // tg,),
        in_specs=[
            pl.BlockSpec((PLAN_WORDS,), lambda i, *_: (i,), memory_space=pltpu.SMEM),
            pl.BlockSpec((TOP_K, tg), lambda i, *_: (0, i)),
            pl.BlockSpec((tg, d), lambda i, *_: (i, 0)),
        ],
        out_specs=pl.BlockSpec(memory_space=pl.ANY),
        scratch_shapes=[pltpu.VMEM((2, ns, d // 2), U32), pltpu.VMEM((bm, d // 2), U32),
                        pltpu.VMEM((ns - SORT_CHUNK, tg), BF16),
                        pltpu.SemaphoreType.DMA((2,)), pltpu.SemaphoreType.DMA],
    )
    return pl.pallas_call(
        functools.partial(_dispatch_kernel, tg=tg, ns=ns, bm=bm),
        grid_spec=grid_spec,
        out_shape=jax.ShapeDtypeStruct((n_rows, d // 2), U32),
        compiler_params=_params("arbitrary"),
        name="moe_dispatch",
    )(tot, zrow, plan, pos, x1)


def _gmm_kernel(nq_ref, brow_ref, bval_ref, bord_ref, bfirst_ref, aexp_ref,
                xs_ref, w1_ref, w3_ref, w2_ref, y_ref,
                xbuf_ref, ybuf_ref, w1s_ref, w3s_ref, w2s_ref, w13_ref, w2b_ref, xsem, ysem, wsem,
                *, layer, bm):
    nq = nq_ref[0]
    na = nq_ref[1]
    f = w1s_ref.shape[2]
    half = xbuf_ref.shape[2]
    nx = xbuf_ref.shape[0]

    def x_copy(q, slot):
        row = pl.multiple_of(brow_ref[q], CHUNK_ALIGN)
        return pltpu.make_async_copy(xs_ref.at[pl.ds(row, bm), :], xbuf_ref.at[slot], xsem.at[slot])

    def w_copies(j, slot):
        e = aexp_ref[j]
        return (pltpu.make_async_copy(w1_ref.at[layer, e], w1s_ref.at[slot], wsem.at[slot]),
                pltpu.make_async_copy(w3_ref.at[layer, e], w3s_ref.at[slot], wsem.at[slot]),
                pltpu.make_async_copy(w2_ref.at[layer, e], w2s_ref.at[slot], wsem.at[slot]))

    def y_copy(slot, row, off, size):
        return pltpu.make_async_copy(ybuf_ref.at[slot, pl.ds(off, size), :],
                                     y_ref.at[pl.ds(pl.multiple_of(row + off, CHUNK_ALIGN), size), :],
                                     ysem.at[slot])

    def for_each_y_copy(slot, row, valid, fn):
        @pl.when(valid == bm)
        def _():
            fn(y_copy(slot, row, 0, bm))

        @pl.when(valid < bm)
        def _():
            for j in range((bm // CHUNK_ALIGN).bit_length() - 1):
                size = CHUNK_ALIGN << j

                @pl.when((valid & size) != 0)
                def _():
                    fn(y_copy(slot, row, pl.multiple_of(valid & (size - 1), CHUNK_ALIGN), size))

    for c in w_copies(0, 0):
        c.start()
    for q0 in range(nx - 1):
        @pl.when(q0 < nq)
        def _():
            x_copy(q0, q0).start()

    def body(q, carry):
        xslot = q % nx
        yslot = q % 2

        @pl.when(q + (nx - 1) < nq)
        def _():
            x_copy(q + (nx - 1), (q + (nx - 1)) % nx).start()

        @pl.when(bfirst_ref[q] == 1)
        def _():
            j = bord_ref[q]
            ws = j % 2
            for c in w_copies(j, ws):
                c.wait()

            @pl.when(j + 1 < na)
            def _():
                for c in w_copies(j + 1, 1 - ws):
                    c.start()
            w13_ref[:, :f] = w1s_ref[ws].astype(BF16)
            w13_ref[:, f:] = w3s_ref[ws].astype(BF16)
            w2b_ref[...] = w2s_ref[ws].astype(BF16)

        x_copy(q, xslot).wait()

        @pl.when(q >= 2)
        def _():
            for_each_y_copy(yslot, 0, bval_ref[jnp.maximum(q - 2, 0)], lambda c: c.wait())

        def swiglu(rows):
            x_lo, x_hi = _unpack_halves(xbuf_ref[xslot, 0:rows, :])
            h13 = (jnp.dot(x_lo, w13_ref[:half, :], preferred_element_type=F32)
                   + jnp.dot(x_hi, w13_ref[half:, :], preferred_element_type=F32))
            h = jax.nn.silu(h13[:, :f]) * h13[:, f:]
            ybuf_ref[yslot, 0:rows, :] = _pack_halves(
                jnp.dot(h.astype(BF16), w2b_ref[...], preferred_element_type=F32))

        @pl.when(bval_ref[q] > bm // 2)
        def _():
            swiglu(bm)

        @pl.when(bval_ref[q] <= bm // 2)
        def _():
            swiglu(bm // 2)
        for_each_y_copy(yslot, brow_ref[q], bval_ref[q], lambda c: c.start())
        return carry

    lax.fori_loop(0, nq, body, 0)

    @pl.when(nq >= 2)
    def _():
        for_each_y_copy(nq % 2, 0, bval_ref[jnp.maximum(nq - 2, 0)], lambda c: c.wait())
    for_each_y_copy((nq - 1) % 2, 0, bval_ref[nq - 1], lambda c: c.wait())


def _gmm(xs, tables, w1, w3, w2, layer, bm):
    n_rows, half = xs.shape
    d = 2 * half
    f = w1.shape[-1]
    any_spec = pl.BlockSpec(memory_space=pl.ANY)
    grid_spec = pltpu.PrefetchScalarGridSpec(
        num_scalar_prefetch=len(tables),
        grid=(1,),
        in_specs=[any_spec, any_spec, any_spec, any_spec],
        out_specs=any_spec,
        scratch_shapes=[pltpu.VMEM((GMM_X_SLOTS, bm, half), U32), pltpu.VMEM((2, bm, half), U32),
                        pltpu.VMEM((2, d, f), F32), pltpu.VMEM((2, d, f), F32), pltpu.VMEM((2, f, d), F32),
                        pltpu.VMEM((d, 2 * f), BF16), pltpu.VMEM((f, d), BF16),
                        pltpu.SemaphoreType.DMA((GMM_X_SLOTS,)), pltpu.SemaphoreType.DMA((2,)),
                        pltpu.SemaphoreType.DMA((2,))],
    )
    return pl.pallas_call(
        functools.partial(_gmm_kernel, layer=layer, bm=bm),
        grid_spec=grid_spec,
        out_shape=jax.ShapeDtypeStruct((n_rows, half), U32),
        compiler_params=_params("arbitrary"),
        name="moe_experts",
    )(*tables, xs, w1, w3, w2)


def _combine_kernel(tot_ref, plan_ref, plan_next_ref, plan_next2_ref, yb_ref, pos_ref, gate_ref, x_ref, ws13_ref,
                    ws2_ref, g_ref, b_ref, o_ref, ybuf_ref, wgt_ref, ylo_ref, yhi_ref, sem, *, alpha, tg, ns):
    g = pl.program_id(0)
    ng = pl.num_programs(0)
    slot = g % COMBINE_SLOTS

    def copy_into(sl):
        def make_copy(buf_row, hbm_row, size):
            return pltpu.make_async_copy(yb_ref.at[pl.ds(hbm_row, size), :],
                                         ybuf_ref.at[sl, pl.ds(buf_row, size), :], sem.at[sl])
        return make_copy

    @pl.when(g == 0)
    def _():
        ybuf_ref[...] = jnp.zeros_like(ybuf_ref)
        _start_group_copies(plan_ref, copy_into(0))

        @pl.when(ng > 1)
        def _():
            _start_group_copies(plan_next_ref, copy_into(1))

    @pl.when(g + 2 < ng)
    def _():
        _start_group_copies(plan_next2_ref, copy_into((g + 2) % COMBINE_SLOTS))

    x = x_ref[...]
    f = ws2_ref.shape[0]
    h13 = jnp.dot(x.astype(BF16), ws13_ref[...], preferred_element_type=F32)
    hs = jax.nn.silu(h13[:, :f]) * h13[:, f:]
    shared = jnp.dot(hs.astype(BF16), ws2_ref[...], preferred_element_type=F32)

    _wait_group_rows(tot_ref[g], ns, copy_into(slot))

    pos = pos_ref[...]
    gate_b = gate_ref[...].astype(BF16)
    local_rows = lax.broadcasted_iota(jnp.int32, (SORT_CHUNK, tg), 0).astype(F32).astype(BF16)
    for c in range(ns // SORT_CHUNK):
        rel = _chunk_relative(pos, c)
        wgt = jnp.zeros((SORT_CHUNK, tg), BF16)
        for k in range(TOP_K):
            wgt = jnp.where(local_rows == rel[k:k + 1, :], gate_b[k:k + 1, :], wgt)
        wgt_ref[c * SORT_CHUNK:(c + 1) * SORT_CHUNK, :] = wgt
        y_lo, y_hi = _unpack_halves(ybuf_ref[slot, c * SORT_CHUNK:(c + 1) * SORT_CHUNK, :])
        ylo_ref[c * SORT_CHUNK:(c + 1) * SORT_CHUNK, :] = y_lo
        yhi_ref[c * SORT_CHUNK:(c + 1) * SORT_CHUNK, :] = y_hi
    wgt_all = wgt_ref[...]
    dn = (((0,), (0,)), ((), ()))
    routed = jnp.concatenate([lax.dot_general(wgt_all, ylo_ref[...], dn, preferred_element_type=F32),
                              lax.dot_general(wgt_all, yhi_ref[...], dn, preferred_element_type=F32)],
                             axis=1)
    o_ref[...] = _layer_norm(alpha * x + (routed + shared), g_ref[...], b_ref[...])


def _combine(yb, pos_kt, gate_kt, plan, tot, x1, ws1, ws3, ws2, g, b, alpha, tg, ns):
    t, d = x1.shape
    ng = t // tg
    f = ws1.shape[1]
    ws13 = jnp.concatenate([ws1, ws3], axis=1).astype(BF16)
    const = lambda i, *_: (0, 0)
    tile = lambda i, *_: (i, 0)
    row = lambda v: v.reshape(1, d)
    grid_spec = pltpu.PrefetchScalarGridSpec(
        num_scalar_prefetch=1,
        grid=(ng,),
        in_specs=[
            pl.BlockSpec((PLAN_WORDS,), lambda i, *_: (i,), memory_space=pltpu.SMEM),
            pl.BlockSpec((PLAN_WORDS,), lambda i, *_: (jnp.minimum(i + 1, ng - 1),),
                         memory_space=pltpu.SMEM),
            pl.BlockSpec((PLAN_WORDS,), lambda i, *_: (jnp.minimum(i + 2, ng - 1),),
                         memory_space=pltpu.SMEM),
            pl.BlockSpec(memory_space=pl.ANY),
            pl.BlockSpec((TOP_K, tg), lambda i, *_: (0, i)),
            pl.BlockSpec((TOP_K, tg), lambda i, *_: (0, i)),
            pl.BlockSpec((tg, d), tile),
            pl.BlockSpec((d, 2 * f), const),
            pl.BlockSpec((f, d), const),
            pl.BlockSpec((1, d), const),
            pl.BlockSpec((1, d), const),
        ],
        out_specs=pl.BlockSpec((tg, d), tile),
        scratch_shapes=[pltpu.VMEM((COMBINE_SLOTS, ns, d // 2), U32), pltpu.VMEM((ns, tg), BF16),
                        pltpu.VMEM((ns, d // 2), BF16), pltpu.VMEM((ns, d // 2), BF16),
                        pltpu.SemaphoreType.DMA((COMBINE_SLOTS,))],
    )
    return pl.pallas_call(
        functools.partial(_combine_kernel, alpha=alpha, tg=tg, ns=ns),
        grid_spec=grid_spec,
        out_shape=jax.ShapeDtypeStruct((t, d), F32),
        compiler_params=_params("arbitrary"),
        name="moe_combine",
    )(tot, plan, plan, plan, yb, pos_kt, gate_kt, x1, ws13, ws2.astype(BF16), row(g), row(b))


def _moe_layer(x1, routing, tg, layer, w1, w3, w2, ws1, ws3, ws2, g, b, alpha):
    ng = x1.shape[0] // tg
    bm = EXPERT_BM
    ns = -(-(tg * TOP_K + N_EXPERTS * (CHUNK_ALIGN - 1)) // SORT_CHUNK) * SORT_CHUNK
    pos, gate, n8_wide, plan_wide = routing
    plan3 = plan_wide.reshape(ng, PLAN_ROWS, LANES)
    tot = plan3[:, PLAN_TOTAL, 0]

    n8 = n8_wide[:, 0].reshape(ng, N_EXPERTS)
    seg = jnp.sum(n8, axis=0)
    pend = jnp.cumsum(seg)
    pstart = pend - seg
    gdest = pstart[None, :] + jnp.cumsum(n8, axis=0) - n8
    max_rows = ng * (tg * TOP_K + N_EXPERTS * (CHUNK_ALIGN - 1))
    n_rows = max_rows + bm

    experts = jnp.arange(N_EXPERTS, dtype=jnp.int32)
    nb = (seg + bm - 1) // bm
    cb = jnp.cumsum(nb)
    blk = jnp.arange(max_rows // bm + N_EXPERTS, dtype=jnp.int32)
    blk_expert = jnp.minimum(jnp.sum((cb[None, :] <= blk[:, None]).astype(jnp.int32), axis=1), N_EXPERTS - 1)
    mine = blk_expert[:, None] == experts[None, :]
    of_expert = lambda v: jnp.sum(jnp.where(mine, v[None, :], 0), axis=1)
    local = blk - of_expert(cb - nb)
    active = seg > 0
    ordinal = jnp.cumsum(active.astype(jnp.int32)) - 1
    active_experts = jnp.sum(jnp.where((ordinal[None, :] == experts[:, None]) & active[None, :],
                                       experts[None, :], 0), axis=1)
    tables = [jnp.stack([cb[-1], jnp.sum(active.astype(jnp.int32))]),
              of_expert(pstart) + local * bm,
              jnp.clip(of_expert(seg) - local * bm, 0, bm),
              of_expert(ordinal),
              (local == 0).astype(jnp.int32),
              active_experts]
    tables = [a.astype(jnp.int32) for a in tables]
    zrow = pend[-1:].astype(jnp.int32)

    lists = plan3[:, PLAN_HALF:PLAN_HALF + N_COPY_CLASSES, :N_EXPERTS]
    full, _ = _copy_split(n8)
    nth = jnp.arange(FULL_COPIES, dtype=jnp.int32) * FULL_COPY
    skip = jnp.concatenate([jnp.broadcast_to((full * FULL_COPY)[:, None, :],
                                             (ng, N_COPY_CLASSES - FULL_COPIES, N_EXPERTS)),
                            jnp.broadcast_to(nth[None, :, None], (ng, FULL_COPIES, N_EXPERTS))], axis=1)
    pick = lists[..., None] == experts
    dst = jnp.sum(jnp.where(pick, (gdest[:, None, :] + skip)[:, :, None, :], 0), axis=-1)
    dst = jnp.pad(dst, ((0, 0), (0, PLAN_HALF - N_COPY_CLASSES), (0, LANES - N_EXPERTS)))
    plan = jnp.concatenate([plan3[:, :PLAN_HALF], dst], axis=1).astype(jnp.int32).reshape(-1)

    xs = _dispatch(x1, pos, plan, tot, zrow, n_rows, tg, ns, bm)
    yb = _gmm(xs, tables, w1, w3, w2, layer, bm)
    return _combine(yb, pos, gate, plan, tot, x1, ws1, ws3, ws2, g, b, alpha, tg, ns)


def kernel(x, ln_gain, ln_bias, pool_w, pool_scale, attn_wqkv, attn_bqkv, attn_sinks, attn_wo, attn_bo,
           conv_w_in, conv_w, conv_w_out, router_w, router_bias, expert_w1, expert_w3, expert_w2,
           shared_w1, shared_w3, shared_w2):
    bsz, seq, d = x.shape
    depth = ln_gain.shape[0]
    alpha = float((2 * depth) ** 0.25)
    h = x.reshape(bsz * seq, d)
    tg = min(GROUP_TG, seq)
    for i in range(depth):
        kind = i % N_MIXERS
        j = i // N_MIXERS
        g1, b1 = ln_gain[i, 0], ln_bias[i, 0]
        route = (router_w[i], router_bias[i], tg)
        if kind == 0:
            h, *routing = _pool_mixer(h, seq, pool_w[j], pool_scale[j], g1, b1, alpha, *route)
        elif kind == 1:
            h, *routing = _swa_mixer(h, seq, attn_wqkv[j], attn_bqkv[j], attn_sinks[j], attn_wo[j],
                                     attn_bo[j], g1, b1, alpha, *route)
        else:
            h, *routing = _conv_mixer(h, seq, conv_w_in[j], conv_w[j], conv_w_out[j], g1, b1, alpha, *route)
        h = _moe_layer(h, routing, tg, i, expert_w1, expert_w3, expert_w2,
                       shared_w1[i], shared_w3[i], shared_w2[i], ln_gain[i, 1], ln_bias[i, 1], alpha)
    return h.reshape(bsz, seq, d)
```
